```python
import jax, jax.numpy as jnp
from jax import lax
import numpy as np

D_MODEL = 1024
BATCH = 8
SEQ = 2048
DEPTH = 4

CONF_WIDTH = 512
CONF_KERNEL = 31
SC_WIDTH = 512
SC_KERNEL = 3
MLA_HEADS = 8
QK_NOPE = 64
QK_ROPE = 32
V_HEAD = 64
Q_LORA = 384
KV_LORA = 256
MLA_WIDTH = MLA_HEADS * V_HEAD
ROPE_THETA = 10000.0
Q_BLOCK = 128
N_BRANCH = 3
LN_EPS = 1e-5
RMS_EPS = 1e-6
DEEPNORM_ALPHA = (2 * DEPTH) ** 0.25
DEEPNORM_BETA = (8 * DEPTH) ** -0.25
ADA_SCALE = 0.5

IN_SIZES = (2 * CONF_WIDTH, CONF_WIDTH, 3 * SC_WIDTH, SC_WIDTH, Q_LORA, KV_LORA, QK_ROPE, MLA_WIDTH, N_BRANCH * D_MODEL)
D_IN = 2 * CONF_WIDTH + CONF_WIDTH + 3 * SC_WIDTH + SC_WIDTH + Q_LORA + KV_LORA + QK_ROPE + MLA_WIDTH + N_BRANCH * D_MODEL

kernel_name = "hybrid_conformer_shortconv_mla_deepnorm_adaln"


def layer_norm(x, g, b):
    x32 = x.astype(jnp.float32)
    mu = jnp.mean(x32, axis=-1, keepdims=True)
    var = jnp.mean(jnp.square(x32 - mu), axis=-1, keepdims=True)
    y = (x32 - mu) * lax.rsqrt(var + LN_EPS)
    return (y * g.astype(jnp.float32) + b.astype(jnp.float32)).astype(x.dtype)


def rms_norm(x, g):
    x32 = x.astype(jnp.float32)
    y = x32 * lax.rsqrt(jnp.mean(jnp.square(x32), axis=-1, keepdims=True) + RMS_EPS)
    return (y * g.astype(jnp.float32)).astype(x.dtype)


def causal_depthwise_conv(x, w):
    k_width, ch = w.shape
    return lax.conv_general_dilated(
        x, w.astype(x.dtype)[:, None, :], window_strides=(1,), padding=((k_width - 1, 0),),
        dimension_numbers=("NWC", "WIO", "NWC"), feature_group_count=ch)


def rope_tables(positions):
    inv_freq = ROPE_THETA ** (-jnp.arange(0, QK_ROPE, 2, dtype=jnp.float32) / QK_ROPE)
    ang = positions.astype(jnp.float32)[..., None] * inv_freq
    return jnp.cos(ang), jnp.sin(ang)


def apply_rope(x, cos, sin):
    cos = cos.astype(x.dtype)
    sin = sin.astype(x.dtype)
    x1, x2 = jnp.split(x, 2, axis=-1)
    return jnp.concatenate([x1 * cos - x2 * sin, x2 * cos + x1 * sin], axis=-1)


def causal_block_attention(q, k, v):
    b, s, h, dv = v.shape
    scale = (QK_NOPE + QK_ROPE) ** -0.5
    key_idx = jnp.arange(s)

    def one_block(i):
        start = i * Q_BLOCK
        qb = lax.dynamic_slice_in_dim(q, start, Q_BLOCK, axis=1)
        sc = jnp.einsum("bqhd,bkhd->bhqk", qb, k, preferred_element_type=jnp.float32) * scale
        q_idx = start + jnp.arange(Q_BLOCK)
        mask = key_idx[None, :] <= q_idx[:, None]
        sc = jnp.where(mask[None, None], sc, -jnp.inf)
        p = jax.nn.softmax(sc, axis=-1).astype(v.dtype)
        return jnp.einsum("bhqk,bkhd->bqhd", p, v)

    out = lax.map(one_block, jnp.arange(s // Q_BLOCK))
    return out.transpose(1, 0, 2, 3, 4).reshape(b, s, h, dv)


def hybrid_layer(x, c_act, cos, sin, w_ada, b_ada, w_in, conv_a_w, conv_a_b, ln_a_g, ln_a_b, w_a_out,
                 conv_b_w, w_b_out, q_norm_g, kv_norm_g, w_uq, w_ukv, w_c_out, w_o, ln_g, ln_b):
    b, s, _ = x.shape
    ada = c_act @ w_ada + b_ada
    shift, scale, gate = jnp.split(ada, 3, axis=-1)
    u = x * (1.0 + scale[:, None, :]) + shift[:, None, :]

    proj = jnp.einsum("bsd,dp->bsp", u, w_in)
    split_pts = [int(v) for v in np.cumsum(IN_SIZES)[:-1]]
    a_in, a_gate, b_in, b_gate, q_lat, kv_lat, k_rope, c_gate, merge_logits = jnp.split(proj, split_pts, axis=-1)

    a1, a2 = jnp.split(a_in, 2, axis=-1)
    a = a1 * jax.nn.sigmoid(a2)
    a = causal_depthwise_conv(a, conv_a_w) + conv_a_b
    a = jax.nn.silu(layer_norm(a, ln_a_g, ln_a_b))
    y_a = jnp.einsum("bsc,cd->bsd", a * jax.nn.silu(a_gate), w_a_out)

    xb, gb, gc = jnp.split(b_in, 3, axis=-1)
    yb = gb * causal_depthwise_conv(gc * xb, conv_b_w)
    y_b = jnp.einsum("bsc,cd->bsd", yb * jax.nn.silu(b_gate), w_b_out)

    q = jnp.einsum("bsr,rk->bsk", rms_norm(q_lat, q_norm_g), w_uq).reshape(b, s, MLA_HEADS, QK_NOPE + QK_ROPE)
    q_nope, q_pe = jnp.split(q, [QK_NOPE], axis=-1)
    q_pe = apply_rope(q_pe, cos[:, :, None, :], sin[:, :, None, :])
    kv = jnp.einsum("bsr,rk->bsk", rms_norm(kv_lat, kv_norm_g), w_ukv).reshape(b, s, MLA_HEADS, QK_NOPE + V_HEAD)
    k_nope, v = jnp.split(kv, [QK_NOPE], axis=-1)
    k_pe = apply_rope(k_rope, cos, sin)[:, :, None, :]
    qf = jnp.concatenate([q_nope, q_pe], axis=-1)
    kf = jnp.concatenate([k_nope, jnp.broadcast_to(k_pe, (b, s, MLA_HEADS, QK_ROPE))], axis=-1)
    o = causal_block_attention(qf, kf, v).reshape(b, s, MLA_WIDTH)
    y_c = jnp.einsum("bsc,cd->bsd", o * jax.nn.silu(c_gate), w_c_out)

    g_a, g_b, g_c = jnp.split(jax.nn.sigmoid(merge_logits), N_BRANCH, axis=-1)
    m = g_a * y_a + g_b * y_b + g_c * y_c
    out = jnp.einsum("bsd,de->bse", m, w_o)

    return layer_norm(DEEPNORM_ALPHA * x + gate[:, None, :] * out, ln_g, ln_b)


def _fwd_setup_inputs(seed: int = 0) -> dict:
    key = jax.random.key(seed)
    ks = jax.random.split(key, 24)
    L, D = DEPTH, D_MODEL

    def nrm(k, shape, scale):
        return jax.random.normal(k, shape, dtype=jnp.float32) * scale

    x = nrm(ks[0], (BATCH, SEQ, D), 1.0)
    c = nrm(ks[1], (BATCH, D), 1.0)
    offsets = jax.random.randint(ks[2], (BATCH, 1), 0, 1024, dtype=jnp.int32)
    positions = (jnp.arange(SEQ, dtype=jnp.int32)[None, :] + offsets).astype(jnp.int32)
    return {
        "x": x,
        "c": c,
        "positions": positions,
        "w_ada": nrm(ks[3], (L, D, 3 * D), ADA_SCALE * D ** -0.5),
        "b_ada": nrm(ks[4], (L, 3 * D), 0.02),
        "w_in": nrm(ks[5], (L, D, D_IN), D ** -0.5),
        "conv_a_w": nrm(ks[6], (L, CONF_KERNEL, CONF_WIDTH), CONF_KERNEL ** -0.5),
        "conv_a_b": nrm(ks[7], (L, CONF_WIDTH), 0.02),
        "ln_a_g": 1.0 + nrm(ks[8], (L, CONF_WIDTH), 0.02),
        "ln_a_b": nrm(ks[9], (L, CONF_WIDTH), 0.02),
        "w_a_out": nrm(ks[10], (L, CONF_WIDTH, D), DEEPNORM_BETA * CONF_WIDTH ** -0.5),
        "conv_b_w": nrm(ks[11], (L, SC_KERNEL, SC_WIDTH), SC_KERNEL ** -0.5),
        "w_b_out": nrm(ks[12], (L, SC_WIDTH, D), DEEPNORM_BETA * SC_WIDTH ** -0.5),
        "q_norm_g": 1.0 + nrm(ks[13], (L, Q_LORA), 0.02),
        "kv_norm_g": 1.0 + nrm(ks[14], (L, KV_LORA), 0.02),
        "w_uq": nrm(ks[15], (L, Q_LORA, MLA_HEADS * (QK_NOPE + QK_ROPE)), Q_LORA ** -0.5),
        "w_ukv": nrm(ks[16], (L, KV_LORA, MLA_HEADS * (QK_NOPE + V_HEAD)), KV_LORA ** -0.5),
        "w_c_out": nrm(ks[17], (L, MLA_WIDTH, D), DEEPNORM_BETA * MLA_WIDTH ** -0.5),
        "w_o": nrm(ks[18], (L, D, D), DEEPNORM_BETA * D ** -0.5),
        "ln_g": 1.0 + nrm(ks[19], (L, D), 0.02),
        "ln_b": nrm(ks[20], (L, D), 0.02),
    }


def _fwd_reference(x, c, positions, w_ada, b_ada, w_in, conv_a_w, conv_a_b, ln_a_g, ln_a_b, w_a_out,
              conv_b_w, w_b_out, q_norm_g, kv_norm_g, w_uq, w_ukv, w_c_out, w_o, ln_g, ln_b):
    c_act = jax.nn.silu(c)
    cos, sin = rope_tables(positions)
    h = x
    for l in range(DEPTH):
        h = hybrid_layer(h, c_act, cos, sin, w_ada[l], b_ada[l], w_in[l], conv_a_w[l], conv_a_b[l],
                         ln_a_g[l], ln_a_b[l], w_a_out[l], conv_b_w[l], w_b_out[l], q_norm_g[l],
                         kv_norm_g[l], w_uq[l], w_ukv[l], w_c_out[l], w_o[l], ln_g[l], ln_b[l])
    return h


import jax as _jax
import jax.numpy as _jnp

TWIN_FORMAT = 'train_step'
FWD_PARAMS = ['x', 'c', 'positions', 'w_ada', 'b_ada', 'w_in', 'conv_a_w', 'conv_a_b', 'ln_a_g', 'ln_a_b', 'w_a_out', 'conv_b_w', 'w_b_out', 'q_norm_g', 'kv_norm_g', 'w_uq', 'w_ukv', 'w_c_out', 'w_o', 'ln_g', 'ln_b']
TWIN_WEIGHTS = ['w_ada', 'b_ada', 'w_in', 'conv_a_w', 'conv_a_b', 'ln_a_g', 'ln_a_b', 'w_a_out', 'conv_b_w', 'w_b_out', 'q_norm_g', 'kv_norm_g', 'w_uq', 'w_ukv', 'w_c_out', 'w_o', 'ln_g', 'ln_b']
TWIN_DIFF_INPUT = 'x'
TWIN_INPUTS = ['x', 'c', 'positions', 'w_ada', 'b_ada', 'w_in', 'conv_a_w', 'conv_a_b', 'ln_a_g', 'ln_a_b', 'w_a_out', 'conv_b_w', 'w_b_out', 'q_norm_g', 'kv_norm_g', 'w_uq', 'w_ukv', 'w_c_out', 'w_o', 'ln_g', 'ln_b', 'loss_target', 'm_w_ada', 'm_b_ada', 'm_w_in', 'm_conv_a_w', 'm_conv_a_b', 'm_ln_a_g', 'm_ln_a_b', 'm_w_a_out', 'm_conv_b_w', 'm_w_b_out', 'm_q_norm_g', 'm_kv_norm_g', 'm_w_uq', 'm_w_ukv', 'm_w_c_out', 'm_w_o', 'm_ln_g', 'm_ln_b', 'v_w_ada', 'v_b_ada', 'v_w_in', 'v_conv_a_w', 'v_conv_a_b', 'v_ln_a_g', 'v_ln_a_b', 'v_w_a_out', 'v_conv_b_w', 'v_w_b_out', 'v_q_norm_g', 'v_kv_norm_g', 'v_w_uq', 'v_w_ukv', 'v_w_c_out', 'v_w_o', 'v_ln_g', 'v_ln_b']
TWIN_OUTPUTS = ['loss', 'grad_x', 'grad_w_ada', 'grad_b_ada', 'grad_w_in', 'grad_conv_a_w', 'grad_conv_a_b', 'grad_ln_a_g', 'grad_ln_a_b', 'grad_w_a_out', 'grad_conv_b_w', 'grad_w_b_out', 'grad_q_norm_g', 'grad_kv_norm_g', 'grad_w_uq', 'grad_w_ukv', 'grad_w_c_out', 'grad_w_o', 'grad_ln_g', 'grad_ln_b', 'delta_w_ada', 'delta_b_ada', 'delta_w_in', 'delta_conv_a_w', 'delta_conv_a_b', 'delta_ln_a_g', 'delta_ln_a_b', 'delta_w_a_out', 'delta_conv_b_w', 'delta_w_b_out', 'delta_q_norm_g', 'delta_kv_norm_g', 'delta_w_uq', 'delta_w_ukv', 'delta_w_c_out', 'delta_w_o', 'delta_ln_g', 'delta_ln_b', 'new_m_w_ada', 'new_m_b_ada', 'new_m_w_in', 'new_m_conv_a_w', 'new_m_conv_a_b', 'new_m_ln_a_g', 'new_m_ln_a_b', 'new_m_w_a_out', 'new_m_conv_b_w', 'new_m_w_b_out', 'new_m_q_norm_g', 'new_m_kv_norm_g', 'new_m_w_uq', 'new_m_w_ukv', 'new_m_w_c_out', 'new_m_w_o', 'new_m_ln_g', 'new_m_ln_b', 'new_v_w_ada', 'new_v_b_ada', 'new_v_w_in', 'new_v_conv_a_w', 'new_v_conv_a_b', 'new_v_ln_a_g', 'new_v_ln_a_b', 'new_v_w_a_out', 'new_v_conv_b_w', 'new_v_w_b_out', 'new_v_q_norm_g', 'new_v_kv_norm_g', 'new_v_w_uq', 'new_v_w_ukv', 'new_v_w_c_out', 'new_v_w_o', 'new_v_ln_g', 'new_v_ln_b']
TWIN_LEAF_KINDS = {'loss': 'loss', 'grad_x': 'grad_x', 'grad_w_ada': 'grad_w', 'grad_b_ada': 'grad_w', 'grad_w_in': 'grad_w', 'grad_conv_a_w': 'grad_w', 'grad_conv_a_b': 'grad_w', 'grad_ln_a_g': 'grad_w', 'grad_ln_a_b': 'grad_w', 'grad_w_a_out': 'grad_w', 'grad_conv_b_w': 'grad_w', 'grad_w_b_out': 'grad_w', 'grad_q_norm_g': 'grad_w', 'grad_kv_norm_g': 'grad_w', 'grad_w_uq': 'grad_w', 'grad_w_ukv': 'grad_w', 'grad_w_c_out': 'grad_w', 'grad_w_o': 'grad_w', 'grad_ln_g': 'grad_w', 'grad_ln_b': 'grad_w', 'delta_w_ada': 'delta_w', 'delta_b_ada': 'delta_w', 'delta_w_in': 'delta_w', 'delta_conv_a_w': 'delta_w', 'delta_conv_a_b': 'delta_w', 'delta_ln_a_g': 'delta_w', 'delta_ln_a_b': 'delta_w', 'delta_w_a_out': 'delta_w', 'delta_conv_b_w': 'delta_w', 'delta_w_b_out': 'delta_w', 'delta_q_norm_g': 'delta_w', 'delta_kv_norm_g': 'delta_w', 'delta_w_uq': 'delta_w', 'delta_w_ukv': 'delta_w', 'delta_w_c_out': 'delta_w', 'delta_w_o': 'delta_w', 'delta_ln_g': 'delta_w', 'delta_ln_b': 'delta_w', 'new_m_w_ada': 'new_m', 'new_m_b_ada': 'new_m', 'new_m_w_in': 'new_m', 'new_m_conv_a_w': 'new_m', 'new_m_conv_a_b': 'new_m', 'new_m_ln_a_g': 'new_m', 'new_m_ln_a_b': 'new_m', 'new_m_w_a_out': 'new_m', 'new_m_conv_b_w': 'new_m', 'new_m_w_b_out': 'new_m', 'new_m_q_norm_g': 'new_m', 'new_m_kv_norm_g': 'new_m', 'new_m_w_uq': 'new_m', 'new_m_w_ukv': 'new_m', 'new_m_w_c_out': 'new_m', 'new_m_w_o': 'new_m', 'new_m_ln_g': 'new_m', 'new_m_ln_b': 'new_m', 'new_v_w_ada': 'new_v', 'new_v_b_ada': 'new_v', 'new_v_w_in': 'new_v', 'new_v_conv_a_w': 'new_v', 'new_v_conv_a_b': 'new_v', 'new_v_ln_a_g': 'new_v', 'new_v_ln_a_b': 'new_v', 'new_v_w_a_out': 'new_v', 'new_v_conv_b_w': 'new_v', 'new_v_w_b_out': 'new_v', 'new_v_q_norm_g': 'new_v', 'new_v_kv_norm_g': 'new_v', 'new_v_w_uq': 'new_v', 'new_v_w_ukv': 'new_v', 'new_v_w_c_out': 'new_v', 'new_v_w_o': 'new_v', 'new_v_ln_g': 'new_v', 'new_v_ln_b': 'new_v'}


def _forward(args):
    return _fwd_reference(*[args[k] for k in FWD_PARAMS])


def _output_shape():
    out = _jax.eval_shape(lambda: _forward(_fwd_setup_inputs(0)))
    return out.shape, out.dtype

N_MICROBATCH = 1
ADAM_LR = 0.001
ADAM_B1 = 0.9
ADAM_B2 = 0.999
ADAM_EPS = 1e-08
ADAM_WD = 0.01
ADAM_STEP = 10
PER_EXAMPLE_BATCH_AXIS = {'x': 0, 'c': 0, 'positions': 0, 'loss_target': 0}
SHARED_INPUTS = []
_WEIGHT_DTYPES = {'w_ada': _jnp.float32, 'b_ada': _jnp.float32, 'w_in': _jnp.float32, 'conv_a_w': _jnp.float32, 'conv_a_b': _jnp.float32, 'ln_a_g': _jnp.float32, 'ln_a_b': _jnp.float32, 'w_a_out': _jnp.float32, 'conv_b_w': _jnp.float32, 'w_b_out': _jnp.float32, 'q_norm_g': _jnp.float32, 'kv_norm_g': _jnp.float32, 'w_uq': _jnp.float32, 'w_ukv': _jnp.float32, 'w_c_out': _jnp.float32, 'w_o': _jnp.float32, 'ln_g': _jnp.float32, 'ln_b': _jnp.float32}
MOMENT_SCALE = {'w_ada': 3.266288e-03, 'b_ada': 5.285259e-03, 'w_in': 1.561423e-03, 'conv_a_w': 1.307574e-03, 'conv_a_b': 2.455266e-03, 'ln_a_g': 1.551373e-03, 'ln_a_b': 1.370318e-03, 'w_a_out': 2.120572e-03, 'conv_b_w': 2.739272e-03, 'w_b_out': 4.631108e-03, 'q_norm_g': 4.112046e-04, 'kv_norm_g': 1.120469e-03, 'w_uq': 2.951948e-04, 'w_ukv': 5.656858e-04, 'w_c_out': 1.279362e-03, 'w_o': 5.237261e-03, 'ln_g': 8.032661e+00, 'ln_b': 3.679681e-01}


def _to_microbatches(a, axis):
    t = _jnp.moveaxis(a, axis, 0)
    t = t.reshape((N_MICROBATCH, t.shape[0] // N_MICROBATCH) + t.shape[1:])
    return _jnp.moveaxis(t, 1, axis + 1)


def setup_inputs(seed: int = 0) -> dict:
    inp = _fwd_setup_inputs(seed)
    key = _jax.random.fold_in(_jax.random.key(seed), 7919)
    shape, _ = _output_shape()
    out = dict(inp)
    out["loss_target"] = _jax.random.normal(_jax.random.fold_in(key, 0), shape, _jnp.float32)
    for i, name in enumerate(TWIN_WEIGHTS):
        w = inp[name].astype(_jnp.float32)
        if MOMENT_SCALE is None:
            s = _jnp.sqrt(_jnp.mean(_jnp.square(w)) + 1e-30)
        else:
            s = MOMENT_SCALE[name]
        km, kv = _jax.random.split(_jax.random.fold_in(key, i + 1))
        out[name] = w
        out["m_" + name] = s * _jax.random.normal(km, w.shape, _jnp.float32)
        out["v_" + name] = (s * s) * _jax.random.uniform(kv, w.shape, _jnp.float32, 0.5, 1.5)
    if N_MICROBATCH > 1:
        for name, axis in PER_EXAMPLE_BATCH_AXIS.items():
            out[name] = _to_microbatches(out[name], axis)
    return {'x': out['x'], 'c': out['c'], 'positions': out['positions'], 'w_ada': out['w_ada'], 'b_ada': out['b_ada'], 'w_in': out['w_in'], 'conv_a_w': out['conv_a_w'], 'conv_a_b': out['conv_a_b'], 'ln_a_g': out['ln_a_g'], 'ln_a_b': out['ln_a_b'], 'w_a_out': out['w_a_out'], 'conv_b_w': out['conv_b_w'], 'w_b_out': out['w_b_out'], 'q_norm_g': out['q_norm_g'], 'kv_norm_g': out['kv_norm_g'], 'w_uq': out['w_uq'], 'w_ukv': out['w_ukv'], 'w_c_out': out['w_c_out'], 'w_o': out['w_o'], 'ln_g': out['ln_g'], 'ln_b': out['ln_b'], 'loss_target': out['loss_target'], 'm_w_ada': out['m_w_ada'], 'm_b_ada': out['m_b_ada'], 'm_w_in': out['m_w_in'], 'm_conv_a_w': out['m_conv_a_w'], 'm_conv_a_b': out['m_conv_a_b'], 'm_ln_a_g': out['m_ln_a_g'], 'm_ln_a_b': out['m_ln_a_b'], 'm_w_a_out': out['m_w_a_out'], 'm_conv_b_w': out['m_conv_b_w'], 'm_w_b_out': out['m_w_b_out'], 'm_q_norm_g': out['m_q_norm_g'], 'm_kv_norm_g': out['m_kv_norm_g'], 'm_w_uq': out['m_w_uq'], 'm_w_ukv': out['m_w_ukv'], 'm_w_c_out': out['m_w_c_out'], 'm_w_o': out['m_w_o'], 'm_ln_g': out['m_ln_g'], 'm_ln_b': out['m_ln_b'], 'v_w_ada': out['v_w_ada'], 'v_b_ada': out['v_b_ada'], 'v_w_in': out['v_w_in'], 'v_conv_a_w': out['v_conv_a_w'], 'v_conv_a_b': out['v_conv_a_b'], 'v_ln_a_g': out['v_ln_a_g'], 'v_ln_a_b': out['v_ln_a_b'], 'v_w_a_out': out['v_w_a_out'], 'v_conv_b_w': out['v_conv_b_w'], 'v_w_b_out': out['v_w_b_out'], 'v_q_norm_g': out['v_q_norm_g'], 'v_kv_norm_g': out['v_kv_norm_g'], 'v_w_uq': out['v_w_uq'], 'v_w_ukv': out['v_w_ukv'], 'v_w_c_out': out['v_w_c_out'], 'v_w_o': out['v_w_o'], 'v_ln_g': out['v_ln_g'], 'v_ln_b': out['v_ln_b']}


def _loss(weights, diff, rest, loss_target):
    with _jax.named_scope("forward"):
        args = {**rest, TWIN_DIFF_INPUT: diff, **{k: w.astype(_WEIGHT_DTYPES[k]) for k, w in weights.items()}}
        y = _forward(args)
    with _jax.named_scope("loss_head"):
        err = _jnp.square(y.astype(_jnp.float32) - loss_target)
        return 0.5 * _jnp.sum(_jnp.mean(err, axis=-1)) if err.ndim else 0.5 * err


def _adamw(w, g, m, v):
    m = ADAM_B1 * m + (1.0 - ADAM_B1) * g
    v = ADAM_B2 * v + (1.0 - ADAM_B2) * _jnp.square(g)
    m_hat = m / (1.0 - ADAM_B1 ** ADAM_STEP)
    v_hat = v / (1.0 - ADAM_B2 ** ADAM_STEP)
    delta = -ADAM_LR * (m_hat / (_jnp.sqrt(v_hat) + ADAM_EPS) + ADAM_WD * w)
    return delta, m, v


def reference(x, c, positions, w_ada, b_ada, w_in, conv_a_w, conv_a_b, ln_a_g, ln_a_b, w_a_out, conv_b_w, w_b_out, q_norm_g, kv_norm_g, w_uq, w_ukv, w_c_out, w_o, ln_g, ln_b, loss_target, m_w_ada, m_b_ada, m_w_in, m_conv_a_w, m_conv_a_b, m_ln_a_g, m_ln_a_b, m_w_a_out, m_conv_b_w, m_w_b_out, m_q_norm_g, m_kv_norm_g, m_w_uq, m_w_ukv, m_w_c_out, m_w_o, m_ln_g, m_ln_b, v_w_ada, v_b_ada, v_w_in, v_conv_a_w, v_conv_a_b, v_ln_a_g, v_ln_a_b, v_w_a_out, v_conv_b_w, v_w_b_out, v_q_norm_g, v_kv_norm_g, v_w_uq, v_w_ukv, v_w_c_out, v_w_o, v_ln_g, v_ln_b):
    given = dict(x=x, c=c, positions=positions, w_ada=w_ada, b_ada=b_ada, w_in=w_in, conv_a_w=conv_a_w, conv_a_b=conv_a_b, ln_a_g=ln_a_g, ln_a_b=ln_a_b, w_a_out=w_a_out, conv_b_w=conv_b_w, w_b_out=w_b_out, q_norm_g=q_norm_g, kv_norm_g=kv_norm_g, w_uq=w_uq, w_ukv=w_ukv, w_c_out=w_c_out, w_o=w_o, ln_g=ln_g, ln_b=ln_b, loss_target=loss_target, m_w_ada=m_w_ada, m_b_ada=m_b_ada, m_w_in=m_w_in, m_conv_a_w=m_conv_a_w, m_conv_a_b=m_conv_a_b, m_ln_a_g=m_ln_a_g, m_ln_a_b=m_ln_a_b, m_w_a_out=m_w_a_out, m_conv_b_w=m_conv_b_w, m_w_b_out=m_w_b_out, m_q_norm_g=m_q_norm_g, m_kv_norm_g=m_kv_norm_g, m_w_uq=m_w_uq, m_w_ukv=m_w_ukv, m_w_c_out=m_w_c_out, m_w_o=m_w_o, m_ln_g=m_ln_g, m_ln_b=m_ln_b, v_w_ada=v_w_ada, v_b_ada=v_b_ada, v_w_in=v_w_in, v_conv_a_w=v_conv_a_w, v_conv_a_b=v_conv_a_b, v_ln_a_g=v_ln_a_g, v_ln_a_b=v_ln_a_b, v_w_a_out=v_w_a_out, v_conv_b_w=v_conv_b_w, v_w_b_out=v_w_b_out, v_q_norm_g=v_q_norm_g, v_kv_norm_g=v_kv_norm_g, v_w_uq=v_w_uq, v_w_ukv=v_w_ukv, v_w_c_out=v_w_c_out, v_w_o=v_w_o, v_ln_g=v_ln_g, v_ln_b=v_ln_b)
    weights = {n: given[n] for n in TWIN_WEIGHTS}
    shared = {n: given[n] for n in SHARED_INPUTS}
    per_example = {n: given[n] for n in ['x', 'c', 'positions']}
    grad_fn = _jax.value_and_grad(_loss, argnums=(0, 1))

    def one_microbatch(ex, loss_target):
        ex = dict(ex)
        diff = ex.pop(TWIN_DIFF_INPUT)
        return grad_fn(weights, diff, {**shared, **ex}, loss_target)

    if N_MICROBATCH == 1:
        loss, (grad_w, grad_x) = one_microbatch(per_example, given["loss_target"])
    else:
        def body(carry, xs):
            loss_sum, grad_sum = carry
            l_k, (gw_k, gx_k) = one_microbatch(xs[0], xs[1])
            with _jax.named_scope("update"):
                return (loss_sum + l_k, _jax.tree.map(_jnp.add, grad_sum, gw_k)), gx_k

        init = (_jnp.zeros((), _jnp.float32), _jax.tree.map(_jnp.zeros_like, weights))
        (loss, grad_w), grad_x = _jax.lax.scan(body, init, (per_example, given["loss_target"]))
    with _jax.named_scope("update"):
        delta_w, new_m, new_v = {}, {}, {}
        for n in TWIN_WEIGHTS:
            delta_w[n], new_m[n], new_v[n] = _adamw(weights[n], grad_w[n], given["m_" + n], given["v_" + n])
    return (loss, grad_x, *[grad_w[n] for n in TWIN_WEIGHTS], *[delta_w[n] for n in TWIN_WEIGHTS],
            *[new_m[n] for n in TWIN_WEIGHTS], *[new_v[n] for n in TWIN_WEIGHTS])
```

```python
import functools

import numpy as np
import jax
import jax.numpy as jnp
from jax import lax
from jax.experimental import pallas as pl
from jax.experimental.pallas import tpu as pltpu

F32 = jnp.float32
BF16 = jnp.bfloat16
MESH = pl.DeviceIdType.MESH

HEADS = 8
QK_NOPE = 64
QK_ROPE = 32
V_HEAD = 64
CONV_K = 31
SC_K = 3
LN_EPS = 1e-5
RMS_EPS = 1e-6
ROPE_THETA = 10000.0
ATT_SCALE = (QK_NOPE + QK_ROPE) ** -0.5
ADAM_LR = 0.001
ADAM_B1 = 0.9
ADAM_B2 = 0.999
ADAM_EPS = 1e-08
ADAM_WD = 0.01
ADAM_STEP = 10

W_AB = 3584
W_G5 = 3584
W_C1 = 896
W_PACK = W_AB + W_G5 + W_C1
COL_BLK = 896
HEAD_PAD = 128
CONV_HALO = 32
NEG_BIG = -1e30

VMEM_LIMIT_MB = 56
ATT_BLOCK = 512
ROW_BLOCK = 256


def _cparams(n_grid, mb=VMEM_LIMIT_MB):
    return pltpu.CompilerParams(dimension_semantics=("arbitrary",) * n_grid, vmem_limit_bytes=mb * 2 ** 20)


def _sig(x):
    return 1.0 / (1.0 + jnp.exp(-x))


def _dsilu(x, s):
    return s * (1.0 + x * (1.0 - s))


def _dot(a, b):
    return jnp.dot(a, b, preferred_element_type=F32)


def _dot_nt(a, b):
    return lax.dot_general(a, b, (((1,), (1,)), ((), ())), preferred_element_type=F32)


def _dot_tn(a, b):
    return lax.dot_general(a, b, (((0,), (0,)), ((), ())), preferred_element_type=F32)


def _row_spec(t, w):
    return pl.BlockSpec((t, w), lambda i: (i, 0))


def _full_spec(shape):
    nd = len(shape)
    return pl.BlockSpec(shape, lambda *_: (0,) * nd)


def _my_place():
    x, y, c = lax.axis_index("x"), lax.axis_index("y"), lax.axis_index("c")
    chips = [(1 - x, y), (x, 1 - y), (1 - x, 1 - y)]
    return x, y, c, chips


def _all_gather_small(v, name):
    m_per, n = v.shape

    def body(x_ref, out_ref, send_sems, recv_sems, local_sem):
        x, y, c, chips = _my_place()
        me, sibling = (x, y, c), (x, y, 1 - c)

        def rows(px, py, pc):
            return out_ref.at[pl.ds((4 * px + 2 * py + pc) * m_per, m_per), :]

        def copy(k, block, to, src=None):
            return pltpu.make_async_remote_copy(
                src_ref=rows(*block) if src is None else src, dst_ref=rows(*block),
                send_sem=send_sems.at[k], recv_sem=recv_sems.at[k], device_id=to, device_id_type=MESH)

        mine = pltpu.make_async_copy(x_ref, rows(*me), local_sem)
        mine.start()
        first = [copy(0, me, sibling, src=x_ref)]
        first += [copy(1 + j, me, (*chip, c), src=x_ref) for j, chip in enumerate(chips)]
        for cp in first:
            cp.start()
        passed = [copy(4 + j, (*chip, c), sibling) for j, chip in enumerate(chips)]
        for j, chip in enumerate(chips):
            copy(1 + j, (*chip, c), me).wait_recv()
            passed[j].start()
        copy(0, sibling, me).wait_recv()
        for j, chip in enumerate(chips):
            copy(4 + j, (*chip, 1 - c), me).wait_recv()
        for cp in first + passed:
            cp.wait_send()
        mine.wait()

    return pl.pallas_call(
        body, name=name,
        out_shape=jax.ShapeDtypeStruct((8 * m_per, n), v.dtype),
        in_specs=[pl.BlockSpec(memory_space=pltpu.VMEM)],
        out_specs=pl.BlockSpec(memory_space=pltpu.VMEM),
        scratch_shapes=[pltpu.SemaphoreType.DMA((7,)), pltpu.SemaphoreType.DMA((7,)), pltpu.SemaphoreType.DMA],
        compiler_params=pltpu.CompilerParams(vmem_limit_bytes=VMEM_LIMIT_MB * 2 ** 20),
    )(v)


def _any_specs(n):
    return [pl.BlockSpec(memory_space=pl.ANY)] * n


def _row_half(ref, half_index, axis):
    half = ref.shape[axis] // 2
    idx = [slice(None)] * axis + [pl.ds(half_index * half, half)]
    return ref.at[tuple(idx)]


def _gather_weights(ws):
    n = len(ws)

    def body(*refs):
        w_refs, out_refs = refs[:n], refs[n:2 * n]
        send_sems, recv_sems, local_sems = refs[2 * n:]
        x, y, c, chips = _my_place()
        me, sibling = (x, y, c), (x, y, 1 - c)

        def slot(a, px, py, pc):
            return _row_half(out_refs[a].at[2 * px + py], pc, 1)

        def copy(a, k, block, to, src=None):
            return pltpu.make_async_remote_copy(
                src_ref=slot(a, *block) if src is None else src, dst_ref=slot(a, *block),
                send_sem=send_sems.at[a, k], recv_sem=recv_sems.at[a, k], device_id=to, device_id_type=MESH)

        started = []
        for a in range(n):
            src = _row_half(w_refs[a], c, 1)
            mine = pltpu.make_async_copy(src, slot(a, *me), local_sems.at[a])
            mine.start()
            started.append(mine)
            first = [copy(a, 0, me, sibling, src=src)]
            first += [copy(a, 1 + j, me, (*chip, c), src=src) for j, chip in enumerate(chips)]
            for cp in first:
                cp.start()
            started += first
        sends = []
        for j, chip in enumerate(chips):
            for a in range(n):
                copy(a, 1 + j, (*chip, c), me).wait_recv()
                fwd = copy(a, 4 + j, (*chip, c), sibling)
                fwd.start()
                sends.append(fwd)
        for a in range(n):
            copy(a, 0, sibling, me).wait_recv()
        for j, chip in enumerate(chips):
            for a in range(n):
                copy(a, 4 + j, (*chip, 1 - c), me).wait_recv()
        for cp in started[1::5] + started[2::5] + started[3::5] + started[4::5] + sends:
            cp.wait_send()
        for cp in started[0::5]:
            cp.wait()

    return pl.pallas_call(
        body, name="gather_weights",
        out_shape=[jax.ShapeDtypeStruct((4,) + w.shape, w.dtype) for w in ws],
        in_specs=_any_specs(n), out_specs=_any_specs(n),
        scratch_shapes=[pltpu.SemaphoreType.DMA((n, 7)), pltpu.SemaphoreType.DMA((n, 7)), pltpu.SemaphoreType.DMA((n,))],
    )(*ws)


def _swap_halves(ps):
    n = len(ps)

    def body(*refs):
        p_refs, land_refs = refs[:n], refs[n:2 * n]
        send_sems, recv_sems = refs[2 * n:]
        x, y, c, _ = _my_place()
        copies = [pltpu.make_async_remote_copy(
            src_ref=_row_half(p_refs[a], 1 - c, 1), dst_ref=land_refs[a], send_sem=send_sems.at[a],
            recv_sem=recv_sems.at[a], device_id=(x, y, 1 - c), device_id_type=MESH) for a in range(n)]
        for cp in copies:
            cp.start()
        for cp in copies:
            cp.wait()

    return pl.pallas_call(
        body, name="grad_swap_halves",
        out_shape=[jax.ShapeDtypeStruct((4, p.shape[1] // 2, p.shape[2]), p.dtype) for p in ps],
        in_specs=_any_specs(n), out_specs=_any_specs(n),
        scratch_shapes=[pltpu.SemaphoreType.DMA((n,)), pltpu.SemaphoreType.DMA((n,))],
    )(*ps)


def _add_halves(ps, lands, c_idx):
    n = len(ps)

    def body(c_ref, *refs):
        del c_ref
        for a in range(n):
            refs[2 * n + a][...] = (refs[a][...].astype(F32) + refs[n + a][...].astype(F32)).astype(BF16)

    def half_block(p):
        return (None, p.shape[1] // 2, p.shape[2])

    grid_spec = pltpu.PrefetchScalarGridSpec(
        num_scalar_prefetch=1, grid=(4,),
        in_specs=[pl.BlockSpec(half_block(p), lambda j, cr: (j, cr[0], 0)) for p in ps]
        + [pl.BlockSpec(half_block(p), lambda j, cr: (j, 0, 0)) for p in ps],
        out_specs=[pl.BlockSpec(half_block(p), lambda j, cr: (j, 0, 0)) for p in ps])
    return pl.pallas_call(
        body, name="grad_add_halves", grid_spec=grid_spec,
        out_shape=[jax.ShapeDtypeStruct((4, p.shape[1] // 2, p.shape[2]), BF16) for p in ps],
        compiler_params=_cparams(1),
    )(c_idx, *ps, *lands)


def _scatter_to_owners(hs):
    n = len(hs)

    def body(*refs):
        h_refs, out_refs = refs[:n], refs[n:2 * n]
        send_sems, recv_sems, local_sems = refs[2 * n:]
        x, y, c, chips = _my_place()
        k_me = 2 * x + y
        started = []
        for a in range(n):
            mine = pltpu.make_async_copy(h_refs[a].at[k_me], out_refs[a].at[k_me], local_sems.at[a])
            mine.start()
            started.append(mine)
        sends = []
        for j, (px, py) in enumerate(chips):
            for a in range(n):
                cp = pltpu.make_async_remote_copy(
                    src_ref=h_refs[a].at[2 * px + py], dst_ref=out_refs[a].at[k_me], send_sem=send_sems.at[a, j],
                    recv_sem=recv_sems.at[a, j], device_id=(px, py, c), device_id_type=MESH)
                cp.start()
                sends.append(cp)
        for j, (px, py) in enumerate(chips):
            for a in range(n):
                pltpu.make_async_remote_copy(
                    src_ref=h_refs[a].at[k_me], dst_ref=out_refs[a].at[2 * px + py], send_sem=send_sems.at[a, j],
                    recv_sem=recv_sems.at[a, j], device_id=(px, py, c), device_id_type=MESH).wait_recv()
        for cp in sends:
            cp.wait_send()
        for cp in started:
            cp.wait()

    return pl.pallas_call(
        body, name="grad_scatter_to_owners",
        out_shape=[jax.ShapeDtypeStruct(h.shape, h.dtype) for h in hs],
        in_specs=_any_specs(n), out_specs=_any_specs(n),
        scratch_shapes=[pltpu.SemaphoreType.DMA((n, 3)), pltpu.SemaphoreType.DMA((n, 3)), pltpu.SemaphoreType.DMA((n,))],
    )(*hs)


def _sum_chips(rcvs, accs, layer, n_layers):
    n = len(rcvs)

    def body(*refs):
        outs = refs[len(refs) - n:]
        for a in range(n):
            r_ref = refs[a]
            acc = r_ref[0].astype(F32) + r_ref[1].astype(F32)
            acc = acc + r_ref[2].astype(F32)
            outs[a][...] = acc + r_ref[3].astype(F32)

    in_specs = [_full_spec(r.shape) for r in rcvs]
    args = list(rcvs)
    aliases = {}
    if accs is not None:
        in_specs += _any_specs(n)
        args += list(accs)
        aliases = {n + a: a for a in range(n)}
    return pl.pallas_call(
        body, name="grad_sum_chips", grid=(1,),
        in_specs=in_specs,
        out_specs=[pl.BlockSpec((None,) + r.shape[1:], lambda i: (layer, 0, 0)) for r in rcvs],
        out_shape=[jax.ShapeDtypeStruct((n_layers,) + r.shape[1:], F32) for r in rcvs],
        input_output_aliases=aliases, compiler_params=_cparams(1),
    )(*args)


def _pair_halves(fs):
    n = len(fs)

    def body(*refs):
        f_refs, out_refs = refs[:n], refs[n:2 * n]
        send_sems, recv_sems, local_sems = refs[2 * n:]
        x, y, c, _ = _my_place()
        started = []
        for a in range(n):
            mine = pltpu.make_async_copy(f_refs[a], _row_half(out_refs[a], c, 1), local_sems.at[a])
            mine.start()
            cp = pltpu.make_async_remote_copy(
                src_ref=f_refs[a], dst_ref=_row_half(out_refs[a], c, 1), send_sem=send_sems.at[a],
                recv_sem=recv_sems.at[a], device_id=(x, y, 1 - c), device_id_type=MESH)
            cp.start()
            started.append((mine, cp))
        for a in range(n):
            pltpu.make_async_remote_copy(
                src_ref=f_refs[a], dst_ref=_row_half(out_refs[a], 1 - c, 1), send_sem=send_sems.at[a],
                recv_sem=recv_sems.at[a], device_id=(x, y, 1 - c), device_id_type=MESH).wait_recv()
        for mine, cp in started:
            cp.wait_send()
            mine.wait()

    return pl.pallas_call(
        body, name="grad_pair_halves",
        out_shape=[jax.ShapeDtypeStruct((f.shape[0], 2 * f.shape[1], f.shape[2]), f.dtype) for f in fs],
        in_specs=_any_specs(n), out_specs=_any_specs(n),
        scratch_shapes=[pltpu.SemaphoreType.DMA((n,)), pltpu.SemaphoreType.DMA((n,)), pltpu.SemaphoreType.DMA((n,))],
    )(*fs)


def _sum_devices(g):
    _, r, n = g.shape

    def body(g_ref, o_ref):
        acc = g_ref[0]
        for d in range(1, 8):
            acc = acc + g_ref[d]
        o_ref[...] = acc

    return pl.pallas_call(
        body, name="small_grad_sum", grid=(1,),
        in_specs=[_full_spec((8, r, n))], out_specs=_full_spec((r, n)),
        out_shape=jax.ShapeDtypeStruct((r, n), F32), compiler_params=_cparams(1),
    )(g)


def _matmul(a, w, bn, name):
    s, k = a.shape
    n = w.shape[1]
    bs = min(1024, s)

    def body(a_ref, w_ref, o_ref):
        o_ref[...] = _dot(a_ref[...], w_ref[...])

    return pl.pallas_call(
        body, name=name, grid=(n // bn, s // bs),
        in_specs=[pl.BlockSpec((bs, k), lambda j, i: (i, 0)), pl.BlockSpec((k, bn), lambda j, i: (0, j))],
        out_specs=pl.BlockSpec((bs, bn), lambda j, i: (i, j)),
        out_shape=jax.ShapeDtypeStruct((s, n), F32), compiler_params=_cparams(2),
    )(a, w)


def _matmul_tn(a, b, bn, name):
    s, m = a.shape
    n = b.shape[1]
    bs = min(512, s)

    def body(a_ref, b_ref, o_ref):
        @pl.when(pl.program_id(1) == 0)
        def _():
            o_ref[...] = jnp.zeros_like(o_ref)

        o_ref[...] += _dot_tn(a_ref[...], b_ref[...])

    return pl.pallas_call(
        body, name=name, grid=(n // bn, s // bs),
        in_specs=[pl.BlockSpec((bs, m), lambda j, i: (i, 0)), pl.BlockSpec((bs, bn), lambda j, i: (i, j))],
        out_specs=pl.BlockSpec((m, bn), lambda j, i: (0, j)),
        out_shape=jax.ShapeDtypeStruct((m, n), F32), compiler_params=_cparams(2),
    )(a, b)


def _modulate(x, scale, shift):
    s, d = x.shape
    t = min(512, s)

    def body(x_ref, sc_ref, sh_ref, u_ref):
        u_ref[...] = (x_ref[...] * (1.0 + sc_ref[...]) + sh_ref[...]).astype(BF16)

    return pl.pallas_call(
        body, name="modulate", grid=(s // t,),
        in_specs=[_row_spec(t, d), _full_spec((1, d)), _full_spec((1, d))],
        out_specs=_row_spec(t, d), out_shape=jax.ShapeDtypeStruct((s, d), BF16), compiler_params=_cparams(1),
    )(x, scale, shift)


def _rope_tables(pos_col):
    s = pos_col.shape[0]
    t = min(512, s)
    inv = ROPE_THETA ** (-np.arange(0, QK_ROPE, 2, dtype=np.float32) / QK_ROPE)
    lane_freq = np.zeros((1, HEAD_PAD), np.float32)
    lane_freq[0, QK_NOPE:QK_NOPE + 16] = inv
    lane_freq[0, QK_NOPE + 16:QK_NOPE + 32] = inv
    lane_mask = np.zeros((1, HEAD_PAD), np.float32)
    lane_mask[0, QK_NOPE:QK_NOPE + QK_ROPE] = 1.0

    def body(p_ref, f_ref, m_ref, cos_ref, sin_ref):
        ang = p_ref[...].astype(F32) * f_ref[...]
        cos_ref[...] = jnp.cos(ang) * m_ref[...]
        sin_ref[...] = jnp.sin(ang) * m_ref[...]

    return pl.pallas_call(
        body, name="rope_tables", grid=(s // t,),
        in_specs=[_row_spec(t, 1), _full_spec((1, HEAD_PAD)), _full_spec((1, HEAD_PAD))],
        out_specs=[_row_spec(t, HEAD_PAD), _row_spec(t, HEAD_PAD)],
        out_shape=[jax.ShapeDtypeStruct((s, HEAD_PAD), F32)] * 2, compiler_params=_cparams(1),
    )(pos_col, jnp.asarray(lane_freq), jnp.asarray(lane_mask))


def _ada_shard(c_all, w_ada_bf):
    n_layers, d, n = w_ada_bf.shape

    def body(c_ref, w_ref, o_ref, ca_ref):
        cv = c_ref[...]
        ca = cv * _sig(cv)
        ca_ref[...] = ca
        o_ref[...] = _dot(ca.astype(BF16), w_ref[...])

    return pl.pallas_call(
        body, name="ada_shard", grid=(n_layers,),
        in_specs=[_full_spec((8, d)), pl.BlockSpec((None, d, n), lambda l: (l, 0, 0))],
        out_specs=[pl.BlockSpec((None, 8, n), lambda l: (l, 0, 0)), _full_spec((8, d))],
        out_shape=[jax.ShapeDtypeStruct((n_layers, 8, n), F32), jax.ShapeDtypeStruct((8, d), F32)],
        compiler_params=_cparams(1),
    )(c_all, w_ada_bf)


def _ada_weight_grad(cact16, dada16):
    n_layers, _, n = dada16.shape
    d = cact16.shape[1]

    def body(c_ref, g_ref, o_ref):
        o_ref[...] = _dot_tn(c_ref[...], g_ref[...])

    return pl.pallas_call(
        body, name="ada_weight_grad", grid=(n_layers,),
        in_specs=[_full_spec((16, d)), pl.BlockSpec((None, 16, n), lambda l: (l, 0, 0))],
        out_specs=pl.BlockSpec((None, d, n), lambda l: (l, 0, 0)),
        out_shape=jax.ShapeDtypeStruct((n_layers, d, n), F32), compiler_params=_cparams(1),
    )(cact16, dada16)


def _layer_norm_rows(v, g, b):
    mu = jnp.mean(v, axis=-1, keepdims=True)
    var = jnp.mean(jnp.square(v - mu), axis=-1, keepdims=True)
    rstd = lax.rsqrt(var + LN_EPS)
    xh = (v - mu) * rstd
    return xh, rstd, xh * g + b


def _layer_norm_bwd_rows(dy_hat, xh, rstd):
    return rstd * (dy_hat - jnp.mean(dy_hat, axis=-1, keepdims=True) - xh * jnp.mean(dy_hat * xh, axis=-1, keepdims=True))


def _mix_ab_fwd(proj, conv_a_w, conv_a_b, ln_a_g, ln_a_b, conv_b_w):
    s = proj.shape[0]
    t = min(ROW_BLOCK, s)
    cw = 512
    halo = CONV_HALO

    def body(p_ref, caw_ref, cab_ref, lg_ref, lb_ref, cbw_ref, acta_ref, actb_ref, ca_ref, cb_ref, abuf, zbuf):
        @pl.when(pl.program_id(0) == 0)
        def _():
            abuf[0:halo, :] = jnp.zeros((halo, cw), F32)
            zbuf[0:8, :] = jnp.zeros((8, cw), F32)

        abuf[halo:halo + t, :] = p_ref[:, 0:512] * _sig(p_ref[:, 512:1024])
        acc = jnp.zeros((t, cw), F32)
        for k in range(CONV_K):
            off = halo - (CONV_K - 1) + k
            acc = acc + caw_ref[k:k + 1, :] * abuf[off:off + t, :]
        ca = acc + cab_ref[...]
        ca_ref[...] = ca
        _, _, ln = _layer_norm_rows(ca, lg_ref[...], lb_ref[...])
        ag = p_ref[:, 1024:1536]
        acta_ref[...] = ((ln * _sig(ln)) * (ag * _sig(ag))).astype(BF16)
        abuf[0:halo, :] = abuf[t:t + halo, :]

        zbuf[8:8 + t, :] = p_ref[:, 2560:3072] * p_ref[:, 1536:2048]
        accb = jnp.zeros((t, cw), F32)
        for k in range(SC_K):
            off = 8 - (SC_K - 1) + k
            accb = accb + cbw_ref[k:k + 1, :] * zbuf[off:off + t, :]
        cb_ref[...] = accb
        bg = p_ref[:, 3072:3584]
        actb_ref[...] = ((p_ref[:, 2048:2560] * accb) * (bg * _sig(bg))).astype(BF16)
        zbuf[0:8, :] = zbuf[t:t + 8, :]

    return pl.pallas_call(
        body, name="mix_ab_fwd", grid=(s // t,),
        in_specs=[_row_spec(t, W_AB), _full_spec((CONV_K, cw)), _full_spec((1, cw)), _full_spec((1, cw)),
                  _full_spec((1, cw)), _full_spec((SC_K, cw))],
        out_specs=[_row_spec(t, cw)] * 4,
        out_shape=[jax.ShapeDtypeStruct((s, cw), BF16)] * 2 + [jax.ShapeDtypeStruct((s, cw), F32)] * 2,
        scratch_shapes=[pltpu.VMEM((halo + t, cw), F32), pltpu.VMEM((8 + t, cw), F32)],
        compiler_params=_cparams(1),
    )(proj, conv_a_w, conv_a_b, ln_a_g, ln_a_b, conv_b_w)


def _mix_ab_bwd(proj, ca, cb, dact_a, dact_b, conv_a_w, ln_a_g, ln_a_b, conv_b_w, dproj):
    s = proj.shape[0]
    t = min(ROW_BLOCK, s)
    nb = s // t
    cw = 512
    halo = CONV_HALO
    per = t // halo

    def body(p_ref, ph_ref, ca_ref, cb_ref, da_ref, db_ref, caw_ref, lg_ref, lb_ref, cbw_ref, dproj_any,
             dg_ref, dcaw_ref, dcab_ref, dlg_ref, dlb_ref, dcbw_ref, abuf, dcabuf, zbuf, dcbbuf):
        del dproj_any
        i = pl.program_id(0)
        r = nb - 1 - i

        @pl.when(i == 0)
        def _():
            dcaw_ref[...] = jnp.zeros_like(dcaw_ref)
            dcab_ref[...] = jnp.zeros_like(dcab_ref)
            dlg_ref[...] = jnp.zeros_like(dlg_ref)
            dlb_ref[...] = jnp.zeros_like(dlb_ref)
            dcbw_ref[...] = jnp.zeros_like(dcbw_ref)
            dcabuf[t:t + halo, :] = jnp.zeros((halo, cw), F32)
            dcbbuf[t:t + 8, :] = jnp.zeros((8, cw), F32)

        keep = (r > 0).astype(F32)
        a1 = p_ref[:, 0:512]
        s2 = _sig(p_ref[:, 512:1024])
        abuf[0:halo, :] = (ph_ref[:, 0:512] * _sig(ph_ref[:, 512:1024])) * keep
        abuf[halo:halo + t, :] = a1 * s2
        ca_v = ca_ref[...]
        xh, rstd, ln = _layer_norm_rows(ca_v, lg_ref[...], lb_ref[...])
        s_ln = _sig(ln)
        ag = p_ref[:, 1024:1536]
        sg = _sig(ag)
        dact = da_ref[...]
        dsa = dact * (ag * sg)
        dg_ref[:, 1024:1536] = (dact * (ln * s_ln) * _dsilu(ag, sg)).astype(BF16)
        dln = dsa * _dsilu(ln, s_ln)
        dlg_ref[...] += jnp.sum(dln * xh, axis=0, keepdims=True)
        dlb_ref[...] += jnp.sum(dln, axis=0, keepdims=True)
        dca = _layer_norm_bwd_rows(dln * lg_ref[...], xh, rstd)
        dcab_ref[...] += jnp.sum(dca, axis=0, keepdims=True)
        dcabuf[0:t, :] = dca
        d_a = jnp.zeros((t, cw), F32)
        for k in range(CONV_K):
            off = (CONV_K - 1) - k
            d_a = d_a + caw_ref[k:k + 1, :] * dcabuf[off:off + t, :]
            src = halo - (CONV_K - 1) + k
            dcaw_ref[k:k + 1, :] += jnp.sum(dca * abuf[src:src + t, :], axis=0, keepdims=True)
        dg_ref[:, 0:512] = (d_a * s2).astype(BF16)
        dg_ref[:, 512:1024] = (d_a * a1 * s2 * (1.0 - s2)).astype(BF16)
        dcabuf[t:t + halo, :] = dcabuf[0:halo, :]

        xb = p_ref[:, 1536:2048]
        gb = p_ref[:, 2048:2560]
        gc = p_ref[:, 2560:3072]
        bg = p_ref[:, 3072:3584]
        zbuf[0:8, :] = (ph_ref[halo - 8:halo, 2560:3072] * ph_ref[halo - 8:halo, 1536:2048]) * keep
        zbuf[8:8 + t, :] = gc * xb
        sbg = _sig(bg)
        cbv = cb_ref[...]
        dactb = db_ref[...]
        dyb = dactb * (bg * sbg)
        dg_ref[:, 3072:3584] = (dactb * (gb * cbv) * _dsilu(bg, sbg)).astype(BF16)
        dg_ref[:, 2048:2560] = (dyb * cbv).astype(BF16)
        dcb = dyb * gb
        dcbbuf[0:t, :] = dcb
        dz = jnp.zeros((t, cw), F32)
        for k in range(SC_K):
            off = (SC_K - 1) - k
            dz = dz + cbw_ref[k:k + 1, :] * dcbbuf[off:off + t, :]
            src = 8 - (SC_K - 1) + k
            dcbw_ref[k:k + 1, :] += jnp.sum(dcb * zbuf[src:src + t, :], axis=0, keepdims=True)
        dg_ref[:, 2560:3072] = (dz * xb).astype(BF16)
        dg_ref[:, 1536:2048] = (dz * gc).astype(BF16)
        dcbbuf[t:t + 8, :] = dcbbuf[0:8, :]

    rev = lambda i: (nb - 1 - i, 0)
    outs = pl.pallas_call(
        body, name="mix_ab_bwd", grid=(nb,),
        in_specs=[pl.BlockSpec((t, W_AB), rev),
                  pl.BlockSpec((halo, W_AB), lambda i: (jnp.maximum((nb - 1 - i) * per - 1, 0), 0)),
                  pl.BlockSpec((t, cw), rev), pl.BlockSpec((t, cw), rev), pl.BlockSpec((t, cw), rev),
                  pl.BlockSpec((t, cw), rev),
                  _full_spec((CONV_K, cw)), _full_spec((1, cw)), _full_spec((1, cw)), _full_spec((SC_K, cw)),
                  pl.BlockSpec(memory_space=pl.ANY)],
        out_specs=[pl.BlockSpec((t, W_AB), rev), _full_spec((CONV_HALO, cw)), _full_spec((1, cw)), _full_spec((1, cw)),
                   _full_spec((1, cw)), _full_spec((8, cw))],
        out_shape=[jax.ShapeDtypeStruct(dproj.shape, BF16), jax.ShapeDtypeStruct((CONV_HALO, cw), F32),
                   jax.ShapeDtypeStruct((1, cw), F32), jax.ShapeDtypeStruct((1, cw), F32),
                   jax.ShapeDtypeStruct((1, cw), F32), jax.ShapeDtypeStruct((8, cw), F32)],
        scratch_shapes=[pltpu.VMEM((halo + t, cw), F32), pltpu.VMEM((t + halo, cw), F32),
                        pltpu.VMEM((8 + t, cw), F32), pltpu.VMEM((t + 8, cw), F32)],
        input_output_aliases={10: 0},
        compiler_params=_cparams(1),
    )(proj, proj, ca, cb, dact_a, dact_b, conv_a_w, ln_a_g, ln_a_b, conv_b_w, dproj)
    return outs


def _lane_is_nope():
    return lax.broadcasted_iota(jnp.int32, (1, HEAD_PAD), 1) < QK_NOPE


def _attn_prep_fwd(proj, q_g, kv_g, wq12, wkv, tcos, tsin):
    s = proj.shape[0]
    t = min(ROW_BLOCK, s)
    c0 = (W_AB + W_G5) // W_C1

    def body(p_ref, qg_ref, kg_ref, wq_ref, wkv_ref, cos_ref, sin_ref, qf_ref, kf_ref, v_ref):
        ql = p_ref[:, 0:384]
        qn = (ql * lax.rsqrt(jnp.mean(jnp.square(ql), axis=-1, keepdims=True) + RMS_EPS) * qg_ref[...]).astype(BF16)
        q12 = _dot(qn, wq_ref[...])
        kvl = p_ref[:, 384:640]
        kvn = (kvl * lax.rsqrt(jnp.mean(jnp.square(kvl), axis=-1, keepdims=True) + RMS_EPS) * kg_ref[...]).astype(BF16)
        kv12 = _dot(kvn, wkv_ref[...])
        tcos_v = cos_ref[...]
        tsin_v = sin_ref[...]
        tq1 = jnp.where(_lane_is_nope(), 1.0, tcos_v)
        kpe = p_ref[:, 640:768] * tcos_v + p_ref[:, 768:896] * tsin_v
        for h in range(HEADS):
            lo = h * HEAD_PAD
            qf_ref[h] = (q12[:, lo:lo + HEAD_PAD] * tq1 + q12[:, 1024 + lo:1024 + lo + HEAD_PAD] * tsin_v).astype(BF16)
            kf_ref[h] = (kv12[:, lo:lo + HEAD_PAD] + kpe).astype(BF16)
        v_ref[...] = kv12[:, 1024:1536].astype(BF16)

    return pl.pallas_call(
        body, name="attn_prep_fwd", grid=(s // t,),
        in_specs=[pl.BlockSpec((t, W_C1), lambda i: (i, c0)), _full_spec((1, 384)), _full_spec((1, 256)),
                  _full_spec((384, 2048)), _full_spec((256, 1536)), _row_spec(t, HEAD_PAD), _row_spec(t, HEAD_PAD)],
        out_specs=[pl.BlockSpec((HEADS, t, HEAD_PAD), lambda i: (0, i, 0)),
                   pl.BlockSpec((HEADS, t, HEAD_PAD), lambda i: (0, i, 0)), _row_spec(t, 512)],
        out_shape=[jax.ShapeDtypeStruct((HEADS, s, HEAD_PAD), BF16), jax.ShapeDtypeStruct((HEADS, s, HEAD_PAD), BF16),
                   jax.ShapeDtypeStruct((s, 512), BF16)],
        compiler_params=_cparams(1),
    )(proj, q_g, kv_g, wq12, wkv, tcos, tsin)


def _attn_prep_bwd(proj, dqf, dkf, dv, q_g, kv_g, wq12, wkv, tcos, tsin, dproj):
    s = proj.shape[0]
    t = min(ROW_BLOCK, s)
    c0 = (W_AB + W_G5) // W_C1

    def body(p_ref, dqf_ref, dkf_ref, dv_ref, qg_ref, kg_ref, wq_ref, wkv_ref, cos_ref, sin_ref, dproj_any,
             dg_ref, qn_ref, kvn_ref, dq12_ref, dkv12_ref, dqg_ref, dkg_ref):
        del dproj_any

        @pl.when(pl.program_id(0) == 0)
        def _():
            dqg_ref[...] = jnp.zeros_like(dqg_ref)
            dkg_ref[...] = jnp.zeros_like(dkg_ref)

        tcos_v = cos_ref[...]
        tsin_v = sin_ref[...]
        tq1 = jnp.where(_lane_is_nope(), 1.0, tcos_v)
        dkpe = jnp.zeros((t, HEAD_PAD), F32)
        for h in range(HEADS):
            lo = h * HEAD_PAD
            dq = dqf_ref[h]
            dq12_ref[:, lo:lo + HEAD_PAD] = (dq * tq1).astype(BF16)
            dq12_ref[:, 1024 + lo:1024 + lo + HEAD_PAD] = (dq * tsin_v).astype(BF16)
            dk = dkf_ref[h]
            dkv12_ref[:, lo:lo + HEAD_PAD] = dk.astype(BF16)
            dkpe = dkpe + dk
        dkv12_ref[:, 1024:1536] = dv_ref[...].astype(BF16)
        dg_ref[:, 640:768] = (dkpe * tcos_v).astype(BF16)
        dg_ref[:, 768:896] = (dkpe * tsin_v).astype(BF16)

        def rms_bwd(xl, g, dn):
            rr = lax.rsqrt(jnp.mean(jnp.square(xl), axis=-1, keepdims=True) + RMS_EPS)
            xn = xl * rr
            tt = dn * g
            return rr * (tt - xn * jnp.mean(tt * xn, axis=-1, keepdims=True)), jnp.sum(dn * xn, axis=0, keepdims=True), xn

        ql = p_ref[:, 0:384]
        dqn = _dot_nt(dq12_ref[...], wq_ref[...])
        dql, dqg, qxn = rms_bwd(ql, qg_ref[...], dqn)
        dg_ref[:, 0:384] = dql.astype(BF16)
        dqg_ref[...] += dqg
        qn_ref[...] = (qxn * qg_ref[...]).astype(BF16)
        kvl = p_ref[:, 384:640]
        dkvn = _dot_nt(dkv12_ref[...], wkv_ref[...])
        dkvl, dkg, kxn = rms_bwd(kvl, kg_ref[...], dkvn)
        dg_ref[:, 384:640] = dkvl.astype(BF16)
        dkg_ref[...] += dkg
        kvn_ref[...] = (kxn * kg_ref[...]).astype(BF16)

    return pl.pallas_call(
        body, name="attn_prep_bwd", grid=(s // t,),
        in_specs=[pl.BlockSpec((t, W_C1), lambda i: (i, c0)),
                  pl.BlockSpec((HEADS, t, HEAD_PAD), lambda i: (0, i, 0)),
                  pl.BlockSpec((HEADS, t, HEAD_PAD), lambda i: (0, i, 0)), _row_spec(t, 512),
                  _full_spec((1, 384)), _full_spec((1, 256)), _full_spec((384, 2048)), _full_spec((256, 1536)),
                  _row_spec(t, HEAD_PAD), _row_spec(t, HEAD_PAD), pl.BlockSpec(memory_space=pl.ANY)],
        out_specs=[pl.BlockSpec((t, W_C1), lambda i: (i, c0)), _row_spec(t, 384), _row_spec(t, 256),
                   _row_spec(t, 2048), _row_spec(t, 1536), _full_spec((1, 384)), _full_spec((1, 256))],
        out_shape=[jax.ShapeDtypeStruct(dproj.shape, BF16), jax.ShapeDtypeStruct((s, 384), BF16),
                   jax.ShapeDtypeStruct((s, 256), BF16), jax.ShapeDtypeStruct((s, 2048), BF16),
                   jax.ShapeDtypeStruct((s, 1536), BF16), jax.ShapeDtypeStruct((1, 384), F32),
                   jax.ShapeDtypeStruct((1, 256), F32)],
        input_output_aliases={10: 0},
        compiler_params=_cparams(1),
    )(proj, dqf, dkf, dv, q_g, kv_g, wq12, wkv, tcos, tsin, dproj)


def _causal_keep(t):
    return lax.broadcasted_iota(jnp.int32, (t, t), 0) >= lax.broadcasted_iota(jnp.int32, (t, t), 1)


def _half_select(e):
    lane = lax.broadcasted_iota(jnp.int32, (1, HEAD_PAD), 1)
    return (lane < V_HEAD) if e == 0 else (lane >= V_HEAD)


def _attention_fwd(qf, kf, v):
    _, s, _ = qf.shape
    t = min(ATT_BLOCK, s)
    nq = s // t

    def body(q_ref, k_ref, v_ref, o_ref, lse_ref, m_scr, l_scr, acc):
        qi = pl.program_id(0)
        kj = pl.program_id(1)

        @pl.when(kj == 0)
        def _():
            m_scr[...] = jnp.full(m_scr.shape, NEG_BIG, F32)
            l_scr[...] = jnp.zeros_like(l_scr)
            acc[...] = jnp.zeros_like(acc)

        def step(diag):
            for h in range(HEADS):
                pair, e = divmod(h, 2)
                sel = _half_select(e)
                sc = _dot_nt(q_ref[h], k_ref[h]) * ATT_SCALE
                if diag:
                    sc = jnp.where(_causal_keep(t), sc, NEG_BIG)
                m_prev = m_scr[h]
                m_new = jnp.maximum(m_prev, jnp.max(sc, axis=1, keepdims=True))
                p = jnp.exp(sc - m_new)
                alpha = jnp.exp(m_prev - m_new)
                l_scr[h] = alpha * l_scr[h] + jnp.sum(p, axis=1, keepdims=True)
                m_scr[h] = m_new
                lo = pair * HEAD_PAD
                v2 = v_ref[:, lo:lo + HEAD_PAD]
                pv = _dot(p.astype(BF16), jnp.where(sel, v2, jnp.zeros_like(v2)))
                acc[:, lo:lo + HEAD_PAD] = acc[:, lo:lo + HEAD_PAD] * jnp.where(sel, alpha, 1.0) + pv

        @pl.when(kj < qi)
        def _():
            step(False)

        @pl.when(kj == qi)
        def _():
            step(True)
            for pair in range(HEADS // 2):
                lo = pair * HEAD_PAD
                inv = jnp.where(_half_select(0), 1.0 / l_scr[2 * pair], 1.0 / l_scr[2 * pair + 1])
                o_ref[:, lo:lo + HEAD_PAD] = acc[:, lo:lo + HEAD_PAD] * inv
            for h in range(HEADS):
                lse_ref[h] = m_scr[h] + jnp.log(l_scr[h])

    return pl.pallas_call(
        body, name="attention_fwd", grid=(nq, nq),
        in_specs=[pl.BlockSpec((HEADS, t, HEAD_PAD), lambda i, j: (0, i, 0)),
                  pl.BlockSpec((HEADS, t, HEAD_PAD), lambda i, j: (0, jnp.minimum(i, j), 0)),
                  pl.BlockSpec((t, 512), lambda i, j: (jnp.minimum(i, j), 0))],
        out_specs=[pl.BlockSpec((t, 512), lambda i, j: (i, 0)), pl.BlockSpec((HEADS, t, 1), lambda i, j: (0, i, 0))],
        out_shape=[jax.ShapeDtypeStruct((s, 512), F32), jax.ShapeDtypeStruct((HEADS, s, 1), F32)],
        scratch_shapes=[pltpu.VMEM((HEADS, t, 1), F32), pltpu.VMEM((HEADS, t, 1), F32), pltpu.VMEM((t, 512), F32)],
        compiler_params=_cparams(2),
    )(qf, kf, v)


def _attention_bwd(qf, kf, v, o, do, lse):
    _, s, _ = qf.shape
    t = min(ATT_BLOCK, s)
    nq = s // t

    def body(q_ref, k_ref, v_ref, o_ref, do_ref, lse_ref, dq_ref, dk_ref, dv_ref, dk_acc, dv_acc):
        kj = pl.program_id(0)
        qi = pl.program_id(1)

        @pl.when((kj == 0) & (qi == 0))
        def _():
            dq_ref[...] = jnp.zeros_like(dq_ref)

        @pl.when(qi == kj)
        def _():
            dk_acc[...] = jnp.zeros_like(dk_acc)
            dv_acc[...] = jnp.zeros_like(dv_acc)

        def step(diag):
            rows = pl.ds(pl.multiple_of(qi * t, t), t)
            for h in range(HEADS):
                pair, e = divmod(h, 2)
                sel = _half_select(e)
                lo = pair * HEAD_PAD
                q = q_ref[h]
                k = k_ref[h]
                sc = _dot_nt(q, k) * ATT_SCALE
                if diag:
                    sc = jnp.where(_causal_keep(t), sc, NEG_BIG)
                p = jnp.exp(sc - lse_ref[h])
                do_pair = do_ref[:, lo:lo + HEAD_PAD]
                do_e = jnp.where(sel, do_pair, 0.0)
                do_b = do_e.astype(BF16)
                v2 = v_ref[:, lo:lo + HEAD_PAD]
                dv_acc[:, lo:lo + HEAD_PAD] += _dot_tn(p.astype(BF16), do_b)
                dp = _dot_nt(do_b, jnp.where(sel, v2, jnp.zeros_like(v2)))
                delta = jnp.sum(do_e * o_ref[:, lo:lo + HEAD_PAD], axis=1, keepdims=True)
                ds = (p * (dp - delta) * ATT_SCALE).astype(BF16)
                dq_ref[h, rows, :] += _dot(ds, k)
                dk_acc[h] += _dot_tn(ds, q)

        @pl.when(qi > kj)
        def _():
            step(False)

        @pl.when(qi == kj)
        def _():
            step(True)

        @pl.when(qi == nq - 1)
        def _():
            dk_ref[...] = dk_acc[...]
            dv_ref[...] = dv_acc[...]

    qmap = lambda j, i: (0, jnp.maximum(i, j), 0)
    return pl.pallas_call(
        body, name="attention_bwd", grid=(nq, nq),
        in_specs=[pl.BlockSpec((HEADS, t, HEAD_PAD), qmap),
                  pl.BlockSpec((HEADS, t, HEAD_PAD), lambda j, i: (0, j, 0)),
                  pl.BlockSpec((t, 512), lambda j, i: (j, 0)),
                  pl.BlockSpec((t, 512), lambda j, i: (jnp.maximum(i, j), 0)),
                  pl.BlockSpec((t, 512), lambda j, i: (jnp.maximum(i, j), 0)),
                  pl.BlockSpec((HEADS, t, 1), qmap)],
        out_specs=[_full_spec((HEADS, s, HEAD_PAD)),
                   pl.BlockSpec((HEADS, t, HEAD_PAD), lambda j, i: (0, j, 0)),
                   pl.BlockSpec((t, 512), lambda j, i: (j, 0))],
        out_shape=[jax.ShapeDtypeStruct((HEADS, s, HEAD_PAD), F32), jax.ShapeDtypeStruct((HEADS, s, HEAD_PAD), F32),
                   jax.ShapeDtypeStruct((s, 512), F32)],
        scratch_shapes=[pltpu.VMEM((HEADS, t, HEAD_PAD), F32), pltpu.VMEM((t, 512), F32)],
        compiler_params=_cparams(2),
    )(qf, kf, v, o, do, lse)


def _merge_fwd(act_a, act_b, o, proj, x, gate, w_a, w_b, w_c, w_o, ln_g, ln_b, alpha):
    s, d = x.shape
    t = min(ROW_BLOCK, s)

    def body(aa_ref, ab_ref, o_ref, p_ref, x_ref, gate_ref, wa_ref, wb_ref, wc_ref, wo_ref, lg_ref, lb_ref,
             xn_ref, xh_ref, rstd_ref):
        cg = p_ref[:, 0:512]
        act_c = (o_ref[...] * (cg * _sig(cg))).astype(BF16)
        m = _sig(p_ref[:, 512:1536]) * _dot(aa_ref[...], wa_ref[...])
        m = m + _sig(p_ref[:, 1536:2560]) * _dot(ab_ref[...], wb_ref[...])
        m = m + _sig(p_ref[:, 2560:3584]) * _dot(act_c, wc_ref[...])
        out = _dot(m.astype(BF16), wo_ref[...])
        z = alpha * x_ref[...] + gate_ref[...] * out
        xh, rstd, y = _layer_norm_rows(z, lg_ref[...], lb_ref[...])
        xn_ref[...] = y
        xh_ref[...] = xh
        rstd_ref[...] = rstd

    return pl.pallas_call(
        body, name="merge_fwd", grid=(s // t,),
        in_specs=[_row_spec(t, 512), _row_spec(t, 512), _row_spec(t, 512),
                  pl.BlockSpec((t, W_G5), lambda i: (i, 1)), _row_spec(t, d), _full_spec((1, d)),
                  _full_spec((512, d)), _full_spec((512, d)), _full_spec((512, d)), _full_spec((d, d)),
                  _full_spec((1, d)), _full_spec((1, d))],
        out_specs=[_row_spec(t, d), _row_spec(t, d), _row_spec(t, 1)],
        out_shape=[jax.ShapeDtypeStruct((s, d), F32), jax.ShapeDtypeStruct((s, d), F32), jax.ShapeDtypeStruct((s, 1), F32)],
        compiler_params=_cparams(1),
    )(act_a, act_b, o, proj, x, gate, w_a, w_b, w_c, w_o, ln_g, ln_b)


def _merge_bwd(dy, xh, rstd, act_a, act_b, o, proj, gate, w_a, w_b, w_c, w_o, ln_g, alpha):
    s, d = dy.shape
    t = min(ROW_BLOCK, s)

    def body(dy_ref, xh_ref, rstd_ref, aa_ref, ab_ref, o_ref, p_ref, gate_ref, wa_ref, wb_ref, wc_ref, wo_ref, lg_ref,
             dxr_ref, dacta_ref, dactb_ref, do_ref, dg_ref, m_ref, dout_ref, dya_ref, dyb_ref, dyc_ref, actc_ref,
             dlg_ref, dlb_ref, dgate_ref):
        @pl.when(pl.program_id(0) == 0)
        def _():
            dlg_ref[...] = jnp.zeros_like(dlg_ref)
            dlb_ref[...] = jnp.zeros_like(dlb_ref)
            dgate_ref[...] = jnp.zeros_like(dgate_ref)

        dyv = dy_ref[...]
        xhv = xh_ref[...]
        dlg_ref[...] += jnp.sum(dyv * xhv, axis=0, keepdims=True)
        dlb_ref[...] += jnp.sum(dyv, axis=0, keepdims=True)
        dz = _layer_norm_bwd_rows(dyv * lg_ref[...], xhv, rstd_ref[...])
        dxr_ref[...] = alpha * dz

        cg = p_ref[:, 0:512]
        scg = _sig(cg)
        silu_cg = cg * scg
        ov = o_ref[...]
        act_c = (ov * silu_cg).astype(BF16)
        actc_ref[...] = act_c
        ya = _dot(aa_ref[...], wa_ref[...])
        yb = _dot(ab_ref[...], wb_ref[...])
        yc = _dot(act_c, wc_ref[...])
        ga = _sig(p_ref[:, 512:1536])
        gb = _sig(p_ref[:, 1536:2560])
        gc = _sig(p_ref[:, 2560:3584])
        mb = (ga * ya + gb * yb + gc * yc).astype(BF16)
        m_ref[...] = mb
        out = _dot(mb, wo_ref[...])
        dgate_ref[...] += jnp.sum(dz * out, axis=0, keepdims=True)
        dout = (gate_ref[...] * dz).astype(BF16)
        dout_ref[...] = dout
        dm = _dot_nt(dout, wo_ref[...])

        dya = (dm * ga).astype(BF16)
        dya_ref[...] = dya
        dg_ref[:, 512:1536] = (dm * ya * ga * (1.0 - ga)).astype(BF16)
        dacta_ref[...] = _dot_nt(dya, wa_ref[...])
        dyb = (dm * gb).astype(BF16)
        dyb_ref[...] = dyb
        dg_ref[:, 1536:2560] = (dm * yb * gb * (1.0 - gb)).astype(BF16)
        dactb_ref[...] = _dot_nt(dyb, wb_ref[...])
        dyc = (dm * gc).astype(BF16)
        dyc_ref[...] = dyc
        dg_ref[:, 2560:3584] = (dm * yc * gc * (1.0 - gc)).astype(BF16)
        dactc = _dot_nt(dyc, wc_ref[...])
        do_ref[...] = dactc * silu_cg
        dg_ref[:, 0:512] = (dactc * ov * _dsilu(cg, scg)).astype(BF16)

    return pl.pallas_call(
        body, name="merge_bwd", grid=(s // t,),
        in_specs=[_row_spec(t, d), _row_spec(t, d), _row_spec(t, 1), _row_spec(t, 512), _row_spec(t, 512),
                  _row_spec(t, 512), pl.BlockSpec((t, W_G5), lambda i: (i, 1)), _full_spec((1, d)),
                  _full_spec((512, d)), _full_spec((512, d)), _full_spec((512, d)), _full_spec((d, d)),
                  _full_spec((1, d))],
        out_specs=[_row_spec(t, d), _row_spec(t, 512), _row_spec(t, 512), _row_spec(t, 512),
                   pl.BlockSpec((t, W_G5), lambda i: (i, 1)),
                   _row_spec(t, d), _row_spec(t, d), _row_spec(t, d), _row_spec(t, d), _row_spec(t, d),
                   _row_spec(t, 512), _full_spec((1, d)), _full_spec((1, d)), _full_spec((1, d))],
        out_shape=[jax.ShapeDtypeStruct((s, d), F32), jax.ShapeDtypeStruct((s, 512), F32),
                   jax.ShapeDtypeStruct((s, 512), F32), jax.ShapeDtypeStruct((s, 512), F32),
                   jax.ShapeDtypeStruct((s, W_PACK), BF16),
                   jax.ShapeDtypeStruct((s, d), BF16), jax.ShapeDtypeStruct((s, d), BF16),
                   jax.ShapeDtypeStruct((s, d), BF16), jax.ShapeDtypeStruct((s, d), BF16),
                   jax.ShapeDtypeStruct((s, d), BF16), jax.ShapeDtypeStruct((s, 512), BF16),
                   jax.ShapeDtypeStruct((1, d), F32), jax.ShapeDtypeStruct((1, d), F32),
                   jax.ShapeDtypeStruct((1, d), F32)],
        compiler_params=_cparams(1),
    )(dy, xh, rstd, act_a, act_b, o, proj, gate, w_a, w_b, w_c, w_o, ln_g)


def _proj_bwd_input(dproj, w_pack, x, scale, dxres):
    s, d = x.shape
    t = min(1024, s)
    nk = W_PACK // COL_BLK

    def body(dg_ref, w_ref, x_ref, sc_ref, dxr_ref, dx_ref, dsc_ref, dsh_ref, acc):
        i = pl.program_id(0)
        k = pl.program_id(1)

        @pl.when((i == 0) & (k == 0))
        def _():
            dsc_ref[...] = jnp.zeros_like(dsc_ref)
            dsh_ref[...] = jnp.zeros_like(dsh_ref)

        @pl.when(k == 0)
        def _():
            acc[...] = jnp.zeros_like(acc)

        acc[...] += _dot_nt(dg_ref[...], w_ref[...])

        @pl.when(k == nk - 1)
        def _():
            du = acc[...]
            dx_ref[...] = du * (1.0 + sc_ref[...]) + dxr_ref[...]
            dsc_ref[...] += jnp.sum(du * x_ref[...], axis=0, keepdims=True)
            dsh_ref[...] += jnp.sum(du, axis=0, keepdims=True)

    return pl.pallas_call(
        body, name="proj_bwd_input", grid=(s // t, nk),
        in_specs=[pl.BlockSpec((t, COL_BLK), lambda i, k: (i, k)), pl.BlockSpec((d, COL_BLK), lambda i, k: (0, k)),
                  pl.BlockSpec((t, d), lambda i, k: (i, 0)), _full_spec((1, d)), pl.BlockSpec((t, d), lambda i, k: (i, 0))],
        out_specs=[pl.BlockSpec((t, d), lambda i, k: (i, 0)), _full_spec((1, d)), _full_spec((1, d))],
        out_shape=[jax.ShapeDtypeStruct((s, d), F32), jax.ShapeDtypeStruct((1, d), F32), jax.ShapeDtypeStruct((1, d), F32)],
        scratch_shapes=[pltpu.VMEM((t, d), F32)],
        compiler_params=_cparams(2),
    )(dproj, w_pack, x, scale, dxres)


def _loss_head(y, target):
    s, d = y.shape
    t = min(512, s)

    def body(y_ref, t_ref, dy_ref, loss_ref):
        @pl.when(pl.program_id(0) == 0)
        def _():
            loss_ref[...] = jnp.zeros_like(loss_ref)

        err = y_ref[...] - t_ref[...]
        dy_ref[...] = err / d
        part = 0.5 * jnp.sum(jnp.mean(jnp.square(err), axis=-1, keepdims=True), axis=0, keepdims=True)
        loss_ref[...] += jnp.broadcast_to(part, loss_ref.shape)

    return pl.pallas_call(
        body, name="loss_head", grid=(s // t,),
        in_specs=[_row_spec(t, d), _row_spec(t, d)],
        out_specs=[_row_spec(t, d), _full_spec((1, 128))],
        out_shape=[jax.ShapeDtypeStruct((s, d), F32), jax.ShapeDtypeStruct((1, 128), F32)],
        compiler_params=_cparams(1),
    )(y, target)


def _adamw(g, w, m, v, name):
    r, n = g.shape
    rb = r
    if r * n * 4 > 2 ** 21:
        for cand in (512, 256, 128, 64, 32, 16, 8):
            if r % cand == 0 and cand * n * 4 <= 2 ** 21:
                rb = cand
                break

    def body(g_ref, w_ref, m_ref, v_ref, d_ref, mo_ref, vo_ref):
        gv = g_ref[...]
        mn = ADAM_B1 * m_ref[...] + (1.0 - ADAM_B1) * gv
        vn = ADAM_B2 * v_ref[...] + (1.0 - ADAM_B2) * jnp.square(gv)
        m_hat = mn / (1.0 - ADAM_B1 ** ADAM_STEP)
        v_hat = vn / (1.0 - ADAM_B2 ** ADAM_STEP)
        d_ref[...] = -ADAM_LR * (m_hat / (jnp.sqrt(v_hat) + ADAM_EPS) + ADAM_WD * w_ref[...])
        mo_ref[...] = mn
        vo_ref[...] = vn

    spec = pl.BlockSpec((rb, n), lambda i: (i, 0))
    return pl.pallas_call(
        body, name=name, grid=(r // rb,),
        in_specs=[spec] * 4, out_specs=[spec] * 3,
        out_shape=[jax.ShapeDtypeStruct((r, n), F32)] * 3, compiler_params=_cparams(1),
    )(g, w, m, v)


SMALL = ("b_ada", "conv_a_b", "ln_a_g", "ln_a_b", "q_norm_g", "kv_norm_g", "ln_g", "ln_b", "conv_a_w", "conv_b_w")
ROWS_256 = (("w_a_out", 512), ("w_b_out", 512), ("w_c_out", 512), ("w_ukv", 256))


def _rot_cols(w):
    return jnp.concatenate([-w[..., 16:], w[..., :16]], axis=-1)


def _unrot_cols(g):
    return jnp.concatenate([g[..., 16:], -g[..., :16]], axis=-1)


def _pack_w_in(w):
    z = lambda n: jnp.zeros(w.shape[:-1] + (n,), w.dtype)
    wk = w[..., 4224:4256]
    return jnp.concatenate([w[..., 0:3584], w[..., 4256:7840], w[..., 3584:4224],
                            z(64), wk, z(32), z(64), _rot_cols(wk), z(32)], axis=-1)


def _unpack_w_in_grad(g):
    c1 = W_AB + W_G5
    gk = g[..., c1 + 640 + 64:c1 + 640 + 96] + _unrot_cols(g[..., c1 + 768 + 64:c1 + 768 + 96])
    return jnp.concatenate([g[..., 0:3584], g[..., c1:c1 + 640], gk, g[..., W_AB:W_AB + W_G5]], axis=-1)


def _pack_w_uq(w):
    lead = w.shape[:-1]
    wh = w.reshape(lead + (HEADS, QK_NOPE + QK_ROPE))
    nope, rope = wh[..., :QK_NOPE], wh[..., QK_NOPE:]
    z32 = jnp.zeros_like(rope)
    q1 = jnp.concatenate([nope, rope, z32], axis=-1).reshape(lead + (HEADS * HEAD_PAD,))
    q2 = jnp.concatenate([jnp.zeros_like(nope), _rot_cols(rope), z32], axis=-1).reshape(lead + (HEADS * HEAD_PAD,))
    return jnp.concatenate([q1, q2], axis=-1)


def _unpack_w_uq_grad(g):
    lead = g.shape[:-1]
    g1 = g[..., :1024].reshape(lead + (HEADS, HEAD_PAD))
    g2 = g[..., 1024:].reshape(lead + (HEADS, HEAD_PAD))
    rope = g1[..., QK_NOPE:QK_NOPE + QK_ROPE] + _unrot_cols(g2[..., QK_NOPE:QK_NOPE + QK_ROPE])
    return jnp.concatenate([g1[..., :QK_NOPE], rope], axis=-1).reshape(lead + (HEADS * (QK_NOPE + QK_ROPE),))


def _pack_w_ukv(w):
    lead = w.shape[:-1]
    wh = w.reshape(lead + (HEADS, QK_NOPE + V_HEAD))
    kn, vv = wh[..., :QK_NOPE], wh[..., QK_NOPE:]
    k1 = jnp.concatenate([kn, jnp.zeros_like(kn)], axis=-1).reshape(lead + (HEADS * HEAD_PAD,))
    return jnp.concatenate([k1, vv.reshape(lead + (HEADS * V_HEAD,))], axis=-1)


def _unpack_w_ukv_grad(g):
    lead = g.shape[:-1]
    gk = g[..., :1024].reshape(lead + (HEADS, HEAD_PAD))[..., :QK_NOPE]
    gv = g[..., 1024:].reshape(lead + (HEADS, V_HEAD))
    return jnp.concatenate([gk, gv], axis=-1).reshape(lead + (HEADS * (QK_NOPE + V_HEAD),))


def _join_cols(g4, layer):
    return jnp.concatenate([g4[j, layer] for j in range(4)], axis=-1)


def _split_cols(w):
    c4 = w.shape[-1] // 4
    return jnp.stack([w[:, j * c4:(j + 1) * c4] for j in range(4)])


def kernel(x, c, positions, w_ada, b_ada, w_in, conv_a_w, conv_a_b, ln_a_g, ln_a_b, w_a_out, conv_b_w, w_b_out, q_norm_g, kv_norm_g, w_uq, w_ukv, w_c_out, w_o, ln_g, ln_b, loss_target, m_w_ada, m_b_ada, m_w_in, m_conv_a_w, m_conv_a_b, m_ln_a_g, m_ln_a_b, m_w_a_out, m_conv_b_w, m_w_b_out, m_q_norm_g, m_kv_norm_g, m_w_uq, m_w_ukv, m_w_c_out, m_w_o, m_ln_g, m_ln_b, v_w_ada, v_b_ada, v_w_in, v_conv_a_w, v_conv_a_b, v_ln_a_g, v_ln_a_b, v_w_a_out, v_conv_b_w, v_w_b_out, v_q_norm_g, v_kv_norm_g, v_w_uq, v_w_ukv, v_w_c_out, v_w_o, v_ln_g, v_ln_b):
    weights = dict(w_ada=w_ada, b_ada=b_ada, w_in=w_in, conv_a_w=conv_a_w, conv_a_b=conv_a_b, ln_a_g=ln_a_g, ln_a_b=ln_a_b,
                   w_a_out=w_a_out, conv_b_w=conv_b_w, w_b_out=w_b_out, q_norm_g=q_norm_g, kv_norm_g=kv_norm_g, w_uq=w_uq,
                   w_ukv=w_ukv, w_c_out=w_c_out, w_o=w_o, ln_g=ln_g, ln_b=ln_b)
    mom_m = dict(w_ada=m_w_ada, b_ada=m_b_ada, w_in=m_w_in, conv_a_w=m_conv_a_w, conv_a_b=m_conv_a_b, ln_a_g=m_ln_a_g,
                 ln_a_b=m_ln_a_b, w_a_out=m_w_a_out, conv_b_w=m_conv_b_w, w_b_out=m_w_b_out, q_norm_g=m_q_norm_g,
                 kv_norm_g=m_kv_norm_g, w_uq=m_w_uq, w_ukv=m_w_ukv, w_c_out=m_w_c_out, w_o=m_w_o, ln_g=m_ln_g, ln_b=m_ln_b)
    mom_v = dict(w_ada=v_w_ada, b_ada=v_b_ada, w_in=v_w_in, conv_a_w=v_conv_a_w, conv_a_b=v_conv_a_b, ln_a_g=v_ln_a_g,
                 ln_a_b=v_ln_a_b, w_a_out=v_w_a_out, conv_b_w=v_conv_b_w, w_b_out=v_w_b_out, q_norm_g=v_q_norm_g,
                 kv_norm_g=v_kv_norm_g, w_uq=v_w_uq, w_ukv=v_w_ukv, w_c_out=v_w_c_out, w_o=v_w_o, ln_g=v_ln_g, ln_b=v_ln_b)
    order = ["w_ada", "b_ada", "w_in", "conv_a_w", "conv_a_b", "ln_a_g", "ln_a_b", "w_a_out", "conv_b_w", "w_b_out",
             "q_norm_g", "kv_norm_g", "w_uq", "w_ukv", "w_c_out", "w_o", "ln_g", "ln_b"]

    n_layers = w_in.shape[0]
    s, d = x.shape[1], x.shape[2]
    alpha = float((2 * n_layers) ** 0.25)
    ix, iy, ic = lax.axis_index("x"), lax.axis_index("y"), lax.axis_index("c")
    me = 4 * ix + 2 * iy + ic
    chip = 2 * ix + iy
    xs = x[0]

    def four(tree, cast):
        return [cast(tree["w_in"]), jnp.concatenate([cast(tree[n]) for n, _ in ROWS_256], axis=-2),
                cast(tree["w_uq"]), cast(tree["w_o"])]

    g_in, g_256, g_uq, g_wo = _gather_weights(four(weights, lambda a: a.astype(BF16)))
    w_pack, wq12, wkv, w_a, w_b, w_c, w_of = [], [], [], [], [], [], []
    for l in range(n_layers):
        w_pack.append(_pack_w_in(_join_cols(g_in, l)))
        wq12.append(_pack_w_uq(_join_cols(g_uq, l)))
        rows = _join_cols(g_256, l)
        w_a.append(rows[0:512])
        w_b.append(rows[512:1024])
        w_c.append(rows[1024:1536])
        wkv.append(_pack_w_ukv(rows[1536:1792]))
        w_of.append(jnp.concatenate([g_wo[j, l] for j in range(4)], axis=0))

    cw_a = conv_a_w.reshape(n_layers * CONV_K, 128)
    cw_b = conv_b_w.reshape(n_layers * SC_K, 128)
    c_rows = d // 128
    first_rows = c_rows + n_layers * (CONV_K + SC_K)
    first_pad = (-first_rows) % 8
    first = jnp.concatenate([c.reshape(c_rows, 128), cw_a, cw_b, jnp.zeros((first_pad, 128), F32)], axis=0)
    fr = first_rows + first_pad
    got = _all_gather_small(first, "gather_cond").reshape(8, fr, 128)
    c_all = got[:, :c_rows].reshape(8, d)
    conv_a_full = jnp.moveaxis(got[0::2, c_rows:c_rows + n_layers * CONV_K].reshape(4, n_layers, CONV_K, 128), 0, 2)
    conv_a_full = conv_a_full.reshape(n_layers, CONV_K, 512)
    b0 = c_rows + n_layers * CONV_K
    conv_b_full = jnp.moveaxis(got[0::2, b0:b0 + n_layers * SC_K].reshape(4, n_layers, SC_K, 128), 0, 2)
    conv_b_full = conv_b_full.reshape(n_layers, SC_K, 512)

    ada_sh, c_act = _ada_shard(c_all, w_ada.astype(BF16))
    ada_all = _all_gather_small(ada_sh.reshape(n_layers * 8, 768), "gather_ada").reshape(8, n_layers, 8, 768)
    ada_mine = lax.dynamic_index_in_dim(ada_all[0::2], me, axis=2, keepdims=False)
    ada = jnp.moveaxis(ada_mine, 0, 1).reshape(n_layers, 3 * d) + b_ada
    shift, scale, gate = ada[:, 0:d], ada[:, d:2 * d], ada[:, 2 * d:3 * d]

    tcos, tsin = _rope_tables(positions.reshape(s, 1))

    saved = []
    h = xs
    for l in range(n_layers):
        u = _modulate(h, scale[l:l + 1], shift[l:l + 1])
        proj = _matmul(u, w_pack[l], COL_BLK, "proj_fwd")
        act_a, act_b, ca, cb = _mix_ab_fwd(proj, conv_a_full[l], conv_a_b[l:l + 1], ln_a_g[l:l + 1], ln_a_b[l:l + 1],
                                           conv_b_full[l])
        qf, kf, vv = _attn_prep_fwd(proj, q_norm_g[l:l + 1], kv_norm_g[l:l + 1], wq12[l], wkv[l], tcos, tsin)
        o, lse = _attention_fwd(qf, kf, vv)
        h_next, xh, rstd = _merge_fwd(act_a, act_b, o, proj, h, gate[l:l + 1], w_a[l], w_b[l], w_c[l], w_of[l],
                                      ln_g[l:l + 1], ln_b[l:l + 1], alpha)
        saved.append(dict(x=h, u=u, proj=proj, act_a=act_a, act_b=act_b, ca=ca, cb=cb, qf=qf, kf=kf, v=vv, o=o, lse=lse,
                          xh=xh, rstd=rstd))
        h = h_next

    dy, loss_part = _loss_head(h, loss_target[0])
    loss = lax.psum(loss_part[0, 0], ("x", "y", "c"))

    c_idx = ic.reshape(1).astype(jnp.int32)
    reduced = None
    small_rows = [None] * n_layers
    for l in reversed(range(n_layers)):
        sv = saved[l]
        (dxres, dact_a, dact_b, do, dproj, m_bf, dout_bf, dya, dyb, dyc, act_c, d_ln_g, d_ln_b, d_gate) = _merge_bwd(
            dy, sv["xh"], sv["rstd"], sv["act_a"], sv["act_b"], sv["o"], sv["proj"], gate[l:l + 1], w_a[l], w_b[l],
            w_c[l], w_of[l], ln_g[l:l + 1], alpha)
        g_o = _matmul_tn(m_bf, dout_bf, d, "grad_w_o")
        g_a = _matmul_tn(sv["act_a"], dya, d, "grad_w_a_out")
        g_b = _matmul_tn(sv["act_b"], dyb, d, "grad_w_b_out")
        g_c = _matmul_tn(act_c, dyc, d, "grad_w_c_out")
        dqf, dkf, dvv = _attention_bwd(sv["qf"], sv["kf"], sv["v"], sv["o"], do, sv["lse"])
        dproj, qn, kvn, dq12, dkv12, d_qg, d_kg = _attn_prep_bwd(
            sv["proj"], dqf, dkf, dvv, q_norm_g[l:l + 1], kv_norm_g[l:l + 1], wq12[l], wkv[l], tcos, tsin, dproj)
        g_uq = _unpack_w_uq_grad(_matmul_tn(qn, dq12, 1024, "grad_w_uq"))
        g_kv = _unpack_w_ukv_grad(_matmul_tn(kvn, dkv12, 768, "grad_w_ukv"))
        dproj, d_caw, d_cab, d_lag, d_lab, d_cbw = _mix_ab_bwd(
            sv["proj"], sv["ca"], sv["cb"], dact_a, dact_b, conv_a_full[l], ln_a_g[l:l + 1], ln_a_b[l:l + 1],
            conv_b_full[l], dproj)
        g_in_l = _unpack_w_in_grad(_matmul_tn(sv["u"], dproj, COL_BLK, "grad_w_in"))
        dy, d_scale, d_shift = _proj_bwd_input(dproj, w_pack[l], sv["x"], scale[l:l + 1], dxres)
        small_rows[l] = jnp.concatenate(
            [d_shift[0], d_scale[0], d_gate[0], d_cab[0], d_lag[0], d_lab[0], d_qg[0], d_kg[0], d_ln_g[0], d_ln_b[0],
             d_caw[:CONV_K].reshape(-1), d_cbw[:SC_K].reshape(-1)])

        partial = [_split_cols(g_in_l).astype(BF16),
                   jnp.concatenate([_split_cols(g) for g in (g_a, g_b, g_c, g_kv)], axis=1).astype(BF16),
                   _split_cols(g_uq).astype(BF16), g_o.reshape(4, d // 4, d).astype(BF16)]
        landed = _swap_halves(partial)
        received = _scatter_to_owners(_add_halves(partial, landed, c_idx))
        reduced = _sum_chips(received, reduced, l, n_layers)
    grad_x = dy[None]
    r_in, r_256, r_uq, r_wo = _pair_halves(reduced)
    grads = {"w_in": r_in, "w_uq": r_uq, "w_o": r_wo}
    row0 = 0
    for n, rows in ROWS_256:
        grads[n] = r_256[:, row0:row0 + rows]
        row0 += rows

    small = jnp.stack(small_rows)
    p_small = small.shape[1]
    n_small = n_layers * p_small
    pad_small = (-n_small) % (8 * 128)
    small_flat = jnp.concatenate([small.reshape(-1), jnp.zeros((pad_small,), F32)]).reshape(-1, 128)
    sr = small_flat.shape[0]
    small_all = _all_gather_small(small_flat, "gather_small_grads").reshape(8, sr, 128)
    small_sum = _sum_devices(small_all).reshape(-1)[:n_small].reshape(n_layers, p_small)
    d_ada_all = small_all.reshape(8, -1)[:, :n_small].reshape(8, n_layers, p_small)[:, :, :3 * d]

    cuts = np.cumsum([0, 3 * d, 512, 512, 512, 384, 256, d, d, CONV_K * 512, SC_K * 512])
    pieces = [small_sum[:, cuts[k]:cuts[k + 1]] for k in range(10)]
    grads.update(zip(SMALL, pieces[:8]))
    ga_full = pieces[8].reshape(n_layers, CONV_K, 512)
    gb_full = pieces[9].reshape(n_layers, SC_K, 512)
    grads["conv_a_w"] = lax.dynamic_slice_in_dim(ga_full, chip * 128, 128, axis=2)
    grads["conv_b_w"] = lax.dynamic_slice_in_dim(gb_full, chip * 128, 128, axis=2)

    dada_sh = lax.dynamic_slice_in_dim(d_ada_all, chip * 768, 768, axis=2)
    dada16 = jnp.concatenate([jnp.moveaxis(dada_sh, 0, 1), jnp.zeros((n_layers, 8, 768), F32)], axis=1).astype(BF16)
    cact16 = jnp.concatenate([c_act, jnp.zeros_like(c_act)], axis=0).astype(BF16)
    grads["w_ada"] = _ada_weight_grad(cact16, dada16)

    def as2d(a):
        return a.reshape(-1, a.shape[-1])

    deltas, new_m, new_v = {}, {}, {}
    for n in ("w_in", "w_a_out", "w_b_out", "w_c_out", "w_uq", "w_ukv", "w_o", "w_ada"):
        dl, mn, vn = _adamw(as2d(grads[n]), as2d(weights[n]), as2d(mom_m[n]), as2d(mom_v[n]), "adamw_" + n)
        deltas[n], new_m[n], new_v[n] = dl.reshape(weights[n].shape), mn.reshape(weights[n].shape), vn.reshape(weights[n].shape)

    def pack_small(tree):
        flat = jnp.concatenate([tree[n].reshape(-1) for n in SMALL])
        pad = (-flat.shape[0]) % (8 * 128)
        return jnp.concatenate([flat, jnp.zeros((pad,), F32)]).reshape(-1, 128)

    dl_s, mn_s, vn_s = _adamw(pack_small(grads), pack_small(weights), pack_small(mom_m), pack_small(mom_v), "adamw_small")
    off = 0
    for n in SMALL:
        sz = int(np.prod(weights[n].shape))
        for tree, flat in ((deltas, dl_s), (new_m, mn_s), (new_v, vn_s)):
            tree[n] = flat.reshape(-1)[off:off + sz].reshape(weights[n].shape)
        off += sz

    return (loss, grad_x, *[grads[n] for n in order], *[deltas[n] for n in order], *[new_m[n] for n in order],
            *[new_v[n] for n in order])
```

```python
import functools

import numpy as np
import jax
import jax.numpy as jnp
from jax import lax
from jax.experimental import pallas as pl
from jax.experimental.pallas import tpu as pltpu

F32 = jnp.float32
BF16 = jnp.bfloat16
MESH = pl.DeviceIdType.MESH

HEADS = 8
QK_NOPE = 64
QK_ROPE = 32
V_HEAD = 64
CONV_K = 31
SC_K = 3
LN_EPS = 1e-5
RMS_EPS = 1e-6
ROPE_THETA = 10000.0
ATT_SCALE = (QK_NOPE + QK_ROPE) ** -0.5
ADAM_LR = 0.001
ADAM_B1 = 0.9
ADAM_B2 = 0.999
ADAM_EPS = 1e-08
ADAM_WD = 0.01
ADAM_STEP = 10

W_AB = 3584
W_G5 = 3584
W_C1 = 896
W_PACK = W_AB + W_G5 + W_C1
COL_BLK = 896
HEAD_PAD = 128
CONV_HALO = 32
NEG_BIG = -1e30

VMEM_LIMIT_MB = 56
ATT_BLOCK = 512
ROW_BLOCK = 256


def _cparams(n_grid, mb=VMEM_LIMIT_MB):
    return pltpu.CompilerParams(dimension_semantics=("arbitrary",) * n_grid, vmem_limit_bytes=mb * 2 ** 20)


def _sig(x):
    return 1.0 / (1.0 + jnp.exp(-x))


def _dsilu(x, s):
    return s * (1.0 + x * (1.0 - s))


def _dot(a, b):
    return jnp.dot(a, b, preferred_element_type=F32)


def _dot_nt(a, b):
    return lax.dot_general(a, b, (((1,), (1,)), ((), ())), preferred_element_type=F32)


def _dot_tn(a, b):
    return lax.dot_general(a, b, (((0,), (0,)), ((), ())), preferred_element_type=F32)


def _row_spec(t, w):
    return pl.BlockSpec((t, w), lambda i: (i, 0))


def _full_spec(shape):
    nd = len(shape)
    return pl.BlockSpec(shape, lambda *_: (0,) * nd)


def _my_place():
    x, y, c = lax.axis_index("x"), lax.axis_index("y"), lax.axis_index("c")
    chips = [(1 - x, y), (x, 1 - y), (1 - x, 1 - y)]
    return x, y, c, chips


def _all_gather_small(v, name):
    m_per, n = v.shape

    def body(x_ref, out_ref, send_sems, recv_sems, local_sem):
        x, y, c, chips = _my_place()
        me, sibling = (x, y, c), (x, y, 1 - c)

        def rows(px, py, pc):
            return out_ref.at[pl.ds((4 * px + 2 * py + pc) * m_per, m_per), :]

        def copy(k, block, to, src=None):
            return pltpu.make_async_remote_copy(
                src_ref=rows(*block) if src is None else src, dst_ref=rows(*block),
                send_sem=send_sems.at[k], recv_sem=recv_sems.at[k], device_id=to, device_id_type=MESH)

        mine = pltpu.make_async_copy(x_ref, rows(*me), local_sem)
        mine.start()
        first = [copy(0, me, sibling, src=x_ref)]
        first += [copy(1 + j, me, (*chip, c), src=x_ref) for j, chip in enumerate(chips)]
        for cp in first:
            cp.start()
        passed = [copy(4 + j, (*chip, c), sibling) for j, chip in enumerate(chips)]
        for j, chip in enumerate(chips):
            copy(1 + j, (*chip, c), me).wait_recv()
            passed[j].start()
        copy(0, sibling, me).wait_recv()
        for j, chip in enumerate(chips):
            copy(4 + j, (*chip, 1 - c), me).wait_recv()
        for cp in first + passed:
            cp.wait_send()
        mine.wait()

    return pl.pallas_call(
        body, name=name,
        out_shape=jax.ShapeDtypeStruct((8 * m_per, n), v.dtype),
        in_specs=[pl.BlockSpec(memory_space=pltpu.VMEM)],
        out_specs=pl.BlockSpec(memory_space=pltpu.VMEM),
        scratch_shapes=[pltpu.SemaphoreType.DMA((7,)), pltpu.SemaphoreType.DMA((7,)), pltpu.SemaphoreType.DMA],
        compiler_params=pltpu.CompilerParams(vmem_limit_bytes=VMEM_LIMIT_MB * 2 ** 20),
    )(v)


def _any_specs(n):
    return [pl.BlockSpec(memory_space=pl.ANY)] * n


def _row_half(ref, half_index, axis):
    half = ref.shape[axis] // 2
    idx = [slice(None)] * axis + [pl.ds(half_index * half, half)]
    return ref.at[tuple(idx)]


class _Exchange:
    def __init__(self, inputs, out_shapes, sems, start, finish, aliased=False):
        self.inputs, self.out_shapes, self.sems = list(inputs), list(out_shapes), list(sems)
        self.start, self.finish, self.aliased = start, finish, aliased


def _call(body, args, exchange=None, *, grid, in_specs, out_specs, out_shape, scratch_shapes=(),
          input_output_aliases=None, **kw):
    aliases = dict(input_output_aliases or {})
    if exchange is None:
        outs = pl.pallas_call(body, grid=grid, in_specs=list(in_specs), out_specs=list(out_specs),
                              out_shape=list(out_shape), scratch_shapes=list(scratch_shapes),
                              input_output_aliases=aliases, **kw)(*args)
        return list(outs), []
    n_in, n_out, n_scr = len(in_specs), len(out_specs), len(scratch_shapes)
    x_in, x_out = len(exchange.inputs), len(exchange.out_shapes)

    def wrapped(*refs):
        ins, xin = refs[:n_in], refs[n_in:n_in + x_in]
        o0 = n_in + x_in
        outs, xout = refs[o0:o0 + n_out], refs[o0 + n_out:o0 + n_out + x_out]
        s0 = o0 + n_out + x_out
        scr, xsem = refs[s0:s0 + n_scr], refs[s0 + n_scr:]
        ids = [pl.program_id(k) for k in range(len(grid))]
        first = functools.reduce(jnp.logical_and, [i == 0 for i in ids])
        last = functools.reduce(jnp.logical_and, [i == g - 1 for i, g in zip(ids, grid)])

        @pl.when(first)
        def _():
            exchange.start(xin, xout, xsem)

        body(*ins, *outs, *scr)

        @pl.when(last)
        def _():
            exchange.finish(xin, xout, xsem)

    if exchange.aliased:
        aliases.update({n_in + a: n_out + a for a in range(x_in)})
    outs = pl.pallas_call(
        wrapped, grid=grid, in_specs=list(in_specs) + _any_specs(x_in), out_specs=list(out_specs) + _any_specs(x_out),
        out_shape=list(out_shape) + exchange.out_shapes, scratch_shapes=list(scratch_shapes) + exchange.sems,
        input_output_aliases=aliases, **kw)(*args, *exchange.inputs)
    return list(outs[:n_out]), list(outs[n_out:])


def _run_exchange(exchange, name):
    x_in, x_out = len(exchange.inputs), len(exchange.out_shapes)

    def body(*refs):
        xin, xout, xsem = refs[:x_in], refs[x_in:x_in + x_out], refs[x_in + x_out:]
        exchange.start(xin, xout, xsem)
        exchange.finish(xin, xout, xsem)

    outs = pl.pallas_call(
        body, name=name, in_specs=_any_specs(x_in), out_specs=_any_specs(x_out), out_shape=exchange.out_shapes,
        scratch_shapes=exchange.sems,
        input_output_aliases={a: a for a in range(x_in)} if exchange.aliased else {})(*exchange.inputs)
    return list(outs)


def _gather_layer_exchange(ws, layer):
    n = len(ws)

    def copies(xin, xout, sems):
        send_sems, recv_sems, local_sems = sems
        x, y, c, chips = _my_place()
        k_me = 2 * x + y
        local, sends, recvs = [], [], []
        for a in range(n):
            src = _row_half(xin[a].at[layer], c, 0)
            dst = _row_half(xout[a].at[k_me], c, 0)
            local.append(pltpu.make_async_copy(src, dst, local_sems.at[a]))
            for j, (px, py) in enumerate(chips):
                sends.append(pltpu.make_async_remote_copy(
                    src_ref=src, dst_ref=dst, send_sem=send_sems.at[a, j], recv_sem=recv_sems.at[a, j],
                    device_id=(px, py, c), device_id_type=MESH))
                recvs.append(pltpu.make_async_remote_copy(
                    src_ref=src, dst_ref=_row_half(xout[a].at[2 * px + py], c, 0), send_sem=send_sems.at[a, j],
                    recv_sem=recv_sems.at[a, j], device_id=(px, py, c), device_id_type=MESH))
        return local, sends, recvs

    def start(xin, xout, sems):
        local, sends, _ = copies(xin, xout, sems)
        for cp in local + sends:
            cp.start()

    def finish(xin, xout, sems):
        local, sends, recvs = copies(xin, xout, sems)
        for cp in recvs:
            cp.wait_recv()
        for cp in sends:
            cp.wait_send()
        for cp in local:
            cp.wait()

    return _Exchange(ws, [jax.ShapeDtypeStruct((4,) + w.shape[1:], w.dtype) for w in ws],
                     [pltpu.SemaphoreType.DMA((n, 3)), pltpu.SemaphoreType.DMA((n, 3)), pltpu.SemaphoreType.DMA((n,))],
                     start, finish)


def _gather_pair_exchange(lands):
    n = len(lands)

    def copies(xout, sems):
        send_sems, recv_sems = sems
        x, y, c, _ = _my_place()
        sends = [pltpu.make_async_remote_copy(
            src_ref=_row_half(xout[a], c, 1), dst_ref=_row_half(xout[a], c, 1), send_sem=send_sems.at[a],
            recv_sem=recv_sems.at[a], device_id=(x, y, 1 - c), device_id_type=MESH) for a in range(n)]
        recvs = [pltpu.make_async_remote_copy(
            src_ref=_row_half(xout[a], c, 1), dst_ref=_row_half(xout[a], 1 - c, 1), send_sem=send_sems.at[a],
            recv_sem=recv_sems.at[a], device_id=(x, y, 1 - c), device_id_type=MESH) for a in range(n)]
        return sends, recvs

    def start(xin, xout, sems):
        for cp in copies(xout, sems)[0]:
            cp.start()

    def finish(xin, xout, sems):
        sends, recvs = copies(xout, sems)
        for cp in recvs:
            cp.wait_recv()
        for cp in sends:
            cp.wait_send()

    return _Exchange(lands, [jax.ShapeDtypeStruct(w.shape, w.dtype) for w in lands],
                     [pltpu.SemaphoreType.DMA((n,)), pltpu.SemaphoreType.DMA((n,))], start, finish, aliased=True)


def _swap_halves(ps):
    n = len(ps)

    def body(*refs):
        p_refs, land_refs = refs[:n], refs[n:2 * n]
        send_sems, recv_sems = refs[2 * n:]
        x, y, c, _ = _my_place()
        copies = [pltpu.make_async_remote_copy(
            src_ref=_row_half(p_refs[a], 1 - c, 1), dst_ref=land_refs[a], send_sem=send_sems.at[a],
            recv_sem=recv_sems.at[a], device_id=(x, y, 1 - c), device_id_type=MESH) for a in range(n)]
        for cp in copies:
            cp.start()
        for cp in copies:
            cp.wait()

    return pl.pallas_call(
        body, name="grad_swap_halves",
        out_shape=[jax.ShapeDtypeStruct((4, p.shape[1] // 2, p.shape[2]), p.dtype) for p in ps],
        in_specs=_any_specs(n), out_specs=_any_specs(n),
        scratch_shapes=[pltpu.SemaphoreType.DMA((n,)), pltpu.SemaphoreType.DMA((n,))],
    )(*ps)


def _add_halves(ps, lands, c_idx):
    n = len(ps)

    def body(c_ref, *refs):
        del c_ref
        for a in range(n):
            refs[2 * n + a][...] = (refs[a][...].astype(F32) + refs[n + a][...].astype(F32)).astype(BF16)

    def half_block(p):
        return (None, p.shape[1] // 2, p.shape[2])

    grid_spec = pltpu.PrefetchScalarGridSpec(
        num_scalar_prefetch=1, grid=(4,),
        in_specs=[pl.BlockSpec(half_block(p), lambda j, cr: (j, cr[0], 0)) for p in ps]
        + [pl.BlockSpec(half_block(p), lambda j, cr: (j, 0, 0)) for p in ps],
        out_specs=[pl.BlockSpec(half_block(p), lambda j, cr: (j, 0, 0)) for p in ps])
    return pl.pallas_call(
        body, name="grad_add_halves", grid_spec=grid_spec,
        out_shape=[jax.ShapeDtypeStruct((4, p.shape[1] // 2, p.shape[2]), BF16) for p in ps],
        compiler_params=_cparams(1),
    )(c_idx, *ps, *lands)


def _scatter_exchange(hs):
    n = len(hs)

    def copies(xin, xout, sems):
        send_sems, recv_sems, local_sems = sems
        x, y, c, chips = _my_place()
        k_me = 2 * x + y
        local = [pltpu.make_async_copy(xin[a].at[k_me], xout[a].at[k_me], local_sems.at[a]) for a in range(n)]
        sends, recvs = [], []
        for j, (px, py) in enumerate(chips):
            for a in range(n):
                sends.append(pltpu.make_async_remote_copy(
                    src_ref=xin[a].at[2 * px + py], dst_ref=xout[a].at[k_me], send_sem=send_sems.at[a, j],
                    recv_sem=recv_sems.at[a, j], device_id=(px, py, c), device_id_type=MESH))
                recvs.append(pltpu.make_async_remote_copy(
                    src_ref=xin[a].at[k_me], dst_ref=xout[a].at[2 * px + py], send_sem=send_sems.at[a, j],
                    recv_sem=recv_sems.at[a, j], device_id=(px, py, c), device_id_type=MESH))
        return local, sends, recvs

    def start(xin, xout, sems):
        local, sends, _ = copies(xin, xout, sems)
        for cp in local + sends:
            cp.start()

    def finish(xin, xout, sems):
        local, sends, recvs = copies(xin, xout, sems)
        for cp in recvs:
            cp.wait_recv()
        for cp in sends:
            cp.wait_send()
        for cp in local:
            cp.wait()

    return _Exchange(hs, [jax.ShapeDtypeStruct(h.shape, h.dtype) for h in hs],
                     [pltpu.SemaphoreType.DMA((n, 3)), pltpu.SemaphoreType.DMA((n, 3)), pltpu.SemaphoreType.DMA((n,))],
                     start, finish)


def _sum_chips(rcvs, accs, layer, n_layers, c_idx):
    n = len(rcvs)

    def body(c_ref, *refs):
        del c_ref
        outs = refs[len(refs) - n:]
        for a in range(n):
            r_ref = refs[a]
            acc = r_ref[0].astype(F32) + r_ref[1].astype(F32)
            acc = acc + r_ref[2].astype(F32)
            outs[a][...] = acc + r_ref[3].astype(F32)

    in_specs = [pl.BlockSpec(r.shape, lambda i, cr: (0, 0, 0)) for r in rcvs]
    args = list(rcvs)
    aliases = {}
    if accs is not None:
        in_specs += _any_specs(n)
        args += list(accs)
        aliases = {1 + n + a: a for a in range(n)}
    grid_spec = pltpu.PrefetchScalarGridSpec(
        num_scalar_prefetch=1, grid=(1,), in_specs=in_specs,
        out_specs=[pl.BlockSpec((None,) + r.shape[1:], lambda i, cr: (layer, cr[0], 0)) for r in rcvs])
    return pl.pallas_call(
        body, name="grad_sum_chips", grid_spec=grid_spec,
        out_shape=[jax.ShapeDtypeStruct((n_layers, 2 * r.shape[1], r.shape[2]), F32) for r in rcvs],
        input_output_aliases=aliases, compiler_params=_cparams(1),
    )(c_idx, *args)


def _sum_devices(g):
    _, r, n = g.shape

    def body(g_ref, o_ref):
        acc = g_ref[0]
        for d in range(1, 8):
            acc = acc + g_ref[d]
        o_ref[...] = acc

    return pl.pallas_call(
        body, name="small_grad_sum", grid=(1,),
        in_specs=[_full_spec((8, r, n))], out_specs=_full_spec((r, n)),
        out_shape=jax.ShapeDtypeStruct((r, n), F32), compiler_params=_cparams(1),
    )(g)


def _matmul(a, w, bn, name):
    s, k = a.shape
    n = w.shape[1]
    bs = min(1024, s)

    def body(a_ref, w_ref, o_ref):
        o_ref[...] = _dot(a_ref[...], w_ref[...])

    return pl.pallas_call(
        body, name=name, grid=(n // bn, s // bs),
        in_specs=[pl.BlockSpec((bs, k), lambda j, i: (i, 0)), pl.BlockSpec((k, bn), lambda j, i: (0, j))],
        out_specs=pl.BlockSpec((bs, bn), lambda j, i: (i, j)),
        out_shape=jax.ShapeDtypeStruct((s, n), F32), compiler_params=_cparams(2),
    )(a, w)


def _matmul_tn(a, b, bn, name):
    s, m = a.shape
    n = b.shape[1]
    bs = min(512, s)

    def body(a_ref, b_ref, o_ref):
        @pl.when(pl.program_id(1) == 0)
        def _():
            o_ref[...] = jnp.zeros_like(o_ref)

        o_ref[...] += _dot_tn(a_ref[...], b_ref[...])

    return pl.pallas_call(
        body, name=name, grid=(n // bn, s // bs),
        in_specs=[pl.BlockSpec((bs, m), lambda j, i: (i, 0)), pl.BlockSpec((bs, bn), lambda j, i: (i, j))],
        out_specs=pl.BlockSpec((m, bn), lambda j, i: (0, j)),
        out_shape=jax.ShapeDtypeStruct((m, n), F32), compiler_params=_cparams(2),
    )(a, b)


def _modulate(x, scale, shift):
    s, d = x.shape
    t = min(512, s)

    def body(x_ref, sc_ref, sh_ref, u_ref):
        u_ref[...] = (x_ref[...] * (1.0 + sc_ref[...]) + sh_ref[...]).astype(BF16)

    return pl.pallas_call(
        body, name="modulate", grid=(s // t,),
        in_specs=[_row_spec(t, d), _full_spec((1, d)), _full_spec((1, d))],
        out_specs=_row_spec(t, d), out_shape=jax.ShapeDtypeStruct((s, d), BF16), compiler_params=_cparams(1),
    )(x, scale, shift)


def _rope_tables(pos_col):
    s = pos_col.shape[0]
    t = min(512, s)
    inv = ROPE_THETA ** (-np.arange(0, QK_ROPE, 2, dtype=np.float32) / QK_ROPE)
    lane_freq = np.zeros((1, HEAD_PAD), np.float32)
    lane_freq[0, QK_NOPE:QK_NOPE + 16] = inv
    lane_freq[0, QK_NOPE + 16:QK_NOPE + 32] = inv
    lane_mask = np.zeros((1, HEAD_PAD), np.float32)
    lane_mask[0, QK_NOPE:QK_NOPE + QK_ROPE] = 1.0

    def body(p_ref, f_ref, m_ref, cos_ref, sin_ref):
        ang = p_ref[...].astype(F32) * f_ref[...]
        cos_ref[...] = jnp.cos(ang) * m_ref[...]
        sin_ref[...] = jnp.sin(ang) * m_ref[...]

    return pl.pallas_call(
        body, name="rope_tables", grid=(s // t,),
        in_specs=[_row_spec(t, 1), _full_spec((1, HEAD_PAD)), _full_spec((1, HEAD_PAD))],
        out_specs=[_row_spec(t, HEAD_PAD), _row_spec(t, HEAD_PAD)],
        out_shape=[jax.ShapeDtypeStruct((s, HEAD_PAD), F32)] * 2, compiler_params=_cparams(1),
    )(pos_col, jnp.asarray(lane_freq), jnp.asarray(lane_mask))


def _ada_shard(c_all, w_ada_bf):
    n_layers, d, n = w_ada_bf.shape

    def body(c_ref, w_ref, o_ref, ca_ref):
        cv = c_ref[...]
        ca = cv * _sig(cv)
        ca_ref[...] = ca
        o_ref[...] = _dot(ca.astype(BF16), w_ref[...])

    return pl.pallas_call(
        body, name="ada_shard", grid=(n_layers,),
        in_specs=[_full_spec((8, d)), pl.BlockSpec((None, d, n), lambda l: (l, 0, 0))],
        out_specs=[pl.BlockSpec((None, 8, n), lambda l: (l, 0, 0)), _full_spec((8, d))],
        out_shape=[jax.ShapeDtypeStruct((n_layers, 8, n), F32), jax.ShapeDtypeStruct((8, d), F32)],
        compiler_params=_cparams(1),
    )(c_all, w_ada_bf)


def _ada_weight_grad(cact16, dada16):
    n_layers, _, n = dada16.shape
    d = cact16.shape[1]

    def body(c_ref, g_ref, o_ref):
        o_ref[...] = _dot_tn(c_ref[...], g_ref[...])

    return pl.pallas_call(
        body, name="ada_weight_grad", grid=(n_layers,),
        in_specs=[_full_spec((16, d)), pl.BlockSpec((None, 16, n), lambda l: (l, 0, 0))],
        out_specs=pl.BlockSpec((None, d, n), lambda l: (l, 0, 0)),
        out_shape=jax.ShapeDtypeStruct((n_layers, d, n), F32), compiler_params=_cparams(1),
    )(cact16, dada16)


def _layer_norm_rows(v, g, b):
    mu = jnp.mean(v, axis=-1, keepdims=True)
    var = jnp.mean(jnp.square(v - mu), axis=-1, keepdims=True)
    rstd = lax.rsqrt(var + LN_EPS)
    xh = (v - mu) * rstd
    return xh, rstd, xh * g + b


def _layer_norm_bwd_rows(dy_hat, xh, rstd):
    return rstd * (dy_hat - jnp.mean(dy_hat, axis=-1, keepdims=True) - xh * jnp.mean(dy_hat * xh, axis=-1, keepdims=True))


def _mix_ab_fwd(proj, conv_a_w, conv_a_b, ln_a_g, ln_a_b, conv_b_w):
    s = proj.shape[0]
    t = min(ROW_BLOCK, s)
    cw = 512
    halo = CONV_HALO

    def body(p_ref, caw_ref, cab_ref, lg_ref, lb_ref, cbw_ref, acta_ref, actb_ref, ca_ref, cb_ref, abuf, zbuf):
        @pl.when(pl.program_id(0) == 0)
        def _():
            abuf[0:halo, :] = jnp.zeros((halo, cw), F32)
            zbuf[0:8, :] = jnp.zeros((8, cw), F32)

        abuf[halo:halo + t, :] = p_ref[:, 0:512] * _sig(p_ref[:, 512:1024])
        acc = jnp.zeros((t, cw), F32)
        for k in range(CONV_K):
            off = halo - (CONV_K - 1) + k
            acc = acc + caw_ref[k:k + 1, :] * abuf[off:off + t, :]
        ca = acc + cab_ref[...]
        ca_ref[...] = ca
        _, _, ln = _layer_norm_rows(ca, lg_ref[...], lb_ref[...])
        ag = p_ref[:, 1024:1536]
        acta_ref[...] = ((ln * _sig(ln)) * (ag * _sig(ag))).astype(BF16)
        abuf[0:halo, :] = abuf[t:t + halo, :]

        zbuf[8:8 + t, :] = p_ref[:, 2560:3072] * p_ref[:, 1536:2048]
        accb = jnp.zeros((t, cw), F32)
        for k in range(SC_K):
            off = 8 - (SC_K - 1) + k
            accb = accb + cbw_ref[k:k + 1, :] * zbuf[off:off + t, :]
        cb_ref[...] = accb
        bg = p_ref[:, 3072:3584]
        actb_ref[...] = ((p_ref[:, 2048:2560] * accb) * (bg * _sig(bg))).astype(BF16)
        zbuf[0:8, :] = zbuf[t:t + 8, :]

    return pl.pallas_call(
        body, name="mix_ab_fwd", grid=(s // t,),
        in_specs=[_row_spec(t, W_AB), _full_spec((CONV_K, cw)), _full_spec((1, cw)), _full_spec((1, cw)),
                  _full_spec((1, cw)), _full_spec((SC_K, cw))],
        out_specs=[_row_spec(t, cw)] * 4,
        out_shape=[jax.ShapeDtypeStruct((s, cw), BF16)] * 2 + [jax.ShapeDtypeStruct((s, cw), F32)] * 2,
        scratch_shapes=[pltpu.VMEM((halo + t, cw), F32), pltpu.VMEM((8 + t, cw), F32)],
        compiler_params=_cparams(1),
    )(proj, conv_a_w, conv_a_b, ln_a_g, ln_a_b, conv_b_w)


def _mix_ab_bwd(proj, ca, cb, dact_a, dact_b, conv_a_w, ln_a_g, ln_a_b, conv_b_w, dproj):
    s = proj.shape[0]
    t = min(ROW_BLOCK, s)
    nb = s // t
    cw = 512
    halo = CONV_HALO
    per = t // halo

    def body(p_ref, ph_ref, ca_ref, cb_ref, da_ref, db_ref, caw_ref, lg_ref, lb_ref, cbw_ref, dproj_any,
             dg_ref, dcaw_ref, dcab_ref, dlg_ref, dlb_ref, dcbw_ref, abuf, dcabuf, zbuf, dcbbuf):
        del dproj_any
        i = pl.program_id(0)
        r = nb - 1 - i

        @pl.when(i == 0)
        def _():
            dcaw_ref[...] = jnp.zeros_like(dcaw_ref)
            dcab_ref[...] = jnp.zeros_like(dcab_ref)
            dlg_ref[...] = jnp.zeros_like(dlg_ref)
            dlb_ref[...] = jnp.zeros_like(dlb_ref)
            dcbw_ref[...] = jnp.zeros_like(dcbw_ref)
            dcabuf[t:t + halo, :] = jnp.zeros((halo, cw), F32)
            dcbbuf[t:t + 8, :] = jnp.zeros((8, cw), F32)

        keep = (r > 0).astype(F32)
        a1 = p_ref[:, 0:512]
        s2 = _sig(p_ref[:, 512:1024])
        abuf[0:halo, :] = (ph_ref[:, 0:512] * _sig(ph_ref[:, 512:1024])) * keep
        abuf[halo:halo + t, :] = a1 * s2
        ca_v = ca_ref[...]
        xh, rstd, ln = _layer_norm_rows(ca_v, lg_ref[...], lb_ref[...])
        s_ln = _sig(ln)
        ag = p_ref[:, 1024:1536]
        sg = _sig(ag)
        dact = da_ref[...]
        dsa = dact * (ag * sg)
        dg_ref[:, 1024:1536] = (dact * (ln * s_ln) * _dsilu(ag, sg)).astype(BF16)
        dln = dsa * _dsilu(ln, s_ln)
        dlg_ref[...] += jnp.sum(dln * xh, axis=0, keepdims=True)
        dlb_ref[...] += jnp.sum(dln, axis=0, keepdims=True)
        dca = _layer_norm_bwd_rows(dln * lg_ref[...], xh, rstd)
        dcab_ref[...] += jnp.sum(dca, axis=0, keepdims=True)
        dcabuf[0:t, :] = dca
        d_a = jnp.zeros((t, cw), F32)
        for k in range(CONV_K):
            off = (CONV_K - 1) - k
            d_a = d_a + caw_ref[k:k + 1, :] * dcabuf[off:off + t, :]
            src = halo - (CONV_K - 1) + k
            dcaw_ref[k:k + 1, :] += jnp.sum(dca * abuf[src:src + t, :], axis=0, keepdims=True)
        dg_ref[:, 0:512] = (d_a * s2).astype(BF16)
        dg_ref[:, 512:1024] = (d_a * a1 * s2 * (1.0 - s2)).astype(BF16)
        dcabuf[t:t + halo, :] = dcabuf[0:halo, :]

        xb = p_ref[:, 1536:2048]
        gb = p_ref[:, 2048:2560]
        gc = p_ref[:, 2560:3072]
        bg = p_ref[:, 3072:3584]
        zbuf[0:8, :] = (ph_ref[halo - 8:halo, 2560:3072] * ph_ref[halo - 8:halo, 1536:2048]) * keep
        zbuf[8:8 + t, :] = gc * xb
        sbg = _sig(bg)
        cbv = cb_ref[...]
        dactb = db_ref[...]
        dyb = dactb * (bg * sbg)
        dg_ref[:, 3072:3584] = (dactb * (gb * cbv) * _dsilu(bg, sbg)).astype(BF16)
        dg_ref[:, 2048:2560] = (dyb * cbv).astype(BF16)
        dcb = dyb * gb
        dcbbuf[0:t, :] = dcb
        dz = jnp.zeros((t, cw), F32)
        for k in range(SC_K):
            off = (SC_K - 1) - k
            dz = dz + cbw_ref[k:k + 1, :] * dcbbuf[off:off + t, :]
            src = 8 - (SC_K - 1) + k
            dcbw_ref[k:k + 1, :] += jnp.sum(dcb * zbuf[src:src + t, :], axis=0, keepdims=True)
        dg_ref[:, 2560:3072] = (dz * xb).astype(BF16)
        dg_ref[:, 1536:2048] = (dz * gc).astype(BF16)
        dcbbuf[t:t + 8, :] = dcbbuf[0:8, :]

    rev = lambda i: (nb - 1 - i, 0)
    outs = pl.pallas_call(
        body, name="mix_ab_bwd", grid=(nb,),
        in_specs=[pl.BlockSpec((t, W_AB), rev),
                  pl.BlockSpec((halo, W_AB), lambda i: (jnp.maximum((nb - 1 - i) * per - 1, 0), 0)),
                  pl.BlockSpec((t, cw), rev), pl.BlockSpec((t, cw), rev), pl.BlockSpec((t, cw), rev),
                  pl.BlockSpec((t, cw), rev),
                  _full_spec((CONV_K, cw)), _full_spec((1, cw)), _full_spec((1, cw)), _full_spec((SC_K, cw)),
                  pl.BlockSpec(memory_space=pl.ANY)],
        out_specs=[pl.BlockSpec((t, W_AB), rev), _full_spec((CONV_HALO, cw)), _full_spec((1, cw)), _full_spec((1, cw)),
                   _full_spec((1, cw)), _full_spec((8, cw))],
        out_shape=[jax.ShapeDtypeStruct(dproj.shape, BF16), jax.ShapeDtypeStruct((CONV_HALO, cw), F32),
                   jax.ShapeDtypeStruct((1, cw), F32), jax.ShapeDtypeStruct((1, cw), F32),
                   jax.ShapeDtypeStruct((1, cw), F32), jax.ShapeDtypeStruct((8, cw), F32)],
        scratch_shapes=[pltpu.VMEM((halo + t, cw), F32), pltpu.VMEM((t + halo, cw), F32),
                        pltpu.VMEM((8 + t, cw), F32), pltpu.VMEM((t + 8, cw), F32)],
        input_output_aliases={10: 0},
        compiler_params=_cparams(1),
    )(proj, proj, ca, cb, dact_a, dact_b, conv_a_w, ln_a_g, ln_a_b, conv_b_w, dproj)
    return outs


def _lane_is_nope():
    return lax.broadcasted_iota(jnp.int32, (1, HEAD_PAD), 1) < QK_NOPE


def _attn_prep_fwd(proj, q_g, kv_g, wq12, wkv, tcos, tsin):
    s = proj.shape[0]
    t = min(ROW_BLOCK, s)
    c0 = (W_AB + W_G5) // W_C1

    def body(p_ref, qg_ref, kg_ref, wq_ref, wkv_ref, cos_ref, sin_ref, qf_ref, kf_ref, v_ref):
        ql = p_ref[:, 0:384]
        qn = (ql * lax.rsqrt(jnp.mean(jnp.square(ql), axis=-1, keepdims=True) + RMS_EPS) * qg_ref[...]).astype(BF16)
        q12 = _dot(qn, wq_ref[...])
        kvl = p_ref[:, 384:640]
        kvn = (kvl * lax.rsqrt(jnp.mean(jnp.square(kvl), axis=-1, keepdims=True) + RMS_EPS) * kg_ref[...]).astype(BF16)
        kv12 = _dot(kvn, wkv_ref[...])
        tcos_v = cos_ref[...]
        tsin_v = sin_ref[...]
        tq1 = jnp.where(_lane_is_nope(), 1.0, tcos_v)
        kpe = p_ref[:, 640:768] * tcos_v + p_ref[:, 768:896] * tsin_v
        for h in range(HEADS):
            lo = h * HEAD_PAD
            qf_ref[h] = (q12[:, lo:lo + HEAD_PAD] * tq1 + q12[:, 1024 + lo:1024 + lo + HEAD_PAD] * tsin_v).astype(BF16)
            kf_ref[h] = (kv12[:, lo:lo + HEAD_PAD] + kpe).astype(BF16)
        v_ref[...] = kv12[:, 1024:1536].astype(BF16)

    return pl.pallas_call(
        body, name="attn_prep_fwd", grid=(s // t,),
        in_specs=[pl.BlockSpec((t, W_C1), lambda i: (i, c0)), _full_spec((1, 384)), _full_spec((1, 256)),
                  _full_spec((384, 2048)), _full_spec((256, 1536)), _row_spec(t, HEAD_PAD), _row_spec(t, HEAD_PAD)],
        out_specs=[pl.BlockSpec((HEADS, t, HEAD_PAD), lambda i: (0, i, 0)),
                   pl.BlockSpec((HEADS, t, HEAD_PAD), lambda i: (0, i, 0)), _row_spec(t, 512)],
        out_shape=[jax.ShapeDtypeStruct((HEADS, s, HEAD_PAD), BF16), jax.ShapeDtypeStruct((HEADS, s, HEAD_PAD), BF16),
                   jax.ShapeDtypeStruct((s, 512), BF16)],
        compiler_params=_cparams(1),
    )(proj, q_g, kv_g, wq12, wkv, tcos, tsin)


def _attn_prep_bwd(proj, dqf, dkf, dv, q_g, kv_g, wq12, wkv, tcos, tsin, dproj):
    s = proj.shape[0]
    t = min(ROW_BLOCK, s)
    c0 = (W_AB + W_G5) // W_C1

    def body(p_ref, dqf_ref, dkf_ref, dv_ref, qg_ref, kg_ref, wq_ref, wkv_ref, cos_ref, sin_ref, dproj_any,
             dg_ref, qn_ref, kvn_ref, dq12_ref, dkv12_ref, dqg_ref, dkg_ref):
        del dproj_any

        @pl.when(pl.program_id(0) == 0)
        def _():
            dqg_ref[...] = jnp.zeros_like(dqg_ref)
            dkg_ref[...] = jnp.zeros_like(dkg_ref)

        tcos_v = cos_ref[...]
        tsin_v = sin_ref[...]
        tq1 = jnp.where(_lane_is_nope(), 1.0, tcos_v)
        dkpe = jnp.zeros((t, HEAD_PAD), F32)
        for h in range(HEADS):
            lo = h * HEAD_PAD
            dq = dqf_ref[h]
            dq12_ref[:, lo:lo + HEAD_PAD] = (dq * tq1).astype(BF16)
            dq12_ref[:, 1024 + lo:1024 + lo + HEAD_PAD] = (dq * tsin_v).astype(BF16)
            dk = dkf_ref[h]
            dkv12_ref[:, lo:lo + HEAD_PAD] = dk.astype(BF16)
            dkpe = dkpe + dk
        dkv12_ref[:, 1024:1536] = dv_ref[...].astype(BF16)
        dg_ref[:, 640:768] = (dkpe * tcos_v).astype(BF16)
        dg_ref[:, 768:896] = (dkpe * tsin_v).astype(BF16)

        def rms_bwd(xl, g, dn):
            rr = lax.rsqrt(jnp.mean(jnp.square(xl), axis=-1, keepdims=True) + RMS_EPS)
            xn = xl * rr
            tt = dn * g
            return rr * (tt - xn * jnp.mean(tt * xn, axis=-1, keepdims=True)), jnp.sum(dn * xn, axis=0, keepdims=True), xn

        ql = p_ref[:, 0:384]
        dqn = _dot_nt(dq12_ref[...], wq_ref[...])
        dql, dqg, qxn = rms_bwd(ql, qg_ref[...], dqn)
        dg_ref[:, 0:384] = dql.astype(BF16)
        dqg_ref[...] += dqg
        qn_ref[...] = (qxn * qg_ref[...]).astype(BF16)
        kvl = p_ref[:, 384:640]
        dkvn = _dot_nt(dkv12_ref[...], wkv_ref[...])
        dkvl, dkg, kxn = rms_bwd(kvl, kg_ref[...], dkvn)
        dg_ref[:, 384:640] = dkvl.astype(BF16)
        dkg_ref[...] += dkg
        kvn_ref[...] = (kxn * kg_ref[...]).astype(BF16)

    return pl.pallas_call(
        body, name="attn_prep_bwd", grid=(s // t,),
        in_specs=[pl.BlockSpec((t, W_C1), lambda i: (i, c0)),
                  pl.BlockSpec((HEADS, t, HEAD_PAD), lambda i: (0, i, 0)),
                  pl.BlockSpec((HEADS, t, HEAD_PAD), lambda i: (0, i, 0)), _row_spec(t, 512),
                  _full_spec((1, 384)), _full_spec((1, 256)), _full_spec((384, 2048)), _full_spec((256, 1536)),
                  _row_spec(t, HEAD_PAD), _row_spec(t, HEAD_PAD), pl.BlockSpec(memory_space=pl.ANY)],
        out_specs=[pl.BlockSpec((t, W_C1), lambda i: (i, c0)), _row_spec(t, 384), _row_spec(t, 256),
                   _row_spec(t, 2048), _row_spec(t, 1536), _full_spec((1, 384)), _full_spec((1, 256))],
        out_shape=[jax.ShapeDtypeStruct(dproj.shape, BF16), jax.ShapeDtypeStruct((s, 384), BF16),
                   jax.ShapeDtypeStruct((s, 256), BF16), jax.ShapeDtypeStruct((s, 2048), BF16),
                   jax.ShapeDtypeStruct((s, 1536), BF16), jax.ShapeDtypeStruct((1, 384), F32),
                   jax.ShapeDtypeStruct((1, 256), F32)],
        input_output_aliases={10: 0},
        compiler_params=_cparams(1),
    )(proj, dqf, dkf, dv, q_g, kv_g, wq12, wkv, tcos, tsin, dproj)


def _causal_keep(t):
    return lax.broadcasted_iota(jnp.int32, (t, t), 0) >= lax.broadcasted_iota(jnp.int32, (t, t), 1)


def _half_select(e):
    lane = lax.broadcasted_iota(jnp.int32, (1, HEAD_PAD), 1)
    return (lane < V_HEAD) if e == 0 else (lane >= V_HEAD)


def _attention_fwd(qf, kf, v, exchange=None):
    _, s, _ = qf.shape
    t = min(ATT_BLOCK, s)
    nq = s // t

    def body(q_ref, k_ref, v_ref, o_ref, lse_ref, m_scr, l_scr, acc):
        qi = pl.program_id(0)
        kj = pl.program_id(1)

        @pl.when(kj == 0)
        def _():
            m_scr[...] = jnp.full(m_scr.shape, NEG_BIG, F32)
            l_scr[...] = jnp.zeros_like(l_scr)
            acc[...] = jnp.zeros_like(acc)

        def step(diag):
            for h in range(HEADS):
                pair, e = divmod(h, 2)
                sel = _half_select(e)
                sc = _dot_nt(q_ref[h], k_ref[h]) * ATT_SCALE
                if diag:
                    sc = jnp.where(_causal_keep(t), sc, NEG_BIG)
                m_prev = m_scr[h]
                m_new = jnp.maximum(m_prev, jnp.max(sc, axis=1, keepdims=True))
                p = jnp.exp(sc - m_new)
                alpha = jnp.exp(m_prev - m_new)
                l_scr[h] = alpha * l_scr[h] + jnp.sum(p, axis=1, keepdims=True)
                m_scr[h] = m_new
                lo = pair * HEAD_PAD
                v2 = v_ref[:, lo:lo + HEAD_PAD]
                pv = _dot(p.astype(BF16), jnp.where(sel, v2, jnp.zeros_like(v2)))
                acc[:, lo:lo + HEAD_PAD] = acc[:, lo:lo + HEAD_PAD] * jnp.where(sel, alpha, 1.0) + pv

        @pl.when(kj < qi)
        def _():
            step(False)

        @pl.when(kj == qi)
        def _():
            step(True)
            for pair in range(HEADS // 2):
                lo = pair * HEAD_PAD
                inv = jnp.where(_half_select(0), 1.0 / l_scr[2 * pair], 1.0 / l_scr[2 * pair + 1])
                o_ref[:, lo:lo + HEAD_PAD] = acc[:, lo:lo + HEAD_PAD] * inv
            for h in range(HEADS):
                lse_ref[h] = m_scr[h] + jnp.log(l_scr[h])

    return _call(
        body, (qf, kf, v), exchange, name="attention_fwd", grid=(nq, nq),
        in_specs=[pl.BlockSpec((HEADS, t, HEAD_PAD), lambda i, j: (0, i, 0)),
                  pl.BlockSpec((HEADS, t, HEAD_PAD), lambda i, j: (0, jnp.minimum(i, j), 0)),
                  pl.BlockSpec((t, 512), lambda i, j: (jnp.minimum(i, j), 0))],
        out_specs=[pl.BlockSpec((t, 512), lambda i, j: (i, 0)), pl.BlockSpec((HEADS, t, 1), lambda i, j: (0, i, 0))],
        out_shape=[jax.ShapeDtypeStruct((s, 512), F32), jax.ShapeDtypeStruct((HEADS, s, 1), F32)],
        scratch_shapes=[pltpu.VMEM((HEADS, t, 1), F32), pltpu.VMEM((HEADS, t, 1), F32), pltpu.VMEM((t, 512), F32)],
        compiler_params=_cparams(2))


def _attention_bwd(qf, kf, v, o, do, lse, exchange=None):
    _, s, _ = qf.shape
    t = min(ATT_BLOCK, s)
    nq = s // t

    def body(q_ref, k_ref, v_ref, o_ref, do_ref, lse_ref, dq_ref, dk_ref, dv_ref, dk_acc, dv_acc):
        kj = pl.program_id(0)
        qi = pl.program_id(1)

        @pl.when((kj == 0) & (qi == 0))
        def _():
            dq_ref[...] = jnp.zeros_like(dq_ref)

        @pl.when(qi == kj)
        def _():
            dk_acc[...] = jnp.zeros_like(dk_acc)
            dv_acc[...] = jnp.zeros_like(dv_acc)

        def step(diag):
            rows = pl.ds(pl.multiple_of(qi * t, t), t)
            for h in range(HEADS):
                pair, e = divmod(h, 2)
                sel = _half_select(e)
                lo = pair * HEAD_PAD
                q = q_ref[h]
                k = k_ref[h]
                sc = _dot_nt(q, k) * ATT_SCALE
                if diag:
                    sc = jnp.where(_causal_keep(t), sc, NEG_BIG)
                p = jnp.exp(sc - lse_ref[h])
                do_pair = do_ref[:, lo:lo + HEAD_PAD]
                do_e = jnp.where(sel, do_pair, 0.0)
                do_b = do_e.astype(BF16)
                v2 = v_ref[:, lo:lo + HEAD_PAD]
                dv_acc[:, lo:lo + HEAD_PAD] += _dot_tn(p.astype(BF16), do_b)
                dp = _dot_nt(do_b, jnp.where(sel, v2, jnp.zeros_like(v2)))
                delta = jnp.sum(do_e * o_ref[:, lo:lo + HEAD_PAD], axis=1, keepdims=True)
                ds = (p * (dp - delta) * ATT_SCALE).astype(BF16)
                dq_ref[h, rows, :] += _dot(ds, k)
                dk_acc[h] += _dot_tn(ds, q)

        @pl.when(qi > kj)
        def _():
            step(False)

        @pl.when(qi == kj)
        def _():
            step(True)

        @pl.when(qi == nq - 1)
        def _():
            dk_ref[...] = dk_acc[...]
            dv_ref[...] = dv_acc[...]

    qmap = lambda j, i: (0, jnp.maximum(i, j), 0)
    return _call(
        body, (qf, kf, v, o, do, lse), exchange, name="attention_bwd", grid=(nq, nq),
        in_specs=[pl.BlockSpec((HEADS, t, HEAD_PAD), qmap),
                  pl.BlockSpec((HEADS, t, HEAD_PAD), lambda j, i: (0, j, 0)),
                  pl.BlockSpec((t, 512), lambda j, i: (j, 0)),
                  pl.BlockSpec((t, 512), lambda j, i: (jnp.maximum(i, j), 0)),
                  pl.BlockSpec((t, 512), lambda j, i: (jnp.maximum(i, j), 0)),
                  pl.BlockSpec((HEADS, t, 1), qmap)],
        out_specs=[_full_spec((HEADS, s, HEAD_PAD)),
                   pl.BlockSpec((HEADS, t, HEAD_PAD), lambda j, i: (0, j, 0)),
                   pl.BlockSpec((t, 512), lambda j, i: (j, 0))],
        out_shape=[jax.ShapeDtypeStruct((HEADS, s, HEAD_PAD), F32), jax.ShapeDtypeStruct((HEADS, s, HEAD_PAD), F32),
                   jax.ShapeDtypeStruct((s, 512), F32)],
        scratch_shapes=[pltpu.VMEM((HEADS, t, HEAD_PAD), F32), pltpu.VMEM((t, 512), F32)],
        compiler_params=_cparams(2))


def _merge_fwd(act_a, act_b, o, proj, x, gate, w_a, w_b, w_c, w_o, ln_g, ln_b, alpha, exchange=None):
    s, d = x.shape
    t = min(ROW_BLOCK, s)

    def body(aa_ref, ab_ref, o_ref, p_ref, x_ref, gate_ref, wa_ref, wb_ref, wc_ref, wo_ref, lg_ref, lb_ref,
             xn_ref, xh_ref, rstd_ref):
        cg = p_ref[:, 0:512]
        act_c = (o_ref[...] * (cg * _sig(cg))).astype(BF16)
        m = _sig(p_ref[:, 512:1536]) * _dot(aa_ref[...], wa_ref[...])
        m = m + _sig(p_ref[:, 1536:2560]) * _dot(ab_ref[...], wb_ref[...])
        m = m + _sig(p_ref[:, 2560:3584]) * _dot(act_c, wc_ref[...])
        out = _dot(m.astype(BF16), wo_ref[...])
        z = alpha * x_ref[...] + gate_ref[...] * out
        xh, rstd, y = _layer_norm_rows(z, lg_ref[...], lb_ref[...])
        xn_ref[...] = y
        xh_ref[...] = xh
        rstd_ref[...] = rstd

    return _call(
        body, (act_a, act_b, o, proj, x, gate, w_a, w_b, w_c, w_o, ln_g, ln_b), exchange, name="merge_fwd", grid=(s // t,),
        in_specs=[_row_spec(t, 512), _row_spec(t, 512), _row_spec(t, 512),
                  pl.BlockSpec((t, W_G5), lambda i: (i, 1)), _row_spec(t, d), _full_spec((1, d)),
                  _full_spec((512, d)), _full_spec((512, d)), _full_spec((512, d)), _full_spec((d, d)),
                  _full_spec((1, d)), _full_spec((1, d))],
        out_specs=[_row_spec(t, d), _row_spec(t, d), _row_spec(t, 1)],
        out_shape=[jax.ShapeDtypeStruct((s, d), F32), jax.ShapeDtypeStruct((s, d), F32), jax.ShapeDtypeStruct((s, 1), F32)],
        compiler_params=_cparams(1))


def _merge_bwd(dy, xh, rstd, act_a, act_b, o, proj, gate, w_a, w_b, w_c, w_o, ln_g, alpha, exchange=None):
    s, d = dy.shape
    t = min(ROW_BLOCK, s)

    def body(dy_ref, xh_ref, rstd_ref, aa_ref, ab_ref, o_ref, p_ref, gate_ref, wa_ref, wb_ref, wc_ref, wo_ref, lg_ref,
             dxr_ref, dacta_ref, dactb_ref, do_ref, dg_ref, m_ref, dout_ref, dya_ref, dyb_ref, dyc_ref, actc_ref,
             dlg_ref, dlb_ref, dgate_ref):
        @pl.when(pl.program_id(0) == 0)
        def _():
            dlg_ref[...] = jnp.zeros_like(dlg_ref)
            dlb_ref[...] = jnp.zeros_like(dlb_ref)
            dgate_ref[...] = jnp.zeros_like(dgate_ref)

        dyv = dy_ref[...]
        xhv = xh_ref[...]
        dlg_ref[...] += jnp.sum(dyv * xhv, axis=0, keepdims=True)
        dlb_ref[...] += jnp.sum(dyv, axis=0, keepdims=True)
        dz = _layer_norm_bwd_rows(dyv * lg_ref[...], xhv, rstd_ref[...])
        dxr_ref[...] = alpha * dz

        cg = p_ref[:, 0:512]
        scg = _sig(cg)
        silu_cg = cg * scg
        ov = o_ref[...]
        act_c = (ov * silu_cg).astype(BF16)
        actc_ref[...] = act_c
        ya = _dot(aa_ref[...], wa_ref[...])
        yb = _dot(ab_ref[...], wb_ref[...])
        yc = _dot(act_c, wc_ref[...])
        ga = _sig(p_ref[:, 512:1536])
        gb = _sig(p_ref[:, 1536:2560])
        gc = _sig(p_ref[:, 2560:3584])
        mb = (ga * ya + gb * yb + gc * yc).astype(BF16)
        m_ref[...] = mb
        out = _dot(mb, wo_ref[...])
        dgate_ref[...] += jnp.sum(dz * out, axis=0, keepdims=True)
        dout = (gate_ref[...] * dz).astype(BF16)
        dout_ref[...] = dout
        dm = _dot_nt(dout, wo_ref[...])

        dya = (dm * ga).astype(BF16)
        dya_ref[...] = dya
        dg_ref[:, 512:1536] = (dm * ya * ga * (1.0 - ga)).astype(BF16)
        dacta_ref[...] = _dot_nt(dya, wa_ref[...])
        dyb = (dm * gb).astype(BF16)
        dyb_ref[...] = dyb
        dg_ref[:, 1536:2560] = (dm * yb * gb * (1.0 - gb)).astype(BF16)
        dactb_ref[...] = _dot_nt(dyb, wb_ref[...])
        dyc = (dm * gc).astype(BF16)
        dyc_ref[...] = dyc
        dg_ref[:, 2560:3584] = (dm * yc * gc * (1.0 - gc)).astype(BF16)
        dactc = _dot_nt(dyc, wc_ref[...])
        do_ref[...] = dactc * silu_cg
        dg_ref[:, 0:512] = (dactc * ov * _dsilu(cg, scg)).astype(BF16)

    return _call(
        body, (dy, xh, rstd, act_a, act_b, o, proj, gate, w_a, w_b, w_c, w_o, ln_g), exchange, name="merge_bwd", grid=(s // t,),
        in_specs=[_row_spec(t, d), _row_spec(t, d), _row_spec(t, 1), _row_spec(t, 512), _row_spec(t, 512),
                  _row_spec(t, 512), pl.BlockSpec((t, W_G5), lambda i: (i, 1)), _full_spec((1, d)),
                  _full_spec((512, d)), _full_spec((512, d)), _full_spec((512, d)), _full_spec((d, d)),
                  _full_spec((1, d))],
        out_specs=[_row_spec(t, d), _row_spec(t, 512), _row_spec(t, 512), _row_spec(t, 512),
                   pl.BlockSpec((t, W_G5), lambda i: (i, 1)),
                   _row_spec(t, d), _row_spec(t, d), _row_spec(t, d), _row_spec(t, d), _row_spec(t, d),
                   _row_spec(t, 512), _full_spec((1, d)), _full_spec((1, d)), _full_spec((1, d))],
        out_shape=[jax.ShapeDtypeStruct((s, d), F32), jax.ShapeDtypeStruct((s, 512), F32),
                   jax.ShapeDtypeStruct((s, 512), F32), jax.ShapeDtypeStruct((s, 512), F32),
                   jax.ShapeDtypeStruct((s, W_PACK), BF16),
                   jax.ShapeDtypeStruct((s, d), BF16), jax.ShapeDtypeStruct((s, d), BF16),
                   jax.ShapeDtypeStruct((s, d), BF16), jax.ShapeDtypeStruct((s, d), BF16),
                   jax.ShapeDtypeStruct((s, d), BF16), jax.ShapeDtypeStruct((s, 512), BF16),
                   jax.ShapeDtypeStruct((1, d), F32), jax.ShapeDtypeStruct((1, d), F32),
                   jax.ShapeDtypeStruct((1, d), F32)],
        compiler_params=_cparams(1))


def _proj_bwd_input(dproj, w_pack, x, scale, dxres):
    s, d = x.shape
    t = min(1024, s)
    nk = W_PACK // COL_BLK

    def body(dg_ref, w_ref, x_ref, sc_ref, dxr_ref, dx_ref, dsc_ref, dsh_ref, acc):
        i = pl.program_id(0)
        k = pl.program_id(1)

        @pl.when((i == 0) & (k == 0))
        def _():
            dsc_ref[...] = jnp.zeros_like(dsc_ref)
            dsh_ref[...] = jnp.zeros_like(dsh_ref)

        @pl.when(k == 0)
        def _():
            acc[...] = jnp.zeros_like(acc)

        acc[...] += _dot_nt(dg_ref[...], w_ref[...])

        @pl.when(k == nk - 1)
        def _():
            du = acc[...]
            dx_ref[...] = du * (1.0 + sc_ref[...]) + dxr_ref[...]
            dsc_ref[...] += jnp.sum(du * x_ref[...], axis=0, keepdims=True)
            dsh_ref[...] += jnp.sum(du, axis=0, keepdims=True)

    return pl.pallas_call(
        body, name="proj_bwd_input", grid=(s // t, nk),
        in_specs=[pl.BlockSpec((t, COL_BLK), lambda i, k: (i, k)), pl.BlockSpec((d, COL_BLK), lambda i, k: (0, k)),
                  pl.BlockSpec((t, d), lambda i, k: (i, 0)), _full_spec((1, d)), pl.BlockSpec((t, d), lambda i, k: (i, 0))],
        out_specs=[pl.BlockSpec((t, d), lambda i, k: (i, 0)), _full_spec((1, d)), _full_spec((1, d))],
        out_shape=[jax.ShapeDtypeStruct((s, d), F32), jax.ShapeDtypeStruct((1, d), F32), jax.ShapeDtypeStruct((1, d), F32)],
        scratch_shapes=[pltpu.VMEM((t, d), F32)],
        compiler_params=_cparams(2),
    )(dproj, w_pack, x, scale, dxres)


def _loss_head(y, target):
    s, d = y.shape
    t = min(512, s)

    def body(y_ref, t_ref, dy_ref, loss_ref):
        @pl.when(pl.program_id(0) == 0)
        def _():
            loss_ref[...] = jnp.zeros_like(loss_ref)

        err = y_ref[...] - t_ref[...]
        dy_ref[...] = err / d
        part = 0.5 * jnp.sum(jnp.mean(jnp.square(err), axis=-1, keepdims=True), axis=0, keepdims=True)
        loss_ref[...] += jnp.broadcast_to(part, loss_ref.shape)

    return pl.pallas_call(
        body, name="loss_head", grid=(s // t,),
        in_specs=[_row_spec(t, d), _row_spec(t, d)],
        out_specs=[_row_spec(t, d), _full_spec((1, 128))],
        out_shape=[jax.ShapeDtypeStruct((s, d), F32), jax.ShapeDtypeStruct((1, 128), F32)],
        compiler_params=_cparams(1),
    )(y, target)


def _adamw(g, w, m, v, name):
    r, n = g.shape
    rb = r
    if r * n * 4 > 2 ** 21:
        for cand in (512, 256, 128, 64, 32, 16, 8):
            if r % cand == 0 and cand * n * 4 <= 2 ** 21:
                rb = cand
                break

    def body(g_ref, w_ref, m_ref, v_ref, d_ref, mo_ref, vo_ref):
        gv = g_ref[...]
        mn = ADAM_B1 * m_ref[...] + (1.0 - ADAM_B1) * gv
        vn = ADAM_B2 * v_ref[...] + (1.0 - ADAM_B2) * jnp.square(gv)
        m_hat = mn / (1.0 - ADAM_B1 ** ADAM_STEP)
        v_hat = vn / (1.0 - ADAM_B2 ** ADAM_STEP)
        d_ref[...] = -ADAM_LR * (m_hat / (jnp.sqrt(v_hat) + ADAM_EPS) + ADAM_WD * w_ref[...])
        mo_ref[...] = mn
        vo_ref[...] = vn

    spec = pl.BlockSpec((rb, n), lambda i: (i, 0))
    return pl.pallas_call(
        body, name=name, grid=(r // rb,),
        in_specs=[spec] * 4, out_specs=[spec] * 3,
        out_shape=[jax.ShapeDtypeStruct((r, n), F32)] * 3, compiler_params=_cparams(1),
    )(g, w, m, v)


SMALL = ("b_ada", "conv_a_b", "ln_a_g", "ln_a_b", "q_norm_g", "kv_norm_g", "ln_g", "ln_b", "conv_a_w", "conv_b_w")
ROWS_256 = (("w_a_out", 512), ("w_b_out", 512), ("w_c_out", 512), ("w_ukv", 256))


def _rot_cols(w):
    return jnp.concatenate([-w[..., 16:], w[..., :16]], axis=-1)


def _unrot_cols(g):
    return jnp.concatenate([g[..., 16:], -g[..., :16]], axis=-1)


def _pack_w_in(w):
    z = lambda n: jnp.zeros(w.shape[:-1] + (n,), w.dtype)
    wk = w[..., 4224:4256]
    return jnp.concatenate([w[..., 0:3584], w[..., 4256:7840], w[..., 3584:4224],
                            z(64), wk, z(32), z(64), _rot_cols(wk), z(32)], axis=-1)


def _unpack_w_in_grad(g):
    c1 = W_AB + W_G5
    gk = g[..., c1 + 640 + 64:c1 + 640 + 96] + _unrot_cols(g[..., c1 + 768 + 64:c1 + 768 + 96])
    return jnp.concatenate([g[..., 0:3584], g[..., c1:c1 + 640], gk, g[..., W_AB:W_AB + W_G5]], axis=-1)


def _pack_w_uq(w):
    lead = w.shape[:-1]
    wh = w.reshape(lead + (HEADS, QK_NOPE + QK_ROPE))
    nope, rope = wh[..., :QK_NOPE], wh[..., QK_NOPE:]
    z32 = jnp.zeros_like(rope)
    q1 = jnp.concatenate([nope, rope, z32], axis=-1).reshape(lead + (HEADS * HEAD_PAD,))
    q2 = jnp.concatenate([jnp.zeros_like(nope), _rot_cols(rope), z32], axis=-1).reshape(lead + (HEADS * HEAD_PAD,))
    return jnp.concatenate([q1, q2], axis=-1)


def _unpack_w_uq_grad(g):
    lead = g.shape[:-1]
    g1 = g[..., :1024].reshape(lead + (HEADS, HEAD_PAD))
    g2 = g[..., 1024:].reshape(lead + (HEADS, HEAD_PAD))
    rope = g1[..., QK_NOPE:QK_NOPE + QK_ROPE] + _unrot_cols(g2[..., QK_NOPE:QK_NOPE + QK_ROPE])
    return jnp.concatenate([g1[..., :QK_NOPE], rope], axis=-1).reshape(lead + (HEADS * (QK_NOPE + QK_ROPE),))


def _pack_w_ukv(w):
    lead = w.shape[:-1]
    wh = w.reshape(lead + (HEADS, QK_NOPE + V_HEAD))
    kn, vv = wh[..., :QK_NOPE], wh[..., QK_NOPE:]
    k1 = jnp.concatenate([kn, jnp.zeros_like(kn)], axis=-1).reshape(lead + (HEADS * HEAD_PAD,))
    return jnp.concatenate([k1, vv.reshape(lead + (HEADS * V_HEAD,))], axis=-1)


def _unpack_w_ukv_grad(g):
    lead = g.shape[:-1]
    gk = g[..., :1024].reshape(lead + (HEADS, HEAD_PAD))[..., :QK_NOPE]
    gv = g[..., 1024:].reshape(lead + (HEADS, V_HEAD))
    return jnp.concatenate([gk, gv], axis=-1).reshape(lead + (HEADS * (QK_NOPE + V_HEAD),))


def _join_cols(g4):
    return jnp.concatenate([g4[j] for j in range(4)], axis=-1)


def _split_cols(w):
    c4 = w.shape[-1] // 4
    return jnp.stack([w[:, j * c4:(j + 1) * c4] for j in range(4)])


def kernel(x, c, positions, w_ada, b_ada, w_in, conv_a_w, conv_a_b, ln_a_g, ln_a_b, w_a_out, conv_b_w, w_b_out, q_norm_g, kv_norm_g, w_uq, w_ukv, w_c_out, w_o, ln_g, ln_b, loss_target, m_w_ada, m_b_ada, m_w_in, m_conv_a_w, m_conv_a_b, m_ln_a_g, m_ln_a_b, m_w_a_out, m_conv_b_w, m_w_b_out, m_q_norm_g, m_kv_norm_g, m_w_uq, m_w_ukv, m_w_c_out, m_w_o, m_ln_g, m_ln_b, v_w_ada, v_b_ada, v_w_in, v_conv_a_w, v_conv_a_b, v_ln_a_g, v_ln_a_b, v_w_a_out, v_conv_b_w, v_w_b_out, v_q_norm_g, v_kv_norm_g, v_w_uq, v_w_ukv, v_w_c_out, v_w_o, v_ln_g, v_ln_b):
    weights = dict(w_ada=w_ada, b_ada=b_ada, w_in=w_in, conv_a_w=conv_a_w, conv_a_b=conv_a_b, ln_a_g=ln_a_g, ln_a_b=ln_a_b,
                   w_a_out=w_a_out, conv_b_w=conv_b_w, w_b_out=w_b_out, q_norm_g=q_norm_g, kv_norm_g=kv_norm_g, w_uq=w_uq,
                   w_ukv=w_ukv, w_c_out=w_c_out, w_o=w_o, ln_g=ln_g, ln_b=ln_b)
    mom_m = dict(w_ada=m_w_ada, b_ada=m_b_ada, w_in=m_w_in, conv_a_w=m_conv_a_w, conv_a_b=m_conv_a_b, ln_a_g=m_ln_a_g,
                 ln_a_b=m_ln_a_b, w_a_out=m_w_a_out, conv_b_w=m_conv_b_w, w_b_out=m_w_b_out, q_norm_g=m_q_norm_g,
                 kv_norm_g=m_kv_norm_g, w_uq=m_w_uq, w_ukv=m_w_ukv, w_c_out=m_w_c_out, w_o=m_w_o, ln_g=m_ln_g, ln_b=m_ln_b)
    mom_v = dict(w_ada=v_w_ada, b_ada=v_b_ada, w_in=v_w_in, conv_a_w=v_conv_a_w, conv_a_b=v_conv_a_b, ln_a_g=v_ln_a_g,
                 ln_a_b=v_ln_a_b, w_a_out=v_w_a_out, conv_b_w=v_conv_b_w, w_b_out=v_w_b_out, q_norm_g=v_q_norm_g,
                 kv_norm_g=v_kv_norm_g, w_uq=v_w_uq, w_ukv=v_w_ukv, w_c_out=v_w_c_out, w_o=v_w_o, ln_g=v_ln_g, ln_b=v_ln_b)
    order = ["w_ada", "b_ada", "w_in", "conv_a_w", "conv_a_b", "ln_a_g", "ln_a_b", "w_a_out", "conv_b_w", "w_b_out",
             "q_norm_g", "kv_norm_g", "w_uq", "w_ukv", "w_c_out", "w_o", "ln_g", "ln_b"]

    n_layers = w_in.shape[0]
    s, d = x.shape[1], x.shape[2]
    alpha = float((2 * n_layers) ** 0.25)
    ix, iy, ic = lax.axis_index("x"), lax.axis_index("y"), lax.axis_index("c")
    me = 4 * ix + 2 * iy + ic
    chip = 2 * ix + iy
    xs = x[0]

    def four(tree, cast):
        return [cast(tree["w_in"]), jnp.concatenate([cast(tree[n]) for n, _ in ROWS_256], axis=-2),
                cast(tree["w_uq"]), cast(tree["w_o"])]

    shards = four(weights, lambda a: a.astype(BF16))
    landed_w = _run_exchange(_gather_layer_exchange(shards, 0), "gather_weights_first")
    landed_w = _run_exchange(_gather_pair_exchange(landed_w), "gather_weights_first_pair")
    w_pack, wq12, wkv, w_a, w_b, w_c, w_of = [], [], [], [], [], [], []

    def unpack_layer(g_in, g_256, g_uq, g_wo):
        w_pack.append(_pack_w_in(_join_cols(g_in)))
        wq12.append(_pack_w_uq(_join_cols(g_uq)))
        rows = _join_cols(g_256)
        w_a.append(rows[0:512])
        w_b.append(rows[512:1024])
        w_c.append(rows[1024:1536])
        wkv.append(_pack_w_ukv(rows[1536:1792]))
        w_of.append(g_wo.reshape(d, d))

    cw_a = conv_a_w.reshape(n_layers * CONV_K, 128)
    cw_b = conv_b_w.reshape(n_layers * SC_K, 128)
    c_rows = d // 128
    first_rows = c_rows + n_layers * (CONV_K + SC_K)
    first_pad = (-first_rows) % 8
    first = jnp.concatenate([c.reshape(c_rows, 128), cw_a, cw_b, jnp.zeros((first_pad, 128), F32)], axis=0)
    fr = first_rows + first_pad
    got = _all_gather_small(first, "gather_cond").reshape(8, fr, 128)
    c_all = got[:, :c_rows].reshape(8, d)
    conv_a_full = jnp.moveaxis(got[0::2, c_rows:c_rows + n_layers * CONV_K].reshape(4, n_layers, CONV_K, 128), 0, 2)
    conv_a_full = conv_a_full.reshape(n_layers, CONV_K, 512)
    b0 = c_rows + n_layers * CONV_K
    conv_b_full = jnp.moveaxis(got[0::2, b0:b0 + n_layers * SC_K].reshape(4, n_layers, SC_K, 128), 0, 2)
    conv_b_full = conv_b_full.reshape(n_layers, SC_K, 512)

    ada_sh, c_act = _ada_shard(c_all, w_ada.astype(BF16))
    ada_all = _all_gather_small(ada_sh.reshape(n_layers * 8, 768), "gather_ada").reshape(8, n_layers, 8, 768)
    ada_mine = lax.dynamic_index_in_dim(ada_all[0::2], me, axis=2, keepdims=False)
    ada = jnp.moveaxis(ada_mine, 0, 1).reshape(n_layers, 3 * d) + b_ada
    shift, scale, gate = ada[:, 0:d], ada[:, d:2 * d], ada[:, 2 * d:3 * d]

    tcos, tsin = _rope_tables(positions.reshape(s, 1))

    saved = []
    h = xs
    for l in range(n_layers):
        unpack_layer(*landed_w)
        u = _modulate(h, scale[l:l + 1], shift[l:l + 1])
        proj = _matmul(u, w_pack[l], COL_BLK, "proj_fwd")
        act_a, act_b, ca, cb = _mix_ab_fwd(proj, conv_a_full[l], conv_a_b[l:l + 1], ln_a_g[l:l + 1], ln_a_b[l:l + 1],
                                           conv_b_full[l])
        qf, kf, vv = _attn_prep_fwd(proj, q_norm_g[l:l + 1], kv_norm_g[l:l + 1], wq12[l], wkv[l], tcos, tsin)
        more = l + 1 < n_layers
        (o, lse), landed_w = _attention_fwd(qf, kf, vv, _gather_layer_exchange(shards, l + 1) if more else None)
        (h_next, xh, rstd), landed_w = _merge_fwd(
            act_a, act_b, o, proj, h, gate[l:l + 1], w_a[l], w_b[l], w_c[l], w_of[l], ln_g[l:l + 1], ln_b[l:l + 1], alpha,
            _gather_pair_exchange(landed_w) if more else None)
        saved.append(dict(x=h, u=u, proj=proj, act_a=act_a, act_b=act_b, ca=ca, cb=cb, qf=qf, kf=kf, v=vv, o=o, lse=lse,
                          xh=xh, rstd=rstd))
        h = h_next

    dy, loss_part = _loss_head(h, loss_target[0])
    loss = lax.psum(loss_part[0, 0], ("x", "y", "c"))

    c_idx = ic.reshape(1).astype(jnp.int32)
    reduced = None
    pending = None
    small_rows = [None] * n_layers
    for l in reversed(range(n_layers)):
        sv = saved[l]
        ((dxres, dact_a, dact_b, do, dproj, m_bf, dout_bf, dya, dyb, dyc, act_c, d_ln_g, d_ln_b, d_gate),
         rcv_rest) = _merge_bwd(
            dy, sv["xh"], sv["rstd"], sv["act_a"], sv["act_b"], sv["o"], sv["proj"], gate[l:l + 1], w_a[l], w_b[l],
            w_c[l], w_of[l], ln_g[l:l + 1], alpha, _scatter_exchange(pending[1:]) if pending else None)
        g_o = _matmul_tn(m_bf, dout_bf, d, "grad_w_o")
        g_a = _matmul_tn(sv["act_a"], dya, d, "grad_w_a_out")
        g_b = _matmul_tn(sv["act_b"], dyb, d, "grad_w_b_out")
        g_c = _matmul_tn(act_c, dyc, d, "grad_w_c_out")
        (dqf, dkf, dvv), rcv_in = _attention_bwd(sv["qf"], sv["kf"], sv["v"], sv["o"], do, sv["lse"],
                                                 _scatter_exchange(pending[:1]) if pending else None)
        if pending:
            reduced = _sum_chips(rcv_in + rcv_rest, reduced, l + 1, n_layers, c_idx)
        dproj, qn, kvn, dq12, dkv12, d_qg, d_kg = _attn_prep_bwd(
            sv["proj"], dqf, dkf, dvv, q_norm_g[l:l + 1], kv_norm_g[l:l + 1], wq12[l], wkv[l], tcos, tsin, dproj)
        g_uq = _unpack_w_uq_grad(_matmul_tn(qn, dq12, 1024, "grad_w_uq"))
        g_kv = _unpack_w_ukv_grad(_matmul_tn(kvn, dkv12, 768, "grad_w_ukv"))
        dproj, d_caw, d_cab, d_lag, d_lab, d_cbw = _mix_ab_bwd(
            sv["proj"], sv["ca"], sv["cb"], dact_a, dact_b, conv_a_full[l], ln_a_g[l:l + 1], ln_a_b[l:l + 1],
            conv_b_full[l], dproj)
        g_in_l = _unpack_w_in_grad(_matmul_tn(sv["u"], dproj, COL_BLK, "grad_w_in"))
        dy, d_scale, d_shift = _proj_bwd_input(dproj, w_pack[l], sv["x"], scale[l:l + 1], dxres)
        small_rows[l] = jnp.concatenate(
            [d_shift[0], d_scale[0], d_gate[0], d_cab[0], d_lag[0], d_lab[0], d_qg[0], d_kg[0], d_ln_g[0], d_ln_b[0],
             d_caw[:CONV_K].reshape(-1), d_cbw[:SC_K].reshape(-1)])

        partial = [_split_cols(g_in_l).astype(BF16),
                   jnp.concatenate([_split_cols(g) for g in (g_a, g_b, g_c, g_kv)], axis=1).astype(BF16),
                   _split_cols(g_uq).astype(BF16), g_o.reshape(4, d // 4, d).astype(BF16)]
        pending = _add_halves(partial, _swap_halves(partial), c_idx)
    grad_x = dy[None]
    received = _run_exchange(_scatter_exchange(pending), "grad_scatter_last")
    reduced = _sum_chips(received, reduced, 0, n_layers, c_idx)
    r_in, r_256, r_uq, r_wo = _run_exchange(_gather_pair_exchange(reduced), "grad_pair_halves")
    grads = {"w_in": r_in, "w_uq": r_uq, "w_o": r_wo}
    row0 = 0
    for n, rows in ROWS_256:
        grads[n] = r_256[:, row0:row0 + rows]
        row0 += rows

    small = jnp.stack(small_rows)
    p_small = small.shape[1]
    n_small = n_layers * p_small
    pad_small = (-n_small) % (8 * 128)
    small_flat = jnp.concatenate([small.reshape(-1), jnp.zeros((pad_small,), F32)]).reshape(-1, 128)
    sr = small_flat.shape[0]
    small_all = _all_gather_small(small_flat, "gather_small_grads").reshape(8, sr, 128)
    small_sum = _sum_devices(small_all).reshape(-1)[:n_small].reshape(n_layers, p_small)
    d_ada_all = small_all.reshape(8, -1)[:, :n_small].reshape(8, n_layers, p_small)[:, :, :3 * d]

    cuts = np.cumsum([0, 3 * d, 512, 512, 512, 384, 256, d, d, CONV_K * 512, SC_K * 512])
    pieces = [small_sum[:, cuts[k]:cuts[k + 1]] for k in range(10)]
    grads.update(zip(SMALL, pieces[:8]))
    ga_full = pieces[8].reshape(n_layers, CONV_K, 512)
    gb_full = pieces[9].reshape(n_layers, SC_K, 512)
    grads["conv_a_w"] = lax.dynamic_slice_in_dim(ga_full, chip * 128, 128, axis=2)
    grads["conv_b_w"] = lax.dynamic_slice_in_dim(gb_full, chip * 128, 128, axis=2)

    dada_sh = lax.dynamic_slice_in_dim(d_ada_all, chip * 768, 768, axis=2)
    dada16 = jnp.concatenate([jnp.moveaxis(dada_sh, 0, 1), jnp.zeros((n_layers, 8, 768), F32)], axis=1).astype(BF16)
    cact16 = jnp.concatenate([c_act, jnp.zeros_like(c_act)], axis=0).astype(BF16)
    grads["w_ada"] = _ada_weight_grad(cact16, dada16)

    def as2d(a):
        return a.reshape(-1, a.shape[-1])

    deltas, new_m, new_v = {}, {}, {}
    for n in ("w_in", "w_a_out", "w_b_out", "w_c_out", "w_uq", "w_ukv", "w_o", "w_ada"):
        dl, mn, vn = _adamw(as2d(grads[n]), as2d(weights[n]), as2d(mom_m[n]), as2d(mom_v[n]), "adamw_" + n)
        deltas[n], new_m[n], new_v[n] = dl.reshape(weights[n].shape), mn.reshape(weights[n].shape), vn.reshape(weights[n].shape)

    def pack_small(tree):
        flat = jnp.concatenate([tree[n].reshape(-1) for n in SMALL])
        pad = (-flat.shape[0]) % (8 * 128)
        return jnp.concatenate([flat, jnp.zeros((pad,), F32)]).reshape(-1, 128)

    dl_s, mn_s, vn_s = _adamw(pack_small(grads), pack_small(weights), pack_small(mom_m), pack_small(mom_v), "adamw_small")
    off = 0
    for n in SMALL:
        sz = int(np.prod(weights[n].shape))
        for tree, flat in ((deltas, dl_s), (new_m, mn_s), (new_v, vn_s)):
            tree[n] = flat.reshape(-1)[off:off + sz].reshape(weights[n].shape)
        off += sz

    return (loss, grad_x, *[grads[n] for n in order], *[deltas[n] for n in order], *[new_m[n] for n in order],
            *[new_v[n] for n in order])
```

```python
import functools

import numpy as np
import jax
import jax.numpy as jnp
from jax import lax
from jax.experimental import pallas as pl
from jax.experimental.pallas import tpu as pltpu

F32 = jnp.float32
BF16 = jnp.bfloat16
MESH = pl.DeviceIdType.MESH

HEADS = 8
QK_NOPE = 64
QK_ROPE = 32
V_HEAD = 64
CONV_K = 31
SC_K = 3
LN_EPS = 1e-5
RMS_EPS = 1e-6
ROPE_THETA = 10000.0
ATT_SCALE = (QK_NOPE + QK_ROPE) ** -0.5
ADAM_LR = 0.001
ADAM_B1 = 0.9
ADAM_B2 = 0.999
ADAM_EPS = 1e-08
ADAM_WD = 0.01
ADAM_STEP = 10

W_AB = 3584
W_G5 = 3584
W_C1 = 896
W_PACK = W_AB + W_G5 + W_C1
COL_BLK = 896
HEAD_PAD = 128
CONV_HALO = 32
NEG_BIG = -1e30

VMEM_LIMIT_MB = 56
ATT_BLOCK = 512
ROW_BLOCK = 256


def _cparams(n_grid, mb=VMEM_LIMIT_MB):
    return pltpu.CompilerParams(dimension_semantics=("arbitrary",) * n_grid, vmem_limit_bytes=mb * 2 ** 20)


def _sig(x):
    return 1.0 / (1.0 + jnp.exp(-x))


def _dsilu(x, s):
    return s * (1.0 + x * (1.0 - s))


def _dot(a, b):
    return jnp.dot(a, b, preferred_element_type=F32)


def _dot_nt(a, b):
    return lax.dot_general(a, b, (((1,), (1,)), ((), ())), preferred_element_type=F32)


def _dot_tn(a, b):
    return lax.dot_general(a, b, (((0,), (0,)), ((), ())), preferred_element_type=F32)


def _row_spec(t, w):
    return pl.BlockSpec((t, w), lambda i: (i, 0))


def _full_spec(shape):
    nd = len(shape)
    return pl.BlockSpec(shape, lambda *_: (0,) * nd)


def _my_place():
    x, y, c = lax.axis_index("x"), lax.axis_index("y"), lax.axis_index("c")
    chips = [(1 - x, y), (x, 1 - y), (1 - x, 1 - y)]
    return x, y, c, chips


def _all_gather_small(v, name):
    m_per, n = v.shape

    def body(x_ref, out_ref, send_sems, recv_sems, local_sem):
        x, y, c, chips = _my_place()
        me, sibling = (x, y, c), (x, y, 1 - c)

        def rows(px, py, pc):
            return out_ref.at[pl.ds((4 * px + 2 * py + pc) * m_per, m_per), :]

        def copy(k, block, to, src=None):
            return pltpu.make_async_remote_copy(
                src_ref=rows(*block) if src is None else src, dst_ref=rows(*block),
                send_sem=send_sems.at[k], recv_sem=recv_sems.at[k], device_id=to, device_id_type=MESH)

        mine = pltpu.make_async_copy(x_ref, rows(*me), local_sem)
        mine.start()
        first = [copy(0, me, sibling, src=x_ref)]
        first += [copy(1 + j, me, (*chip, c), src=x_ref) for j, chip in enumerate(chips)]
        for cp in first:
            cp.start()
        passed = [copy(4 + j, (*chip, c), sibling) for j, chip in enumerate(chips)]
        for j, chip in enumerate(chips):
            copy(1 + j, (*chip, c), me).wait_recv()
            passed[j].start()
        copy(0, sibling, me).wait_recv()
        for j, chip in enumerate(chips):
            copy(4 + j, (*chip, 1 - c), me).wait_recv()
        for cp in first + passed:
            cp.wait_send()
        mine.wait()

    return pl.pallas_call(
        body, name=name,
        out_shape=jax.ShapeDtypeStruct((8 * m_per, n), v.dtype),
        in_specs=[pl.BlockSpec(memory_space=pltpu.VMEM)],
        out_specs=pl.BlockSpec(memory_space=pltpu.VMEM),
        scratch_shapes=[pltpu.SemaphoreType.DMA((7,)), pltpu.SemaphoreType.DMA((7,)), pltpu.SemaphoreType.DMA],
        compiler_params=pltpu.CompilerParams(vmem_limit_bytes=VMEM_LIMIT_MB * 2 ** 20),
    )(v)


def _any_specs(n):
    return [pl.BlockSpec(memory_space=pl.ANY)] * n


def _row_half(ref, half_index, axis):
    half = ref.shape[axis] // 2
    idx = [slice(None)] * axis + [pl.ds(half_index * half, half)]
    return ref.at[tuple(idx)]


class _Exchange:
    def __init__(self, inputs, out_shapes, sems, start, finish, aliases=None):
        self.inputs, self.out_shapes, self.sems = list(inputs), list(out_shapes), list(sems)
        self.start, self.finish, self.aliases = start, finish, dict(aliases or {})


def _call(body, args, exchange=None, *, grid, in_specs, out_specs, out_shape, scratch_shapes=(),
          input_output_aliases=None, **kw):
    aliases = dict(input_output_aliases or {})
    if exchange is None:
        outs = pl.pallas_call(body, grid=grid, in_specs=list(in_specs), out_specs=list(out_specs),
                              out_shape=list(out_shape), scratch_shapes=list(scratch_shapes),
                              input_output_aliases=aliases, **kw)(*args)
        return list(outs), []
    n_in, n_out, n_scr = len(in_specs), len(out_specs), len(scratch_shapes)
    x_in, x_out = len(exchange.inputs), len(exchange.out_shapes)

    def wrapped(*refs):
        ins, xin = refs[:n_in], refs[n_in:n_in + x_in]
        o0 = n_in + x_in
        outs, xout = refs[o0:o0 + n_out], refs[o0 + n_out:o0 + n_out + x_out]
        s0 = o0 + n_out + x_out
        scr, xsem = refs[s0:s0 + n_scr], refs[s0 + n_scr:]
        ids = [pl.program_id(k) for k in range(len(grid))]
        first = functools.reduce(jnp.logical_and, [i == 0 for i in ids])
        last = functools.reduce(jnp.logical_and, [i == g - 1 for i, g in zip(ids, grid)])

        @pl.when(first)
        def _():
            exchange.start(xin, xout, xsem)

        body(*ins, *outs, *scr)

        @pl.when(last)
        def _():
            exchange.finish(xin, xout, xsem)

    aliases.update({n_in + i: n_out + o for i, o in exchange.aliases.items()})
    outs = pl.pallas_call(
        wrapped, grid=grid, in_specs=list(in_specs) + _any_specs(x_in), out_specs=list(out_specs) + _any_specs(x_out),
        out_shape=list(out_shape) + exchange.out_shapes, scratch_shapes=list(scratch_shapes) + exchange.sems,
        input_output_aliases=aliases, **kw)(*args, *exchange.inputs)
    return list(outs[:n_out]), list(outs[n_out:])


def _run_exchange(exchange, name):
    x_in, x_out = len(exchange.inputs), len(exchange.out_shapes)

    def body(*refs):
        xin, xout, xsem = refs[:x_in], refs[x_in:x_in + x_out], refs[x_in + x_out:]
        exchange.start(xin, xout, xsem)
        exchange.finish(xin, xout, xsem)

    outs = pl.pallas_call(
        body, name=name, in_specs=_any_specs(x_in), out_specs=_any_specs(x_out), out_shape=exchange.out_shapes,
        scratch_shapes=exchange.sems, input_output_aliases=exchange.aliases)(*exchange.inputs)
    return list(outs)


def _gather_layer_exchange(ws, layer, part=(0, 1), lands=None):
    n = len(ws)
    p, n_parts = part

    def copies(xin, xout, sems):
        send_sems, recv_sems, local_sems = sems
        x, y, c, chips = _my_place()
        k_me = 2 * x + y
        local, sends, recvs = [], [], []
        for a in range(n):
            half = ws[a].shape[1] // 2
            size = half // n_parts
            rows = pl.ds(c * half + p * size, size)
            src = xin[a].at[layer, rows]
            dst = xout[a].at[k_me, rows]
            local.append(pltpu.make_async_copy(src, dst, local_sems.at[a]))
            for j, (px, py) in enumerate(chips):
                sends.append(pltpu.make_async_remote_copy(
                    src_ref=src, dst_ref=dst, send_sem=send_sems.at[a, j], recv_sem=recv_sems.at[a, j],
                    device_id=(px, py, c), device_id_type=MESH))
                recvs.append(pltpu.make_async_remote_copy(
                    src_ref=src, dst_ref=xout[a].at[2 * px + py, rows], send_sem=send_sems.at[a, j],
                    recv_sem=recv_sems.at[a, j], device_id=(px, py, c), device_id_type=MESH))
        return local, sends, recvs

    def start(xin, xout, sems):
        local, sends, _ = copies(xin, xout, sems)
        for cp in local + sends:
            cp.start()

    def finish(xin, xout, sems):
        local, sends, recvs = copies(xin, xout, sems)
        for cp in recvs:
            cp.wait_recv()
        for cp in sends:
            cp.wait_send()
        for cp in local:
            cp.wait()

    return _Exchange(list(ws) + list(lands or []), [jax.ShapeDtypeStruct((4,) + w.shape[1:], w.dtype) for w in ws],
                     [pltpu.SemaphoreType.DMA((n, 3)), pltpu.SemaphoreType.DMA((n, 3)), pltpu.SemaphoreType.DMA((n,))],
                     start, finish, aliases={n + a: a for a in range(n)} if lands else None)


def _gather_pair_exchange(lands):
    n = len(lands)

    def copies(xout, sems):
        send_sems, recv_sems = sems
        x, y, c, _ = _my_place()
        sends = [pltpu.make_async_remote_copy(
            src_ref=_row_half(xout[a], c, 1), dst_ref=_row_half(xout[a], c, 1), send_sem=send_sems.at[a],
            recv_sem=recv_sems.at[a], device_id=(x, y, 1 - c), device_id_type=MESH) for a in range(n)]
        recvs = [pltpu.make_async_remote_copy(
            src_ref=_row_half(xout[a], c, 1), dst_ref=_row_half(xout[a], 1 - c, 1), send_sem=send_sems.at[a],
            recv_sem=recv_sems.at[a], device_id=(x, y, 1 - c), device_id_type=MESH) for a in range(n)]
        return sends, recvs

    def start(xin, xout, sems):
        for cp in copies(xout, sems)[0]:
            cp.start()

    def finish(xin, xout, sems):
        sends, recvs = copies(xout, sems)
        for cp in recvs:
            cp.wait_recv()
        for cp in sends:
            cp.wait_send()

    return _Exchange(lands, [jax.ShapeDtypeStruct(w.shape, w.dtype) for w in lands],
                     [pltpu.SemaphoreType.DMA((n,)), pltpu.SemaphoreType.DMA((n,))], start, finish,
                     aliases={a: a for a in range(n)})


def _swap_halves(ps):
    n = len(ps)

    def body(*refs):
        p_refs, land_refs = refs[:n], refs[n:2 * n]
        send_sems, recv_sems = refs[2 * n:]
        x, y, c, _ = _my_place()
        copies = [pltpu.make_async_remote_copy(
            src_ref=_row_half(p_refs[a], 1 - c, 1), dst_ref=land_refs[a], send_sem=send_sems.at[a],
            recv_sem=recv_sems.at[a], device_id=(x, y, 1 - c), device_id_type=MESH) for a in range(n)]
        for cp in copies:
            cp.start()
        for cp in copies:
            cp.wait()

    return pl.pallas_call(
        body, name="grad_swap_halves",
        out_shape=[jax.ShapeDtypeStruct((4, p.shape[1] // 2, p.shape[2]), p.dtype) for p in ps],
        in_specs=_any_specs(n), out_specs=_any_specs(n),
        scratch_shapes=[pltpu.SemaphoreType.DMA((n,)), pltpu.SemaphoreType.DMA((n,))],
    )(*ps)


def _add_halves(ps, lands, c_idx):
    n = len(ps)

    def body(c_ref, *refs):
        del c_ref
        for a in range(n):
            refs[2 * n + a][...] = (refs[a][...].astype(F32) + refs[n + a][...].astype(F32)).astype(BF16)

    def half_block(p):
        return (None, p.shape[1] // 2, p.shape[2])

    grid_spec = pltpu.PrefetchScalarGridSpec(
        num_scalar_prefetch=1, grid=(4,),
        in_specs=[pl.BlockSpec(half_block(p), lambda j, cr: (j, cr[0], 0)) for p in ps]
        + [pl.BlockSpec(half_block(p), lambda j, cr: (j, 0, 0)) for p in ps],
        out_specs=[pl.BlockSpec(half_block(p), lambda j, cr: (j, 0, 0)) for p in ps])
    return pl.pallas_call(
        body, name="grad_add_halves", grid_spec=grid_spec,
        out_shape=[jax.ShapeDtypeStruct((4, p.shape[1] // 2, p.shape[2]), BF16) for p in ps],
        compiler_params=_cparams(1),
    )(c_idx, *ps, *lands)


def _scatter_exchange(hs):
    n = len(hs)

    def copies(xin, xout, sems):
        send_sems, recv_sems, local_sems = sems
        x, y, c, chips = _my_place()
        k_me = 2 * x + y
        local = [pltpu.make_async_copy(xin[a].at[k_me], xout[a].at[k_me], local_sems.at[a]) for a in range(n)]
        sends, recvs = [], []
        for j, (px, py) in enumerate(chips):
            for a in range(n):
                sends.append(pltpu.make_async_remote_copy(
                    src_ref=xin[a].at[2 * px + py], dst_ref=xout[a].at[k_me], send_sem=send_sems.at[a, j],
                    recv_sem=recv_sems.at[a, j], device_id=(px, py, c), device_id_type=MESH))
                recvs.append(pltpu.make_async_remote_copy(
                    src_ref=xin[a].at[k_me], dst_ref=xout[a].at[2 * px + py], send_sem=send_sems.at[a, j],
                    recv_sem=recv_sems.at[a, j], device_id=(px, py, c), device_id_type=MESH))
        return local, sends, recvs

    def start(xin, xout, sems):
        local, sends, _ = copies(xin, xout, sems)
        for cp in local + sends:
            cp.start()

    def finish(xin, xout, sems):
        local, sends, recvs = copies(xin, xout, sems)
        for cp in recvs:
            cp.wait_recv()
        for cp in sends:
            cp.wait_send()
        for cp in local:
            cp.wait()

    return _Exchange(hs, [jax.ShapeDtypeStruct(h.shape, h.dtype) for h in hs],
                     [pltpu.SemaphoreType.DMA((n, 3)), pltpu.SemaphoreType.DMA((n, 3)), pltpu.SemaphoreType.DMA((n,))],
                     start, finish)


def _sum_chips(rcvs, accs, layer, n_layers, c_idx):
    n = len(rcvs)

    def body(c_ref, *refs):
        del c_ref
        outs = refs[len(refs) - n:]
        for a in range(n):
            r_ref = refs[a]
            acc = r_ref[0].astype(F32) + r_ref[1].astype(F32)
            acc = acc + r_ref[2].astype(F32)
            outs[a][...] = acc + r_ref[3].astype(F32)

    in_specs = [pl.BlockSpec(r.shape, lambda i, cr: (0, 0, 0)) for r in rcvs]
    args = list(rcvs)
    aliases = {}
    if accs is not None:
        in_specs += _any_specs(n)
        args += list(accs)
        aliases = {1 + n + a: a for a in range(n)}
    grid_spec = pltpu.PrefetchScalarGridSpec(
        num_scalar_prefetch=1, grid=(1,), in_specs=in_specs,
        out_specs=[pl.BlockSpec((None,) + r.shape[1:], lambda i, cr: (layer, cr[0], 0)) for r in rcvs])
    return pl.pallas_call(
        body, name="grad_sum_chips", grid_spec=grid_spec,
        out_shape=[jax.ShapeDtypeStruct((n_layers, 2 * r.shape[1], r.shape[2]), F32) for r in rcvs],
        input_output_aliases=aliases, compiler_params=_cparams(1),
    )(c_idx, *args)


def _sum_devices(g):
    _, r, n = g.shape

    def body(g_ref, o_ref):
        acc = g_ref[0]
        for d in range(1, 8):
            acc = acc + g_ref[d]
        o_ref[...] = acc

    return pl.pallas_call(
        body, name="small_grad_sum", grid=(1,),
        in_specs=[_full_spec((8, r, n))], out_specs=_full_spec((r, n)),
        out_shape=jax.ShapeDtypeStruct((r, n), F32), compiler_params=_cparams(1),
    )(g)


def _matmul(a, w, bn, name, exchange=None):
    s, k = a.shape
    n = w.shape[1]
    bs = min(2048, s)

    def body(a_ref, w_ref, o_ref):
        o_ref[...] = _dot(a_ref[...], w_ref[...])

    return _call(
        body, (a, w), exchange, name=name, grid=(n // bn, s // bs),
        in_specs=[pl.BlockSpec((bs, k), lambda j, i: (i, 0)), pl.BlockSpec((k, bn), lambda j, i: (0, j))],
        out_specs=[pl.BlockSpec((bs, bn), lambda j, i: (i, j))],
        out_shape=[jax.ShapeDtypeStruct((s, n), F32)], compiler_params=_cparams(2))


def _matmul_tn(a, b, bn, name):
    s, m = a.shape
    n = b.shape[1]
    bs = min(512, s)

    def body(a_ref, b_ref, o_ref):
        @pl.when(pl.program_id(1) == 0)
        def _():
            o_ref[...] = jnp.zeros_like(o_ref)

        o_ref[...] += _dot_tn(a_ref[...], b_ref[...])

    return pl.pallas_call(
        body, name=name, grid=(n // bn, s // bs),
        in_specs=[pl.BlockSpec((bs, m), lambda j, i: (i, 0)), pl.BlockSpec((bs, bn), lambda j, i: (i, j))],
        out_specs=pl.BlockSpec((m, bn), lambda j, i: (0, j)),
        out_shape=jax.ShapeDtypeStruct((m, n), F32), compiler_params=_cparams(2),
    )(a, b)


def _modulate(x, scale, shift):
    s, d = x.shape
    t = min(512, s)

    def body(x_ref, sc_ref, sh_ref, u_ref):
        u_ref[...] = (x_ref[...] * (1.0 + sc_ref[...]) + sh_ref[...]).astype(BF16)

    return pl.pallas_call(
        body, name="modulate", grid=(s // t,),
        in_specs=[_row_spec(t, d), _full_spec((1, d)), _full_spec((1, d))],
        out_specs=_row_spec(t, d), out_shape=jax.ShapeDtypeStruct((s, d), BF16), compiler_params=_cparams(1),
    )(x, scale, shift)


def _rope_tables(pos_col):
    s = pos_col.shape[0]
    t = min(512, s)
    inv = ROPE_THETA ** (-np.arange(0, QK_ROPE, 2, dtype=np.float32) / QK_ROPE)
    lane_freq = np.zeros((1, HEAD_PAD), np.float32)
    lane_freq[0, QK_NOPE:QK_NOPE + 16] = inv
    lane_freq[0, QK_NOPE + 16:QK_NOPE + 32] = inv
    lane_mask = np.zeros((1, HEAD_PAD), np.float32)
    lane_mask[0, QK_NOPE:QK_NOPE + QK_ROPE] = 1.0

    def body(p_ref, f_ref, m_ref, cos_ref, sin_ref):
        ang = p_ref[...].astype(F32) * f_ref[...]
        cos_ref[...] = jnp.cos(ang) * m_ref[...]
        sin_ref[...] = jnp.sin(ang) * m_ref[...]

    return pl.pallas_call(
        body, name="rope_tables", grid=(s // t,),
        in_specs=[_row_spec(t, 1), _full_spec((1, HEAD_PAD)), _full_spec((1, HEAD_PAD))],
        out_specs=[_row_spec(t, HEAD_PAD), _row_spec(t, HEAD_PAD)],
        out_shape=[jax.ShapeDtypeStruct((s, HEAD_PAD), F32)] * 2, compiler_params=_cparams(1),
    )(pos_col, jnp.asarray(lane_freq), jnp.asarray(lane_mask))


def _ada_shard(c_all, w_ada_bf):
    n_layers, d, n = w_ada_bf.shape

    def body(c_ref, w_ref, o_ref, ca_ref):
        cv = c_ref[...]
        ca = cv * _sig(cv)
        ca_ref[...] = ca
        o_ref[...] = _dot(ca.astype(BF16), w_ref[...])

    return pl.pallas_call(
        body, name="ada_shard", grid=(n_layers,),
        in_specs=[_full_spec((8, d)), pl.BlockSpec((None, d, n), lambda l: (l, 0, 0))],
        out_specs=[pl.BlockSpec((None, 8, n), lambda l: (l, 0, 0)), _full_spec((8, d))],
        out_shape=[jax.ShapeDtypeStruct((n_layers, 8, n), F32), jax.ShapeDtypeStruct((8, d), F32)],
        compiler_params=_cparams(1),
    )(c_all, w_ada_bf)


def _ada_weight_grad(cact16, dada16):
    n_layers, _, n = dada16.shape
    d = cact16.shape[1]

    def body(c_ref, g_ref, o_ref):
        o_ref[...] = _dot_tn(c_ref[...], g_ref[...])

    return pl.pallas_call(
        body, name="ada_weight_grad", grid=(n_layers,),
        in_specs=[_full_spec((16, d)), pl.BlockSpec((None, 16, n), lambda l: (l, 0, 0))],
        out_specs=pl.BlockSpec((None, d, n), lambda l: (l, 0, 0)),
        out_shape=jax.ShapeDtypeStruct((n_layers, d, n), F32), compiler_params=_cparams(1),
    )(cact16, dada16)


def _layer_norm_rows(v, g, b):
    mu = jnp.mean(v, axis=-1, keepdims=True)
    var = jnp.mean(jnp.square(v - mu), axis=-1, keepdims=True)
    rstd = lax.rsqrt(var + LN_EPS)
    xh = (v - mu) * rstd
    return xh, rstd, xh * g + b


def _layer_norm_bwd_rows(dy_hat, xh, rstd):
    return rstd * (dy_hat - jnp.mean(dy_hat, axis=-1, keepdims=True) - xh * jnp.mean(dy_hat * xh, axis=-1, keepdims=True))


def _windows(buf_ref, offsets, t):
    for r in range(8):
        group = sorted(o for o in offsets if o % 8 == r)
        if group:
            lo = group[0]
            span = buf_ref[lo:group[-1] + t, :]
            for o in group:
                yield o, span[o - lo:o - lo + t]


def _mix_ab_fwd(proj, conv_a_w, conv_a_b, ln_a_g, ln_a_b, conv_b_w, exchange=None):
    s = proj.shape[0]
    t = min(ROW_BLOCK, s)
    cw = 512
    halo = CONV_HALO

    def body(p_ref, caw_ref, cab_ref, lg_ref, lb_ref, cbw_ref, acta_ref, actb_ref, ca_ref, cb_ref, abuf, zbuf):
        @pl.when(pl.program_id(0) == 0)
        def _():
            abuf[0:halo, :] = jnp.zeros((halo, cw), F32)
            zbuf[0:8, :] = jnp.zeros((8, cw), F32)

        abuf[halo:halo + t, :] = p_ref[:, 0:512] * _sig(p_ref[:, 512:1024])
        acc = jnp.zeros((t, cw), F32)
        first = halo - (CONV_K - 1)
        for off, win in _windows(abuf, range(first, first + CONV_K), t):
            acc = acc + caw_ref[off - first:off - first + 1, :] * win
        ca = acc + cab_ref[...]
        ca_ref[...] = ca
        _, _, ln = _layer_norm_rows(ca, lg_ref[...], lb_ref[...])
        ag = p_ref[:, 1024:1536]
        acta_ref[...] = ((ln * _sig(ln)) * (ag * _sig(ag))).astype(BF16)
        abuf[0:halo, :] = abuf[t:t + halo, :]

        zbuf[8:8 + t, :] = p_ref[:, 2560:3072] * p_ref[:, 1536:2048]
        accb = jnp.zeros((t, cw), F32)
        for k in range(SC_K):
            off = 8 - (SC_K - 1) + k
            accb = accb + cbw_ref[k:k + 1, :] * zbuf[off:off + t, :]
        cb_ref[...] = accb
        bg = p_ref[:, 3072:3584]
        actb_ref[...] = ((p_ref[:, 2048:2560] * accb) * (bg * _sig(bg))).astype(BF16)
        zbuf[0:8, :] = zbuf[t:t + 8, :]

    return _call(
        body, (proj, conv_a_w, conv_a_b, ln_a_g, ln_a_b, conv_b_w), exchange, name="mix_ab_fwd", grid=(s // t,),
        in_specs=[_row_spec(t, W_AB), _full_spec((CONV_K, cw)), _full_spec((1, cw)), _full_spec((1, cw)),
                  _full_spec((1, cw)), _full_spec((SC_K, cw))],
        out_specs=[_row_spec(t, cw)] * 4,
        out_shape=[jax.ShapeDtypeStruct((s, cw), BF16)] * 2 + [jax.ShapeDtypeStruct((s, cw), F32)] * 2,
        scratch_shapes=[pltpu.VMEM((halo + t, cw), F32), pltpu.VMEM((8 + t, cw), F32)],
        compiler_params=_cparams(1))


def _mix_ab_bwd(proj, ca, cb, dact_a, dact_b, conv_a_w, ln_a_g, ln_a_b, conv_b_w, dproj):
    s = proj.shape[0]
    t = min(ROW_BLOCK, s)
    nb = s // t
    cw = 512
    halo = CONV_HALO
    per = t // halo

    def body(p_ref, ph_ref, ca_ref, cb_ref, da_ref, db_ref, caw_ref, lg_ref, lb_ref, cbw_ref, dproj_any,
             dg_ref, dcaw_ref, dcab_ref, dlg_ref, dlb_ref, dcbw_ref, abuf, dcabuf, zbuf, dcbbuf):
        del dproj_any
        i = pl.program_id(0)
        r = nb - 1 - i

        @pl.when(i == 0)
        def _():
            dcaw_ref[...] = jnp.zeros_like(dcaw_ref)
            dcab_ref[...] = jnp.zeros_like(dcab_ref)
            dlg_ref[...] = jnp.zeros_like(dlg_ref)
            dlb_ref[...] = jnp.zeros_like(dlb_ref)
            dcbw_ref[...] = jnp.zeros_like(dcbw_ref)
            dcabuf[t:t + halo, :] = jnp.zeros((halo, cw), F32)
            dcbbuf[t:t + 8, :] = jnp.zeros((8, cw), F32)

        keep = (r > 0).astype(F32)
        a1 = p_ref[:, 0:512]
        s2 = _sig(p_ref[:, 512:1024])
        abuf[0:halo, :] = (ph_ref[:, 0:512] * _sig(ph_ref[:, 512:1024])) * keep
        abuf[halo:halo + t, :] = a1 * s2
        ca_v = ca_ref[...]
        xh, rstd, ln = _layer_norm_rows(ca_v, lg_ref[...], lb_ref[...])
        s_ln = _sig(ln)
        ag = p_ref[:, 1024:1536]
        sg = _sig(ag)
        dact = da_ref[...]
        dsa = dact * (ag * sg)
        dg_ref[:, 1024:1536] = (dact * (ln * s_ln) * _dsilu(ag, sg)).astype(BF16)
        dln = dsa * _dsilu(ln, s_ln)
        dlg_ref[...] += jnp.sum(dln * xh, axis=0, keepdims=True)
        dlb_ref[...] += jnp.sum(dln, axis=0, keepdims=True)
        dca = _layer_norm_bwd_rows(dln * lg_ref[...], xh, rstd)
        dcab_ref[...] += jnp.sum(dca, axis=0, keepdims=True)
        dcabuf[0:t, :] = dca
        d_a = jnp.zeros((t, cw), F32)
        for off, win in _windows(dcabuf, range(CONV_K), t):
            k = (CONV_K - 1) - off
            d_a = d_a + caw_ref[k:k + 1, :] * win
        first = halo - (CONV_K - 1)
        for off, win in _windows(abuf, range(first, first + CONV_K), t):
            dcaw_ref[off - first:off - first + 1, :] += jnp.sum(dca * win, axis=0, keepdims=True)
        dg_ref[:, 0:512] = (d_a * s2).astype(BF16)
        dg_ref[:, 512:1024] = (d_a * a1 * s2 * (1.0 - s2)).astype(BF16)
        dcabuf[t:t + halo, :] = dcabuf[0:halo, :]

        xb = p_ref[:, 1536:2048]
        gb = p_ref[:, 2048:2560]
        gc = p_ref[:, 2560:3072]
        bg = p_ref[:, 3072:3584]
        zbuf[0:8, :] = (ph_ref[halo - 8:halo, 2560:3072] * ph_ref[halo - 8:halo, 1536:2048]) * keep
        zbuf[8:8 + t, :] = gc * xb
        sbg = _sig(bg)
        cbv = cb_ref[...]
        dactb = db_ref[...]
        dyb = dactb * (bg * sbg)
        dg_ref[:, 3072:3584] = (dactb * (gb * cbv) * _dsilu(bg, sbg)).astype(BF16)
        dg_ref[:, 2048:2560] = (dyb * cbv).astype(BF16)
        dcb = dyb * gb
        dcbbuf[0:t, :] = dcb
        dz = jnp.zeros((t, cw), F32)
        for k in range(SC_K):
            off = (SC_K - 1) - k
            dz = dz + cbw_ref[k:k + 1, :] * dcbbuf[off:off + t, :]
            src = 8 - (SC_K - 1) + k
            dcbw_ref[k:k + 1, :] += jnp.sum(dcb * zbuf[src:src + t, :], axis=0, keepdims=True)
        dg_ref[:, 2560:3072] = (dz * xb).astype(BF16)
        dg_ref[:, 1536:2048] = (dz * gc).astype(BF16)
        dcbbuf[t:t + 8, :] = dcbbuf[0:8, :]

    rev = lambda i: (nb - 1 - i, 0)
    outs = pl.pallas_call(
        body, name="mix_ab_bwd", grid=(nb,),
        in_specs=[pl.BlockSpec((t, W_AB), rev),
                  pl.BlockSpec((halo, W_AB), lambda i: (jnp.maximum((nb - 1 - i) * per - 1, 0), 0)),
                  pl.BlockSpec((t, cw), rev), pl.BlockSpec((t, cw), rev), pl.BlockSpec((t, cw), rev),
                  pl.BlockSpec((t, cw), rev),
                  _full_spec((CONV_K, cw)), _full_spec((1, cw)), _full_spec((1, cw)), _full_spec((SC_K, cw)),
                  pl.BlockSpec(memory_space=pl.ANY)],
        out_specs=[pl.BlockSpec((t, W_AB), rev), _full_spec((CONV_HALO, cw)), _full_spec((1, cw)), _full_spec((1, cw)),
                   _full_spec((1, cw)), _full_spec((8, cw))],
        out_shape=[jax.ShapeDtypeStruct(dproj.shape, BF16), jax.ShapeDtypeStruct((CONV_HALO, cw), F32),
                   jax.ShapeDtypeStruct((1, cw), F32), jax.ShapeDtypeStruct((1, cw), F32),
                   jax.ShapeDtypeStruct((1, cw), F32), jax.ShapeDtypeStruct((8, cw), F32)],
        scratch_shapes=[pltpu.VMEM((halo + t, cw), F32), pltpu.VMEM((t + halo, cw), F32),
                        pltpu.VMEM((8 + t, cw), F32), pltpu.VMEM((t + 8, cw), F32)],
        input_output_aliases={10: 0},
        compiler_params=_cparams(1),
    )(proj, proj, ca, cb, dact_a, dact_b, conv_a_w, ln_a_g, ln_a_b, conv_b_w, dproj)
    return outs


def _lane_is_nope():
    return lax.broadcasted_iota(jnp.int32, (1, HEAD_PAD), 1) < QK_NOPE


def _attn_prep_fwd(proj, q_g, kv_g, wq12, wkv, tcos, tsin):
    s = proj.shape[0]
    t = min(ROW_BLOCK, s)
    c0 = (W_AB + W_G5) // W_C1

    def body(p_ref, qg_ref, kg_ref, wq_ref, wkv_ref, cos_ref, sin_ref, qf_ref, kf_ref, v_ref):
        ql = p_ref[:, 0:384]
        qn = (ql * lax.rsqrt(jnp.mean(jnp.square(ql), axis=-1, keepdims=True) + RMS_EPS) * qg_ref[...]).astype(BF16)
        q12 = _dot(qn, wq_ref[...])
        kvl = p_ref[:, 384:640]
        kvn = (kvl * lax.rsqrt(jnp.mean(jnp.square(kvl), axis=-1, keepdims=True) + RMS_EPS) * kg_ref[...]).astype(BF16)
        kv12 = _dot(kvn, wkv_ref[...])
        tcos_v = cos_ref[...]
        tsin_v = sin_ref[...]
        tq1 = jnp.where(_lane_is_nope(), 1.0, tcos_v)
        kpe = p_ref[:, 640:768] * tcos_v + p_ref[:, 768:896] * tsin_v
        for h in range(HEADS):
            lo = h * HEAD_PAD
            qf_ref[h] = (q12[:, lo:lo + HEAD_PAD] * tq1 + q12[:, 1024 + lo:1024 + lo + HEAD_PAD] * tsin_v).astype(BF16)
            kf_ref[h] = (kv12[:, lo:lo + HEAD_PAD] + kpe).astype(BF16)
        v_ref[...] = kv12[:, 1024:1536].astype(BF16)

    return pl.pallas_call(
        body, name="attn_prep_fwd", grid=(s // t,),
        in_specs=[pl.BlockSpec((t, W_C1), lambda i: (i, c0)), _full_spec((1, 384)), _full_spec((1, 256)),
                  _full_spec((384, 2048)), _full_spec((256, 1536)), _row_spec(t, HEAD_PAD), _row_spec(t, HEAD_PAD)],
        out_specs=[pl.BlockSpec((HEADS, t, HEAD_PAD), lambda i: (0, i, 0)),
                   pl.BlockSpec((HEADS, t, HEAD_PAD), lambda i: (0, i, 0)), _row_spec(t, 512)],
        out_shape=[jax.ShapeDtypeStruct((HEADS, s, HEAD_PAD), BF16), jax.ShapeDtypeStruct((HEADS, s, HEAD_PAD), BF16),
                   jax.ShapeDtypeStruct((s, 512), BF16)],
        compiler_params=_cparams(1),
    )(proj, q_g, kv_g, wq12, wkv, tcos, tsin)


def _attn_prep_bwd(proj, dqf, dkf, dv, q_g, kv_g, wq12, wkv, tcos, tsin, dproj):
    s = proj.shape[0]
    t = min(ROW_BLOCK, s)
    c0 = (W_AB + W_G5) // W_C1

    def body(p_ref, dqf_ref, dkf_ref, dv_ref, qg_ref, kg_ref, wq_ref, wkv_ref, cos_ref, sin_ref, dproj_any,
             dg_ref, qn_ref, kvn_ref, dq12_ref, dkv12_ref, dqg_ref, dkg_ref):
        del dproj_any

        @pl.when(pl.program_id(0) == 0)
        def _():
            dqg_ref[...] = jnp.zeros_like(dqg_ref)
            dkg_ref[...] = jnp.zeros_like(dkg_ref)

        tcos_v = cos_ref[...]
        tsin_v = sin_ref[...]
        tq1 = jnp.where(_lane_is_nope(), 1.0, tcos_v)
        dkpe = jnp.zeros((t, HEAD_PAD), F32)
        for h in range(HEADS):
            lo = h * HEAD_PAD
            dq = dqf_ref[h]
            dq12_ref[:, lo:lo + HEAD_PAD] = (dq * tq1).astype(BF16)
            dq12_ref[:, 1024 + lo:1024 + lo + HEAD_PAD] = (dq * tsin_v).astype(BF16)
            dk = dkf_ref[h]
            dkv12_ref[:, lo:lo + HEAD_PAD] = dk.astype(BF16)
            dkpe = dkpe + dk
        dkv12_ref[:, 1024:1536] = dv_ref[...].astype(BF16)
        dg_ref[:, 640:768] = (dkpe * tcos_v).astype(BF16)
        dg_ref[:, 768:896] = (dkpe * tsin_v).astype(BF16)

        def rms_bwd(xl, g, dn):
            rr = lax.rsqrt(jnp.mean(jnp.square(xl), axis=-1, keepdims=True) + RMS_EPS)
            xn = xl * rr
            tt = dn * g
            return rr * (tt - xn * jnp.mean(tt * xn, axis=-1, keepdims=True)), jnp.sum(dn * xn, axis=0, keepdims=True), xn

        ql = p_ref[:, 0:384]
        dqn = _dot_nt(dq12_ref[...], wq_ref[...])
        dql, dqg, qxn = rms_bwd(ql, qg_ref[...], dqn)
        dg_ref[:, 0:384] = dql.astype(BF16)
        dqg_ref[...] += dqg
        qn_ref[...] = (qxn * qg_ref[...]).astype(BF16)
        kvl = p_ref[:, 384:640]
        dkvn = _dot_nt(dkv12_ref[...], wkv_ref[...])
        dkvl, dkg, kxn = rms_bwd(kvl, kg_ref[...], dkvn)
        dg_ref[:, 384:640] = dkvl.astype(BF16)
        dkg_ref[...] += dkg
        kvn_ref[...] = (kxn * kg_ref[...]).astype(BF16)

    return pl.pallas_call(
        body, name="attn_prep_bwd", grid=(s // t,),
        in_specs=[pl.BlockSpec((t, W_C1), lambda i: (i, c0)),
                  pl.BlockSpec((HEADS, t, HEAD_PAD), lambda i: (0, i, 0)),
                  pl.BlockSpec((HEADS, t, HEAD_PAD), lambda i: (0, i, 0)), _row_spec(t, 512),
                  _full_spec((1, 384)), _full_spec((1, 256)), _full_spec((384, 2048)), _full_spec((256, 1536)),
                  _row_spec(t, HEAD_PAD), _row_spec(t, HEAD_PAD), pl.BlockSpec(memory_space=pl.ANY)],
        out_specs=[pl.BlockSpec((t, W_C1), lambda i: (i, c0)), _row_spec(t, 384), _row_spec(t, 256),
                   _row_spec(t, 2048), _row_spec(t, 1536), _full_spec((1, 384)), _full_spec((1, 256))],
        out_shape=[jax.ShapeDtypeStruct(dproj.shape, BF16), jax.ShapeDtypeStruct((s, 384), BF16),
                   jax.ShapeDtypeStruct((s, 256), BF16), jax.ShapeDtypeStruct((s, 2048), BF16),
                   jax.ShapeDtypeStruct((s, 1536), BF16), jax.ShapeDtypeStruct((1, 384), F32),
                   jax.ShapeDtypeStruct((1, 256), F32)],
        input_output_aliases={10: 0},
        compiler_params=_cparams(1),
    )(proj, dqf, dkf, dv, q_g, kv_g, wq12, wkv, tcos, tsin, dproj)


def _causal_keep(t):
    return lax.broadcasted_iota(jnp.int32, (t, t), 0) >= lax.broadcasted_iota(jnp.int32, (t, t), 1)


def _half_select(e):
    lane = lax.broadcasted_iota(jnp.int32, (1, HEAD_PAD), 1)
    return (lane < V_HEAD) if e == 0 else (lane >= V_HEAD)


def _attention_fwd(qf, kf, v, exchange=None):
    _, s, _ = qf.shape
    t = min(ATT_BLOCK, s)
    nq = s // t

    def body(q_ref, k_ref, v_ref, o_ref, lse_ref):
        def rows_of_block(blk):
            past = blk * t
            for pair in range(HEADS // 2):
                lo = pair * HEAD_PAD
                v_diag = v_ref[past:past + t, lo:lo + HEAD_PAD]
                v_past = v_ref[0:past, lo:lo + HEAD_PAD] if blk else None
                out = None
                for e in range(2):
                    h = 2 * pair + e
                    sel = _half_select(e)
                    q = q_ref[h]
                    s_diag = jnp.where(_causal_keep(t), _dot_nt(q, k_ref[h, past:past + t, :]) * ATT_SCALE, NEG_BIG)
                    m = jnp.max(s_diag, axis=1, keepdims=True)
                    if blk:
                        s_past = _dot_nt(q, k_ref[h, 0:past, :]) * ATT_SCALE
                        m = jnp.maximum(m, jnp.max(s_past, axis=1, keepdims=True))
                    p_diag = jnp.exp(s_diag - m)
                    norm = jnp.sum(p_diag, axis=1, keepdims=True)
                    pv = _dot(p_diag.astype(BF16), jnp.where(sel, v_diag, jnp.zeros_like(v_diag)))
                    if blk:
                        p_past = jnp.exp(s_past - m)
                        norm = norm + jnp.sum(p_past, axis=1, keepdims=True)
                        pv = pv + _dot(p_past.astype(BF16), jnp.where(sel, v_past, jnp.zeros_like(v_past)))
                    pv = pv * (1.0 / norm)
                    out = pv if out is None else out + pv
                    lse_ref[h] = m + jnp.log(norm)
                o_ref[:, lo:lo + HEAD_PAD] = out

        for blk in range(nq):
            pl.when(pl.program_id(0) == blk)(functools.partial(rows_of_block, blk))

    return _call(
        body, (qf, kf, v), exchange, name="attention_fwd", grid=(nq,),
        in_specs=[pl.BlockSpec((HEADS, t, HEAD_PAD), lambda i: (0, i, 0)), _full_spec((HEADS, s, HEAD_PAD)),
                  _full_spec((s, 512))],
        out_specs=[_row_spec(t, 512), pl.BlockSpec((HEADS, t, 1), lambda i: (0, i, 0))],
        out_shape=[jax.ShapeDtypeStruct((s, 512), F32), jax.ShapeDtypeStruct((HEADS, s, 1), F32)],
        compiler_params=_cparams(1))


def _attention_bwd(qf, kf, v, o, do, lse, exchange=None):
    _, s, _ = qf.shape
    t = min(ATT_BLOCK, s)
    nq = s // t

    def body(q_ref, k_ref, v_ref, o_ref, do_ref, lse_ref, dq_ref, dk_ref, dv_ref, dk_acc, dv_acc):
        kj = pl.program_id(0)
        qi = pl.program_id(1)

        @pl.when((kj == 0) & (qi == 0))
        def _():
            dq_ref[...] = jnp.zeros_like(dq_ref)

        @pl.when(qi == kj)
        def _():
            dk_acc[...] = jnp.zeros_like(dk_acc)
            dv_acc[...] = jnp.zeros_like(dv_acc)

        def step(diag):
            rows = pl.ds(pl.multiple_of(qi * t, t), t)
            for h in range(HEADS):
                pair, e = divmod(h, 2)
                sel = _half_select(e)
                lo = pair * HEAD_PAD
                q = q_ref[h]
                k = k_ref[h]
                sc = _dot_nt(q, k) * ATT_SCALE
                if diag:
                    sc = jnp.where(_causal_keep(t), sc, NEG_BIG)
                p = jnp.exp(sc - lse_ref[h])
                do_pair = do_ref[:, lo:lo + HEAD_PAD]
                do_e = jnp.where(sel, do_pair, 0.0)
                do_b = do_e.astype(BF16)
                v2 = v_ref[:, lo:lo + HEAD_PAD]
                dv_acc[:, lo:lo + HEAD_PAD] += _dot_tn(p.astype(BF16), do_b)
                dp = _dot_nt(do_b, jnp.where(sel, v2, jnp.zeros_like(v2)))
                delta = jnp.sum(do_e * o_ref[:, lo:lo + HEAD_PAD], axis=1, keepdims=True)
                ds = (p * (dp - delta) * ATT_SCALE).astype(BF16)
                dq_ref[h, rows, :] += _dot(ds, k)
                dk_acc[h] += _dot_tn(ds, q)

        @pl.when(qi > kj)
        def _():
            step(False)

        @pl.when(qi == kj)
        def _():
            step(True)

        @pl.when(qi == nq - 1)
        def _():
            dk_ref[...] = dk_acc[...]
            dv_ref[...] = dv_acc[...]

    qmap = lambda j, i: (0, jnp.maximum(i, j), 0)
    return _call(
        body, (qf, kf, v, o, do, lse), exchange, name="attention_bwd", grid=(nq, nq),
        in_specs=[pl.BlockSpec((HEADS, t, HEAD_PAD), qmap),
                  pl.BlockSpec((HEADS, t, HEAD_PAD), lambda j, i: (0, j, 0)),
                  pl.BlockSpec((t, 512), lambda j, i: (j, 0)),
                  pl.BlockSpec((t, 512), lambda j, i: (jnp.maximum(i, j), 0)),
                  pl.BlockSpec((t, 512), lambda j, i: (jnp.maximum(i, j), 0)),
                  pl.BlockSpec((HEADS, t, 1), qmap)],
        out_specs=[_full_spec((HEADS, s, HEAD_PAD)),
                   pl.BlockSpec((HEADS, t, HEAD_PAD), lambda j, i: (0, j, 0)),
                   pl.BlockSpec((t, 512), lambda j, i: (j, 0))],
        out_shape=[jax.ShapeDtypeStruct((HEADS, s, HEAD_PAD), F32), jax.ShapeDtypeStruct((HEADS, s, HEAD_PAD), F32),
                   jax.ShapeDtypeStruct((s, 512), F32)],
        scratch_shapes=[pltpu.VMEM((HEADS, t, HEAD_PAD), F32), pltpu.VMEM((t, 512), F32)],
        compiler_params=_cparams(2))


def _merge_fwd(act_a, act_b, o, proj, x, gate, w_a, w_b, w_c, w_o, ln_g, ln_b, alpha, exchange=None):
    s, d = x.shape
    t = min(ROW_BLOCK, s)

    def body(aa_ref, ab_ref, o_ref, p_ref, x_ref, gate_ref, wa_ref, wb_ref, wc_ref, wo_ref, lg_ref, lb_ref,
             xn_ref, xh_ref, rstd_ref):
        cg = p_ref[:, 0:512]
        act_c = (o_ref[...] * (cg * _sig(cg))).astype(BF16)
        m = _sig(p_ref[:, 512:1536]) * _dot(aa_ref[...], wa_ref[...])
        m = m + _sig(p_ref[:, 1536:2560]) * _dot(ab_ref[...], wb_ref[...])
        m = m + _sig(p_ref[:, 2560:3584]) * _dot(act_c, wc_ref[...])
        out = _dot(m.astype(BF16), wo_ref[...])
        z = alpha * x_ref[...] + gate_ref[...] * out
        xh, rstd, y = _layer_norm_rows(z, lg_ref[...], lb_ref[...])
        xn_ref[...] = y
        xh_ref[...] = xh
        rstd_ref[...] = rstd

    return _call(
        body, (act_a, act_b, o, proj, x, gate, w_a, w_b, w_c, w_o, ln_g, ln_b), exchange, name="merge_fwd", grid=(s // t,),
        in_specs=[_row_spec(t, 512), _row_spec(t, 512), _row_spec(t, 512),
                  pl.BlockSpec((t, W_G5), lambda i: (i, 1)), _row_spec(t, d), _full_spec((1, d)),
                  _full_spec((512, d)), _full_spec((512, d)), _full_spec((512, d)), _full_spec((d, d)),
                  _full_spec((1, d)), _full_spec((1, d))],
        out_specs=[_row_spec(t, d), _row_spec(t, d), _row_spec(t, 1)],
        out_shape=[jax.ShapeDtypeStruct((s, d), F32), jax.ShapeDtypeStruct((s, d), F32), jax.ShapeDtypeStruct((s, 1), F32)],
        compiler_params=_cparams(1))


def _merge_bwd(dy, xh, rstd, act_a, act_b, o, proj, gate, w_a, w_b, w_c, w_o, ln_g, alpha, exchange=None):
    s, d = dy.shape
    t = min(ROW_BLOCK, s)

    def body(dy_ref, xh_ref, rstd_ref, aa_ref, ab_ref, o_ref, p_ref, gate_ref, wa_ref, wb_ref, wc_ref, wo_ref, lg_ref,
             dxr_ref, dacta_ref, dactb_ref, do_ref, dg_ref, m_ref, dout_ref, dya_ref, dyb_ref, dyc_ref, actc_ref,
             dlg_ref, dlb_ref, dgate_ref):
        @pl.when(pl.program_id(0) == 0)
        def _():
            dlg_ref[...] = jnp.zeros_like(dlg_ref)
            dlb_ref[...] = jnp.zeros_like(dlb_ref)
            dgate_ref[...] = jnp.zeros_like(dgate_ref)

        dyv = dy_ref[...]
        xhv = xh_ref[...]
        dlg_ref[...] += jnp.sum(dyv * xhv, axis=0, keepdims=True)
        dlb_ref[...] += jnp.sum(dyv, axis=0, keepdims=True)
        dz = _layer_norm_bwd_rows(dyv * lg_ref[...], xhv, rstd_ref[...])
        dxr_ref[...] = alpha * dz

        cg = p_ref[:, 0:512]
        scg = _sig(cg)
        silu_cg = cg * scg
        ov = o_ref[...]
        act_c = (ov * silu_cg).astype(BF16)
        actc_ref[...] = act_c
        ya = _dot(aa_ref[...], wa_ref[...])
        yb = _dot(ab_ref[...], wb_ref[...])
        yc = _dot(act_c, wc_ref[...])
        ga = _sig(p_ref[:, 512:1536])
        gb = _sig(p_ref[:, 1536:2560])
        gc = _sig(p_ref[:, 2560:3584])
        mb = (ga * ya + gb * yb + gc * yc).astype(BF16)
        m_ref[...] = mb
        out = _dot(mb, wo_ref[...])
        dgate_ref[...] += jnp.sum(dz * out, axis=0, keepdims=True)
        dout = (gate_ref[...] * dz).astype(BF16)
        dout_ref[...] = dout
        dm = _dot_nt(dout, wo_ref[...])

        dya = (dm * ga).astype(BF16)
        dya_ref[...] = dya
        dg_ref[:, 512:1536] = (dm * ya * ga * (1.0 - ga)).astype(BF16)
        dacta_ref[...] = _dot_nt(dya, wa_ref[...])
        dyb = (dm * gb).astype(BF16)
        dyb_ref[...] = dyb
        dg_ref[:, 1536:2560] = (dm * yb * gb * (1.0 - gb)).astype(BF16)
        dactb_ref[...] = _dot_nt(dyb, wb_ref[...])
        dyc = (dm * gc).astype(BF16)
        dyc_ref[...] = dyc
        dg_ref[:, 2560:3584] = (dm * yc * gc * (1.0 - gc)).astype(BF16)
        dactc = _dot_nt(dyc, wc_ref[...])
        do_ref[...] = dactc * silu_cg
        dg_ref[:, 0:512] = (dactc * ov * _dsilu(cg, scg)).astype(BF16)

    return _call(
        body, (dy, xh, rstd, act_a, act_b, o, proj, gate, w_a, w_b, w_c, w_o, ln_g), exchange, name="merge_bwd", grid=(s // t,),
        in_specs=[_row_spec(t, d), _row_spec(t, d), _row_spec(t, 1), _row_spec(t, 512), _row_spec(t, 512),
                  _row_spec(t, 512), pl.BlockSpec((t, W_G5), lambda i: (i, 1)), _full_spec((1, d)),
                  _full_spec((512, d)), _full_spec((512, d)), _full_spec((512, d)), _full_spec((d, d)),
                  _full_spec((1, d))],
        out_specs=[_row_spec(t, d), _row_spec(t, 512), _row_spec(t, 512), _row_spec(t, 512),
                   pl.BlockSpec((t, W_G5), lambda i: (i, 1)),
                   _row_spec(t, d), _row_spec(t, d), _row_spec(t, d), _row_spec(t, d), _row_spec(t, d),
                   _row_spec(t, 512), _full_spec((1, d)), _full_spec((1, d)), _full_spec((1, d))],
        out_shape=[jax.ShapeDtypeStruct((s, d), F32), jax.ShapeDtypeStruct((s, 512), F32),
                   jax.ShapeDtypeStruct((s, 512), F32), jax.ShapeDtypeStruct((s, 512), F32),
                   jax.ShapeDtypeStruct((s, W_PACK), BF16),
                   jax.ShapeDtypeStruct((s, d), BF16), jax.ShapeDtypeStruct((s, d), BF16),
                   jax.ShapeDtypeStruct((s, d), BF16), jax.ShapeDtypeStruct((s, d), BF16),
                   jax.ShapeDtypeStruct((s, d), BF16), jax.ShapeDtypeStruct((s, 512), BF16),
                   jax.ShapeDtypeStruct((1, d), F32), jax.ShapeDtypeStruct((1, d), F32),
                   jax.ShapeDtypeStruct((1, d), F32)],
        compiler_params=_cparams(1))


def _proj_bwd_input(dproj, w_pack, x, scale, dxres):
    s, d = x.shape
    t = min(1024, s)
    nk = W_PACK // COL_BLK

    def body(dg_ref, w_ref, x_ref, sc_ref, dxr_ref, dx_ref, dsc_ref, dsh_ref, acc):
        i = pl.program_id(0)
        k = pl.program_id(1)

        @pl.when((i == 0) & (k == 0))
        def _():
            dsc_ref[...] = jnp.zeros_like(dsc_ref)
            dsh_ref[...] = jnp.zeros_like(dsh_ref)

        @pl.when(k == 0)
        def _():
            acc[...] = jnp.zeros_like(acc)

        acc[...] += _dot_nt(dg_ref[...], w_ref[...])

        @pl.when(k == nk - 1)
        def _():
            du = acc[...]
            dx_ref[...] = du * (1.0 + sc_ref[...]) + dxr_ref[...]
            dsc_ref[...] += jnp.sum(du * x_ref[...], axis=0, keepdims=True)
            dsh_ref[...] += jnp.sum(du, axis=0, keepdims=True)

    return pl.pallas_call(
        body, name="proj_bwd_input", grid=(s // t, nk),
        in_specs=[pl.BlockSpec((t, COL_BLK), lambda i, k: (i, k)), pl.BlockSpec((d, COL_BLK), lambda i, k: (0, k)),
                  pl.BlockSpec((t, d), lambda i, k: (i, 0)), _full_spec((1, d)), pl.BlockSpec((t, d), lambda i, k: (i, 0))],
        out_specs=[pl.BlockSpec((t, d), lambda i, k: (i, 0)), _full_spec((1, d)), _full_spec((1, d))],
        out_shape=[jax.ShapeDtypeStruct((s, d), F32), jax.ShapeDtypeStruct((1, d), F32), jax.ShapeDtypeStruct((1, d), F32)],
        scratch_shapes=[pltpu.VMEM((t, d), F32)],
        compiler_params=_cparams(2),
    )(dproj, w_pack, x, scale, dxres)


def _loss_head(y, target):
    s, d = y.shape
    t = min(512, s)

    def body(y_ref, t_ref, dy_ref, loss_ref):
        @pl.when(pl.program_id(0) == 0)
        def _():
            loss_ref[...] = jnp.zeros_like(loss_ref)

        err = y_ref[...] - t_ref[...]
        dy_ref[...] = err / d
        part = 0.5 * jnp.sum(jnp.mean(jnp.square(err), axis=-1, keepdims=True), axis=0, keepdims=True)
        loss_ref[...] += jnp.broadcast_to(part, loss_ref.shape)

    return pl.pallas_call(
        body, name="loss_head", grid=(s // t,),
        in_specs=[_row_spec(t, d), _row_spec(t, d)],
        out_specs=[_row_spec(t, d), _full_spec((1, 128))],
        out_shape=[jax.ShapeDtypeStruct((s, d), F32), jax.ShapeDtypeStruct((1, 128), F32)],
        compiler_params=_cparams(1),
    )(y, target)


def _adamw(g, w, m, v, name):
    r, n = g.shape
    rb = r
    if r * n * 4 > 2 ** 21:
        for cand in (512, 256, 128, 64, 32, 16, 8):
            if r % cand == 0 and cand * n * 4 <= 2 ** 21:
                rb = cand
                break

    def body(g_ref, w_ref, m_ref, v_ref, d_ref, mo_ref, vo_ref):
        gv = g_ref[...]
        mn = ADAM_B1 * m_ref[...] + (1.0 - ADAM_B1) * gv
        vn = ADAM_B2 * v_ref[...] + (1.0 - ADAM_B2) * jnp.square(gv)
        m_hat = mn / (1.0 - ADAM_B1 ** ADAM_STEP)
        v_hat = vn / (1.0 - ADAM_B2 ** ADAM_STEP)
        d_ref[...] = -ADAM_LR * (m_hat / (jnp.sqrt(v_hat) + ADAM_EPS) + ADAM_WD * w_ref[...])
        mo_ref[...] = mn
        vo_ref[...] = vn

    spec = pl.BlockSpec((rb, n), lambda i: (i, 0))
    return pl.pallas_call(
        body, name=name, grid=(r // rb,),
        in_specs=[spec] * 4, out_specs=[spec] * 3,
        out_shape=[jax.ShapeDtypeStruct((r, n), F32)] * 3, compiler_params=_cparams(1),
    )(g, w, m, v)


SMALL = ("b_ada", "conv_a_b", "ln_a_g", "ln_a_b", "q_norm_g", "kv_norm_g", "ln_g", "ln_b", "conv_a_w", "conv_b_w")
ROWS_256 = (("w_a_out", 512), ("w_b_out", 512), ("w_c_out", 512), ("w_ukv", 256))


def _rot_cols(w):
    return jnp.concatenate([-w[..., 16:], w[..., :16]], axis=-1)


def _unrot_cols(g):
    return jnp.concatenate([g[..., 16:], -g[..., :16]], axis=-1)


def _pack_w_in(w):
    z = lambda n: jnp.zeros(w.shape[:-1] + (n,), w.dtype)
    wk = w[..., 4224:4256]
    return jnp.concatenate([w[..., 0:3584], w[..., 4256:7840], w[..., 3584:4224],
                            z(64), wk, z(32), z(64), _rot_cols(wk), z(32)], axis=-1)


def _unpack_w_in_grad(g):
    c1 = W_AB + W_G5
    gk = g[..., c1 + 640 + 64:c1 + 640 + 96] + _unrot_cols(g[..., c1 + 768 + 64:c1 + 768 + 96])
    return jnp.concatenate([g[..., 0:3584], g[..., c1:c1 + 640], gk, g[..., W_AB:W_AB + W_G5]], axis=-1)


def _pack_w_uq(w):
    lead = w.shape[:-1]
    wh = w.reshape(lead + (HEADS, QK_NOPE + QK_ROPE))
    nope, rope = wh[..., :QK_NOPE], wh[..., QK_NOPE:]
    z32 = jnp.zeros_like(rope)
    q1 = jnp.concatenate([nope, rope, z32], axis=-1).reshape(lead + (HEADS * HEAD_PAD,))
    q2 = jnp.concatenate([jnp.zeros_like(nope), _rot_cols(rope), z32], axis=-1).reshape(lead + (HEADS * HEAD_PAD,))
    return jnp.concatenate([q1, q2], axis=-1)


def _unpack_w_uq_grad(g):
    lead = g.shape[:-1]
    g1 = g[..., :1024].reshape(lead + (HEADS, HEAD_PAD))
    g2 = g[..., 1024:].reshape(lead + (HEADS, HEAD_PAD))
    rope = g1[..., QK_NOPE:QK_NOPE + QK_ROPE] + _unrot_cols(g2[..., QK_NOPE:QK_NOPE + QK_ROPE])
    return jnp.concatenate([g1[..., :QK_NOPE], rope], axis=-1).reshape(lead + (HEADS * (QK_NOPE + QK_ROPE),))


def _pack_w_ukv(w):
    lead = w.shape[:-1]
    wh = w.reshape(lead + (HEADS, QK_NOPE + V_HEAD))
    kn, vv = wh[..., :QK_NOPE], wh[..., QK_NOPE:]
    k1 = jnp.concatenate([kn, jnp.zeros_like(kn)], axis=-1).reshape(lead + (HEADS * HEAD_PAD,))
    return jnp.concatenate([k1, vv.reshape(lead + (HEADS * V_HEAD,))], axis=-1)


def _unpack_w_ukv_grad(g):
    lead = g.shape[:-1]
    gk = g[..., :1024].reshape(lead + (HEADS, HEAD_PAD))[..., :QK_NOPE]
    gv = g[..., 1024:].reshape(lead + (HEADS, V_HEAD))
    return jnp.concatenate([gk, gv], axis=-1).reshape(lead + (HEADS * (QK_NOPE + V_HEAD),))


def _join_cols(g4):
    return jnp.concatenate([g4[j] for j in range(4)], axis=-1)


def _split_cols(w):
    c4 = w.shape[-1] // 4
    return jnp.stack([w[:, j * c4:(j + 1) * c4] for j in range(4)])


def kernel(x, c, positions, w_ada, b_ada, w_in, conv_a_w, conv_a_b, ln_a_g, ln_a_b, w_a_out, conv_b_w, w_b_out, q_norm_g, kv_norm_g, w_uq, w_ukv, w_c_out, w_o, ln_g, ln_b, loss_target, m_w_ada, m_b_ada, m_w_in, m_conv_a_w, m_conv_a_b, m_ln_a_g, m_ln_a_b, m_w_a_out, m_conv_b_w, m_w_b_out, m_q_norm_g, m_kv_norm_g, m_w_uq, m_w_ukv, m_w_c_out, m_w_o, m_ln_g, m_ln_b, v_w_ada, v_b_ada, v_w_in, v_conv_a_w, v_conv_a_b, v_ln_a_g, v_ln_a_b, v_w_a_out, v_conv_b_w, v_w_b_out, v_q_norm_g, v_kv_norm_g, v_w_uq, v_w_ukv, v_w_c_out, v_w_o, v_ln_g, v_ln_b):
    weights = dict(w_ada=w_ada, b_ada=b_ada, w_in=w_in, conv_a_w=conv_a_w, conv_a_b=conv_a_b, ln_a_g=ln_a_g, ln_a_b=ln_a_b,
                   w_a_out=w_a_out, conv_b_w=conv_b_w, w_b_out=w_b_out, q_norm_g=q_norm_g, kv_norm_g=kv_norm_g, w_uq=w_uq,
                   w_ukv=w_ukv, w_c_out=w_c_out, w_o=w_o, ln_g=ln_g, ln_b=ln_b)
    mom_m = dict(w_ada=m_w_ada, b_ada=m_b_ada, w_in=m_w_in, conv_a_w=m_conv_a_w, conv_a_b=m_conv_a_b, ln_a_g=m_ln_a_g,
                 ln_a_b=m_ln_a_b, w_a_out=m_w_a_out, conv_b_w=m_conv_b_w, w_b_out=m_w_b_out, q_norm_g=m_q_norm_g,
                 kv_norm_g=m_kv_norm_g, w_uq=m_w_uq, w_ukv=m_w_ukv, w_c_out=m_w_c_out, w_o=m_w_o, ln_g=m_ln_g, ln_b=m_ln_b)
    mom_v = dict(w_ada=v_w_ada, b_ada=v_b_ada, w_in=v_w_in, conv_a_w=v_conv_a_w, conv_a_b=v_conv_a_b, ln_a_g=v_ln_a_g,
                 ln_a_b=v_ln_a_b, w_a_out=v_w_a_out, conv_b_w=v_conv_b_w, w_b_out=v_w_b_out, q_norm_g=v_q_norm_g,
                 kv_norm_g=v_kv_norm_g, w_uq=v_w_uq, w_ukv=v_w_ukv, w_c_out=v_w_c_out, w_o=v_w_o, ln_g=v_ln_g, ln_b=v_ln_b)
    order = ["w_ada", "b_ada", "w_in", "conv_a_w", "conv_a_b", "ln_a_g", "ln_a_b", "w_a_out", "conv_b_w", "w_b_out",
             "q_norm_g", "kv_norm_g", "w_uq", "w_ukv", "w_c_out", "w_o", "ln_g", "ln_b"]

    n_layers = w_in.shape[0]
    s, d = x.shape[1], x.shape[2]
    alpha = float((2 * n_layers) ** 0.25)
    ix, iy, ic = lax.axis_index("x"), lax.axis_index("y"), lax.axis_index("c")
    me = 4 * ix + 2 * iy + ic
    chip = 2 * ix + iy
    xs = x[0]

    def four(tree, cast):
        return [cast(tree["w_in"]), jnp.concatenate([cast(tree[n]) for n, _ in ROWS_256], axis=-2),
                cast(tree["w_uq"]), cast(tree["w_o"])]

    shards = four(weights, lambda a: a.astype(BF16))
    landed_w = _run_exchange(_gather_layer_exchange(shards, 0), "gather_weights_first")
    landed_w = _run_exchange(_gather_pair_exchange(landed_w), "gather_weights_first_pair")
    w_pack, wq12, wkv, w_a, w_b, w_c, w_of = [], [], [], [], [], [], []

    def unpack_layer(g_in, g_256, g_uq, g_wo):
        w_pack.append(_pack_w_in(_join_cols(g_in)))
        wq12.append(_pack_w_uq(_join_cols(g_uq)))
        rows = _join_cols(g_256)
        w_a.append(rows[0:512])
        w_b.append(rows[512:1024])
        w_c.append(rows[1024:1536])
        wkv.append(_pack_w_ukv(rows[1536:1792]))
        w_of.append(g_wo.reshape(d, d))

    cw_a = conv_a_w.reshape(n_layers * CONV_K, 128)
    cw_b = conv_b_w.reshape(n_layers * SC_K, 128)
    c_rows = d // 128
    first_rows = c_rows + n_layers * (CONV_K + SC_K)
    first_pad = (-first_rows) % 8
    first = jnp.concatenate([c.reshape(c_rows, 128), cw_a, cw_b, jnp.zeros((first_pad, 128), F32)], axis=0)
    fr = first_rows + first_pad
    got = _all_gather_small(first, "gather_cond").reshape(8, fr, 128)
    c_all = got[:, :c_rows].reshape(8, d)
    conv_a_full = jnp.moveaxis(got[0::2, c_rows:c_rows + n_layers * CONV_K].reshape(4, n_layers, CONV_K, 128), 0, 2)
    conv_a_full = conv_a_full.reshape(n_layers, CONV_K, 512)
    b0 = c_rows + n_layers * CONV_K
    conv_b_full = jnp.moveaxis(got[0::2, b0:b0 + n_layers * SC_K].reshape(4, n_layers, SC_K, 128), 0, 2)
    conv_b_full = conv_b_full.reshape(n_layers, SC_K, 512)

    ada_sh, c_act = _ada_shard(c_all, w_ada.astype(BF16))
    ada_all = _all_gather_small(ada_sh.reshape(n_layers * 8, 768), "gather_ada").reshape(8, n_layers, 8, 768)
    ada_mine = lax.dynamic_index_in_dim(ada_all[0::2], me, axis=2, keepdims=False)
    ada = jnp.moveaxis(ada_mine, 0, 1).reshape(n_layers, 3 * d) + b_ada
    shift, scale, gate = ada[:, 0:d], ada[:, d:2 * d], ada[:, 2 * d:3 * d]

    tcos, tsin = _rope_tables(positions.reshape(s, 1))

    saved = []
    h = xs
    for l in range(n_layers):
        unpack_layer(*landed_w)
        u = _modulate(h, scale[l:l + 1], shift[l:l + 1])
        more = l + 1 < n_layers
        (proj,), lw_in = _matmul(u, w_pack[l], COL_BLK, "proj_fwd",
                                 _gather_layer_exchange(shards[:1], l + 1, (0, 2)) if more else None)
        (act_a, act_b, ca, cb), lw_rest = _mix_ab_fwd(
            proj, conv_a_full[l], conv_a_b[l:l + 1], ln_a_g[l:l + 1], ln_a_b[l:l + 1], conv_b_full[l],
            _gather_layer_exchange(shards[1:], l + 1) if more else None)
        qf, kf, vv = _attn_prep_fwd(proj, q_norm_g[l:l + 1], kv_norm_g[l:l + 1], wq12[l], wkv[l], tcos, tsin)
        (o, lse), lw_in = _attention_fwd(
            qf, kf, vv, _gather_layer_exchange(shards[:1], l + 1, (1, 2), lw_in) if more else None)
        (h_next, xh, rstd), landed_w = _merge_fwd(
            act_a, act_b, o, proj, h, gate[l:l + 1], w_a[l], w_b[l], w_c[l], w_of[l], ln_g[l:l + 1], ln_b[l:l + 1], alpha,
            _gather_pair_exchange(lw_in + lw_rest) if more else None)
        saved.append(dict(x=h, u=u, proj=proj, act_a=act_a, act_b=act_b, ca=ca, cb=cb, qf=qf, kf=kf, v=vv, o=o, lse=lse,
                          xh=xh, rstd=rstd))
        h = h_next

    dy, loss_part = _loss_head(h, loss_target[0])
    loss = lax.psum(loss_part[0, 0], ("x", "y", "c"))

    c_idx = ic.reshape(1).astype(jnp.int32)
    reduced = None
    pending = None
    small_rows = [None] * n_layers
    for l in reversed(range(n_layers)):
        sv = saved[l]
        ((dxres, dact_a, dact_b, do, dproj, m_bf, dout_bf, dya, dyb, dyc, act_c, d_ln_g, d_ln_b, d_gate),
         rcv_rest) = _merge_bwd(
            dy, sv["xh"], sv["rstd"], sv["act_a"], sv["act_b"], sv["o"], sv["proj"], gate[l:l + 1], w_a[l], w_b[l],
            w_c[l], w_of[l], ln_g[l:l + 1], alpha, _scatter_exchange(pending[1:]) if pending else None)
        g_o = _matmul_tn(m_bf, dout_bf, d, "grad_w_o")
        g_a = _matmul_tn(sv["act_a"], dya, d, "grad_w_a_out")
        g_b = _matmul_tn(sv["act_b"], dyb, d, "grad_w_b_out")
        g_c = _matmul_tn(act_c, dyc, d, "grad_w_c_out")
        (dqf, dkf, dvv), rcv_in = _attention_bwd(sv["qf"], sv["kf"], sv["v"], sv["o"], do, sv["lse"],
                                                 _scatter_exchange(pending[:1]) if pending else None)
        if pending:
            reduced = _sum_chips(rcv_in + rcv_rest, reduced, l + 1, n_layers, c_idx)
        dproj, qn, kvn, dq12, dkv12, d_qg, d_kg = _attn_prep_bwd(
            sv["proj"], dqf, dkf, dvv, q_norm_g[l:l + 1], kv_norm_g[l:l + 1], wq12[l], wkv[l], tcos, tsin, dproj)
        g_uq = _unpack_w_uq_grad(_matmul_tn(qn, dq12, 1024, "grad_w_uq"))
        g_kv = _unpack_w_ukv_grad(_matmul_tn(kvn, dkv12, 768, "grad_w_ukv"))
        dproj, d_caw, d_cab, d_lag, d_lab, d_cbw = _mix_ab_bwd(
            sv["proj"], sv["ca"], sv["cb"], dact_a, dact_b, conv_a_full[l], ln_a_g[l:l + 1], ln_a_b[l:l + 1],
            conv_b_full[l], dproj)
        g_in_l = _unpack_w_in_grad(_matmul_tn(sv["u"], dproj, COL_BLK, "grad_w_in"))
        dy, d_scale, d_shift = _proj_bwd_input(dproj, w_pack[l], sv["x"], scale[l:l + 1], dxres)
        small_rows[l] = jnp.concatenate(
            [d_shift[0], d_scale[0], d_gate[0], d_cab[0], d_lag[0], d_lab[0], d_qg[0], d_kg[0], d_ln_g[0], d_ln_b[0],
             d_caw[:CONV_K].reshape(-1), d_cbw[:SC_K].reshape(-1)])

        partial = [_split_cols(g_in_l).astype(BF16),
                   jnp.concatenate([_split_cols(g) for g in (g_a, g_b, g_c, g_kv)], axis=1).astype(BF16),
                   _split_cols(g_uq).astype(BF16), g_o.reshape(4, d // 4, d).astype(BF16)]
        pending = _add_halves(partial, _swap_halves(partial), c_idx)
    grad_x = dy[None]
    received = _run_exchange(_scatter_exchange(pending), "grad_scatter_last")
    reduced = _sum_chips(received, reduced, 0, n_layers, c_idx)
    r_in, r_256, r_uq, r_wo = _run_exchange(_gather_pair_exchange(reduced), "grad_pair_halves")
    grads = {"w_in": r_in, "w_uq": r_uq, "w_o": r_wo}
    row0 = 0
    for n, rows in ROWS_256:
        grads[n] = r_256[:, row0:row0 + rows]
        row0 += rows

    small = jnp.stack(small_rows)
    p_small = small.shape[1]
    n_small = n_layers * p_small
    pad_small = (-n_small) % (8 * 128)
    small_flat = jnp.concatenate([small.reshape(-1), jnp.zeros((pad_small,), F32)]).reshape(-1, 128)
    sr = small_flat.shape[0]
    small_all = _all_gather_small(small_flat, "gather_small_grads").reshape(8, sr, 128)
    small_sum = _sum_devices(small_all).reshape(-1)[:n_small].reshape(n_layers, p_small)
    d_ada_all = small_all.reshape(8, -1)[:, :n_small].reshape(8, n_layers, p_small)[:, :, :3 * d]

    cuts = np.cumsum([0, 3 * d, 512, 512, 512, 384, 256, d, d, CONV_K * 512, SC_K * 512])
    pieces = [small_sum[:, cuts[k]:cuts[k + 1]] for k in range(10)]
    grads.update(zip(SMALL, pieces[:8]))
    ga_full = pieces[8].reshape(n_layers, CONV_K, 512)
    gb_full = pieces[9].reshape(n_layers, SC_K, 512)
    grads["conv_a_w"] = lax.dynamic_slice_in_dim(ga_full, chip * 128, 128, axis=2)
    grads["conv_b_w"] = lax.dynamic_slice_in_dim(gb_full, chip * 128, 128, axis=2)

    dada_sh = lax.dynamic_slice_in_dim(d_ada_all, chip * 768, 768, axis=2)
    dada16 = jnp.concatenate([jnp.moveaxis(dada_sh, 0, 1), jnp.zeros((n_layers, 8, 768), F32)], axis=1).astype(BF16)
    cact16 = jnp.concatenate([c_act, jnp.zeros_like(c_act)], axis=0).astype(BF16)
    grads["w_ada"] = _ada_weight_grad(cact16, dada16)

    def as2d(a):
        return a.reshape(-1, a.shape[-1])

    deltas, new_m, new_v = {}, {}, {}
    for n in ("w_in", "w_a_out", "w_b_out", "w_c_out", "w_uq", "w_ukv", "w_o", "w_ada"):
        dl, mn, vn = _adamw(as2d(grads[n]), as2d(weights[n]), as2d(mom_m[n]), as2d(mom_v[n]), "adamw_" + n)
        deltas[n], new_m[n], new_v[n] = dl.reshape(weights[n].shape), mn.reshape(weights[n].shape), vn.reshape(weights[n].shape)

    def pack_small(tree):
        flat = jnp.concatenate([tree[n].reshape(-1) for n in SMALL])
        pad = (-flat.shape[0]) % (8 * 128)
        return jnp.concatenate([flat, jnp.zeros((pad,), F32)]).reshape(-1, 128)

    dl_s, mn_s, vn_s = _adamw(pack_small(grads), pack_small(weights), pack_small(mom_m), pack_small(mom_v), "adamw_small")
    off = 0
    for n in SMALL:
        sz = int(np.prod(weights[n].shape))
        for tree, flat in ((deltas, dl_s), (new_m, mn_s), (new_v, vn_s)):
            tree[n] = flat.reshape(-1)[off:off + sz].reshape(weights[n].shape)
        off += sz

    return (loss, grad_x, *[grads[n] for n in order], *[deltas[n] for n in order], *[new_m[n] for n in order],
            *[new_v[n] for n in order])
```

```python
import functools

import numpy as np
import jax
import jax.numpy as jnp
from jax import lax
from jax.experimental import pallas as pl
from jax.experimental.pallas import tpu as pltpu

F32 = jnp.float32
BF16 = jnp.bfloat16
MESH = pl.DeviceIdType.MESH

HEADS = 8
QK_NOPE = 64
QK_ROPE = 32
V_HEAD = 64
CONV_K = 31
SC_K = 3
LN_EPS = 1e-5
RMS_EPS = 1e-6
ROPE_THETA = 10000.0
ATT_SCALE = (QK_NOPE + QK_ROPE) ** -0.5
ADAM_LR = 0.001
ADAM_B1 = 0.9
ADAM_B2 = 0.999
ADAM_EPS = 1e-08
ADAM_WD = 0.01
ADAM_STEP = 10

W_AB = 3584
W_G5 = 3584
W_C1 = 896
W_PACK = W_AB + W_G5 + W_C1
COL_BLK = 896
HEAD_PAD = 128
CONV_HALO = 32
NEG_BIG = -1e30

VMEM_LIMIT_MB = 56
ATT_BLOCK = 512
ROW_BLOCK = 256


def _cparams(n_grid, mb=VMEM_LIMIT_MB):
    return pltpu.CompilerParams(dimension_semantics=("arbitrary",) * n_grid, vmem_limit_bytes=mb * 2 ** 20)


def _sig(x):
    return 1.0 / (1.0 + jnp.exp(-x))


def _dsilu(x, s):
    return s * (1.0 + x * (1.0 - s))


def _dot(a, b):
    return jnp.dot(a, b, preferred_element_type=F32)


def _dot_nt(a, b):
    return lax.dot_general(a, b, (((1,), (1,)), ((), ())), preferred_element_type=F32)


def _dot_tn(a, b):
    return lax.dot_general(a, b, (((0,), (0,)), ((), ())), preferred_element_type=F32)


def _row_spec(t, w):
    return pl.BlockSpec((t, w), lambda i: (i, 0))


def _full_spec(shape):
    nd = len(shape)
    return pl.BlockSpec(shape, lambda *_: (0,) * nd)


def _my_place():
    x, y, c = lax.axis_index("x"), lax.axis_index("y"), lax.axis_index("c")
    chips = [(1 - x, y), (x, 1 - y), (1 - x, 1 - y)]
    return x, y, c, chips


def _all_gather_small(v, name):
    m_per, n = v.shape

    def body(x_ref, out_ref, send_sems, recv_sems, local_sem):
        x, y, c, chips = _my_place()
        me, sibling = (x, y, c), (x, y, 1 - c)

        def rows(px, py, pc):
            return out_ref.at[pl.ds((4 * px + 2 * py + pc) * m_per, m_per), :]

        def copy(k, block, to, src=None):
            return pltpu.make_async_remote_copy(
                src_ref=rows(*block) if src is None else src, dst_ref=rows(*block),
                send_sem=send_sems.at[k], recv_sem=recv_sems.at[k], device_id=to, device_id_type=MESH)

        mine = pltpu.make_async_copy(x_ref, rows(*me), local_sem)
        mine.start()
        first = [copy(0, me, sibling, src=x_ref)]
        first += [copy(1 + j, me, (*chip, c), src=x_ref) for j, chip in enumerate(chips)]
        for cp in first:
            cp.start()
        passed = [copy(4 + j, (*chip, c), sibling) for j, chip in enumerate(chips)]
        for j, chip in enumerate(chips):
            copy(1 + j, (*chip, c), me).wait_recv()
            passed[j].start()
        copy(0, sibling, me).wait_recv()
        for j, chip in enumerate(chips):
            copy(4 + j, (*chip, 1 - c), me).wait_recv()
        for cp in first + passed:
            cp.wait_send()
        mine.wait()

    return pl.pallas_call(
        body, name=name,
        out_shape=jax.ShapeDtypeStruct((8 * m_per, n), v.dtype),
        in_specs=[pl.BlockSpec(memory_space=pltpu.VMEM)],
        out_specs=pl.BlockSpec(memory_space=pltpu.VMEM),
        scratch_shapes=[pltpu.SemaphoreType.DMA((7,)), pltpu.SemaphoreType.DMA((7,)), pltpu.SemaphoreType.DMA],
        compiler_params=pltpu.CompilerParams(vmem_limit_bytes=VMEM_LIMIT_MB * 2 ** 20),
    )(v)


def _any_specs(n):
    return [pl.BlockSpec(memory_space=pl.ANY)] * n


def _splits_rows(shape):
    return shape[-2] % 32 == 0


def _window(shape, c, p=0, n_parts=1):
    r, n = shape[-2], shape[-1]
    if _splits_rows(shape):
        size = r // 2 // n_parts
        return (pl.ds(c * (r // 2) + p * size, size), slice(None))
    size = n // 2 // n_parts
    return (slice(None), pl.ds(c * (n // 2) + p * size, size))


def _half_shape(shape):
    r, n = shape[-2], shape[-1]
    return (r // 2, n) if _splits_rows(shape) else (r, n // 2)


def _half_block_index(shape, c):
    return (c, 0) if _splits_rows(shape) else (0, c)


class _Exchange:
    def __init__(self, inputs, out_shapes, sems, start, finish, aliases=None):
        self.inputs, self.out_shapes, self.sems = list(inputs), list(out_shapes), list(sems)
        self.start, self.finish, self.aliases = start, finish, dict(aliases or {})


def _call(body, args, exchange=None, *, grid, in_specs, out_specs, out_shape, scratch_shapes=(),
          input_output_aliases=None, **kw):
    aliases = dict(input_output_aliases or {})
    if exchange is None:
        outs = pl.pallas_call(body, grid=grid, in_specs=list(in_specs), out_specs=list(out_specs),
                              out_shape=list(out_shape), scratch_shapes=list(scratch_shapes),
                              input_output_aliases=aliases, **kw)(*args)
        return list(outs), []
    n_in, n_out, n_scr = len(in_specs), len(out_specs), len(scratch_shapes)
    x_in, x_out = len(exchange.inputs), len(exchange.out_shapes)

    def wrapped(*refs):
        ins, xin = refs[:n_in], refs[n_in:n_in + x_in]
        o0 = n_in + x_in
        outs, xout = refs[o0:o0 + n_out], refs[o0 + n_out:o0 + n_out + x_out]
        s0 = o0 + n_out + x_out
        scr, xsem = refs[s0:s0 + n_scr], refs[s0 + n_scr:]
        ids = [pl.program_id(k) for k in range(len(grid))]
        first = functools.reduce(jnp.logical_and, [i == 0 for i in ids])
        last = functools.reduce(jnp.logical_and, [i == g - 1 for i, g in zip(ids, grid)])

        @pl.when(first)
        def _():
            exchange.start(xin, xout, xsem)

        body(*ins, *outs, *scr)

        @pl.when(last)
        def _():
            exchange.finish(xin, xout, xsem)

    aliases.update({n_in + i: n_out + o for i, o in exchange.aliases.items()})
    outs = pl.pallas_call(
        wrapped, grid=grid, in_specs=list(in_specs) + _any_specs(x_in), out_specs=list(out_specs) + _any_specs(x_out),
        out_shape=list(out_shape) + exchange.out_shapes, scratch_shapes=list(scratch_shapes) + exchange.sems,
        input_output_aliases=aliases, **kw)(*args, *exchange.inputs)
    return list(outs[:n_out]), list(outs[n_out:])


def _run_exchange(exchange, name):
    x_in, x_out = len(exchange.inputs), len(exchange.out_shapes)

    def body(*refs):
        xin, xout, xsem = refs[:x_in], refs[x_in:x_in + x_out], refs[x_in + x_out:]
        exchange.start(xin, xout, xsem)
        exchange.finish(xin, xout, xsem)

    outs = pl.pallas_call(
        body, name=name, in_specs=_any_specs(x_in), out_specs=_any_specs(x_out), out_shape=exchange.out_shapes,
        scratch_shapes=exchange.sems, input_output_aliases=exchange.aliases)(*exchange.inputs)
    return list(outs)


def _gather_layer_exchange(ws, layer, part=(0, 1), lands=None):
    n = len(ws)
    p, n_parts = part

    def copies(xin, xout, sems):
        send_sems, recv_sems, local_sems = sems
        x, y, c, chips = _my_place()
        k_me = 2 * x + y
        local, sends, recvs = [], [], []
        for a in range(n):
            win = _window(ws[a].shape, c, p, n_parts)
            src = xin[a].at[(layer,) + win]
            dst = xout[a].at[(k_me,) + win]
            local.append(pltpu.make_async_copy(src, dst, local_sems.at[a]))
            for j, (px, py) in enumerate(chips):
                sends.append(pltpu.make_async_remote_copy(
                    src_ref=src, dst_ref=dst, send_sem=send_sems.at[a, j], recv_sem=recv_sems.at[a, j],
                    device_id=(px, py, c), device_id_type=MESH))
                recvs.append(pltpu.make_async_remote_copy(
                    src_ref=src, dst_ref=xout[a].at[(2 * px + py,) + win], send_sem=send_sems.at[a, j],
                    recv_sem=recv_sems.at[a, j], device_id=(px, py, c), device_id_type=MESH))
        return local, sends, recvs

    def start(xin, xout, sems):
        local, sends, _ = copies(xin, xout, sems)
        for cp in local + sends:
            cp.start()

    def finish(xin, xout, sems):
        local, sends, recvs = copies(xin, xout, sems)
        for cp in recvs:
            cp.wait_recv()
        for cp in sends:
            cp.wait_send()
        for cp in local:
            cp.wait()

    return _Exchange(list(ws) + list(lands or []), [jax.ShapeDtypeStruct((4,) + w.shape[1:], w.dtype) for w in ws],
                     [pltpu.SemaphoreType.DMA((n, 3)), pltpu.SemaphoreType.DMA((n, 3)), pltpu.SemaphoreType.DMA((n,))],
                     start, finish, aliases={n + a: a for a in range(n)} if lands else None)


def _gather_pair_exchange(lands):
    n = len(lands)

    def copies(xout, sems):
        send_sems, recv_sems = sems
        x, y, c, _ = _my_place()
        def half(a, core):
            return xout[a].at[(slice(None),) + _window(lands[a].shape, core)]

        sends = [pltpu.make_async_remote_copy(
            src_ref=half(a, c), dst_ref=half(a, c), send_sem=send_sems.at[a],
            recv_sem=recv_sems.at[a], device_id=(x, y, 1 - c), device_id_type=MESH) for a in range(n)]
        recvs = [pltpu.make_async_remote_copy(
            src_ref=half(a, c), dst_ref=half(a, 1 - c), send_sem=send_sems.at[a],
            recv_sem=recv_sems.at[a], device_id=(x, y, 1 - c), device_id_type=MESH) for a in range(n)]
        return sends, recvs

    def start(xin, xout, sems):
        for cp in copies(xout, sems)[0]:
            cp.start()

    def finish(xin, xout, sems):
        sends, recvs = copies(xout, sems)
        for cp in recvs:
            cp.wait_recv()
        for cp in sends:
            cp.wait_send()

    return _Exchange(lands, [jax.ShapeDtypeStruct(w.shape, w.dtype) for w in lands],
                     [pltpu.SemaphoreType.DMA((n,)), pltpu.SemaphoreType.DMA((n,))], start, finish,
                     aliases={a: a for a in range(n)})


def _swap_halves(ps):
    n = len(ps)

    def body(*refs):
        p_refs, land_refs = refs[:n], refs[n:2 * n]
        send_sems, recv_sems = refs[2 * n:]
        x, y, c, _ = _my_place()
        copies = [pltpu.make_async_remote_copy(
            src_ref=p_refs[a].at[(slice(None),) + _window(ps[a].shape, 1 - c)], dst_ref=land_refs[a],
            send_sem=send_sems.at[a], recv_sem=recv_sems.at[a], device_id=(x, y, 1 - c), device_id_type=MESH)
            for a in range(n)]
        for cp in copies:
            cp.start()
        for cp in copies:
            cp.wait()

    return pl.pallas_call(
        body, name="grad_swap_halves",
        out_shape=[jax.ShapeDtypeStruct((4,) + _half_shape(p.shape), p.dtype) for p in ps],
        in_specs=_any_specs(n), out_specs=_any_specs(n),
        scratch_shapes=[pltpu.SemaphoreType.DMA((n,)), pltpu.SemaphoreType.DMA((n,))],
    )(*ps)


def _add_halves(ps, lands, c_idx):
    n = len(ps)

    def body(c_ref, *refs):
        del c_ref
        for a in range(n):
            refs[2 * n + a][...] = (refs[a][...].astype(F32) + refs[n + a][...].astype(F32)).astype(BF16)

    def half_block(p):
        return (None,) + _half_shape(p.shape)

    def own_half(p):
        return lambda j, cr: (j,) + _half_block_index(p.shape, cr[0])

    grid_spec = pltpu.PrefetchScalarGridSpec(
        num_scalar_prefetch=1, grid=(4,),
        in_specs=[pl.BlockSpec(half_block(p), own_half(p)) for p in ps]
        + [pl.BlockSpec(half_block(p), lambda j, cr: (j, 0, 0)) for p in ps],
        out_specs=[pl.BlockSpec(half_block(p), lambda j, cr: (j, 0, 0)) for p in ps])
    return pl.pallas_call(
        body, name="grad_add_halves", grid_spec=grid_spec,
        out_shape=[jax.ShapeDtypeStruct((4,) + _half_shape(p.shape), BF16) for p in ps],
        compiler_params=_cparams(1),
    )(c_idx, *ps, *lands)


def _scatter_exchange(hs):
    n = len(hs)

    def copies(xin, xout, sems):
        send_sems, recv_sems, local_sems = sems
        x, y, c, chips = _my_place()
        k_me = 2 * x + y
        local = [pltpu.make_async_copy(xin[a].at[k_me], xout[a].at[k_me], local_sems.at[a]) for a in range(n)]
        sends, recvs = [], []
        for j, (px, py) in enumerate(chips):
            for a in range(n):
                sends.append(pltpu.make_async_remote_copy(
                    src_ref=xin[a].at[2 * px + py], dst_ref=xout[a].at[k_me], send_sem=send_sems.at[a, j],
                    recv_sem=recv_sems.at[a, j], device_id=(px, py, c), device_id_type=MESH))
                recvs.append(pltpu.make_async_remote_copy(
                    src_ref=xin[a].at[k_me], dst_ref=xout[a].at[2 * px + py], send_sem=send_sems.at[a, j],
                    recv_sem=recv_sems.at[a, j], device_id=(px, py, c), device_id_type=MESH))
        return local, sends, recvs

    def start(xin, xout, sems):
        local, sends, _ = copies(xin, xout, sems)
        for cp in local + sends:
            cp.start()

    def finish(xin, xout, sems):
        local, sends, recvs = copies(xin, xout, sems)
        for cp in recvs:
            cp.wait_recv()
        for cp in sends:
            cp.wait_send()
        for cp in local:
            cp.wait()

    return _Exchange(hs, [jax.ShapeDtypeStruct(h.shape, h.dtype) for h in hs],
                     [pltpu.SemaphoreType.DMA((n, 3)), pltpu.SemaphoreType.DMA((n, 3)), pltpu.SemaphoreType.DMA((n,))],
                     start, finish)


def _sum_chips(rcvs, shapes, accs, layer, n_layers, c_idx):
    n = len(rcvs)

    def body(c_ref, *refs):
        del c_ref
        outs = refs[len(refs) - n:]
        for a in range(n):
            r_ref = refs[a]
            acc = r_ref[0].astype(F32) + r_ref[1].astype(F32)
            acc = acc + r_ref[2].astype(F32)
            outs[a][...] = acc + r_ref[3].astype(F32)

    in_specs = [pl.BlockSpec(r.shape, lambda i, cr: (0, 0, 0)) for r in rcvs]
    args = list(rcvs)
    aliases = {}
    if accs is not None:
        in_specs += _any_specs(n)
        args += list(accs)
        aliases = {1 + n + a: a for a in range(n)}
    def own_half(shape):
        return lambda i, cr: (layer,) + _half_block_index(shape, cr[0])

    grid_spec = pltpu.PrefetchScalarGridSpec(
        num_scalar_prefetch=1, grid=(1,), in_specs=in_specs,
        out_specs=[pl.BlockSpec((None,) + r.shape[1:], own_half(sh)) for r, sh in zip(rcvs, shapes)])
    return pl.pallas_call(
        body, name="grad_sum_chips", grid_spec=grid_spec,
        out_shape=[jax.ShapeDtypeStruct((n_layers,) + tuple(sh), F32) for sh in shapes],
        input_output_aliases=aliases, compiler_params=_cparams(1),
    )(c_idx, *args)


def _sum_devices(g):
    _, r, n = g.shape

    def body(g_ref, o_ref):
        acc = g_ref[0]
        for d in range(1, 8):
            acc = acc + g_ref[d]
        o_ref[...] = acc

    return pl.pallas_call(
        body, name="small_grad_sum", grid=(1,),
        in_specs=[_full_spec((8, r, n))], out_specs=_full_spec((r, n)),
        out_shape=jax.ShapeDtypeStruct((r, n), F32), compiler_params=_cparams(1),
    )(g)


def _matmul_nt(a, wt, bn, name, exchange=None):
    s, k = a.shape
    n = wt.shape[0]
    bs = min(2048, s)

    def body(a_ref, w_ref, o_ref):
        o_ref[...] = _dot_nt(a_ref[...], w_ref[...])

    return _call(
        body, (a, wt), exchange, name=name, grid=(n // bn, s // bs),
        in_specs=[pl.BlockSpec((bs, k), lambda j, i: (i, 0)), pl.BlockSpec((bn, k), lambda j, i: (j, 0))],
        out_specs=[pl.BlockSpec((bs, bn), lambda j, i: (i, j))],
        out_shape=[jax.ShapeDtypeStruct((s, n), F32)], compiler_params=_cparams(2))


def _matmul_tn(a, b, bn, name, bm=None):
    s, m = a.shape
    n = b.shape[1]
    bs = min(512, s)
    bm = m if bm is None else bm

    def body(a_ref, b_ref, o_ref):
        @pl.when(pl.program_id(2) == 0)
        def _():
            o_ref[...] = jnp.zeros_like(o_ref)

        o_ref[...] += _dot_tn(a_ref[...], b_ref[...])

    return pl.pallas_call(
        body, name=name, grid=(m // bm, n // bn, s // bs),
        in_specs=[pl.BlockSpec((bs, bm), lambda h, j, i: (i, h)), pl.BlockSpec((bs, bn), lambda h, j, i: (i, j))],
        out_specs=pl.BlockSpec((bm, bn), lambda h, j, i: (h, j)),
        out_shape=jax.ShapeDtypeStruct((m, n), F32), compiler_params=_cparams(3),
    )(a, b)


def _modulate(x, scale, shift):
    s, d = x.shape
    t = min(512, s)

    def body(x_ref, sc_ref, sh_ref, u_ref):
        u_ref[...] = (x_ref[...] * (1.0 + sc_ref[...]) + sh_ref[...]).astype(BF16)

    return pl.pallas_call(
        body, name="modulate", grid=(s // t,),
        in_specs=[_row_spec(t, d), _full_spec((1, d)), _full_spec((1, d))],
        out_specs=_row_spec(t, d), out_shape=jax.ShapeDtypeStruct((s, d), BF16), compiler_params=_cparams(1),
    )(x, scale, shift)


def _rope_tables(pos_col):
    s = pos_col.shape[0]
    t = min(512, s)
    inv = ROPE_THETA ** (-np.arange(0, QK_ROPE, 2, dtype=np.float32) / QK_ROPE)
    lane_freq = np.zeros((1, HEAD_PAD), np.float32)
    lane_freq[0, QK_NOPE:QK_NOPE + 16] = inv
    lane_freq[0, QK_NOPE + 16:QK_NOPE + 32] = inv
    lane_mask = np.zeros((1, HEAD_PAD), np.float32)
    lane_mask[0, QK_NOPE:QK_NOPE + QK_ROPE] = 1.0

    def body(p_ref, f_ref, m_ref, cos_ref, sin_ref):
        ang = p_ref[...].astype(F32) * f_ref[...]
        cos_ref[...] = jnp.cos(ang) * m_ref[...]
        sin_ref[...] = jnp.sin(ang) * m_ref[...]

    return pl.pallas_call(
        body, name="rope_tables", grid=(s // t,),
        in_specs=[_row_spec(t, 1), _full_spec((1, HEAD_PAD)), _full_spec((1, HEAD_PAD))],
        out_specs=[_row_spec(t, HEAD_PAD), _row_spec(t, HEAD_PAD)],
        out_shape=[jax.ShapeDtypeStruct((s, HEAD_PAD), F32)] * 2, compiler_params=_cparams(1),
    )(pos_col, jnp.asarray(lane_freq), jnp.asarray(lane_mask))


def _ada_shard(c_all, w_ada_bf):
    n_layers, d, n = w_ada_bf.shape

    def body(c_ref, w_ref, o_ref, ca_ref):
        cv = c_ref[...]
        ca = cv * _sig(cv)
        ca_ref[...] = ca
        o_ref[...] = _dot(ca.astype(BF16), w_ref[...])

    return pl.pallas_call(
        body, name="ada_shard", grid=(n_layers,),
        in_specs=[_full_spec((8, d)), pl.BlockSpec((None, d, n), lambda l: (l, 0, 0))],
        out_specs=[pl.BlockSpec((None, 8, n), lambda l: (l, 0, 0)), _full_spec((8, d))],
        out_shape=[jax.ShapeDtypeStruct((n_layers, 8, n), F32), jax.ShapeDtypeStruct((8, d), F32)],
        compiler_params=_cparams(1),
    )(c_all, w_ada_bf)


def _ada_weight_grad(cact16, dada16):
    n_layers, _, n = dada16.shape
    d = cact16.shape[1]

    def body(c_ref, g_ref, o_ref):
        o_ref[...] = _dot_tn(c_ref[...], g_ref[...])

    return pl.pallas_call(
        body, name="ada_weight_grad", grid=(n_layers,),
        in_specs=[_full_spec((16, d)), pl.BlockSpec((None, 16, n), lambda l: (l, 0, 0))],
        out_specs=pl.BlockSpec((None, d, n), lambda l: (l, 0, 0)),
        out_shape=jax.ShapeDtypeStruct((n_layers, d, n), F32), compiler_params=_cparams(1),
    )(cact16, dada16)


def _layer_norm_rows(v, g, b):
    mu = jnp.mean(v, axis=-1, keepdims=True)
    var = jnp.mean(jnp.square(v - mu), axis=-1, keepdims=True)
    rstd = lax.rsqrt(var + LN_EPS)
    xh = (v - mu) * rstd
    return xh, rstd, xh * g + b


def _layer_norm_bwd_rows(dy_hat, xh, rstd):
    return rstd * (dy_hat - jnp.mean(dy_hat, axis=-1, keepdims=True) - xh * jnp.mean(dy_hat * xh, axis=-1, keepdims=True))


def _windows(buf_ref, offsets, t, shift_ref):
    for r in range(8):
        group = sorted(o for o in offsets if o % 8 == r)
        if not group:
            continue
        lo = group[0]
        if r == 0:
            for o in group:
                yield o, buf_ref[o:o + t, :]
        else:
            length = group[-1] - lo + t
            shift_ref[0:length, :] = buf_ref[lo:lo + length, :]
            for o in group:
                yield o, shift_ref[o - lo:o - lo + t, :]


def _mix_ab_fwd(proj, conv_a_w, conv_a_b, ln_a_g, ln_a_b, conv_b_w, exchange=None):
    s = proj.shape[0]
    t = min(ROW_BLOCK, s)
    cw = 512
    halo = CONV_HALO

    def body(p_ref, caw_ref, cab_ref, lg_ref, lb_ref, cbw_ref, acta_ref, actb_ref, ca_ref, cb_ref, abuf, zbuf, shifted):
        @pl.when(pl.program_id(0) == 0)
        def _():
            abuf[0:halo, :] = jnp.zeros((halo, cw), F32)
            zbuf[0:8, :] = jnp.zeros((8, cw), F32)

        abuf[halo:halo + t, :] = p_ref[:, 0:512] * _sig(p_ref[:, 512:1024])
        acc = jnp.zeros((t, cw), F32)
        first = halo - (CONV_K - 1)
        for off, win in _windows(abuf, range(first, first + CONV_K), t, shifted):
            acc = acc + caw_ref[off - first:off - first + 1, :] * win
        ca = acc + cab_ref[...]
        ca_ref[...] = ca
        _, _, ln = _layer_norm_rows(ca, lg_ref[...], lb_ref[...])
        ag = p_ref[:, 1024:1536]
        acta_ref[...] = ((ln * _sig(ln)) * (ag * _sig(ag))).astype(BF16)
        abuf[0:halo, :] = abuf[t:t + halo, :]

        zbuf[8:8 + t, :] = p_ref[:, 2560:3072] * p_ref[:, 1536:2048]
        accb = jnp.zeros((t, cw), F32)
        for k in range(SC_K):
            off = 8 - (SC_K - 1) + k
            accb = accb + cbw_ref[k:k + 1, :] * zbuf[off:off + t, :]
        cb_ref[...] = accb
        bg = p_ref[:, 3072:3584]
        actb_ref[...] = ((p_ref[:, 2048:2560] * accb) * (bg * _sig(bg))).astype(BF16)
        zbuf[0:8, :] = zbuf[t:t + 8, :]

    return _call(
        body, (proj, conv_a_w, conv_a_b, ln_a_g, ln_a_b, conv_b_w), exchange, name="mix_ab_fwd", grid=(s // t,),
        in_specs=[_row_spec(t, W_AB), _full_spec((CONV_K, cw)), _full_spec((1, cw)), _full_spec((1, cw)),
                  _full_spec((1, cw)), _full_spec((SC_K, cw))],
        out_specs=[_row_spec(t, cw)] * 4,
        out_shape=[jax.ShapeDtypeStruct((s, cw), BF16)] * 2 + [jax.ShapeDtypeStruct((s, cw), F32)] * 2,
        scratch_shapes=[pltpu.VMEM((halo + t, cw), F32), pltpu.VMEM((8 + t, cw), F32), pltpu.VMEM((halo + t, cw), F32)],
        compiler_params=_cparams(1))


def _mix_ab_bwd(proj, ca, cb, dact_a, dact_b, conv_a_w, ln_a_g, ln_a_b, conv_b_w, dproj):
    s = proj.shape[0]
    t = min(ROW_BLOCK, s)
    nb = s // t
    cw = 512
    halo = CONV_HALO
    per = t // halo

    def body(p_ref, ph_ref, ca_ref, cb_ref, da_ref, db_ref, caw_ref, lg_ref, lb_ref, cbw_ref, dproj_any,
             dg_ref, dcaw_ref, dcab_ref, dlg_ref, dlb_ref, dcbw_ref, abuf, dcabuf, zbuf, dcbbuf, shifted):
        del dproj_any
        i = pl.program_id(0)
        r = nb - 1 - i

        @pl.when(i == 0)
        def _():
            dcaw_ref[...] = jnp.zeros_like(dcaw_ref)
            dcab_ref[...] = jnp.zeros_like(dcab_ref)
            dlg_ref[...] = jnp.zeros_like(dlg_ref)
            dlb_ref[...] = jnp.zeros_like(dlb_ref)
            dcbw_ref[...] = jnp.zeros_like(dcbw_ref)
            dcabuf[t:t + halo, :] = jnp.zeros((halo, cw), F32)
            dcbbuf[t:t + 8, :] = jnp.zeros((8, cw), F32)

        keep = (r > 0).astype(F32)
        a1 = p_ref[:, 0:512]
        s2 = _sig(p_ref[:, 512:1024])
        abuf[0:halo, :] = (ph_ref[:, 0:512] * _sig(ph_ref[:, 512:1024])) * keep
        abuf[halo:halo + t, :] = a1 * s2
        ca_v = ca_ref[...]
        xh, rstd, ln = _layer_norm_rows(ca_v, lg_ref[...], lb_ref[...])
        s_ln = _sig(ln)
        ag = p_ref[:, 1024:1536]
        sg = _sig(ag)
        dact = da_ref[...]
        dsa = dact * (ag * sg)
        dg_ref[:, 1024:1536] = (dact * (ln * s_ln) * _dsilu(ag, sg)).astype(BF16)
        dln = dsa * _dsilu(ln, s_ln)
        dlg_ref[...] += jnp.sum(dln * xh, axis=0, keepdims=True)
        dlb_ref[...] += jnp.sum(dln, axis=0, keepdims=True)
        dca = _layer_norm_bwd_rows(dln * lg_ref[...], xh, rstd)
        dcab_ref[...] += jnp.sum(dca, axis=0, keepdims=True)
        dcabuf[0:t, :] = dca
        d_a = jnp.zeros((t, cw), F32)
        for off, win in _windows(dcabuf, range(CONV_K), t, shifted):
            k = (CONV_K - 1) - off
            d_a = d_a + caw_ref[k:k + 1, :] * win
        first = halo - (CONV_K - 1)
        for off, win in _windows(abuf, range(first, first + CONV_K), t, shifted):
            dcaw_ref[off - first:off - first + 1, :] += jnp.sum(dca * win, axis=0, keepdims=True)
        dg_ref[:, 0:512] = (d_a * s2).astype(BF16)
        dg_ref[:, 512:1024] = (d_a * a1 * s2 * (1.0 - s2)).astype(BF16)
        dcabuf[t:t + halo, :] = dcabuf[0:halo, :]

        xb = p_ref[:, 1536:2048]
        gb = p_ref[:, 2048:2560]
        gc = p_ref[:, 2560:3072]
        bg = p_ref[:, 3072:3584]
        zbuf[0:8, :] = (ph_ref[halo - 8:halo, 2560:3072] * ph_ref[halo - 8:halo, 1536:2048]) * keep
        zbuf[8:8 + t, :] = gc * xb
        sbg = _sig(bg)
        cbv = cb_ref[...]
        dactb = db_ref[...]
        dyb = dactb * (bg * sbg)
        dg_ref[:, 3072:3584] = (dactb * (gb * cbv) * _dsilu(bg, sbg)).astype(BF16)
        dg_ref[:, 2048:2560] = (dyb * cbv).astype(BF16)
        dcb = dyb * gb
        dcbbuf[0:t, :] = dcb
        dz = jnp.zeros((t, cw), F32)
        for k in range(SC_K):
            off = (SC_K - 1) - k
            dz = dz + cbw_ref[k:k + 1, :] * dcbbuf[off:off + t, :]
            src = 8 - (SC_K - 1) + k
            dcbw_ref[k:k + 1, :] += jnp.sum(dcb * zbuf[src:src + t, :], axis=0, keepdims=True)
        dg_ref[:, 2560:3072] = (dz * xb).astype(BF16)
        dg_ref[:, 1536:2048] = (dz * gc).astype(BF16)
        dcbbuf[t:t + 8, :] = dcbbuf[0:8, :]

    rev = lambda i: (nb - 1 - i, 0)
    outs = pl.pallas_call(
        body, name="mix_ab_bwd", grid=(nb,),
        in_specs=[pl.BlockSpec((t, W_AB), rev),
                  pl.BlockSpec((halo, W_AB), lambda i: (jnp.maximum((nb - 1 - i) * per - 1, 0), 0)),
                  pl.BlockSpec((t, cw), rev), pl.BlockSpec((t, cw), rev), pl.BlockSpec((t, cw), rev),
                  pl.BlockSpec((t, cw), rev),
                  _full_spec((CONV_K, cw)), _full_spec((1, cw)), _full_spec((1, cw)), _full_spec((SC_K, cw)),
                  pl.BlockSpec(memory_space=pl.ANY)],
        out_specs=[pl.BlockSpec((t, W_AB), rev), _full_spec((CONV_HALO, cw)), _full_spec((1, cw)), _full_spec((1, cw)),
                   _full_spec((1, cw)), _full_spec((8, cw))],
        out_shape=[jax.ShapeDtypeStruct(dproj.shape, BF16), jax.ShapeDtypeStruct((CONV_HALO, cw), F32),
                   jax.ShapeDtypeStruct((1, cw), F32), jax.ShapeDtypeStruct((1, cw), F32),
                   jax.ShapeDtypeStruct((1, cw), F32), jax.ShapeDtypeStruct((8, cw), F32)],
        scratch_shapes=[pltpu.VMEM((halo + t, cw), F32), pltpu.VMEM((t + halo, cw), F32),
                        pltpu.VMEM((8 + t, cw), F32), pltpu.VMEM((t + 8, cw), F32), pltpu.VMEM((halo + t, cw), F32)],
        input_output_aliases={10: 0},
        compiler_params=_cparams(1),
    )(proj, proj, ca, cb, dact_a, dact_b, conv_a_w, ln_a_g, ln_a_b, conv_b_w, dproj)
    return outs


def _lane_is_nope():
    return lax.broadcasted_iota(jnp.int32, (1, HEAD_PAD), 1) < QK_NOPE


def _attn_prep_fwd(proj, q_g, kv_g, wq12, wkv, tcos, tsin):
    s = proj.shape[0]
    t = min(ROW_BLOCK, s)
    c0 = (W_AB + W_G5) // W_C1

    def body(p_ref, qg_ref, kg_ref, wq_ref, wkv_ref, cos_ref, sin_ref, qf_ref, kf_ref, v_ref):
        ql = p_ref[:, 0:384]
        qn = (ql * lax.rsqrt(jnp.mean(jnp.square(ql), axis=-1, keepdims=True) + RMS_EPS) * qg_ref[...]).astype(BF16)
        q12 = _dot(qn, wq_ref[...])
        kvl = p_ref[:, 384:640]
        kvn = (kvl * lax.rsqrt(jnp.mean(jnp.square(kvl), axis=-1, keepdims=True) + RMS_EPS) * kg_ref[...]).astype(BF16)
        kv12 = _dot(kvn, wkv_ref[...])
        tcos_v = cos_ref[...]
        tsin_v = sin_ref[...]
        tq1 = jnp.where(_lane_is_nope(), 1.0, tcos_v)
        kpe = p_ref[:, 640:768] * tcos_v + p_ref[:, 768:896] * tsin_v
        for h in range(HEADS):
            lo = h * HEAD_PAD
            qf_ref[h] = (q12[:, lo:lo + HEAD_PAD] * tq1 + q12[:, 1024 + lo:1024 + lo + HEAD_PAD] * tsin_v).astype(BF16)
            kf_ref[h] = (kv12[:, lo:lo + HEAD_PAD] + kpe).astype(BF16)
        v_ref[...] = kv12[:, 1024:1536].astype(BF16)

    return pl.pallas_call(
        body, name="attn_prep_fwd", grid=(s // t,),
        in_specs=[pl.BlockSpec((t, W_C1), lambda i: (i, c0)), _full_spec((1, 384)), _full_spec((1, 256)),
                  _full_spec((384, 2048)), _full_spec((256, 1536)), _row_spec(t, HEAD_PAD), _row_spec(t, HEAD_PAD)],
        out_specs=[pl.BlockSpec((HEADS, t, HEAD_PAD), lambda i: (0, i, 0)),
                   pl.BlockSpec((HEADS, t, HEAD_PAD), lambda i: (0, i, 0)), _row_spec(t, 512)],
        out_shape=[jax.ShapeDtypeStruct((HEADS, s, HEAD_PAD), BF16), jax.ShapeDtypeStruct((HEADS, s, HEAD_PAD), BF16),
                   jax.ShapeDtypeStruct((s, 512), BF16)],
        compiler_params=_cparams(1),
    )(proj, q_g, kv_g, wq12, wkv, tcos, tsin)


def _attn_prep_bwd(proj, dqf, dkf, dv, q_g, kv_g, wq12, wkv, tcos, tsin, dproj):
    s = proj.shape[0]
    t = min(ROW_BLOCK, s)
    c0 = (W_AB + W_G5) // W_C1

    def body(p_ref, dqf_ref, dkf_ref, dv_ref, qg_ref, kg_ref, wq_ref, wkv_ref, cos_ref, sin_ref, dproj_any,
             dg_ref, qn_ref, kvn_ref, dq12_ref, dkv12_ref, dqg_ref, dkg_ref):
        del dproj_any

        @pl.when(pl.program_id(0) == 0)
        def _():
            dqg_ref[...] = jnp.zeros_like(dqg_ref)
            dkg_ref[...] = jnp.zeros_like(dkg_ref)

        tcos_v = cos_ref[...]
        tsin_v = sin_ref[...]
        tq1 = jnp.where(_lane_is_nope(), 1.0, tcos_v)
        dkpe = jnp.zeros((t, HEAD_PAD), F32)
        for h in range(HEADS):
            lo = h * HEAD_PAD
            dq = dqf_ref[h]
            dq12_ref[:, lo:lo + HEAD_PAD] = (dq * tq1).astype(BF16)
            dq12_ref[:, 1024 + lo:1024 + lo + HEAD_PAD] = (dq * tsin_v).astype(BF16)
            dk = dkf_ref[h]
            dkv12_ref[:, lo:lo + HEAD_PAD] = dk.astype(BF16)
            dkpe = dkpe + dk
        dkv12_ref[:, 1024:1536] = dv_ref[...].astype(BF16)
        dg_ref[:, 640:768] = (dkpe * tcos_v).astype(BF16)
        dg_ref[:, 768:896] = (dkpe * tsin_v).astype(BF16)

        def rms_bwd(xl, g, dn):
            rr = lax.rsqrt(jnp.mean(jnp.square(xl), axis=-1, keepdims=True) + RMS_EPS)
            xn = xl * rr
            tt = dn * g
            return rr * (tt - xn * jnp.mean(tt * xn, axis=-1, keepdims=True)), jnp.sum(dn * xn, axis=0, keepdims=True), xn

        ql = p_ref[:, 0:384]
        dqn = _dot_nt(dq12_ref[...], wq_ref[...])
        dql, dqg, qxn = rms_bwd(ql, qg_ref[...], dqn)
        dg_ref[:, 0:384] = dql.astype(BF16)
        dqg_ref[...] += dqg
        qn_ref[...] = (qxn * qg_ref[...]).astype(BF16)
        kvl = p_ref[:, 384:640]
        dkvn = _dot_nt(dkv12_ref[...], wkv_ref[...])
        dkvl, dkg, kxn = rms_bwd(kvl, kg_ref[...], dkvn)
        dg_ref[:, 384:640] = dkvl.astype(BF16)
        dkg_ref[...] += dkg
        kvn_ref[...] = (kxn * kg_ref[...]).astype(BF16)

    return pl.pallas_call(
        body, name="attn_prep_bwd", grid=(s // t,),
        in_specs=[pl.BlockSpec((t, W_C1), lambda i: (i, c0)),
                  pl.BlockSpec((HEADS, t, HEAD_PAD), lambda i: (0, i, 0)),
                  pl.BlockSpec((HEADS, t, HEAD_PAD), lambda i: (0, i, 0)), _row_spec(t, 512),
                  _full_spec((1, 384)), _full_spec((1, 256)), _full_spec((384, 2048)), _full_spec((256, 1536)),
                  _row_spec(t, HEAD_PAD), _row_spec(t, HEAD_PAD), pl.BlockSpec(memory_space=pl.ANY)],
        out_specs=[pl.BlockSpec((t, W_C1), lambda i: (i, c0)), _row_spec(t, 384), _row_spec(t, 256),
                   _row_spec(t, 2048), _row_spec(t, 1536), _full_spec((1, 384)), _full_spec((1, 256))],
        out_shape=[jax.ShapeDtypeStruct(dproj.shape, BF16), jax.ShapeDtypeStruct((s, 384), BF16),
                   jax.ShapeDtypeStruct((s, 256), BF16), jax.ShapeDtypeStruct((s, 2048), BF16),
                   jax.ShapeDtypeStruct((s, 1536), BF16), jax.ShapeDtypeStruct((1, 384), F32),
                   jax.ShapeDtypeStruct((1, 256), F32)],
        input_output_aliases={10: 0},
        compiler_params=_cparams(1),
    )(proj, dqf, dkf, dv, q_g, kv_g, wq12, wkv, tcos, tsin, dproj)


def _causal_keep(t):
    return lax.broadcasted_iota(jnp.int32, (t, t), 0) >= lax.broadcasted_iota(jnp.int32, (t, t), 1)


def _half_select(e):
    lane = lax.broadcasted_iota(jnp.int32, (1, HEAD_PAD), 1)
    return (lane < V_HEAD) if e == 0 else (lane >= V_HEAD)


def _attention_fwd(qf, kf, v, exchange=None):
    _, s, _ = qf.shape
    t = min(ATT_BLOCK, s)
    nq = s // t

    def body(q_ref, k_ref, v_ref, o_ref, lse_ref):
        def rows_of_block(blk):
            past = blk * t
            for pair in range(HEADS // 2):
                lo = pair * HEAD_PAD
                v_diag = v_ref[past:past + t, lo:lo + HEAD_PAD]
                v_past = v_ref[0:past, lo:lo + HEAD_PAD] if blk else None
                out = None
                for e in range(2):
                    h = 2 * pair + e
                    sel = _half_select(e)
                    q = q_ref[h]
                    s_diag = jnp.where(_causal_keep(t), _dot_nt(q, k_ref[h, past:past + t, :]) * ATT_SCALE, NEG_BIG)
                    m = jnp.max(s_diag, axis=1, keepdims=True)
                    if blk:
                        s_past = _dot_nt(q, k_ref[h, 0:past, :]) * ATT_SCALE
                        m = jnp.maximum(m, jnp.max(s_past, axis=1, keepdims=True))
                    p_diag = jnp.exp(s_diag - m)
                    norm = jnp.sum(p_diag, axis=1, keepdims=True)
                    pv = _dot(p_diag.astype(BF16), jnp.where(sel, v_diag, jnp.zeros_like(v_diag)))
                    if blk:
                        p_past = jnp.exp(s_past - m)
                        norm = norm + jnp.sum(p_past, axis=1, keepdims=True)
                        pv = pv + _dot(p_past.astype(BF16), jnp.where(sel, v_past, jnp.zeros_like(v_past)))
                    pv = pv * (1.0 / norm)
                    out = pv if out is None else out + pv
                    lse_ref[h] = m + jnp.log(norm)
                o_ref[:, lo:lo + HEAD_PAD] = out

        for blk in range(nq):
            pl.when(pl.program_id(0) == blk)(functools.partial(rows_of_block, blk))

    return _call(
        body, (qf, kf, v), exchange, name="attention_fwd", grid=(nq,),
        in_specs=[pl.BlockSpec((HEADS, t, HEAD_PAD), lambda i: (0, i, 0)), _full_spec((HEADS, s, HEAD_PAD)),
                  _full_spec((s, 512))],
        out_specs=[_row_spec(t, 512), pl.BlockSpec((HEADS, t, 1), lambda i: (0, i, 0))],
        out_shape=[jax.ShapeDtypeStruct((s, 512), F32), jax.ShapeDtypeStruct((HEADS, s, 1), F32)],
        compiler_params=_cparams(1))


def _attention_bwd(qf, kf, v, o, do, lse, exchange=None):
    _, s, _ = qf.shape
    t = min(ATT_BLOCK, s)
    nq = s // t

    def body(q_ref, k_ref, v_ref, o_ref, do_ref, lse_ref, dq_ref, dk_ref, dv_ref, dk_acc, dv_acc):
        kj = pl.program_id(0)
        qi = pl.program_id(1)

        @pl.when((kj == 0) & (qi == 0))
        def _():
            dq_ref[...] = jnp.zeros_like(dq_ref)

        @pl.when(qi == kj)
        def _():
            dk_acc[...] = jnp.zeros_like(dk_acc)
            dv_acc[...] = jnp.zeros_like(dv_acc)

        def step(diag):
            rows = pl.ds(pl.multiple_of(qi * t, t), t)
            for h in range(HEADS):
                pair, e = divmod(h, 2)
                sel = _half_select(e)
                lo = pair * HEAD_PAD
                q = q_ref[h]
                k = k_ref[h]
                sc = _dot_nt(q, k) * ATT_SCALE
                if diag:
                    sc = jnp.where(_causal_keep(t), sc, NEG_BIG)
                p = jnp.exp(sc - lse_ref[h])
                do_pair = do_ref[:, lo:lo + HEAD_PAD]
                do_e = jnp.where(sel, do_pair, 0.0)
                do_b = do_e.astype(BF16)
                v2 = v_ref[:, lo:lo + HEAD_PAD]
                dv_acc[:, lo:lo + HEAD_PAD] += _dot_tn(p.astype(BF16), do_b)
                dp = _dot_nt(do_b, jnp.where(sel, v2, jnp.zeros_like(v2)))
                delta = jnp.sum(do_e * o_ref[:, lo:lo + HEAD_PAD], axis=1, keepdims=True)
                ds = (p * (dp - delta) * ATT_SCALE).astype(BF16)
                dq_ref[h, rows, :] += _dot(ds, k)
                dk_acc[h] += _dot_tn(ds, q)

        @pl.when(qi > kj)
        def _():
            step(False)

        @pl.when(qi == kj)
        def _():
            step(True)

        @pl.when(qi == nq - 1)
        def _():
            dk_ref[...] = dk_acc[...]
            dv_ref[...] = dv_acc[...]

    qmap = lambda j, i: (0, jnp.maximum(i, j), 0)
    return _call(
        body, (qf, kf, v, o, do, lse), exchange, name="attention_bwd", grid=(nq, nq),
        in_specs=[pl.BlockSpec((HEADS, t, HEAD_PAD), qmap),
                  pl.BlockSpec((HEADS, t, HEAD_PAD), lambda j, i: (0, j, 0)),
                  pl.BlockSpec((t, 512), lambda j, i: (j, 0)),
                  pl.BlockSpec((t, 512), lambda j, i: (jnp.maximum(i, j), 0)),
                  pl.BlockSpec((t, 512), lambda j, i: (jnp.maximum(i, j), 0)),
                  pl.BlockSpec((HEADS, t, 1), qmap)],
        out_specs=[_full_spec((HEADS, s, HEAD_PAD)),
                   pl.BlockSpec((HEADS, t, HEAD_PAD), lambda j, i: (0, j, 0)),
                   pl.BlockSpec((t, 512), lambda j, i: (j, 0))],
        out_shape=[jax.ShapeDtypeStruct((HEADS, s, HEAD_PAD), F32), jax.ShapeDtypeStruct((HEADS, s, HEAD_PAD), F32),
                   jax.ShapeDtypeStruct((s, 512), F32)],
        scratch_shapes=[pltpu.VMEM((HEADS, t, HEAD_PAD), F32), pltpu.VMEM((t, 512), F32)],
        compiler_params=_cparams(2))


def _merge_fwd(act_a, act_b, o, proj, x, gate, w_a, w_b, w_c, w_o, ln_g, ln_b, alpha, exchange=None):
    s, d = x.shape
    t = min(ROW_BLOCK, s)

    def body(aa_ref, ab_ref, o_ref, p_ref, x_ref, gate_ref, wa_ref, wb_ref, wc_ref, wo_ref, lg_ref, lb_ref,
             xn_ref, xh_ref, rstd_ref):
        cg = p_ref[:, 0:512]
        act_c = (o_ref[...] * (cg * _sig(cg))).astype(BF16)
        m = _sig(p_ref[:, 512:1536]) * _dot(aa_ref[...], wa_ref[...])
        m = m + _sig(p_ref[:, 1536:2560]) * _dot(ab_ref[...], wb_ref[...])
        m = m + _sig(p_ref[:, 2560:3584]) * _dot(act_c, wc_ref[...])
        out = _dot(m.astype(BF16), wo_ref[...])
        z = alpha * x_ref[...] + gate_ref[...] * out
        xh, rstd, y = _layer_norm_rows(z, lg_ref[...], lb_ref[...])
        xn_ref[...] = y
        xh_ref[...] = xh
        rstd_ref[...] = rstd

    return _call(
        body, (act_a, act_b, o, proj, x, gate, w_a, w_b, w_c, w_o, ln_g, ln_b), exchange, name="merge_fwd", grid=(s // t,),
        in_specs=[_row_spec(t, 512), _row_spec(t, 512), _row_spec(t, 512),
                  pl.BlockSpec((t, W_G5), lambda i: (i, 1)), _row_spec(t, d), _full_spec((1, d)),
                  _full_spec((512, d)), _full_spec((512, d)), _full_spec((512, d)), _full_spec((d, d)),
                  _full_spec((1, d)), _full_spec((1, d))],
        out_specs=[_row_spec(t, d), _row_spec(t, d), _row_spec(t, 1)],
        out_shape=[jax.ShapeDtypeStruct((s, d), F32), jax.ShapeDtypeStruct((s, d), F32), jax.ShapeDtypeStruct((s, 1), F32)],
        compiler_params=_cparams(1))


def _merge_bwd(dy, xh, rstd, act_a, act_b, o, proj, gate, w_a, w_b, w_c, w_o, ln_g, alpha, exchange=None):
    s, d = dy.shape
    t = min(ROW_BLOCK, s)

    def body(dy_ref, xh_ref, rstd_ref, aa_ref, ab_ref, o_ref, p_ref, gate_ref, wa_ref, wb_ref, wc_ref, wo_ref, lg_ref,
             dxr_ref, dacta_ref, dactb_ref, do_ref, dg_ref, m_ref, dout_ref, dya_ref, dyb_ref, dyc_ref, actc_ref,
             dlg_ref, dlb_ref, dgate_ref):
        @pl.when(pl.program_id(0) == 0)
        def _():
            dlg_ref[...] = jnp.zeros_like(dlg_ref)
            dlb_ref[...] = jnp.zeros_like(dlb_ref)
            dgate_ref[...] = jnp.zeros_like(dgate_ref)

        dyv = dy_ref[...]
        xhv = xh_ref[...]
        dlg_ref[...] += jnp.sum(dyv * xhv, axis=0, keepdims=True)
        dlb_ref[...] += jnp.sum(dyv, axis=0, keepdims=True)
        dz = _layer_norm_bwd_rows(dyv * lg_ref[...], xhv, rstd_ref[...])
        dxr_ref[...] = alpha * dz

        cg = p_ref[:, 0:512]
        scg = _sig(cg)
        silu_cg = cg * scg
        ov = o_ref[...]
        act_c = (ov * silu_cg).astype(BF16)
        actc_ref[...] = act_c
        ya = _dot(aa_ref[...], wa_ref[...])
        yb = _dot(ab_ref[...], wb_ref[...])
        yc = _dot(act_c, wc_ref[...])
        ga = _sig(p_ref[:, 512:1536])
        gb = _sig(p_ref[:, 1536:2560])
        gc = _sig(p_ref[:, 2560:3584])
        mb = (ga * ya + gb * yb + gc * yc).astype(BF16)
        m_ref[...] = mb
        out = _dot(mb, wo_ref[...])
        dgate_ref[...] += jnp.sum(dz * out, axis=0, keepdims=True)
        dout = (gate_ref[...] * dz).astype(BF16)
        dout_ref[...] = dout
        dm = _dot_nt(dout, wo_ref[...])

        dya = (dm * ga).astype(BF16)
        dya_ref[...] = dya
        dg_ref[:, 512:1536] = (dm * ya * ga * (1.0 - ga)).astype(BF16)
        dacta_ref[...] = _dot_nt(dya, wa_ref[...])
        dyb = (dm * gb).astype(BF16)
        dyb_ref[...] = dyb
        dg_ref[:, 1536:2560] = (dm * yb * gb * (1.0 - gb)).astype(BF16)
        dactb_ref[...] = _dot_nt(dyb, wb_ref[...])
        dyc = (dm * gc).astype(BF16)
        dyc_ref[...] = dyc
        dg_ref[:, 2560:3584] = (dm * yc * gc * (1.0 - gc)).astype(BF16)
        dactc = _dot_nt(dyc, wc_ref[...])
        do_ref[...] = dactc * silu_cg
        dg_ref[:, 0:512] = (dactc * ov * _dsilu(cg, scg)).astype(BF16)

    return _call(
        body, (dy, xh, rstd, act_a, act_b, o, proj, gate, w_a, w_b, w_c, w_o, ln_g), exchange, name="merge_bwd", grid=(s // t,),
        in_specs=[_row_spec(t, d), _row_spec(t, d), _row_spec(t, 1), _row_spec(t, 512), _row_spec(t, 512),
                  _row_spec(t, 512), pl.BlockSpec((t, W_G5), lambda i: (i, 1)), _full_spec((1, d)),
                  _full_spec((512, d)), _full_spec((512, d)), _full_spec((512, d)), _full_spec((d, d)),
                  _full_spec((1, d))],
        out_specs=[_row_spec(t, d), _row_spec(t, 512), _row_spec(t, 512), _row_spec(t, 512),
                   pl.BlockSpec((t, W_G5), lambda i: (i, 1)),
                   _row_spec(t, d), _row_spec(t, d), _row_spec(t, d), _row_spec(t, d), _row_spec(t, d),
                   _row_spec(t, 512), _full_spec((1, d)), _full_spec((1, d)), _full_spec((1, d))],
        out_shape=[jax.ShapeDtypeStruct((s, d), F32), jax.ShapeDtypeStruct((s, 512), F32),
                   jax.ShapeDtypeStruct((s, 512), F32), jax.ShapeDtypeStruct((s, 512), F32),
                   jax.ShapeDtypeStruct((s, W_PACK), BF16),
                   jax.ShapeDtypeStruct((s, d), BF16), jax.ShapeDtypeStruct((s, d), BF16),
                   jax.ShapeDtypeStruct((s, d), BF16), jax.ShapeDtypeStruct((s, d), BF16),
                   jax.ShapeDtypeStruct((s, d), BF16), jax.ShapeDtypeStruct((s, 512), BF16),
                   jax.ShapeDtypeStruct((1, d), F32), jax.ShapeDtypeStruct((1, d), F32),
                   jax.ShapeDtypeStruct((1, d), F32)],
        compiler_params=_cparams(1))


def _proj_bwd_input(dproj, w_pack, x, scale, dxres):
    s, d = x.shape
    t = min(1024, s)
    nk = W_PACK // COL_BLK

    def body(dg_ref, w_ref, x_ref, sc_ref, dxr_ref, dx_ref, dsc_ref, dsh_ref, acc):
        i = pl.program_id(0)
        k = pl.program_id(1)

        @pl.when((i == 0) & (k == 0))
        def _():
            dsc_ref[...] = jnp.zeros_like(dsc_ref)
            dsh_ref[...] = jnp.zeros_like(dsh_ref)

        @pl.when(k == 0)
        def _():
            acc[...] = jnp.zeros_like(acc)

        acc[...] += _dot(dg_ref[...], w_ref[...])

        @pl.when(k == nk - 1)
        def _():
            du = acc[...]
            dx_ref[...] = du * (1.0 + sc_ref[...]) + dxr_ref[...]
            dsc_ref[...] += jnp.sum(du * x_ref[...], axis=0, keepdims=True)
            dsh_ref[...] += jnp.sum(du, axis=0, keepdims=True)

    return pl.pallas_call(
        body, name="proj_bwd_input", grid=(s // t, nk),
        in_specs=[pl.BlockSpec((t, COL_BLK), lambda i, k: (i, k)), pl.BlockSpec((COL_BLK, d), lambda i, k: (k, 0)),
                  pl.BlockSpec((t, d), lambda i, k: (i, 0)), _full_spec((1, d)), pl.BlockSpec((t, d), lambda i, k: (i, 0))],
        out_specs=[pl.BlockSpec((t, d), lambda i, k: (i, 0)), _full_spec((1, d)), _full_spec((1, d))],
        out_shape=[jax.ShapeDtypeStruct((s, d), F32), jax.ShapeDtypeStruct((1, d), F32), jax.ShapeDtypeStruct((1, d), F32)],
        scratch_shapes=[pltpu.VMEM((t, d), F32)],
        compiler_params=_cparams(2),
    )(dproj, w_pack, x, scale, dxres)


def _loss_head(y, target):
    s, d = y.shape
    t = min(512, s)

    def body(y_ref, t_ref, dy_ref, loss_ref):
        @pl.when(pl.program_id(0) == 0)
        def _():
            loss_ref[...] = jnp.zeros_like(loss_ref)

        err = y_ref[...] - t_ref[...]
        dy_ref[...] = err / d
        part = 0.5 * jnp.sum(jnp.mean(jnp.square(err), axis=-1, keepdims=True), axis=0, keepdims=True)
        loss_ref[...] += jnp.broadcast_to(part, loss_ref.shape)

    return pl.pallas_call(
        body, name="loss_head", grid=(s // t,),
        in_specs=[_row_spec(t, d), _row_spec(t, d)],
        out_specs=[_row_spec(t, d), _full_spec((1, 128))],
        out_shape=[jax.ShapeDtypeStruct((s, d), F32), jax.ShapeDtypeStruct((1, 128), F32)],
        compiler_params=_cparams(1),
    )(y, target)


def _adamw(g, w, m, v, name):
    r, n = g.shape
    block_bytes = 5 * 2 ** 19
    rb = r
    if r * n * 4 > block_bytes:
        rb = max(k for k in range(8, r, 8) if r % k == 0 and k * n * 4 <= block_bytes)

    def body(g_ref, w_ref, m_ref, v_ref, d_ref, mo_ref, vo_ref):
        gv = g_ref[...]
        mn = ADAM_B1 * m_ref[...] + (1.0 - ADAM_B1) * gv
        vn = ADAM_B2 * v_ref[...] + (1.0 - ADAM_B2) * jnp.square(gv)
        m_hat = mn / (1.0 - ADAM_B1 ** ADAM_STEP)
        v_hat = vn / (1.0 - ADAM_B2 ** ADAM_STEP)
        d_ref[...] = -ADAM_LR * (m_hat / (jnp.sqrt(v_hat) + ADAM_EPS) + ADAM_WD * w_ref[...])
        mo_ref[...] = mn
        vo_ref[...] = vn

    spec = pl.BlockSpec((rb, n), lambda i: (i, 0))
    return pl.pallas_call(
        body, name=name, grid=(r // rb,),
        in_specs=[spec] * 4, out_specs=[spec] * 3,
        out_shape=[jax.ShapeDtypeStruct((r, n), F32)] * 3, compiler_params=_cparams(1),
    )(g, w, m, v)


SMALL = ("b_ada", "conv_a_b", "ln_a_g", "ln_a_b", "q_norm_g", "kv_norm_g", "ln_g", "ln_b", "conv_a_w", "conv_b_w")
ROWS_256 = (("w_a_out", 512), ("w_b_out", 512), ("w_c_out", 512), ("w_ukv", 256))


def _rot_cols(w):
    return jnp.concatenate([-w[..., 16:], w[..., :16]], axis=-1)


def _unrot_cols(g):
    return jnp.concatenate([g[..., 16:], -g[..., :16]], axis=-1)


def _pack_w_in(wt):
    z = lambda n: jnp.zeros((n, wt.shape[1]), wt.dtype)
    wk = wt[4224:4256]
    rot = jnp.concatenate([-wk[16:], wk[:16]], axis=0)
    return jnp.concatenate([wt[0:3584], wt[4256:7840], wt[3584:4224], z(64), wk, z(32), z(64), rot, z(32)], axis=0)


def _unpack_w_in_grad(g):
    c1 = W_AB + W_G5
    g_rot = g[c1 + 768 + 64:c1 + 768 + 96]
    gk = g[c1 + 640 + 64:c1 + 640 + 96] + jnp.concatenate([g_rot[16:], -g_rot[:16]], axis=0)
    return jnp.concatenate([g[0:3584], g[c1:c1 + 640], gk, g[W_AB:W_AB + W_G5]], axis=0)


def _pack_w_uq(w):
    lead = w.shape[:-1]
    wh = w.reshape(lead + (HEADS, QK_NOPE + QK_ROPE))
    nope, rope = wh[..., :QK_NOPE], wh[..., QK_NOPE:]
    z32 = jnp.zeros_like(rope)
    q1 = jnp.concatenate([nope, rope, z32], axis=-1).reshape(lead + (HEADS * HEAD_PAD,))
    q2 = jnp.concatenate([jnp.zeros_like(nope), _rot_cols(rope), z32], axis=-1).reshape(lead + (HEADS * HEAD_PAD,))
    return jnp.concatenate([q1, q2], axis=-1)


def _unpack_w_uq_grad(g):
    lead = g.shape[:-1]
    g1 = g[..., :1024].reshape(lead + (HEADS, HEAD_PAD))
    g2 = g[..., 1024:].reshape(lead + (HEADS, HEAD_PAD))
    rope = g1[..., QK_NOPE:QK_NOPE + QK_ROPE] + _unrot_cols(g2[..., QK_NOPE:QK_NOPE + QK_ROPE])
    return jnp.concatenate([g1[..., :QK_NOPE], rope], axis=-1).reshape(lead + (HEADS * (QK_NOPE + QK_ROPE),))


def _pack_w_ukv(w):
    lead = w.shape[:-1]
    wh = w.reshape(lead + (HEADS, QK_NOPE + V_HEAD))
    kn, vv = wh[..., :QK_NOPE], wh[..., QK_NOPE:]
    k1 = jnp.concatenate([kn, jnp.zeros_like(kn)], axis=-1).reshape(lead + (HEADS * HEAD_PAD,))
    return jnp.concatenate([k1, vv.reshape(lead + (HEADS * V_HEAD,))], axis=-1)


def _unpack_w_ukv_grad(g):
    lead = g.shape[:-1]
    gk = g[..., :1024].reshape(lead + (HEADS, HEAD_PAD))[..., :QK_NOPE]
    gv = g[..., 1024:].reshape(lead + (HEADS, V_HEAD))
    return jnp.concatenate([gk, gv], axis=-1).reshape(lead + (HEADS * (QK_NOPE + V_HEAD),))


def _join_cols(g4):
    return jnp.concatenate([g4[j] for j in range(4)], axis=-1)


def _split_cols(w):
    c4 = w.shape[-1] // 4
    return jnp.stack([w[:, j * c4:(j + 1) * c4] for j in range(4)])


def kernel(x, c, positions, w_ada, b_ada, w_in, conv_a_w, conv_a_b, ln_a_g, ln_a_b, w_a_out, conv_b_w, w_b_out, q_norm_g, kv_norm_g, w_uq, w_ukv, w_c_out, w_o, ln_g, ln_b, loss_target, m_w_ada, m_b_ada, m_w_in, m_conv_a_w, m_conv_a_b, m_ln_a_g, m_ln_a_b, m_w_a_out, m_conv_b_w, m_w_b_out, m_q_norm_g, m_kv_norm_g, m_w_uq, m_w_ukv, m_w_c_out, m_w_o, m_ln_g, m_ln_b, v_w_ada, v_b_ada, v_w_in, v_conv_a_w, v_conv_a_b, v_ln_a_g, v_ln_a_b, v_w_a_out, v_conv_b_w, v_w_b_out, v_q_norm_g, v_kv_norm_g, v_w_uq, v_w_ukv, v_w_c_out, v_w_o, v_ln_g, v_ln_b):
    weights = dict(w_ada=w_ada, b_ada=b_ada, w_in=w_in, conv_a_w=conv_a_w, conv_a_b=conv_a_b, ln_a_g=ln_a_g, ln_a_b=ln_a_b,
                   w_a_out=w_a_out, conv_b_w=conv_b_w, w_b_out=w_b_out, q_norm_g=q_norm_g, kv_norm_g=kv_norm_g, w_uq=w_uq,
                   w_ukv=w_ukv, w_c_out=w_c_out, w_o=w_o, ln_g=ln_g, ln_b=ln_b)
    mom_m = dict(w_ada=m_w_ada, b_ada=m_b_ada, w_in=m_w_in, conv_a_w=m_conv_a_w, conv_a_b=m_conv_a_b, ln_a_g=m_ln_a_g,
                 ln_a_b=m_ln_a_b, w_a_out=m_w_a_out, conv_b_w=m_conv_b_w, w_b_out=m_w_b_out, q_norm_g=m_q_norm_g,
                 kv_norm_g=m_kv_norm_g, w_uq=m_w_uq, w_ukv=m_w_ukv, w_c_out=m_w_c_out, w_o=m_w_o, ln_g=m_ln_g, ln_b=m_ln_b)
    mom_v = dict(w_ada=v_w_ada, b_ada=v_b_ada, w_in=v_w_in, conv_a_w=v_conv_a_w, conv_a_b=v_conv_a_b, ln_a_g=v_ln_a_g,
                 ln_a_b=v_ln_a_b, w_a_out=v_w_a_out, conv_b_w=v_conv_b_w, w_b_out=v_w_b_out, q_norm_g=v_q_norm_g,
                 kv_norm_g=v_kv_norm_g, w_uq=v_w_uq, w_ukv=v_w_ukv, w_c_out=v_w_c_out, w_o=v_w_o, ln_g=v_ln_g, ln_b=v_ln_b)
    order = ["w_ada", "b_ada", "w_in", "conv_a_w", "conv_a_b", "ln_a_g", "ln_a_b", "w_a_out", "conv_b_w", "w_b_out",
             "q_norm_g", "kv_norm_g", "w_uq", "w_ukv", "w_c_out", "w_o", "ln_g", "ln_b"]

    n_layers = w_in.shape[0]
    s, d = x.shape[1], x.shape[2]
    alpha = float((2 * n_layers) ** 0.25)
    ix, iy, ic = lax.axis_index("x"), lax.axis_index("y"), lax.axis_index("c")
    me = 4 * ix + 2 * iy + ic
    chip = 2 * ix + iy
    xs = x[0]

    def four(tree, cast):
        return [cast(jnp.swapaxes(tree["w_in"], 1, 2)), jnp.concatenate([cast(tree[n]) for n, _ in ROWS_256], axis=-2),
                cast(tree["w_uq"]), cast(tree["w_o"])]

    shards = four(weights, lambda a: a.astype(BF16))
    landed_w = _run_exchange(_gather_layer_exchange(shards, 0), "gather_weights_first")
    landed_w = _run_exchange(_gather_pair_exchange(landed_w), "gather_weights_first_pair")
    w_pack, wq12, wkv, w_a, w_b, w_c, w_of = [], [], [], [], [], [], []

    def unpack_layer(g_in, g_256, g_uq, g_wo):
        w_pack.append(_pack_w_in(g_in.reshape(-1, d)))
        wq12.append(_pack_w_uq(_join_cols(g_uq)))
        rows = _join_cols(g_256)
        w_a.append(rows[0:512])
        w_b.append(rows[512:1024])
        w_c.append(rows[1024:1536])
        wkv.append(_pack_w_ukv(rows[1536:1792]))
        w_of.append(g_wo.reshape(d, d))

    cw_a = conv_a_w.reshape(n_layers * CONV_K, 128)
    cw_b = conv_b_w.reshape(n_layers * SC_K, 128)
    c_rows = d // 128
    first_rows = c_rows + n_layers * (CONV_K + SC_K)
    first_pad = (-first_rows) % 8
    first = jnp.concatenate([c.reshape(c_rows, 128), cw_a, cw_b, jnp.zeros((first_pad, 128), F32)], axis=0)
    fr = first_rows + first_pad
    got = _all_gather_small(first, "gather_cond").reshape(8, fr, 128)
    c_all = got[:, :c_rows].reshape(8, d)
    conv_a_full = jnp.moveaxis(got[0::2, c_rows:c_rows + n_layers * CONV_K].reshape(4, n_layers, CONV_K, 128), 0, 2)
    conv_a_full = conv_a_full.reshape(n_layers, CONV_K, 512)
    b0 = c_rows + n_layers * CONV_K
    conv_b_full = jnp.moveaxis(got[0::2, b0:b0 + n_layers * SC_K].reshape(4, n_layers, SC_K, 128), 0, 2)
    conv_b_full = conv_b_full.reshape(n_layers, SC_K, 512)

    ada_sh, c_act = _ada_shard(c_all, w_ada.astype(BF16))
    ada_all = _all_gather_small(ada_sh.reshape(n_layers * 8, 768), "gather_ada").reshape(8, n_layers, 8, 768)
    ada_mine = lax.dynamic_index_in_dim(ada_all[0::2], me, axis=2, keepdims=False)
    ada = jnp.moveaxis(ada_mine, 0, 1).reshape(n_layers, 3 * d) + b_ada
    shift, scale, gate = ada[:, 0:d], ada[:, d:2 * d], ada[:, 2 * d:3 * d]

    tcos, tsin = _rope_tables(positions.reshape(s, 1))

    saved = []
    h = xs
    for l in range(n_layers):
        unpack_layer(*landed_w)
        u = _modulate(h, scale[l:l + 1], shift[l:l + 1])
        more = l + 1 < n_layers
        (proj,), lw_in = _matmul_nt(u, w_pack[l], COL_BLK, "proj_fwd",
                                    _gather_layer_exchange(shards[:1], l + 1, (0, 2)) if more else None)
        (act_a, act_b, ca, cb), lw_rest = _mix_ab_fwd(
            proj, conv_a_full[l], conv_a_b[l:l + 1], ln_a_g[l:l + 1], ln_a_b[l:l + 1], conv_b_full[l],
            _gather_layer_exchange(shards[1:], l + 1) if more else None)
        qf, kf, vv = _attn_prep_fwd(proj, q_norm_g[l:l + 1], kv_norm_g[l:l + 1], wq12[l], wkv[l], tcos, tsin)
        (o, lse), lw_in = _attention_fwd(
            qf, kf, vv, _gather_layer_exchange(shards[:1], l + 1, (1, 2), lw_in) if more else None)
        (h_next, xh, rstd), landed_w = _merge_fwd(
            act_a, act_b, o, proj, h, gate[l:l + 1], w_a[l], w_b[l], w_c[l], w_of[l], ln_g[l:l + 1], ln_b[l:l + 1], alpha,
            _gather_pair_exchange(lw_in + lw_rest) if more else None)
        saved.append(dict(x=h, u=u, proj=proj, act_a=act_a, act_b=act_b, ca=ca, cb=cb, qf=qf, kf=kf, v=vv, o=o, lse=lse,
                          xh=xh, rstd=rstd))
        h = h_next

    dy, loss_part = _loss_head(h, loss_target[0])
    loss = lax.psum(loss_part[0, 0], ("x", "y", "c"))

    c_idx = ic.reshape(1).astype(jnp.int32)
    reduced = None
    pending = None
    small_rows = [None] * n_layers
    for l in reversed(range(n_layers)):
        sv = saved[l]
        ((dxres, dact_a, dact_b, do, dproj, m_bf, dout_bf, dya, dyb, dyc, act_c, d_ln_g, d_ln_b, d_gate),
         rcv_rest) = _merge_bwd(
            dy, sv["xh"], sv["rstd"], sv["act_a"], sv["act_b"], sv["o"], sv["proj"], gate[l:l + 1], w_a[l], w_b[l],
            w_c[l], w_of[l], ln_g[l:l + 1], alpha, _scatter_exchange(pending[1:]) if pending else None)
        g_o = _matmul_tn(m_bf, dout_bf, d, "grad_w_o")
        g_a = _matmul_tn(sv["act_a"], dya, d, "grad_w_a_out")
        g_b = _matmul_tn(sv["act_b"], dyb, d, "grad_w_b_out")
        g_c = _matmul_tn(act_c, dyc, d, "grad_w_c_out")
        (dqf, dkf, dvv), rcv_in = _attention_bwd(sv["qf"], sv["kf"], sv["v"], sv["o"], do, sv["lse"],
                                                 _scatter_exchange(pending[:1]) if pending else None)
        if pending:
            reduced = _sum_chips(rcv_in + rcv_rest, shapes, reduced, l + 1, n_layers, c_idx)
        dproj, qn, kvn, dq12, dkv12, d_qg, d_kg = _attn_prep_bwd(
            sv["proj"], dqf, dkf, dvv, q_norm_g[l:l + 1], kv_norm_g[l:l + 1], wq12[l], wkv[l], tcos, tsin, dproj)
        g_uq = _unpack_w_uq_grad(_matmul_tn(qn, dq12, 1024, "grad_w_uq"))
        g_kv = _unpack_w_ukv_grad(_matmul_tn(kvn, dkv12, 768, "grad_w_ukv"))
        dproj, d_caw, d_cab, d_lag, d_lab, d_cbw = _mix_ab_bwd(
            sv["proj"], sv["ca"], sv["cb"], dact_a, dact_b, conv_a_full[l], ln_a_g[l:l + 1], ln_a_b[l:l + 1],
            conv_b_full[l], dproj)
        g_in_l = _unpack_w_in_grad(_matmul_tn(dproj, sv["u"], d, "grad_w_in", bm=COL_BLK))
        dy, d_scale, d_shift = _proj_bwd_input(dproj, w_pack[l], sv["x"], scale[l:l + 1], dxres)
        small_rows[l] = jnp.concatenate(
            [d_shift[0], d_scale[0], d_gate[0], d_cab[0], d_lag[0], d_lab[0], d_qg[0], d_kg[0], d_ln_g[0], d_ln_b[0],
             d_caw[:CONV_K].reshape(-1), d_cbw[:SC_K].reshape(-1)])

        partial = [g_in_l.reshape(4, -1, d).astype(BF16),
                   jnp.concatenate([_split_cols(g) for g in (g_a, g_b, g_c, g_kv)], axis=1).astype(BF16),
                   _split_cols(g_uq).astype(BF16), g_o.reshape(4, d // 4, d).astype(BF16)]
        shapes = [p.shape[1:] for p in partial]
        pending = _add_halves(partial, _swap_halves(partial), c_idx)
    grad_x = dy[None]
    received = _run_exchange(_scatter_exchange(pending), "grad_scatter_last")
    reduced = _sum_chips(received, shapes, reduced, 0, n_layers, c_idx)
    r_in, r_256, r_uq, r_wo = _run_exchange(_gather_pair_exchange(reduced), "grad_pair_halves")
    grads = {"w_in": jnp.swapaxes(r_in, 1, 2), "w_uq": r_uq, "w_o": r_wo}
    row0 = 0
    for n, rows in ROWS_256:
        grads[n] = r_256[:, row0:row0 + rows]
        row0 += rows

    small = jnp.stack(small_rows)
    p_small = small.shape[1]
    n_small = n_layers * p_small
    pad_small = (-n_small) % (8 * 128)
    small_flat = jnp.concatenate([small.reshape(-1), jnp.zeros((pad_small,), F32)]).reshape(-1, 128)
    sr = small_flat.shape[0]
    small_all = _all_gather_small(small_flat, "gather_small_grads").reshape(8, sr, 128)
    small_sum = _sum_devices(small_all).reshape(-1)[:n_small].reshape(n_layers, p_small)
    d_ada_all = small_all.reshape(8, -1)[:, :n_small].reshape(8, n_layers, p_small)[:, :, :3 * d]

    cuts = np.cumsum([0, 3 * d, 512, 512, 512, 384, 256, d, d, CONV_K * 512, SC_K * 512])
    pieces = [small_sum[:, cuts[k]:cuts[k + 1]] for k in range(10)]
    grads.update(zip(SMALL, pieces[:8]))
    ga_full = pieces[8].reshape(n_layers, CONV_K, 512)
    gb_full = pieces[9].reshape(n_layers, SC_K, 512)
    grads["conv_a_w"] = lax.dynamic_slice_in_dim(ga_full, chip * 128, 128, axis=2)
    grads["conv_b_w"] = lax.dynamic_slice_in_dim(gb_full, chip * 128, 128, axis=2)

    dada_sh = lax.dynamic_slice_in_dim(d_ada_all, chip * 768, 768, axis=2)
    dada16 = jnp.concatenate([jnp.moveaxis(dada_sh, 0, 1), jnp.zeros((n_layers, 8, 768), F32)], axis=1).astype(BF16)
    cact16 = jnp.concatenate([c_act, jnp.zeros_like(c_act)], axis=0).astype(BF16)
    grads["w_ada"] = _ada_weight_grad(cact16, dada16)

    def as2d(a):
        return a.reshape(-1, a.shape[-1])

    deltas, new_m, new_v = {}, {}, {}
    for n in ("w_in", "w_a_out", "w_b_out", "w_c_out", "w_uq", "w_ukv", "w_o", "w_ada"):
        view = (lambda a: jnp.swapaxes(a, 1, 2)) if n == "w_in" else (lambda a: a)
        shape = view(weights[n]).shape
        dl, mn, vn = _adamw(as2d(view(grads[n])), as2d(view(weights[n])), as2d(view(mom_m[n])), as2d(view(mom_v[n])),
                            "adamw_" + n)
        deltas[n], new_m[n], new_v[n] = view(dl.reshape(shape)), view(mn.reshape(shape)), view(vn.reshape(shape))

    def pack_small(tree):
        flat = jnp.concatenate([tree[n].reshape(-1) for n in SMALL])
        pad = (-flat.shape[0]) % (8 * 128)
        return jnp.concatenate([flat, jnp.zeros((pad,), F32)]).reshape(-1, 128)

    dl_s, mn_s, vn_s = _adamw(pack_small(grads), pack_small(weights), pack_small(mom_m), pack_small(mom_v), "adamw_small")
    off = 0
    for n in SMALL:
        sz = int(np.prod(weights[n].shape))
        for tree, flat in ((deltas, dl_s), (new_m, mn_s), (new_v, vn_s)):
            tree[n] = flat.reshape(-1)[off:off + sz].reshape(weights[n].shape)
        off += sz

    return (loss, grad_x, *[grads[n] for n in order], *[deltas[n] for n in order], *[new_m[n] for n in order],
            *[new_v[n] for n in order])
```

```python
import functools

import numpy as np
import jax
import jax.numpy as jnp
from jax import lax
from jax.experimental import pallas as pl
from jax.experimental.pallas import tpu as pltpu

F32 = jnp.float32
BF16 = jnp.bfloat16
MESH = pl.DeviceIdType.MESH

HEADS = 8
QK_NOPE = 64
QK_ROPE = 32
V_HEAD = 64
CONV_K = 31
SC_K = 3
LN_EPS = 1e-5
RMS_EPS = 1e-6
ROPE_THETA = 10000.0
ATT_SCALE = (QK_NOPE + QK_ROPE) ** -0.5
ADAM_LR = 0.001
ADAM_B1 = 0.9
ADAM_B2 = 0.999
ADAM_EPS = 1e-08
ADAM_WD = 0.01
ADAM_STEP = 10

W_AB = 3584
W_G5 = 3584
W_C1 = 896
W_PACK = W_AB + W_G5 + W_C1
COL_BLK = 896
HEAD_PAD = 128
CONV_HALO = 32
NEG_BIG = -1e30

VMEM_LIMIT_MB = 56
ATT_BLOCK = 512
ROW_BLOCK = 256


def _cparams(n_grid, mb=VMEM_LIMIT_MB):
    return pltpu.CompilerParams(dimension_semantics=("arbitrary",) * n_grid, vmem_limit_bytes=mb * 2 ** 20)


def _sig(x):
    return 1.0 / (1.0 + jnp.exp(-x))


def _dsilu(x, s):
    return s * (1.0 + x * (1.0 - s))


def _dot(a, b):
    return jnp.dot(a, b, preferred_element_type=F32)


def _dot_nt(a, b):
    return lax.dot_general(a, b, (((1,), (1,)), ((), ())), preferred_element_type=F32)


def _dot_tn(a, b):
    return lax.dot_general(a, b, (((0,), (0,)), ((), ())), preferred_element_type=F32)


def _row_spec(t, w):
    return pl.BlockSpec((t, w), lambda i: (i, 0))


def _full_spec(shape):
    nd = len(shape)
    return pl.BlockSpec(shape, lambda *_: (0,) * nd)


def _my_place():
    x, y, c = lax.axis_index("x"), lax.axis_index("y"), lax.axis_index("c")
    chips = [(1 - x, y), (x, 1 - y), (1 - x, 1 - y)]
    return x, y, c, chips


def _all_gather_small(v, name):
    m_per, n = v.shape

    def body(x_ref, out_ref, send_sems, recv_sems, local_sem):
        x, y, c, chips = _my_place()
        me, sibling = (x, y, c), (x, y, 1 - c)

        def rows(px, py, pc):
            return out_ref.at[pl.ds((4 * px + 2 * py + pc) * m_per, m_per), :]

        def copy(k, block, to, src=None):
            return pltpu.make_async_remote_copy(
                src_ref=rows(*block) if src is None else src, dst_ref=rows(*block),
                send_sem=send_sems.at[k], recv_sem=recv_sems.at[k], device_id=to, device_id_type=MESH)

        mine = pltpu.make_async_copy(x_ref, rows(*me), local_sem)
        mine.start()
        first = [copy(0, me, sibling, src=x_ref)]
        first += [copy(1 + j, me, (*chip, c), src=x_ref) for j, chip in enumerate(chips)]
        for cp in first:
            cp.start()
        passed = [copy(4 + j, (*chip, c), sibling) for j, chip in enumerate(chips)]
        for j, chip in enumerate(chips):
            copy(1 + j, (*chip, c), me).wait_recv()
            passed[j].start()
        copy(0, sibling, me).wait_recv()
        for j, chip in enumerate(chips):
            copy(4 + j, (*chip, 1 - c), me).wait_recv()
        for cp in first + passed:
            cp.wait_send()
        mine.wait()

    return pl.pallas_call(
        body, name=name,
        out_shape=jax.ShapeDtypeStruct((8 * m_per, n), v.dtype),
        in_specs=[pl.BlockSpec(memory_space=pltpu.VMEM)],
        out_specs=pl.BlockSpec(memory_space=pltpu.VMEM),
        scratch_shapes=[pltpu.SemaphoreType.DMA((7,)), pltpu.SemaphoreType.DMA((7,)), pltpu.SemaphoreType.DMA],
        compiler_params=pltpu.CompilerParams(vmem_limit_bytes=VMEM_LIMIT_MB * 2 ** 20),
    )(v)


def _any_specs(n):
    return [pl.BlockSpec(memory_space=pl.ANY)] * n


def _splits_rows(shape):
    return shape[-2] % 32 == 0


def _window(shape, c, p=0, n_parts=1):
    r, n = shape[-2], shape[-1]
    if _splits_rows(shape):
        size = r // 2 // n_parts
        return (pl.ds(c * (r // 2) + p * size, size), slice(None))
    size = n // 2 // n_parts
    return (slice(None), pl.ds(c * (n // 2) + p * size, size))


def _half_shape(shape):
    r, n = shape[-2], shape[-1]
    return (r // 2, n) if _splits_rows(shape) else (r, n // 2)


def _half_block_index(shape, c):
    return (c, 0) if _splits_rows(shape) else (0, c)


class _Exchange:
    def __init__(self, inputs, out_shapes, sems, start, finish, aliases=None):
        self.inputs, self.out_shapes, self.sems = list(inputs), list(out_shapes), list(sems)
        self.start, self.finish, self.aliases = start, finish, dict(aliases or {})


def _call(body, args, exchange=None, *, grid, in_specs, out_specs, out_shape, scratch_shapes=(),
          input_output_aliases=None, **kw):
    aliases = dict(input_output_aliases or {})
    if exchange is None:
        outs = pl.pallas_call(body, grid=grid, in_specs=list(in_specs), out_specs=list(out_specs),
                              out_shape=list(out_shape), scratch_shapes=list(scratch_shapes),
                              input_output_aliases=aliases, **kw)(*args)
        return list(outs), []
    n_in, n_out, n_scr = len(in_specs), len(out_specs), len(scratch_shapes)
    x_in, x_out = len(exchange.inputs), len(exchange.out_shapes)

    def wrapped(*refs):
        ins, xin = refs[:n_in], refs[n_in:n_in + x_in]
        o0 = n_in + x_in
        outs, xout = refs[o0:o0 + n_out], refs[o0 + n_out:o0 + n_out + x_out]
        s0 = o0 + n_out + x_out
        scr, xsem = refs[s0:s0 + n_scr], refs[s0 + n_scr:]
        ids = [pl.program_id(k) for k in range(len(grid))]
        first = functools.reduce(jnp.logical_and, [i == 0 for i in ids])
        last = functools.reduce(jnp.logical_and, [i == g - 1 for i, g in zip(ids, grid)])

        @pl.when(first)
        def _():
            exchange.start(xin, xout, xsem)

        body(*ins, *outs, *scr)

        @pl.when(last)
        def _():
            exchange.finish(xin, xout, xsem)

    aliases.update({n_in + i: n_out + o for i, o in exchange.aliases.items()})
    outs = pl.pallas_call(
        wrapped, grid=grid, in_specs=list(in_specs) + _any_specs(x_in), out_specs=list(out_specs) + _any_specs(x_out),
        out_shape=list(out_shape) + exchange.out_shapes, scratch_shapes=list(scratch_shapes) + exchange.sems,
        input_output_aliases=aliases, **kw)(*args, *exchange.inputs)
    return list(outs[:n_out]), list(outs[n_out:])


def _run_exchange(exchange, name):
    x_in, x_out = len(exchange.inputs), len(exchange.out_shapes)

    def body(*refs):
        xin, xout, xsem = refs[:x_in], refs[x_in:x_in + x_out], refs[x_in + x_out:]
        exchange.start(xin, xout, xsem)
        exchange.finish(xin, xout, xsem)

    outs = pl.pallas_call(
        body, name=name, in_specs=_any_specs(x_in), out_specs=_any_specs(x_out), out_shape=exchange.out_shapes,
        scratch_shapes=exchange.sems, input_output_aliases=exchange.aliases)(*exchange.inputs)
    return list(outs)


def _gather_layer_exchange(ws, layer, part=(0, 1), lands=None):
    n = len(ws)
    p, n_parts = part

    def copies(xin, xout, sems):
        send_sems, recv_sems, local_sems = sems
        x, y, c, chips = _my_place()
        k_me = 2 * x + y
        local, sends, recvs = [], [], []
        for a in range(n):
            win = _window(ws[a].shape, c, p, n_parts)
            src = xin[a].at[(layer,) + win]
            dst = xout[a].at[(k_me,) + win]
            local.append(pltpu.make_async_copy(src, dst, local_sems.at[a]))
            for j, (px, py) in enumerate(chips):
                sends.append(pltpu.make_async_remote_copy(
                    src_ref=src, dst_ref=dst, send_sem=send_sems.at[a, j], recv_sem=recv_sems.at[a, j],
                    device_id=(px, py, c), device_id_type=MESH))
                recvs.append(pltpu.make_async_remote_copy(
                    src_ref=src, dst_ref=xout[a].at[(2 * px + py,) + win], send_sem=send_sems.at[a, j],
                    recv_sem=recv_sems.at[a, j], device_id=(px, py, c), device_id_type=MESH))
        return local, sends, recvs

    def start(xin, xout, sems):
        local, sends, _ = copies(xin, xout, sems)
        for cp in local + sends:
            cp.start()

    def finish(xin, xout, sems):
        local, sends, recvs = copies(xin, xout, sems)
        for cp in recvs:
            cp.wait_recv()
        for cp in sends:
            cp.wait_send()
        for cp in local:
            cp.wait()

    return _Exchange(list(ws) + list(lands or []), [jax.ShapeDtypeStruct((4,) + w.shape[1:], w.dtype) for w in ws],
                     [pltpu.SemaphoreType.DMA((n, 3)), pltpu.SemaphoreType.DMA((n, 3)), pltpu.SemaphoreType.DMA((n,))],
                     start, finish, aliases={n + a: a for a in range(n)} if lands else None)


def _gather_pair_exchange(lands):
    n = len(lands)

    def copies(xout, sems):
        send_sems, recv_sems = sems
        x, y, c, _ = _my_place()
        def half(a, core):
            return xout[a].at[(slice(None),) + _window(lands[a].shape, core)]

        sends = [pltpu.make_async_remote_copy(
            src_ref=half(a, c), dst_ref=half(a, c), send_sem=send_sems.at[a],
            recv_sem=recv_sems.at[a], device_id=(x, y, 1 - c), device_id_type=MESH) for a in range(n)]
        recvs = [pltpu.make_async_remote_copy(
            src_ref=half(a, c), dst_ref=half(a, 1 - c), send_sem=send_sems.at[a],
            recv_sem=recv_sems.at[a], device_id=(x, y, 1 - c), device_id_type=MESH) for a in range(n)]
        return sends, recvs

    def start(xin, xout, sems):
        for cp in copies(xout, sems)[0]:
            cp.start()

    def finish(xin, xout, sems):
        sends, recvs = copies(xout, sems)
        for cp in recvs:
            cp.wait_recv()
        for cp in sends:
            cp.wait_send()

    return _Exchange(lands, [jax.ShapeDtypeStruct(w.shape, w.dtype) for w in lands],
                     [pltpu.SemaphoreType.DMA((n,)), pltpu.SemaphoreType.DMA((n,))], start, finish,
                     aliases={a: a for a in range(n)})


def _swap_exchange(ps):
    n = len(ps)

    def copies(xin, xout, sems):
        send_sems, recv_sems = sems
        x, y, c, _ = _my_place()
        return [pltpu.make_async_remote_copy(
            src_ref=xin[a].at[(slice(None),) + _window(ps[a].shape, 1 - c)], dst_ref=xout[a],
            send_sem=send_sems.at[a], recv_sem=recv_sems.at[a], device_id=(x, y, 1 - c), device_id_type=MESH)
            for a in range(n)]

    def start(xin, xout, sems):
        for cp in copies(xin, xout, sems):
            cp.start()

    def finish(xin, xout, sems):
        for cp in copies(xin, xout, sems):
            cp.wait()

    return _Exchange(ps, [jax.ShapeDtypeStruct((4,) + _half_shape(p.shape), p.dtype) for p in ps],
                     [pltpu.SemaphoreType.DMA((n,)), pltpu.SemaphoreType.DMA((n,))], start, finish)


def _add_halves(ps, lands, c_idx):
    n = len(ps)

    def body(c_ref, *refs):
        del c_ref
        for a in range(n):
            refs[2 * n + a][...] = (refs[a][...].astype(F32) + refs[n + a][...].astype(F32)).astype(BF16)

    def half_block(p):
        return (None,) + _half_shape(p.shape)

    def own_half(p):
        return lambda j, cr: (j,) + _half_block_index(p.shape, cr[0])

    grid_spec = pltpu.PrefetchScalarGridSpec(
        num_scalar_prefetch=1, grid=(4,),
        in_specs=[pl.BlockSpec(half_block(p), own_half(p)) for p in ps]
        + [pl.BlockSpec(half_block(p), lambda j, cr: (j, 0, 0)) for p in ps],
        out_specs=[pl.BlockSpec(half_block(p), lambda j, cr: (j, 0, 0)) for p in ps])
    return pl.pallas_call(
        body, name="grad_add_halves", grid_spec=grid_spec,
        out_shape=[jax.ShapeDtypeStruct((4,) + _half_shape(p.shape), BF16) for p in ps],
        compiler_params=_cparams(1),
    )(c_idx, *ps, *lands)


def _scatter_exchange(hs):
    n = len(hs)

    def copies(xin, xout, sems):
        send_sems, recv_sems, local_sems = sems
        x, y, c, chips = _my_place()
        k_me = 2 * x + y
        local = [pltpu.make_async_copy(xin[a].at[k_me], xout[a].at[k_me], local_sems.at[a]) for a in range(n)]
        sends, recvs = [], []
        for j, (px, py) in enumerate(chips):
            for a in range(n):
                sends.append(pltpu.make_async_remote_copy(
                    src_ref=xin[a].at[2 * px + py], dst_ref=xout[a].at[k_me], send_sem=send_sems.at[a, j],
                    recv_sem=recv_sems.at[a, j], device_id=(px, py, c), device_id_type=MESH))
                recvs.append(pltpu.make_async_remote_copy(
                    src_ref=xin[a].at[k_me], dst_ref=xout[a].at[2 * px + py], send_sem=send_sems.at[a, j],
                    recv_sem=recv_sems.at[a, j], device_id=(px, py, c), device_id_type=MESH))
        return local, sends, recvs

    def start(xin, xout, sems):
        local, sends, _ = copies(xin, xout, sems)
        for cp in local + sends:
            cp.start()

    def finish(xin, xout, sems):
        local, sends, recvs = copies(xin, xout, sems)
        for cp in recvs:
            cp.wait_recv()
        for cp in sends:
            cp.wait_send()
        for cp in local:
            cp.wait()

    return _Exchange(hs, [jax.ShapeDtypeStruct(h.shape, h.dtype) for h in hs],
                     [pltpu.SemaphoreType.DMA((n, 3)), pltpu.SemaphoreType.DMA((n, 3)), pltpu.SemaphoreType.DMA((n,))],
                     start, finish)


def _sum_chips(rcvs, shapes, accs, layer, n_layers, c_idx):
    n = len(rcvs)

    def body(c_ref, *refs):
        del c_ref
        outs = refs[len(refs) - n:]
        for a in range(n):
            r_ref = refs[a]
            acc = r_ref[0].astype(F32) + r_ref[1].astype(F32)
            acc = acc + r_ref[2].astype(F32)
            outs[a][...] = acc + r_ref[3].astype(F32)

    in_specs = [pl.BlockSpec(r.shape, lambda i, cr: (0, 0, 0)) for r in rcvs]
    args = list(rcvs)
    aliases = {}
    if accs is not None:
        in_specs += _any_specs(n)
        args += list(accs)
        aliases = {1 + n + a: a for a in range(n)}
    def own_half(shape):
        return lambda i, cr: (layer,) + _half_block_index(shape, cr[0])

    grid_spec = pltpu.PrefetchScalarGridSpec(
        num_scalar_prefetch=1, grid=(1,), in_specs=in_specs,
        out_specs=[pl.BlockSpec((None,) + r.shape[1:], own_half(sh)) for r, sh in zip(rcvs, shapes)])
    return pl.pallas_call(
        body, name="grad_sum_chips", grid_spec=grid_spec,
        out_shape=[jax.ShapeDtypeStruct((n_layers,) + tuple(sh), F32) for sh in shapes],
        input_output_aliases=aliases, compiler_params=_cparams(1),
    )(c_idx, *args)


def _sum_devices(g):
    _, r, n = g.shape

    def body(g_ref, o_ref):
        acc = g_ref[0]
        for d in range(1, 8):
            acc = acc + g_ref[d]
        o_ref[...] = acc

    return pl.pallas_call(
        body, name="small_grad_sum", grid=(1,),
        in_specs=[_full_spec((8, r, n))], out_specs=_full_spec((r, n)),
        out_shape=jax.ShapeDtypeStruct((r, n), F32), compiler_params=_cparams(1),
    )(g)


def _matmul_nt(a, wt, bn, name, exchange=None):
    s, k = a.shape
    n = wt.shape[0]
    bs = min(2048, s)

    def body(a_ref, w_ref, o_ref):
        o_ref[...] = _dot_nt(a_ref[...], w_ref[...])

    return _call(
        body, (a, wt), exchange, name=name, grid=(n // bn, s // bs),
        in_specs=[pl.BlockSpec((bs, k), lambda j, i: (i, 0)), pl.BlockSpec((bn, k), lambda j, i: (j, 0))],
        out_specs=[pl.BlockSpec((bs, bn), lambda j, i: (i, j))],
        out_shape=[jax.ShapeDtypeStruct((s, n), F32)], compiler_params=_cparams(2))


def _matmul_tn(a, b, bn, name, bm=None, whole_sum=False):
    s, m = a.shape
    n = b.shape[1]
    bs = s if whole_sum else min(512, s)
    bm = m if bm is None else bm

    def body(a_ref, b_ref, o_ref):
        if whole_sum:
            o_ref[...] = _dot_tn(a_ref[...], b_ref[...]).astype(BF16)
            return

        @pl.when(pl.program_id(2) == 0)
        def _():
            o_ref[...] = jnp.zeros_like(o_ref)

        o_ref[...] += _dot_tn(a_ref[...], b_ref[...])

    return pl.pallas_call(
        body, name=name, grid=(m // bm, n // bn, s // bs),
        in_specs=[pl.BlockSpec((bs, bm), lambda h, j, i: (i, h)), pl.BlockSpec((bs, bn), lambda h, j, i: (i, j))],
        out_specs=pl.BlockSpec((bm, bn), lambda h, j, i: (h, j)),
        out_shape=jax.ShapeDtypeStruct((m, n), BF16 if whole_sum else F32), compiler_params=_cparams(3),
    )(a, b)


def _modulate(x, scale, shift):
    s, d = x.shape
    t = min(512, s)

    def body(x_ref, sc_ref, sh_ref, u_ref):
        u_ref[...] = (x_ref[...] * (1.0 + sc_ref[...]) + sh_ref[...]).astype(BF16)

    return pl.pallas_call(
        body, name="modulate", grid=(s // t,),
        in_specs=[_row_spec(t, d), _full_spec((1, d)), _full_spec((1, d))],
        out_specs=_row_spec(t, d), out_shape=jax.ShapeDtypeStruct((s, d), BF16), compiler_params=_cparams(1),
    )(x, scale, shift)


def _rope_tables(pos_col):
    s = pos_col.shape[0]
    t = min(512, s)
    inv = ROPE_THETA ** (-np.arange(0, QK_ROPE, 2, dtype=np.float32) / QK_ROPE)
    lane_freq = np.zeros((1, HEAD_PAD), np.float32)
    lane_freq[0, QK_NOPE:QK_NOPE + 16] = inv
    lane_freq[0, QK_NOPE + 16:QK_NOPE + 32] = inv
    lane_mask = np.zeros((1, HEAD_PAD), np.float32)
    lane_mask[0, QK_NOPE:QK_NOPE + QK_ROPE] = 1.0

    def body(p_ref, f_ref, m_ref, cos_ref, sin_ref):
        ang = p_ref[...].astype(F32) * f_ref[...]
        cos_ref[...] = jnp.cos(ang) * m_ref[...]
        sin_ref[...] = jnp.sin(ang) * m_ref[...]

    return pl.pallas_call(
        body, name="rope_tables", grid=(s // t,),
        in_specs=[_row_spec(t, 1), _full_spec((1, HEAD_PAD)), _full_spec((1, HEAD_PAD))],
        out_specs=[_row_spec(t, HEAD_PAD), _row_spec(t, HEAD_PAD)],
        out_shape=[jax.ShapeDtypeStruct((s, HEAD_PAD), F32)] * 2, compiler_params=_cparams(1),
    )(pos_col, jnp.asarray(lane_freq), jnp.asarray(lane_mask))


def _ada_shard(c_all, w_ada_bf):
    n_layers, d, n = w_ada_bf.shape

    def body(c_ref, w_ref, o_ref, ca_ref):
        cv = c_ref[...]
        ca = cv * _sig(cv)
        ca_ref[...] = ca
        o_ref[...] = _dot(ca.astype(BF16), w_ref[...])

    return pl.pallas_call(
        body, name="ada_shard", grid=(n_layers,),
        in_specs=[_full_spec((8, d)), pl.BlockSpec((None, d, n), lambda l: (l, 0, 0))],
        out_specs=[pl.BlockSpec((None, 8, n), lambda l: (l, 0, 0)), _full_spec((8, d))],
        out_shape=[jax.ShapeDtypeStruct((n_layers, 8, n), F32), jax.ShapeDtypeStruct((8, d), F32)],
        compiler_params=_cparams(1),
    )(c_all, w_ada_bf)


def _ada_weight_grad(cact16, dada16):
    n_layers, _, n = dada16.shape
    d = cact16.shape[1]

    def body(c_ref, g_ref, o_ref):
        o_ref[...] = _dot_tn(c_ref[...], g_ref[...])

    return pl.pallas_call(
        body, name="ada_weight_grad", grid=(n_layers,),
        in_specs=[_full_spec((16, d)), pl.BlockSpec((None, 16, n), lambda l: (l, 0, 0))],
        out_specs=pl.BlockSpec((None, d, n), lambda l: (l, 0, 0)),
        out_shape=jax.ShapeDtypeStruct((n_layers, d, n), F32), compiler_params=_cparams(1),
    )(cact16, dada16)


def _layer_norm_rows(v, g, b):
    mu = jnp.mean(v, axis=-1, keepdims=True)
    var = jnp.mean(jnp.square(v - mu), axis=-1, keepdims=True)
    rstd = lax.rsqrt(var + LN_EPS)
    xh = (v - mu) * rstd
    return xh, rstd, xh * g + b


def _layer_norm_bwd_rows(dy_hat, xh, rstd):
    return rstd * (dy_hat - jnp.mean(dy_hat, axis=-1, keepdims=True) - xh * jnp.mean(dy_hat * xh, axis=-1, keepdims=True))


def _windows(buf_ref, offsets, t, shift_ref):
    for r in range(8):
        group = sorted(o for o in offsets if o % 8 == r)
        if not group:
            continue
        lo = group[0]
        if r == 0:
            for o in group:
                yield o, buf_ref[o:o + t, :]
        else:
            length = group[-1] - lo + t
            shift_ref[0:length, :] = buf_ref[lo:lo + length, :]
            for o in group:
                yield o, shift_ref[o - lo:o - lo + t, :]


def _mix_ab_fwd(proj, conv_a_w, conv_a_b, ln_a_g, ln_a_b, conv_b_w, exchange=None):
    s = proj.shape[0]
    t = min(ROW_BLOCK, s)
    cw = 512
    halo = CONV_HALO

    def body(p_ref, caw_ref, cab_ref, lg_ref, lb_ref, cbw_ref, acta_ref, actb_ref, ca_ref, cb_ref, abuf, zbuf, shifted):
        @pl.when(pl.program_id(0) == 0)
        def _():
            abuf[0:halo, :] = jnp.zeros((halo, cw), F32)
            zbuf[0:8, :] = jnp.zeros((8, cw), F32)

        abuf[halo:halo + t, :] = p_ref[:, 0:512] * _sig(p_ref[:, 512:1024])
        acc = jnp.zeros((t, cw), F32)
        first = halo - (CONV_K - 1)
        for off, win in _windows(abuf, range(first, first + CONV_K), t, shifted):
            acc = acc + caw_ref[off - first:off - first + 1, :] * win
        ca = acc + cab_ref[...]
        ca_ref[...] = ca
        _, _, ln = _layer_norm_rows(ca, lg_ref[...], lb_ref[...])
        ag = p_ref[:, 1024:1536]
        acta_ref[...] = ((ln * _sig(ln)) * (ag * _sig(ag))).astype(BF16)
        abuf[0:halo, :] = abuf[t:t + halo, :]

        zbuf[8:8 + t, :] = p_ref[:, 2560:3072] * p_ref[:, 1536:2048]
        accb = jnp.zeros((t, cw), F32)
        for k in range(SC_K):
            off = 8 - (SC_K - 1) + k
            accb = accb + cbw_ref[k:k + 1, :] * zbuf[off:off + t, :]
        cb_ref[...] = accb
        bg = p_ref[:, 3072:3584]
        actb_ref[...] = ((p_ref[:, 2048:2560] * accb) * (bg * _sig(bg))).astype(BF16)
        zbuf[0:8, :] = zbuf[t:t + 8, :]

    return _call(
        body, (proj, conv_a_w, conv_a_b, ln_a_g, ln_a_b, conv_b_w), exchange, name="mix_ab_fwd", grid=(s // t,),
        in_specs=[_row_spec(t, W_AB), _full_spec((CONV_K, cw)), _full_spec((1, cw)), _full_spec((1, cw)),
                  _full_spec((1, cw)), _full_spec((SC_K, cw))],
        out_specs=[_row_spec(t, cw)] * 4,
        out_shape=[jax.ShapeDtypeStruct((s, cw), BF16)] * 2 + [jax.ShapeDtypeStruct((s, cw), F32)] * 2,
        scratch_shapes=[pltpu.VMEM((halo + t, cw), F32), pltpu.VMEM((8 + t, cw), F32), pltpu.VMEM((halo + t, cw), F32)],
        compiler_params=_cparams(1))


def _mix_ab_bwd(proj, ca, cb, dact_a, dact_b, conv_a_w, ln_a_g, ln_a_b, conv_b_w, dproj):
    s = proj.shape[0]
    t = min(ROW_BLOCK, s)
    nb = s // t
    cw = 512
    halo = CONV_HALO
    per = t // halo

    def body(p_ref, ph_ref, ca_ref, cb_ref, da_ref, db_ref, caw_ref, lg_ref, lb_ref, cbw_ref, dproj_any,
             dg_ref, dcaw_ref, dcab_ref, dlg_ref, dlb_ref, dcbw_ref, abuf, dcabuf, zbuf, dcbbuf, shifted):
        del dproj_any
        i = pl.program_id(0)
        r = nb - 1 - i

        @pl.when(i == 0)
        def _():
            dcaw_ref[...] = jnp.zeros_like(dcaw_ref)
            dcab_ref[...] = jnp.zeros_like(dcab_ref)
            dlg_ref[...] = jnp.zeros_like(dlg_ref)
            dlb_ref[...] = jnp.zeros_like(dlb_ref)
            dcbw_ref[...] = jnp.zeros_like(dcbw_ref)
            dcabuf[t:t + halo, :] = jnp.zeros((halo, cw), F32)
            dcbbuf[t:t + 8, :] = jnp.zeros((8, cw), F32)

        keep = (r > 0).astype(F32)
        a1 = p_ref[:, 0:512]
        s2 = _sig(p_ref[:, 512:1024])
        abuf[0:halo, :] = (ph_ref[:, 0:512] * _sig(ph_ref[:, 512:1024])) * keep
        abuf[halo:halo + t, :] = a1 * s2
        ca_v = ca_ref[...]
        xh, rstd, ln = _layer_norm_rows(ca_v, lg_ref[...], lb_ref[...])
        s_ln = _sig(ln)
        ag = p_ref[:, 1024:1536]
        sg = _sig(ag)
        dact = da_ref[...]
        dsa = dact * (ag * sg)
        dg_ref[:, 1024:1536] = (dact * (ln * s_ln) * _dsilu(ag, sg)).astype(BF16)
        dln = dsa * _dsilu(ln, s_ln)
        dlg_ref[...] += jnp.sum(dln * xh, axis=0, keepdims=True)
        dlb_ref[...] += jnp.sum(dln, axis=0, keepdims=True)
        dca = _layer_norm_bwd_rows(dln * lg_ref[...], xh, rstd)
        dcab_ref[...] += jnp.sum(dca, axis=0, keepdims=True)
        dcabuf[0:t, :] = dca
        d_a = jnp.zeros((t, cw), F32)
        for off, win in _windows(dcabuf, range(CONV_K), t, shifted):
            k = (CONV_K - 1) - off
            d_a = d_a + caw_ref[k:k + 1, :] * win
        first = halo - (CONV_K - 1)
        for off, win in _windows(abuf, range(first, first + CONV_K), t, shifted):
            dcaw_ref[off - first:off - first + 1, :] += jnp.sum(dca * win, axis=0, keepdims=True)
        dg_ref[:, 0:512] = (d_a * s2).astype(BF16)
        dg_ref[:, 512:1024] = (d_a * a1 * s2 * (1.0 - s2)).astype(BF16)
        dcabuf[t:t + halo, :] = dcabuf[0:halo, :]

        xb = p_ref[:, 1536:2048]
        gb = p_ref[:, 2048:2560]
        gc = p_ref[:, 2560:3072]
        bg = p_ref[:, 3072:3584]
        zbuf[0:8, :] = (ph_ref[halo - 8:halo, 2560:3072] * ph_ref[halo - 8:halo, 1536:2048]) * keep
        zbuf[8:8 + t, :] = gc * xb
        sbg = _sig(bg)
        cbv = cb_ref[...]
        dactb = db_ref[...]
        dyb = dactb * (bg * sbg)
        dg_ref[:, 3072:3584] = (dactb * (gb * cbv) * _dsilu(bg, sbg)).astype(BF16)
        dg_ref[:, 2048:2560] = (dyb * cbv).astype(BF16)
        dcb = dyb * gb
        dcbbuf[0:t, :] = dcb
        dz = jnp.zeros((t, cw), F32)
        for k in range(SC_K):
            off = (SC_K - 1) - k
            dz = dz + cbw_ref[k:k + 1, :] * dcbbuf[off:off + t, :]
            src = 8 - (SC_K - 1) + k
            dcbw_ref[k:k + 1, :] += jnp.sum(dcb * zbuf[src:src + t, :], axis=0, keepdims=True)
        dg_ref[:, 2560:3072] = (dz * xb).astype(BF16)
        dg_ref[:, 1536:2048] = (dz * gc).astype(BF16)
        dcbbuf[t:t + 8, :] = dcbbuf[0:8, :]

    rev = lambda i: (nb - 1 - i, 0)
    outs = pl.pallas_call(
        body, name="mix_ab_bwd", grid=(nb,),
        in_specs=[pl.BlockSpec((t, W_AB), rev),
                  pl.BlockSpec((halo, W_AB), lambda i: (jnp.maximum((nb - 1 - i) * per - 1, 0), 0)),
                  pl.BlockSpec((t, cw), rev), pl.BlockSpec((t, cw), rev), pl.BlockSpec((t, cw), rev),
                  pl.BlockSpec((t, cw), rev),
                  _full_spec((CONV_K, cw)), _full_spec((1, cw)), _full_spec((1, cw)), _full_spec((SC_K, cw)),
                  pl.BlockSpec(memory_space=pl.ANY)],
        out_specs=[pl.BlockSpec((t, W_AB), rev), _full_spec((CONV_HALO, cw)), _full_spec((1, cw)), _full_spec((1, cw)),
                   _full_spec((1, cw)), _full_spec((8, cw))],
        out_shape=[jax.ShapeDtypeStruct(dproj.shape, BF16), jax.ShapeDtypeStruct((CONV_HALO, cw), F32),
                   jax.ShapeDtypeStruct((1, cw), F32), jax.ShapeDtypeStruct((1, cw), F32),
                   jax.ShapeDtypeStruct((1, cw), F32), jax.ShapeDtypeStruct((8, cw), F32)],
        scratch_shapes=[pltpu.VMEM((halo + t, cw), F32), pltpu.VMEM((t + halo, cw), F32),
                        pltpu.VMEM((8 + t, cw), F32), pltpu.VMEM((t + 8, cw), F32), pltpu.VMEM((halo + t, cw), F32)],
        input_output_aliases={10: 0},
        compiler_params=_cparams(1),
    )(proj, proj, ca, cb, dact_a, dact_b, conv_a_w, ln_a_g, ln_a_b, conv_b_w, dproj)
    return outs


def _lane_is_nope():
    return lax.broadcasted_iota(jnp.int32, (1, HEAD_PAD), 1) < QK_NOPE


def _attn_prep_fwd(proj, q_g, kv_g, wq12, wkv, tcos, tsin):
    s = proj.shape[0]
    t = min(ROW_BLOCK, s)
    c0 = (W_AB + W_G5) // W_C1

    def body(p_ref, qg_ref, kg_ref, wq_ref, wkv_ref, cos_ref, sin_ref, qf_ref, kf_ref, v_ref):
        ql = p_ref[:, 0:384]
        qn = (ql * lax.rsqrt(jnp.mean(jnp.square(ql), axis=-1, keepdims=True) + RMS_EPS) * qg_ref[...]).astype(BF16)
        q12 = _dot(qn, wq_ref[...])
        kvl = p_ref[:, 384:640]
        kvn = (kvl * lax.rsqrt(jnp.mean(jnp.square(kvl), axis=-1, keepdims=True) + RMS_EPS) * kg_ref[...]).astype(BF16)
        kv12 = _dot(kvn, wkv_ref[...])
        tcos_v = cos_ref[...]
        tsin_v = sin_ref[...]
        tq1 = jnp.where(_lane_is_nope(), 1.0, tcos_v)
        kpe = p_ref[:, 640:768] * tcos_v + p_ref[:, 768:896] * tsin_v
        for h in range(HEADS):
            lo = h * HEAD_PAD
            qf_ref[h] = (q12[:, lo:lo + HEAD_PAD] * tq1 + q12[:, 1024 + lo:1024 + lo + HEAD_PAD] * tsin_v).astype(BF16)
            kf_ref[h] = (kv12[:, lo:lo + HEAD_PAD] + kpe).astype(BF16)
        v_ref[...] = kv12[:, 1024:1536].astype(BF16)

    return pl.pallas_call(
        body, name="attn_prep_fwd", grid=(s // t,),
        in_specs=[pl.BlockSpec((t, W_C1), lambda i: (i, c0)), _full_spec((1, 384)), _full_spec((1, 256)),
                  _full_spec((384, 2048)), _full_spec((256, 1536)), _row_spec(t, HEAD_PAD), _row_spec(t, HEAD_PAD)],
        out_specs=[pl.BlockSpec((HEADS, t, HEAD_PAD), lambda i: (0, i, 0)),
                   pl.BlockSpec((HEADS, t, HEAD_PAD), lambda i: (0, i, 0)), _row_spec(t, 512)],
        out_shape=[jax.ShapeDtypeStruct((HEADS, s, HEAD_PAD), BF16), jax.ShapeDtypeStruct((HEADS, s, HEAD_PAD), BF16),
                   jax.ShapeDtypeStruct((s, 512), BF16)],
        compiler_params=_cparams(1),
    )(proj, q_g, kv_g, wq12, wkv, tcos, tsin)


def _attn_prep_bwd(proj, dqf, dkf, dv, q_g, kv_g, wq12, wkv, tcos, tsin, dproj):
    s = proj.shape[0]
    t = min(ROW_BLOCK, s)
    c0 = (W_AB + W_G5) // W_C1

    def body(p_ref, dqf_ref, dkf_ref, dv_ref, qg_ref, kg_ref, wq_ref, wkv_ref, cos_ref, sin_ref, dproj_any,
             dg_ref, qn_ref, kvn_ref, dq12_ref, dkv12_ref, dqg_ref, dkg_ref):
        del dproj_any

        @pl.when(pl.program_id(0) == 0)
        def _():
            dqg_ref[...] = jnp.zeros_like(dqg_ref)
            dkg_ref[...] = jnp.zeros_like(dkg_ref)

        tcos_v = cos_ref[...]
        tsin_v = sin_ref[...]
        tq1 = jnp.where(_lane_is_nope(), 1.0, tcos_v)
        dkpe = jnp.zeros((t, HEAD_PAD), F32)
        for h in range(HEADS):
            lo = h * HEAD_PAD
            dq = dqf_ref[h]
            dq12_ref[:, lo:lo + HEAD_PAD] = (dq * tq1).astype(BF16)
            dq12_ref[:, 1024 + lo:1024 + lo + HEAD_PAD] = (dq * tsin_v).astype(BF16)
            dk = dkf_ref[h]
            dkv12_ref[:, lo:lo + HEAD_PAD] = dk.astype(BF16)
            dkpe = dkpe + dk
        dkv12_ref[:, 1024:1536] = dv_ref[...].astype(BF16)
        dg_ref[:, 640:768] = (dkpe * tcos_v).astype(BF16)
        dg_ref[:, 768:896] = (dkpe * tsin_v).astype(BF16)

        def rms_bwd(xl, g, dn):
            rr = lax.rsqrt(jnp.mean(jnp.square(xl), axis=-1, keepdims=True) + RMS_EPS)
            xn = xl * rr
            tt = dn * g
            return rr * (tt - xn * jnp.mean(tt * xn, axis=-1, keepdims=True)), jnp.sum(dn * xn, axis=0, keepdims=True), xn

        ql = p_ref[:, 0:384]
        dqn = _dot_nt(dq12_ref[...], wq_ref[...])
        dql, dqg, qxn = rms_bwd(ql, qg_ref[...], dqn)
        dg_ref[:, 0:384] = dql.astype(BF16)
        dqg_ref[...] += dqg
        qn_ref[...] = (qxn * qg_ref[...]).astype(BF16)
        kvl = p_ref[:, 384:640]
        dkvn = _dot_nt(dkv12_ref[...], wkv_ref[...])
        dkvl, dkg, kxn = rms_bwd(kvl, kg_ref[...], dkvn)
        dg_ref[:, 384:640] = dkvl.astype(BF16)
        dkg_ref[...] += dkg
        kvn_ref[...] = (kxn * kg_ref[...]).astype(BF16)

    return pl.pallas_call(
        body, name="attn_prep_bwd", grid=(s // t,),
        in_specs=[pl.BlockSpec((t, W_C1), lambda i: (i, c0)),
                  pl.BlockSpec((HEADS, t, HEAD_PAD), lambda i: (0, i, 0)),
                  pl.BlockSpec((HEADS, t, HEAD_PAD), lambda i: (0, i, 0)), _row_spec(t, 512),
                  _full_spec((1, 384)), _full_spec((1, 256)), _full_spec((384, 2048)), _full_spec((256, 1536)),
                  _row_spec(t, HEAD_PAD), _row_spec(t, HEAD_PAD), pl.BlockSpec(memory_space=pl.ANY)],
        out_specs=[pl.BlockSpec((t, W_C1), lambda i: (i, c0)), _row_spec(t, 384), _row_spec(t, 256),
                   _row_spec(t, 2048), _row_spec(t, 1536), _full_spec((1, 384)), _full_spec((1, 256))],
        out_shape=[jax.ShapeDtypeStruct(dproj.shape, BF16), jax.ShapeDtypeStruct((s, 384), BF16),
                   jax.ShapeDtypeStruct((s, 256), BF16), jax.ShapeDtypeStruct((s, 2048), BF16),
                   jax.ShapeDtypeStruct((s, 1536), BF16), jax.ShapeDtypeStruct((1, 384), F32),
                   jax.ShapeDtypeStruct((1, 256), F32)],
        input_output_aliases={10: 0},
        compiler_params=_cparams(1),
    )(proj, dqf, dkf, dv, q_g, kv_g, wq12, wkv, tcos, tsin, dproj)


def _causal_keep(t):
    return lax.broadcasted_iota(jnp.int32, (t, t), 0) >= lax.broadcasted_iota(jnp.int32, (t, t), 1)


def _half_select(e):
    lane = lax.broadcasted_iota(jnp.int32, (1, HEAD_PAD), 1)
    return (lane < V_HEAD) if e == 0 else (lane >= V_HEAD)


def _attention_fwd(qf, kf, v, exchange=None):
    _, s, _ = qf.shape
    t = min(ATT_BLOCK, s)
    nq = s // t

    def body(q_ref, k_ref, v_ref, o_ref, lse_ref):
        def rows_of_block(blk):
            past = blk * t
            for pair in range(HEADS // 2):
                lo = pair * HEAD_PAD
                v_diag = v_ref[past:past + t, lo:lo + HEAD_PAD]
                v_past = v_ref[0:past, lo:lo + HEAD_PAD] if blk else None
                out = None
                for e in range(2):
                    h = 2 * pair + e
                    sel = _half_select(e)
                    q = q_ref[h]
                    s_diag = jnp.where(_causal_keep(t), _dot_nt(q, k_ref[h, past:past + t, :]) * ATT_SCALE, NEG_BIG)
                    m = jnp.max(s_diag, axis=1, keepdims=True)
                    if blk:
                        s_past = _dot_nt(q, k_ref[h, 0:past, :]) * ATT_SCALE
                        m = jnp.maximum(m, jnp.max(s_past, axis=1, keepdims=True))
                    p_diag = jnp.exp(s_diag - m)
                    norm = jnp.sum(p_diag, axis=1, keepdims=True)
                    pv = _dot(p_diag.astype(BF16), jnp.where(sel, v_diag, jnp.zeros_like(v_diag)))
                    if blk:
                        p_past = jnp.exp(s_past - m)
                        norm = norm + jnp.sum(p_past, axis=1, keepdims=True)
                        pv = pv + _dot(p_past.astype(BF16), jnp.where(sel, v_past, jnp.zeros_like(v_past)))
                    pv = pv * (1.0 / norm)
                    out = pv if out is None else out + pv
                    lse_ref[h] = m + jnp.log(norm)
                o_ref[:, lo:lo + HEAD_PAD] = out

        for blk in range(nq):
            pl.when(pl.program_id(0) == blk)(functools.partial(rows_of_block, blk))

    return _call(
        body, (qf, kf, v), exchange, name="attention_fwd", grid=(nq,),
        in_specs=[pl.BlockSpec((HEADS, t, HEAD_PAD), lambda i: (0, i, 0)), _full_spec((HEADS, s, HEAD_PAD)),
                  _full_spec((s, 512))],
        out_specs=[_row_spec(t, 512), pl.BlockSpec((HEADS, t, 1), lambda i: (0, i, 0))],
        out_shape=[jax.ShapeDtypeStruct((s, 512), F32), jax.ShapeDtypeStruct((HEADS, s, 1), F32)],
        compiler_params=_cparams(1))


def _attention_bwd(qf, kf, v, o, do, lse, exchange=None):
    _, s, _ = qf.shape
    t = min(ATT_BLOCK, s)
    nq = s // t

    def body(q_ref, k_ref, v_ref, o_ref, do_ref, lse_ref, dq_ref, dk_ref, dv_ref, dk_acc, dv_acc):
        kj = pl.program_id(0)
        qi = pl.program_id(1)

        @pl.when((kj == 0) & (qi == 0))
        def _():
            dq_ref[...] = jnp.zeros_like(dq_ref)

        @pl.when(qi == kj)
        def _():
            dk_acc[...] = jnp.zeros_like(dk_acc)
            dv_acc[...] = jnp.zeros_like(dv_acc)

        def step(diag):
            rows = pl.ds(pl.multiple_of(qi * t, t), t)
            for h in range(HEADS):
                pair, e = divmod(h, 2)
                sel = _half_select(e)
                lo = pair * HEAD_PAD
                q = q_ref[h]
                k = k_ref[h]
                sc = _dot_nt(q, k) * ATT_SCALE
                if diag:
                    sc = jnp.where(_causal_keep(t), sc, NEG_BIG)
                p = jnp.exp(sc - lse_ref[h])
                do_pair = do_ref[:, lo:lo + HEAD_PAD]
                do_e = jnp.where(sel, do_pair, 0.0)
                do_b = do_e.astype(BF16)
                v2 = v_ref[:, lo:lo + HEAD_PAD]
                dv_acc[:, lo:lo + HEAD_PAD] += _dot_tn(p.astype(BF16), do_b)
                dp = _dot_nt(do_b, jnp.where(sel, v2, jnp.zeros_like(v2)))
                delta = jnp.sum(do_e * o_ref[:, lo:lo + HEAD_PAD], axis=1, keepdims=True)
                ds = (p * (dp - delta) * ATT_SCALE).astype(BF16)
                dq_ref[h, rows, :] += _dot(ds, k)
                dk_acc[h] += _dot_tn(ds, q)

        @pl.when(qi > kj)
        def _():
            step(False)

        @pl.when(qi == kj)
        def _():
            step(True)

        @pl.when(qi == nq - 1)
        def _():
            dk_ref[...] = dk_acc[...]
            dv_ref[...] = dv_acc[...]

    qmap = lambda j, i: (0, jnp.maximum(i, j), 0)
    return _call(
        body, (qf, kf, v, o, do, lse), exchange, name="attention_bwd", grid=(nq, nq),
        in_specs=[pl.BlockSpec((HEADS, t, HEAD_PAD), qmap),
                  pl.BlockSpec((HEADS, t, HEAD_PAD), lambda j, i: (0, j, 0)),
                  pl.BlockSpec((t, 512), lambda j, i: (j, 0)),
                  pl.BlockSpec((t, 512), lambda j, i: (jnp.maximum(i, j), 0)),
                  pl.BlockSpec((t, 512), lambda j, i: (jnp.maximum(i, j), 0)),
                  pl.BlockSpec((HEADS, t, 1), qmap)],
        out_specs=[_full_spec((HEADS, s, HEAD_PAD)),
                   pl.BlockSpec((HEADS, t, HEAD_PAD), lambda j, i: (0, j, 0)),
                   pl.BlockSpec((t, 512), lambda j, i: (j, 0))],
        out_shape=[jax.ShapeDtypeStruct((HEADS, s, HEAD_PAD), F32), jax.ShapeDtypeStruct((HEADS, s, HEAD_PAD), F32),
                   jax.ShapeDtypeStruct((s, 512), F32)],
        scratch_shapes=[pltpu.VMEM((HEADS, t, HEAD_PAD), F32), pltpu.VMEM((t, 512), F32)],
        compiler_params=_cparams(2))


def _merge_fwd(act_a, act_b, o, proj, x, gate, w_a, w_b, w_c, w_o, ln_g, ln_b, alpha, exchange=None):
    s, d = x.shape
    t = min(ROW_BLOCK, s)

    def body(aa_ref, ab_ref, o_ref, p_ref, x_ref, gate_ref, wa_ref, wb_ref, wc_ref, wo_ref, lg_ref, lb_ref,
             xn_ref, xh_ref, rstd_ref):
        cg = p_ref[:, 0:512]
        act_c = (o_ref[...] * (cg * _sig(cg))).astype(BF16)
        m = _sig(p_ref[:, 512:1536]) * _dot(aa_ref[...], wa_ref[...])
        m = m + _sig(p_ref[:, 1536:2560]) * _dot(ab_ref[...], wb_ref[...])
        m = m + _sig(p_ref[:, 2560:3584]) * _dot(act_c, wc_ref[...])
        out = _dot(m.astype(BF16), wo_ref[...])
        z = alpha * x_ref[...] + gate_ref[...] * out
        xh, rstd, y = _layer_norm_rows(z, lg_ref[...], lb_ref[...])
        xn_ref[...] = y
        xh_ref[...] = xh
        rstd_ref[...] = rstd

    return _call(
        body, (act_a, act_b, o, proj, x, gate, w_a, w_b, w_c, w_o, ln_g, ln_b), exchange, name="merge_fwd", grid=(s // t,),
        in_specs=[_row_spec(t, 512), _row_spec(t, 512), _row_spec(t, 512),
                  pl.BlockSpec((t, W_G5), lambda i: (i, 1)), _row_spec(t, d), _full_spec((1, d)),
                  _full_spec((512, d)), _full_spec((512, d)), _full_spec((512, d)), _full_spec((d, d)),
                  _full_spec((1, d)), _full_spec((1, d))],
        out_specs=[_row_spec(t, d), _row_spec(t, d), _row_spec(t, 1)],
        out_shape=[jax.ShapeDtypeStruct((s, d), F32), jax.ShapeDtypeStruct((s, d), F32), jax.ShapeDtypeStruct((s, 1), F32)],
        compiler_params=_cparams(1))


def _merge_bwd(dy, xh, rstd, act_a, act_b, o, proj, gate, w_a, w_b, w_c, w_o, ln_g, alpha, exchange=None):
    s, d = dy.shape
    t = min(ROW_BLOCK, s)

    def body(dy_ref, xh_ref, rstd_ref, aa_ref, ab_ref, o_ref, p_ref, gate_ref, wa_ref, wb_ref, wc_ref, wo_ref, lg_ref,
             dxr_ref, dacta_ref, dactb_ref, do_ref, dg_ref, m_ref, dout_ref, dya_ref, dyb_ref, dyc_ref, actc_ref,
             dlg_ref, dlb_ref, dgate_ref):
        @pl.when(pl.program_id(0) == 0)
        def _():
            dlg_ref[...] = jnp.zeros_like(dlg_ref)
            dlb_ref[...] = jnp.zeros_like(dlb_ref)
            dgate_ref[...] = jnp.zeros_like(dgate_ref)

        dyv = dy_ref[...]
        xhv = xh_ref[...]
        dlg_ref[...] += jnp.sum(dyv * xhv, axis=0, keepdims=True)
        dlb_ref[...] += jnp.sum(dyv, axis=0, keepdims=True)
        dz = _layer_norm_bwd_rows(dyv * lg_ref[...], xhv, rstd_ref[...])
        dxr_ref[...] = alpha * dz

        cg = p_ref[:, 0:512]
        scg = _sig(cg)
        silu_cg = cg * scg
        ov = o_ref[...]
        act_c = (ov * silu_cg).astype(BF16)
        actc_ref[...] = act_c
        ya = _dot(aa_ref[...], wa_ref[...])
        yb = _dot(ab_ref[...], wb_ref[...])
        yc = _dot(act_c, wc_ref[...])
        ga = _sig(p_ref[:, 512:1536])
        gb = _sig(p_ref[:, 1536:2560])
        gc = _sig(p_ref[:, 2560:3584])
        mb = (ga * ya + gb * yb + gc * yc).astype(BF16)
        m_ref[...] = mb
        out = _dot(mb, wo_ref[...])
        dgate_ref[...] += jnp.sum(dz * out, axis=0, keepdims=True)
        dout = (gate_ref[...] * dz).astype(BF16)
        dout_ref[...] = dout
        dm = _dot_nt(dout, wo_ref[...])

        dya = (dm * ga).astype(BF16)
        dya_ref[...] = dya
        dg_ref[:, 512:1536] = (dm * ya * ga * (1.0 - ga)).astype(BF16)
        dacta_ref[...] = _dot_nt(dya, wa_ref[...])
        dyb = (dm * gb).astype(BF16)
        dyb_ref[...] = dyb
        dg_ref[:, 1536:2560] = (dm * yb * gb * (1.0 - gb)).astype(BF16)
        dactb_ref[...] = _dot_nt(dyb, wb_ref[...])
        dyc = (dm * gc).astype(BF16)
        dyc_ref[...] = dyc
        dg_ref[:, 2560:3584] = (dm * yc * gc * (1.0 - gc)).astype(BF16)
        dactc = _dot_nt(dyc, wc_ref[...])
        do_ref[...] = dactc * silu_cg
        dg_ref[:, 0:512] = (dactc * ov * _dsilu(cg, scg)).astype(BF16)

    return _call(
        body, (dy, xh, rstd, act_a, act_b, o, proj, gate, w_a, w_b, w_c, w_o, ln_g), exchange, name="merge_bwd", grid=(s // t,),
        in_specs=[_row_spec(t, d), _row_spec(t, d), _row_spec(t, 1), _row_spec(t, 512), _row_spec(t, 512),
                  _row_spec(t, 512), pl.BlockSpec((t, W_G5), lambda i: (i, 1)), _full_spec((1, d)),
                  _full_spec((512, d)), _full_spec((512, d)), _full_spec((512, d)), _full_spec((d, d)),
                  _full_spec((1, d))],
        out_specs=[_row_spec(t, d), _row_spec(t, 512), _row_spec(t, 512), _row_spec(t, 512),
                   pl.BlockSpec((t, W_G5), lambda i: (i, 1)),
                   _row_spec(t, d), _row_spec(t, d), _row_spec(t, d), _row_spec(t, d), _row_spec(t, d),
                   _row_spec(t, 512), _full_spec((1, d)), _full_spec((1, d)), _full_spec((1, d))],
        out_shape=[jax.ShapeDtypeStruct((s, d), F32), jax.ShapeDtypeStruct((s, 512), F32),
                   jax.ShapeDtypeStruct((s, 512), F32), jax.ShapeDtypeStruct((s, 512), F32),
                   jax.ShapeDtypeStruct((s, W_PACK), BF16),
                   jax.ShapeDtypeStruct((s, d), BF16), jax.ShapeDtypeStruct((s, d), BF16),
                   jax.ShapeDtypeStruct((s, d), BF16), jax.ShapeDtypeStruct((s, d), BF16),
                   jax.ShapeDtypeStruct((s, d), BF16), jax.ShapeDtypeStruct((s, 512), BF16),
                   jax.ShapeDtypeStruct((1, d), F32), jax.ShapeDtypeStruct((1, d), F32),
                   jax.ShapeDtypeStruct((1, d), F32)],
        compiler_params=_cparams(1))


def _proj_bwd_input(dproj, w_pack, x, scale, dxres, exchange=None):
    s, d = x.shape
    t = min(1024, s)
    nk = W_PACK // COL_BLK

    def body(dg_ref, w_ref, x_ref, sc_ref, dxr_ref, dx_ref, dsc_ref, dsh_ref, acc):
        i = pl.program_id(0)
        k = pl.program_id(1)

        @pl.when((i == 0) & (k == 0))
        def _():
            dsc_ref[...] = jnp.zeros_like(dsc_ref)
            dsh_ref[...] = jnp.zeros_like(dsh_ref)

        @pl.when(k == 0)
        def _():
            acc[...] = jnp.zeros_like(acc)

        acc[...] += _dot(dg_ref[...], w_ref[...])

        @pl.when(k == nk - 1)
        def _():
            du = acc[...]
            dx_ref[...] = du * (1.0 + sc_ref[...]) + dxr_ref[...]
            dsc_ref[...] += jnp.sum(du * x_ref[...], axis=0, keepdims=True)
            dsh_ref[...] += jnp.sum(du, axis=0, keepdims=True)

    return _call(
        body, (dproj, w_pack, x, scale, dxres), exchange, name="proj_bwd_input", grid=(s // t, nk),
        in_specs=[pl.BlockSpec((t, COL_BLK), lambda i, k: (i, k)), pl.BlockSpec((COL_BLK, d), lambda i, k: (k, 0)),
                  pl.BlockSpec((t, d), lambda i, k: (i, 0)), _full_spec((1, d)), pl.BlockSpec((t, d), lambda i, k: (i, 0))],
        out_specs=[pl.BlockSpec((t, d), lambda i, k: (i, 0)), _full_spec((1, d)), _full_spec((1, d))],
        out_shape=[jax.ShapeDtypeStruct((s, d), F32), jax.ShapeDtypeStruct((1, d), F32), jax.ShapeDtypeStruct((1, d), F32)],
        scratch_shapes=[pltpu.VMEM((t, d), F32)],
        compiler_params=_cparams(2))


def _loss_head(y, target):
    s, d = y.shape
    t = min(512, s)

    def body(y_ref, t_ref, dy_ref, loss_ref):
        @pl.when(pl.program_id(0) == 0)
        def _():
            loss_ref[...] = jnp.zeros_like(loss_ref)

        err = y_ref[...] - t_ref[...]
        dy_ref[...] = err / d
        part = 0.5 * jnp.sum(jnp.mean(jnp.square(err), axis=-1, keepdims=True), axis=0, keepdims=True)
        loss_ref[...] += jnp.broadcast_to(part, loss_ref.shape)

    return pl.pallas_call(
        body, name="loss_head", grid=(s // t,),
        in_specs=[_row_spec(t, d), _row_spec(t, d)],
        out_specs=[_row_spec(t, d), _full_spec((1, 128))],
        out_shape=[jax.ShapeDtypeStruct((s, d), F32), jax.ShapeDtypeStruct((1, 128), F32)],
        compiler_params=_cparams(1),
    )(y, target)


def _adamw(g, w, m, v, name):
    r, n = g.shape
    block_bytes = 5 * 2 ** 19
    rb = r
    if r * n * 4 > block_bytes:
        rb = max(k for k in range(8, r, 8) if r % k == 0 and k * n * 4 <= block_bytes)

    def body(g_ref, w_ref, m_ref, v_ref, d_ref, mo_ref, vo_ref):
        gv = g_ref[...]
        mn = ADAM_B1 * m_ref[...] + (1.0 - ADAM_B1) * gv
        vn = ADAM_B2 * v_ref[...] + (1.0 - ADAM_B2) * jnp.square(gv)
        m_hat = mn / (1.0 - ADAM_B1 ** ADAM_STEP)
        v_hat = vn / (1.0 - ADAM_B2 ** ADAM_STEP)
        d_ref[...] = -ADAM_LR * (m_hat / (jnp.sqrt(v_hat) + ADAM_EPS) + ADAM_WD * w_ref[...])
        mo_ref[...] = mn
        vo_ref[...] = vn

    spec = pl.BlockSpec((rb, n), lambda i: (i, 0))
    return pl.pallas_call(
        body, name=name, grid=(r // rb,),
        in_specs=[spec] * 4, out_specs=[spec] * 3,
        out_shape=[jax.ShapeDtypeStruct((r, n), F32)] * 3, compiler_params=_cparams(1),
    )(g, w, m, v)


SMALL = ("b_ada", "conv_a_b", "ln_a_g", "ln_a_b", "q_norm_g", "kv_norm_g", "ln_g", "ln_b", "conv_a_w", "conv_b_w")
ROWS_256 = (("w_a_out", 512), ("w_b_out", 512), ("w_c_out", 512), ("w_ukv", 256))


def _rot_cols(w):
    return jnp.concatenate([-w[..., 16:], w[..., :16]], axis=-1)


def _unrot_cols(g):
    return jnp.concatenate([g[..., 16:], -g[..., :16]], axis=-1)


def _pack_w_in(wt):
    z = lambda n: jnp.zeros((n, wt.shape[1]), wt.dtype)
    wk = wt[4224:4256]
    rot = jnp.concatenate([-wk[16:], wk[:16]], axis=0)
    return jnp.concatenate([wt[0:3584], wt[4256:7840], wt[3584:4224], z(64), wk, z(32), z(64), rot, z(32)], axis=0)


def _unpack_w_in_grad(g):
    c1 = W_AB + W_G5
    g_rot = g[c1 + 768 + 64:c1 + 768 + 96].astype(F32)
    gk = g[c1 + 640 + 64:c1 + 640 + 96].astype(F32) + jnp.concatenate([g_rot[16:], -g_rot[:16]], axis=0)
    return jnp.concatenate([g[0:3584], g[c1:c1 + 640], gk.astype(g.dtype), g[W_AB:W_AB + W_G5]], axis=0)


def _pack_w_uq(w):
    lead = w.shape[:-1]
    wh = w.reshape(lead + (HEADS, QK_NOPE + QK_ROPE))
    nope, rope = wh[..., :QK_NOPE], wh[..., QK_NOPE:]
    z32 = jnp.zeros_like(rope)
    q1 = jnp.concatenate([nope, rope, z32], axis=-1).reshape(lead + (HEADS * HEAD_PAD,))
    q2 = jnp.concatenate([jnp.zeros_like(nope), _rot_cols(rope), z32], axis=-1).reshape(lead + (HEADS * HEAD_PAD,))
    return jnp.concatenate([q1, q2], axis=-1)


def _unpack_w_uq_grad(g):
    lead = g.shape[:-1]
    g1 = g[..., :1024].reshape(lead + (HEADS, HEAD_PAD))
    g2 = g[..., 1024:].reshape(lead + (HEADS, HEAD_PAD))
    rope = g1[..., QK_NOPE:QK_NOPE + QK_ROPE] + _unrot_cols(g2[..., QK_NOPE:QK_NOPE + QK_ROPE])
    return jnp.concatenate([g1[..., :QK_NOPE], rope], axis=-1).reshape(lead + (HEADS * (QK_NOPE + QK_ROPE),))


def _pack_w_ukv(w):
    lead = w.shape[:-1]
    wh = w.reshape(lead + (HEADS, QK_NOPE + V_HEAD))
    kn, vv = wh[..., :QK_NOPE], wh[..., QK_NOPE:]
    k1 = jnp.concatenate([kn, jnp.zeros_like(kn)], axis=-1).reshape(lead + (HEADS * HEAD_PAD,))
    return jnp.concatenate([k1, vv.reshape(lead + (HEADS * V_HEAD,))], axis=-1)


def _unpack_w_ukv_grad(g):
    lead = g.shape[:-1]
    gk = g[..., :1024].reshape(lead + (HEADS, HEAD_PAD))[..., :QK_NOPE]
    gv = g[..., 1024:].reshape(lead + (HEADS, V_HEAD))
    return jnp.concatenate([gk, gv], axis=-1).reshape(lead + (HEADS * (QK_NOPE + V_HEAD),))


def _join_cols(g4):
    return jnp.concatenate([g4[j] for j in range(4)], axis=-1)


def _split_cols(w):
    c4 = w.shape[-1] // 4
    return jnp.stack([w[:, j * c4:(j + 1) * c4] for j in range(4)])


def kernel(x, c, positions, w_ada, b_ada, w_in, conv_a_w, conv_a_b, ln_a_g, ln_a_b, w_a_out, conv_b_w, w_b_out, q_norm_g, kv_norm_g, w_uq, w_ukv, w_c_out, w_o, ln_g, ln_b, loss_target, m_w_ada, m_b_ada, m_w_in, m_conv_a_w, m_conv_a_b, m_ln_a_g, m_ln_a_b, m_w_a_out, m_conv_b_w, m_w_b_out, m_q_norm_g, m_kv_norm_g, m_w_uq, m_w_ukv, m_w_c_out, m_w_o, m_ln_g, m_ln_b, v_w_ada, v_b_ada, v_w_in, v_conv_a_w, v_conv_a_b, v_ln_a_g, v_ln_a_b, v_w_a_out, v_conv_b_w, v_w_b_out, v_q_norm_g, v_kv_norm_g, v_w_uq, v_w_ukv, v_w_c_out, v_w_o, v_ln_g, v_ln_b):
    weights = dict(w_ada=w_ada, b_ada=b_ada, w_in=w_in, conv_a_w=conv_a_w, conv_a_b=conv_a_b, ln_a_g=ln_a_g, ln_a_b=ln_a_b,
                   w_a_out=w_a_out, conv_b_w=conv_b_w, w_b_out=w_b_out, q_norm_g=q_norm_g, kv_norm_g=kv_norm_g, w_uq=w_uq,
                   w_ukv=w_ukv, w_c_out=w_c_out, w_o=w_o, ln_g=ln_g, ln_b=ln_b)
    mom_m = dict(w_ada=m_w_ada, b_ada=m_b_ada, w_in=m_w_in, conv_a_w=m_conv_a_w, conv_a_b=m_conv_a_b, ln_a_g=m_ln_a_g,
                 ln_a_b=m_ln_a_b, w_a_out=m_w_a_out, conv_b_w=m_conv_b_w, w_b_out=m_w_b_out, q_norm_g=m_q_norm_g,
                 kv_norm_g=m_kv_norm_g, w_uq=m_w_uq, w_ukv=m_w_ukv, w_c_out=m_w_c_out, w_o=m_w_o, ln_g=m_ln_g, ln_b=m_ln_b)
    mom_v = dict(w_ada=v_w_ada, b_ada=v_b_ada, w_in=v_w_in, conv_a_w=v_conv_a_w, conv_a_b=v_conv_a_b, ln_a_g=v_ln_a_g,
                 ln_a_b=v_ln_a_b, w_a_out=v_w_a_out, conv_b_w=v_conv_b_w, w_b_out=v_w_b_out, q_norm_g=v_q_norm_g,
                 kv_norm_g=v_kv_norm_g, w_uq=v_w_uq, w_ukv=v_w_ukv, w_c_out=v_w_c_out, w_o=v_w_o, ln_g=v_ln_g, ln_b=v_ln_b)
    order = ["w_ada", "b_ada", "w_in", "conv_a_w", "conv_a_b", "ln_a_g", "ln_a_b", "w_a_out", "conv_b_w", "w_b_out",
             "q_norm_g", "kv_norm_g", "w_uq", "w_ukv", "w_c_out", "w_o", "ln_g", "ln_b"]

    n_layers = w_in.shape[0]
    s, d = x.shape[1], x.shape[2]
    alpha = float((2 * n_layers) ** 0.25)
    ix, iy, ic = lax.axis_index("x"), lax.axis_index("y"), lax.axis_index("c")
    me = 4 * ix + 2 * iy + ic
    chip = 2 * ix + iy
    xs = x[0]

    def four(tree, cast):
        return [cast(jnp.swapaxes(tree["w_in"], 1, 2)), jnp.concatenate([cast(tree[n]) for n, _ in ROWS_256], axis=-2),
                cast(tree["w_uq"]), cast(tree["w_o"])]

    shards = four(weights, lambda a: a.astype(BF16))
    landed_w = _run_exchange(_gather_layer_exchange(shards, 0), "gather_weights_first")
    landed_w = _run_exchange(_gather_pair_exchange(landed_w), "gather_weights_first_pair")
    w_pack, wq12, wkv, w_a, w_b, w_c, w_of = [], [], [], [], [], [], []

    def unpack_layer(g_in, g_256, g_uq, g_wo):
        w_pack.append(_pack_w_in(g_in.reshape(-1, d)))
        wq12.append(_pack_w_uq(_join_cols(g_uq)))
        rows = _join_cols(g_256)
        w_a.append(rows[0:512])
        w_b.append(rows[512:1024])
        w_c.append(rows[1024:1536])
        wkv.append(_pack_w_ukv(rows[1536:1792]))
        w_of.append(g_wo.reshape(d, d))

    cw_a = conv_a_w.reshape(n_layers * CONV_K, 128)
    cw_b = conv_b_w.reshape(n_layers * SC_K, 128)
    c_rows = d // 128
    first_rows = c_rows + n_layers * (CONV_K + SC_K)
    first_pad = (-first_rows) % 8
    first = jnp.concatenate([c.reshape(c_rows, 128), cw_a, cw_b, jnp.zeros((first_pad, 128), F32)], axis=0)
    fr = first_rows + first_pad
    got = _all_gather_small(first, "gather_cond").reshape(8, fr, 128)
    c_all = got[:, :c_rows].reshape(8, d)
    conv_a_full = jnp.moveaxis(got[0::2, c_rows:c_rows + n_layers * CONV_K].reshape(4, n_layers, CONV_K, 128), 0, 2)
    conv_a_full = conv_a_full.reshape(n_layers, CONV_K, 512)
    b0 = c_rows + n_layers * CONV_K
    conv_b_full = jnp.moveaxis(got[0::2, b0:b0 + n_layers * SC_K].reshape(4, n_layers, SC_K, 128), 0, 2)
    conv_b_full = conv_b_full.reshape(n_layers, SC_K, 512)

    ada_sh, c_act = _ada_shard(c_all, w_ada.astype(BF16))
    ada_all = _all_gather_small(ada_sh.reshape(n_layers * 8, 768), "gather_ada").reshape(8, n_layers, 8, 768)
    ada_mine = lax.dynamic_index_in_dim(ada_all[0::2], me, axis=2, keepdims=False)
    ada = jnp.moveaxis(ada_mine, 0, 1).reshape(n_layers, 3 * d) + b_ada
    shift, scale, gate = ada[:, 0:d], ada[:, d:2 * d], ada[:, 2 * d:3 * d]

    tcos, tsin = _rope_tables(positions.reshape(s, 1))

    saved = []
    h = xs
    for l in range(n_layers):
        unpack_layer(*landed_w)
        u = _modulate(h, scale[l:l + 1], shift[l:l + 1])
        more = l + 1 < n_layers
        (proj,), lw_in = _matmul_nt(u, w_pack[l], COL_BLK, "proj_fwd",
                                    _gather_layer_exchange(shards[:1], l + 1, (0, 2)) if more else None)
        (act_a, act_b, ca, cb), lw_rest = _mix_ab_fwd(
            proj, conv_a_full[l], conv_a_b[l:l + 1], ln_a_g[l:l + 1], ln_a_b[l:l + 1], conv_b_full[l],
            _gather_layer_exchange(shards[1:], l + 1) if more else None)
        qf, kf, vv = _attn_prep_fwd(proj, q_norm_g[l:l + 1], kv_norm_g[l:l + 1], wq12[l], wkv[l], tcos, tsin)
        (o, lse), lw_in = _attention_fwd(
            qf, kf, vv, _gather_layer_exchange(shards[:1], l + 1, (1, 2), lw_in) if more else None)
        (h_next, xh, rstd), landed_w = _merge_fwd(
            act_a, act_b, o, proj, h, gate[l:l + 1], w_a[l], w_b[l], w_c[l], w_of[l], ln_g[l:l + 1], ln_b[l:l + 1], alpha,
            _gather_pair_exchange(lw_in + lw_rest) if more else None)
        saved.append(dict(x=h, u=u, proj=proj, act_a=act_a, act_b=act_b, ca=ca, cb=cb, qf=qf, kf=kf, v=vv, o=o, lse=lse,
                          xh=xh, rstd=rstd))
        h = h_next

    dy, loss_part = _loss_head(h, loss_target[0])
    loss = lax.psum(loss_part[0, 0], ("x", "y", "c"))

    c_idx = ic.reshape(1).astype(jnp.int32)
    reduced = None
    pending = None
    small_rows = [None] * n_layers
    for l in reversed(range(n_layers)):
        sv = saved[l]
        ((dxres, dact_a, dact_b, do, dproj, m_bf, dout_bf, dya, dyb, dyc, act_c, d_ln_g, d_ln_b, d_gate),
         rcv_rest) = _merge_bwd(
            dy, sv["xh"], sv["rstd"], sv["act_a"], sv["act_b"], sv["o"], sv["proj"], gate[l:l + 1], w_a[l], w_b[l],
            w_c[l], w_of[l], ln_g[l:l + 1], alpha, _scatter_exchange(pending[1:]) if pending else None)
        g_o = _matmul_tn(m_bf, dout_bf, d, "grad_w_o")
        g_a = _matmul_tn(sv["act_a"], dya, d, "grad_w_a_out")
        g_b = _matmul_tn(sv["act_b"], dyb, d, "grad_w_b_out")
        g_c = _matmul_tn(act_c, dyc, d, "grad_w_c_out")
        (dqf, dkf, dvv), rcv_in = _attention_bwd(sv["qf"], sv["kf"], sv["v"], sv["o"], do, sv["lse"],
                                                 _scatter_exchange(pending[:1]) if pending else None)
        if pending:
            reduced = _sum_chips(rcv_in + rcv_rest, shapes, reduced, l + 1, n_layers, c_idx)
        dproj, qn, kvn, dq12, dkv12, d_qg, d_kg = _attn_prep_bwd(
            sv["proj"], dqf, dkf, dvv, q_norm_g[l:l + 1], kv_norm_g[l:l + 1], wq12[l], wkv[l], tcos, tsin, dproj)
        g_uq = _unpack_w_uq_grad(_matmul_tn(qn, dq12, 1024, "grad_w_uq"))
        g_kv = _unpack_w_ukv_grad(_matmul_tn(kvn, dkv12, 768, "grad_w_ukv"))
        dproj, d_caw, d_cab, d_lag, d_lab, d_cbw = _mix_ab_bwd(
            sv["proj"], sv["ca"], sv["cb"], dact_a, dact_b, conv_a_full[l], ln_a_g[l:l + 1], ln_a_b[l:l + 1],
            conv_b_full[l], dproj)
        g_in_l = _unpack_w_in_grad(_matmul_tn(dproj, sv["u"], d, "grad_w_in", bm=COL_BLK, whole_sum=True))
        partial = [g_in_l.reshape(4, -1, d),
                   jnp.concatenate([_split_cols(g) for g in (g_a, g_b, g_c, g_kv)], axis=1).astype(BF16),
                   _split_cols(g_uq).astype(BF16), g_o.reshape(4, d // 4, d).astype(BF16)]
        shapes = [p.shape[1:] for p in partial]
        (dy, d_scale, d_shift), from_sibling = _proj_bwd_input(dproj, w_pack[l], sv["x"], scale[l:l + 1], dxres,
                                                               _swap_exchange(partial))
        small_rows[l] = jnp.concatenate(
            [d_shift[0], d_scale[0], d_gate[0], d_cab[0], d_lag[0], d_lab[0], d_qg[0], d_kg[0], d_ln_g[0], d_ln_b[0],
             d_caw[:CONV_K].reshape(-1), d_cbw[:SC_K].reshape(-1)])
        pending = _add_halves(partial, from_sibling, c_idx)
    grad_x = dy[None]
    received = _run_exchange(_scatter_exchange(pending), "grad_scatter_last")
    reduced = _sum_chips(received, shapes, reduced, 0, n_layers, c_idx)
    r_in, r_256, r_uq, r_wo = _run_exchange(_gather_pair_exchange(reduced), "grad_pair_halves")
    grads = {"w_in": jnp.swapaxes(r_in, 1, 2), "w_uq": r_uq, "w_o": r_wo}
    row0 = 0
    for n, rows in ROWS_256:
        grads[n] = r_256[:, row0:row0 + rows]
        row0 += rows

    small = jnp.stack(small_rows)
    p_small = small.shape[1]
    n_small = n_layers * p_small
    pad_small = (-n_small) % (8 * 128)
    small_flat = jnp.concatenate([small.reshape(-1), jnp.zeros((pad_small,), F32)]).reshape(-1, 128)
    sr = small_flat.shape[0]
    small_all = _all_gather_small(small_flat, "gather_small_grads").reshape(8, sr, 128)
    small_sum = _sum_devices(small_all).reshape(-1)[:n_small].reshape(n_layers, p_small)
    d_ada_all = small_all.reshape(8, -1)[:, :n_small].reshape(8, n_layers, p_small)[:, :, :3 * d]

    cuts = np.cumsum([0, 3 * d, 512, 512, 512, 384, 256, d, d, CONV_K * 512, SC_K * 512])
    pieces = [small_sum[:, cuts[k]:cuts[k + 1]] for k in range(10)]
    grads.update(zip(SMALL, pieces[:8]))
    ga_full = pieces[8].reshape(n_layers, CONV_K, 512)
    gb_full = pieces[9].reshape(n_layers, SC_K, 512)
    grads["conv_a_w"] = lax.dynamic_slice_in_dim(ga_full, chip * 128, 128, axis=2)
    grads["conv_b_w"] = lax.dynamic_slice_in_dim(gb_full, chip * 128, 128, axis=2)

    dada_sh = lax.dynamic_slice_in_dim(d_ada_all, chip * 768, 768, axis=2)
    dada16 = jnp.concatenate([jnp.moveaxis(dada_sh, 0, 1), jnp.zeros((n_layers, 8, 768), F32)], axis=1).astype(BF16)
    cact16 = jnp.concatenate([c_act, jnp.zeros_like(c_act)], axis=0).astype(BF16)
    grads["w_ada"] = _ada_weight_grad(cact16, dada16)

    def as2d(a):
        return a.reshape(-1, a.shape[-1])

    deltas, new_m, new_v = {}, {}, {}
    for n in ("w_in", "w_a_out", "w_b_out", "w_c_out", "w_uq", "w_ukv", "w_o", "w_ada"):
        view = (lambda a: jnp.swapaxes(a, 1, 2)) if n == "w_in" else (lambda a: a)
        shape = view(weights[n]).shape
        dl, mn, vn = _adamw(as2d(view(grads[n])), as2d(view(weights[n])), as2d(view(mom_m[n])), as2d(view(mom_v[n])),
                            "adamw_" + n)
        deltas[n], new_m[n], new_v[n] = view(dl.reshape(shape)), view(mn.reshape(shape)), view(vn.reshape(shape))

    def pack_small(tree):
        flat = jnp.concatenate([tree[n].reshape(-1) for n in SMALL])
        pad = (-flat.shape[0]) % (8 * 128)
        return jnp.concatenate([flat, jnp.zeros((pad,), F32)]).reshape(-1, 128)

    dl_s, mn_s, vn_s = _adamw(pack_small(grads), pack_small(weights), pack_small(mom_m), pack_small(mom_v), "adamw_small")
    off = 0
    for n in SMALL:
        sz = int(np.prod(weights[n].shape))
        for tree, flat in ((deltas, dl_s), (new_m, mn_s), (new_v, vn_s)):
            tree[n] = flat.reshape(-1)[off:off + sz].reshape(weights[n].shape)
        off += sz

    return (loss, grad_x, *[grads[n] for n in order], *[deltas[n] for n in order], *[new_m[n] for n in order],
            *[new_v[n] for n in order])
```

```python
import functools

import numpy as np
import jax
import jax.numpy as jnp
from jax import lax
from jax.experimental import pallas as pl
from jax.experimental.pallas import tpu as pltpu

F32 = jnp.float32
BF16 = jnp.bfloat16
MESH = pl.DeviceIdType.MESH

HEADS = 8
QK_NOPE = 64
QK_ROPE = 32
V_HEAD = 64
CONV_K = 31
SC_K = 3
LN_EPS = 1e-5
RMS_EPS = 1e-6
ROPE_THETA = 10000.0
ATT_SCALE = (QK_NOPE + QK_ROPE) ** -0.5
ADAM_LR = 0.001
ADAM_B1 = 0.9
ADAM_B2 = 0.999
ADAM_EPS = 1e-08
ADAM_WD = 0.01
ADAM_STEP = 10

W_AB = 3584
W_G5 = 3584
W_C1 = 896
W_PACK = W_AB + W_G5 + W_C1
COL_BLK = 896
HEAD_PAD = 128
CONV_HALO = 32
NEG_BIG = -1e30

VMEM_LIMIT_MB = 56
ATT_BLOCK = 512
ROW_BLOCK = 256


def _cparams(n_grid, mb=VMEM_LIMIT_MB):
    return pltpu.CompilerParams(dimension_semantics=("arbitrary",) * n_grid, vmem_limit_bytes=mb * 2 ** 20)


def _sig(x):
    return 1.0 / (1.0 + jnp.exp(-x))


def _dsilu(x, s):
    return s * (1.0 + x * (1.0 - s))


def _dot(a, b):
    return jnp.dot(a, b, preferred_element_type=F32)


def _dot_nt(a, b):
    return lax.dot_general(a, b, (((1,), (1,)), ((), ())), preferred_element_type=F32)


def _dot_tn(a, b):
    return lax.dot_general(a, b, (((0,), (0,)), ((), ())), preferred_element_type=F32)


def _row_spec(t, w):
    return pl.BlockSpec((t, w), lambda i: (i, 0))


def _full_spec(shape):
    nd = len(shape)
    return pl.BlockSpec(shape, lambda *_: (0,) * nd)


def _layer_param(p):
    arr, layer = p
    return arr, pl.BlockSpec((None,) + arr.shape[1:], lambda *_: (layer, 0, 0))


def _my_place():
    x, y, c = lax.axis_index("x"), lax.axis_index("y"), lax.axis_index("c")
    chips = [(1 - x, y), (x, 1 - y), (1 - x, 1 - y)]
    return x, y, c, chips


def _all_gather_small(v, name):
    m_per, n = v.shape

    def body(x_ref, out_ref, send_sems, recv_sems, local_sem):
        x, y, c, chips = _my_place()
        me, sibling = (x, y, c), (x, y, 1 - c)

        def rows(px, py, pc):
            return out_ref.at[pl.ds((4 * px + 2 * py + pc) * m_per, m_per), :]

        def copy(k, block, to, src=None):
            return pltpu.make_async_remote_copy(
                src_ref=rows(*block) if src is None else src, dst_ref=rows(*block),
                send_sem=send_sems.at[k], recv_sem=recv_sems.at[k], device_id=to, device_id_type=MESH)

        mine = pltpu.make_async_copy(x_ref, rows(*me), local_sem)
        mine.start()
        first = [copy(0, me, sibling, src=x_ref)]
        first += [copy(1 + j, me, (*chip, c), src=x_ref) for j, chip in enumerate(chips)]
        for cp in first:
            cp.start()
        passed = [copy(4 + j, (*chip, c), sibling) for j, chip in enumerate(chips)]
        for j, chip in enumerate(chips):
            copy(1 + j, (*chip, c), me).wait_recv()
            passed[j].start()
        copy(0, sibling, me).wait_recv()
        for j, chip in enumerate(chips):
            copy(4 + j, (*chip, 1 - c), me).wait_recv()
        for cp in first + passed:
            cp.wait_send()
        mine.wait()

    return pl.pallas_call(
        body, name=name,
        out_shape=jax.ShapeDtypeStruct((8 * m_per, n), v.dtype),
        in_specs=[pl.BlockSpec(memory_space=pltpu.VMEM)],
        out_specs=pl.BlockSpec(memory_space=pltpu.VMEM),
        scratch_shapes=[pltpu.SemaphoreType.DMA((7,)), pltpu.SemaphoreType.DMA((7,)), pltpu.SemaphoreType.DMA],
        compiler_params=pltpu.CompilerParams(vmem_limit_bytes=VMEM_LIMIT_MB * 2 ** 20),
    )(v)


def _any_specs(n):
    return [pl.BlockSpec(memory_space=pl.ANY)] * n


def _splits_rows(shape):
    return shape[-2] % 32 == 0


def _window(shape, c, p=0, n_parts=1):
    r, n = shape[-2], shape[-1]
    if _splits_rows(shape):
        size = r // 2 // n_parts
        return (pl.ds(c * (r // 2) + p * size, size), slice(None))
    size = n // 2 // n_parts
    return (slice(None), pl.ds(c * (n // 2) + p * size, size))


def _half_shape(shape):
    r, n = shape[-2], shape[-1]
    return (r // 2, n) if _splits_rows(shape) else (r, n // 2)


def _half_block_index(shape, c):
    return (c, 0) if _splits_rows(shape) else (0, c)


class _Exchange:
    def __init__(self, inputs, out_shapes, sems, start, finish, aliases=None):
        self.inputs, self.out_shapes, self.sems = list(inputs), list(out_shapes), list(sems)
        self.start, self.finish, self.aliases = start, finish, dict(aliases or {})


def _call(body, args, exchange=None, *, grid, in_specs, out_specs, out_shape, scratch_shapes=(),
          input_output_aliases=None, **kw):
    aliases = dict(input_output_aliases or {})
    if exchange is None:
        outs = pl.pallas_call(body, grid=grid, in_specs=list(in_specs), out_specs=list(out_specs),
                              out_shape=list(out_shape), scratch_shapes=list(scratch_shapes),
                              input_output_aliases=aliases, **kw)(*args)
        return list(outs), []
    n_in, n_out, n_scr = len(in_specs), len(out_specs), len(scratch_shapes)
    x_in, x_out = len(exchange.inputs), len(exchange.out_shapes)

    def wrapped(*refs):
        ins, xin = refs[:n_in], refs[n_in:n_in + x_in]
        o0 = n_in + x_in
        outs, xout = refs[o0:o0 + n_out], refs[o0 + n_out:o0 + n_out + x_out]
        s0 = o0 + n_out + x_out
        scr, xsem = refs[s0:s0 + n_scr], refs[s0 + n_scr:]
        ids = [pl.program_id(k) for k in range(len(grid))]
        first = functools.reduce(jnp.logical_and, [i == 0 for i in ids])
        last = functools.reduce(jnp.logical_and, [i == g - 1 for i, g in zip(ids, grid)])

        @pl.when(first)
        def _():
            exchange.start(xin, xout, xsem)

        body(*ins, *outs, *scr)

        @pl.when(last)
        def _():
            exchange.finish(xin, xout, xsem)

    aliases.update({n_in + i: n_out + o for i, o in exchange.aliases.items()})
    outs = pl.pallas_call(
        wrapped, grid=grid, in_specs=list(in_specs) + _any_specs(x_in), out_specs=list(out_specs) + _any_specs(x_out),
        out_shape=list(out_shape) + exchange.out_shapes, scratch_shapes=list(scratch_shapes) + exchange.sems,
        input_output_aliases=aliases, **kw)(*args, *exchange.inputs)
    return list(outs[:n_out]), list(outs[n_out:])


def _run_exchange(exchange, name):
    x_in, x_out = len(exchange.inputs), len(exchange.out_shapes)

    def body(*refs):
        xin, xout, xsem = refs[:x_in], refs[x_in:x_in + x_out], refs[x_in + x_out:]
        exchange.start(xin, xout, xsem)
        exchange.finish(xin, xout, xsem)

    outs = pl.pallas_call(
        body, name=name, in_specs=_any_specs(x_in), out_specs=_any_specs(x_out), out_shape=exchange.out_shapes,
        scratch_shapes=exchange.sems, input_output_aliases=exchange.aliases)(*exchange.inputs)
    return list(outs)


def _gather_layer_exchange(ws, layer, part=(0, 1), lands=None):
    n = len(ws)
    p, n_parts = part

    def copies(xin, xout, sems):
        send_sems, recv_sems, local_sems = sems
        x, y, c, chips = _my_place()
        k_me = 2 * x + y
        local, sends, recvs = [], [], []
        for a in range(n):
            win = _window(ws[a].shape, c, p, n_parts)
            src = xin[a].at[(layer,) + win]
            dst = xout[a].at[(k_me,) + win]
            local.append(pltpu.make_async_copy(src, dst, local_sems.at[a]))
            for j, (px, py) in enumerate(chips):
                sends.append(pltpu.make_async_remote_copy(
                    src_ref=src, dst_ref=dst, send_sem=send_sems.at[a, j], recv_sem=recv_sems.at[a, j],
                    device_id=(px, py, c), device_id_type=MESH))
                recvs.append(pltpu.make_async_remote_copy(
                    src_ref=src, dst_ref=xout[a].at[(2 * px + py,) + win], send_sem=send_sems.at[a, j],
                    recv_sem=recv_sems.at[a, j], device_id=(px, py, c), device_id_type=MESH))
        return local, sends, recvs

    def start(xin, xout, sems):
        local, sends, _ = copies(xin, xout, sems)
        for cp in local + sends:
            cp.start()

    def finish(xin, xout, sems):
        local, sends, recvs = copies(xin, xout, sems)
        for cp in recvs:
            cp.wait_recv()
        for cp in sends:
            cp.wait_send()
        for cp in local:
            cp.wait()

    return _Exchange(list(ws) + list(lands or []), [jax.ShapeDtypeStruct((4,) + w.shape[1:], w.dtype) for w in ws],
                     [pltpu.SemaphoreType.DMA((n, 3)), pltpu.SemaphoreType.DMA((n, 3)), pltpu.SemaphoreType.DMA((n,))],
                     start, finish, aliases={n + a: a for a in range(n)} if lands else None)


def _gather_pair_exchange(lands):
    n = len(lands)

    def copies(xout, sems):
        send_sems, recv_sems = sems
        x, y, c, _ = _my_place()
        def half(a, core):
            return xout[a].at[(slice(None),) + _window(lands[a].shape, core)]

        sends = [pltpu.make_async_remote_copy(
            src_ref=half(a, c), dst_ref=half(a, c), send_sem=send_sems.at[a],
            recv_sem=recv_sems.at[a], device_id=(x, y, 1 - c), device_id_type=MESH) for a in range(n)]
        recvs = [pltpu.make_async_remote_copy(
            src_ref=half(a, c), dst_ref=half(a, 1 - c), send_sem=send_sems.at[a],
            recv_sem=recv_sems.at[a], device_id=(x, y, 1 - c), device_id_type=MESH) for a in range(n)]
        return sends, recvs

    def start(xin, xout, sems):
        for cp in copies(xout, sems)[0]:
            cp.start()

    def finish(xin, xout, sems):
        sends, recvs = copies(xout, sems)
        for cp in recvs:
            cp.wait_recv()
        for cp in sends:
            cp.wait_send()

    return _Exchange(lands, [jax.ShapeDtypeStruct(w.shape, w.dtype) for w in lands],
                     [pltpu.SemaphoreType.DMA((n,)), pltpu.SemaphoreType.DMA((n,))], start, finish,
                     aliases={a: a for a in range(n)})


def _swap_exchange(ps):
    n = len(ps)

    def copies(xin, xout, sems):
        send_sems, recv_sems = sems
        x, y, c, _ = _my_place()
        return [pltpu.make_async_remote_copy(
            src_ref=xin[a].at[(slice(None),) + _window(ps[a].shape, 1 - c)], dst_ref=xout[a],
            send_sem=send_sems.at[a], recv_sem=recv_sems.at[a], device_id=(x, y, 1 - c), device_id_type=MESH)
            for a in range(n)]

    def start(xin, xout, sems):
        for cp in copies(xin, xout, sems):
            cp.start()

    def finish(xin, xout, sems):
        for cp in copies(xin, xout, sems):
            cp.wait()

    return _Exchange(ps, [jax.ShapeDtypeStruct((4,) + _half_shape(p.shape), p.dtype) for p in ps],
                     [pltpu.SemaphoreType.DMA((n,)), pltpu.SemaphoreType.DMA((n,))], start, finish)


def _add_halves(ps, lands, c_idx):
    n = len(ps)

    def body(c_ref, *refs):
        del c_ref
        for a in range(n):
            refs[2 * n + a][...] = (refs[a][...].astype(F32) + refs[n + a][...].astype(F32)).astype(BF16)

    def half_block(p):
        return (None,) + _half_shape(p.shape)

    def own_half(p):
        return lambda j, cr: (j,) + _half_block_index(p.shape, cr[0])

    grid_spec = pltpu.PrefetchScalarGridSpec(
        num_scalar_prefetch=1, grid=(4,),
        in_specs=[pl.BlockSpec(half_block(p), own_half(p)) for p in ps]
        + [pl.BlockSpec(half_block(p), lambda j, cr: (j, 0, 0)) for p in ps],
        out_specs=[pl.BlockSpec(half_block(p), lambda j, cr: (j, 0, 0)) for p in ps])
    return pl.pallas_call(
        body, name="grad_add_halves", grid_spec=grid_spec,
        out_shape=[jax.ShapeDtypeStruct((4,) + _half_shape(p.shape), BF16) for p in ps],
        compiler_params=_cparams(1),
    )(c_idx, *ps, *lands)


def _scatter_exchange(hs):
    n = len(hs)

    def copies(xin, xout, sems):
        send_sems, recv_sems, local_sems = sems
        x, y, c, chips = _my_place()
        k_me = 2 * x + y
        local = [pltpu.make_async_copy(xin[a].at[k_me], xout[a].at[k_me], local_sems.at[a]) for a in range(n)]
        sends, recvs = [], []
        for j, (px, py) in enumerate(chips):
            for a in range(n):
                sends.append(pltpu.make_async_remote_copy(
                    src_ref=xin[a].at[2 * px + py], dst_ref=xout[a].at[k_me], send_sem=send_sems.at[a, j],
                    recv_sem=recv_sems.at[a, j], device_id=(px, py, c), device_id_type=MESH))
                recvs.append(pltpu.make_async_remote_copy(
                    src_ref=xin[a].at[k_me], dst_ref=xout[a].at[2 * px + py], send_sem=send_sems.at[a, j],
                    recv_sem=recv_sems.at[a, j], device_id=(px, py, c), device_id_type=MESH))
        return local, sends, recvs

    def start(xin, xout, sems):
        local, sends, _ = copies(xin, xout, sems)
        for cp in local + sends:
            cp.start()

    def finish(xin, xout, sems):
        local, sends, recvs = copies(xin, xout, sems)
        for cp in recvs:
            cp.wait_recv()
        for cp in sends:
            cp.wait_send()
        for cp in local:
            cp.wait()

    return _Exchange(hs, [jax.ShapeDtypeStruct(h.shape, h.dtype) for h in hs],
                     [pltpu.SemaphoreType.DMA((n, 3)), pltpu.SemaphoreType.DMA((n, 3)), pltpu.SemaphoreType.DMA((n,))],
                     start, finish)


def _sum_chips(rcvs, shapes, accs, layer, n_layers, c_idx):
    n = len(rcvs)

    def body(c_ref, *refs):
        del c_ref
        outs = refs[len(refs) - n:]
        for a in range(n):
            r_ref = refs[a]
            acc = r_ref[0].astype(F32) + r_ref[1].astype(F32)
            acc = acc + r_ref[2].astype(F32)
            outs[a][...] = acc + r_ref[3].astype(F32)

    in_specs = [pl.BlockSpec(r.shape, lambda i, cr: (0, 0, 0)) for r in rcvs]
    args = list(rcvs)
    aliases = {}
    if accs is not None:
        in_specs += _any_specs(n)
        args += list(accs)
        aliases = {1 + n + a: a for a in range(n)}
    def own_half(shape):
        return lambda i, cr: (layer,) + _half_block_index(shape, cr[0])

    grid_spec = pltpu.PrefetchScalarGridSpec(
        num_scalar_prefetch=1, grid=(1,), in_specs=in_specs,
        out_specs=[pl.BlockSpec((None,) + r.shape[1:], own_half(sh)) for r, sh in zip(rcvs, shapes)])
    return pl.pallas_call(
        body, name="grad_sum_chips", grid_spec=grid_spec,
        out_shape=[jax.ShapeDtypeStruct((n_layers,) + tuple(sh), F32) for sh in shapes],
        input_output_aliases=aliases, compiler_params=_cparams(1),
    )(c_idx, *args)


def _sum_devices(g):
    _, r, n = g.shape

    def body(g_ref, o_ref):
        acc = g_ref[0]
        for d in range(1, 8):
            acc = acc + g_ref[d]
        o_ref[...] = acc

    return pl.pallas_call(
        body, name="small_grad_sum", grid=(1,),
        in_specs=[_full_spec((8, r, n))], out_specs=_full_spec((r, n)),
        out_shape=jax.ShapeDtypeStruct((r, n), F32), compiler_params=_cparams(1),
    )(g)


def _matmul_nt(a, wt, bn, name, exchange=None):
    s, k = a.shape
    n = wt.shape[0]
    bs = min(2048, s)

    def body(a_ref, w_ref, o_ref):
        o_ref[...] = _dot_nt(a_ref[...], w_ref[...])

    return _call(
        body, (a, wt), exchange, name=name, grid=(n // bn, s // bs),
        in_specs=[pl.BlockSpec((bs, k), lambda j, i: (i, 0)), pl.BlockSpec((bn, k), lambda j, i: (j, 0))],
        out_specs=[pl.BlockSpec((bs, bn), lambda j, i: (i, j))],
        out_shape=[jax.ShapeDtypeStruct((s, n), F32)], compiler_params=_cparams(2))


def _matmul_tn(a, b, bn, name, bm=None, whole_sum=False):
    s, m = a.shape
    n = b.shape[1]
    bs = s if whole_sum else min(512, s)
    bm = m if bm is None else bm

    def body(a_ref, b_ref, o_ref):
        if whole_sum:
            o_ref[...] = _dot_tn(a_ref[...], b_ref[...]).astype(BF16)
            return

        @pl.when(pl.program_id(2) == 0)
        def _():
            o_ref[...] = jnp.zeros_like(o_ref)

        o_ref[...] += _dot_tn(a_ref[...], b_ref[...])

    return pl.pallas_call(
        body, name=name, grid=(m // bm, n // bn, s // bs),
        in_specs=[pl.BlockSpec((bs, bm), lambda h, j, i: (i, h)), pl.BlockSpec((bs, bn), lambda h, j, i: (i, j))],
        out_specs=pl.BlockSpec((bm, bn), lambda h, j, i: (h, j)),
        out_shape=jax.ShapeDtypeStruct((m, n), BF16 if whole_sum else F32), compiler_params=_cparams(3),
    )(a, b)


def _modulate(x, scale, shift):
    s, d = x.shape
    t = min(512, s)
    (scale, scale_spec), (shift, shift_spec) = _layer_param(scale), _layer_param(shift)

    def body(x_ref, sc_ref, sh_ref, u_ref):
        u_ref[...] = (x_ref[...] * (1.0 + sc_ref[...]) + sh_ref[...]).astype(BF16)

    return pl.pallas_call(
        body, name="modulate", grid=(s // t,),
        in_specs=[_row_spec(t, d), scale_spec, shift_spec],
        out_specs=_row_spec(t, d), out_shape=jax.ShapeDtypeStruct((s, d), BF16), compiler_params=_cparams(1),
    )(x, scale, shift)


def _rope_tables(pos_col):
    s = pos_col.shape[0]
    t = min(512, s)
    inv = ROPE_THETA ** (-np.arange(0, QK_ROPE, 2, dtype=np.float32) / QK_ROPE)
    lane_freq = np.zeros((1, HEAD_PAD), np.float32)
    lane_freq[0, QK_NOPE:QK_NOPE + 16] = inv
    lane_freq[0, QK_NOPE + 16:QK_NOPE + 32] = inv
    lane_mask = np.zeros((1, HEAD_PAD), np.float32)
    lane_mask[0, QK_NOPE:QK_NOPE + QK_ROPE] = 1.0

    def body(p_ref, f_ref, m_ref, cos_ref, sin_ref):
        ang = p_ref[...].astype(F32) * f_ref[...]
        cos_ref[...] = jnp.cos(ang) * m_ref[...]
        sin_ref[...] = jnp.sin(ang) * m_ref[...]

    return pl.pallas_call(
        body, name="rope_tables", grid=(s // t,),
        in_specs=[_row_spec(t, 1), _full_spec((1, HEAD_PAD)), _full_spec((1, HEAD_PAD))],
        out_specs=[_row_spec(t, HEAD_PAD), _row_spec(t, HEAD_PAD)],
        out_shape=[jax.ShapeDtypeStruct((s, HEAD_PAD), F32)] * 2, compiler_params=_cparams(1),
    )(pos_col, jnp.asarray(lane_freq), jnp.asarray(lane_mask))


def _ada_shard(c_all, w_ada_bf):
    n_layers, d, n = w_ada_bf.shape

    def body(c_ref, w_ref, o_ref, ca_ref):
        cv = c_ref[...]
        ca = cv * _sig(cv)
        ca_ref[...] = ca
        o_ref[...] = _dot(ca.astype(BF16), w_ref[...])

    return pl.pallas_call(
        body, name="ada_shard", grid=(n_layers,),
        in_specs=[_full_spec((8, d)), pl.BlockSpec((None, d, n), lambda l: (l, 0, 0))],
        out_specs=[pl.BlockSpec((None, 8, n), lambda l: (l, 0, 0)), _full_spec((8, d))],
        out_shape=[jax.ShapeDtypeStruct((n_layers, 8, n), F32), jax.ShapeDtypeStruct((8, d), F32)],
        compiler_params=_cparams(1),
    )(c_all, w_ada_bf)


def _ada_weight_grad(cact16, dada16):
    n_layers, _, n = dada16.shape
    d = cact16.shape[1]

    def body(c_ref, g_ref, o_ref):
        o_ref[...] = _dot_tn(c_ref[...], g_ref[...])

    return pl.pallas_call(
        body, name="ada_weight_grad", grid=(n_layers,),
        in_specs=[_full_spec((16, d)), pl.BlockSpec((None, 16, n), lambda l: (l, 0, 0))],
        out_specs=pl.BlockSpec((None, d, n), lambda l: (l, 0, 0)),
        out_shape=jax.ShapeDtypeStruct((n_layers, d, n), F32), compiler_params=_cparams(1),
    )(cact16, dada16)


def _layer_norm_rows(v, g, b):
    mu = jnp.mean(v, axis=-1, keepdims=True)
    var = jnp.mean(jnp.square(v - mu), axis=-1, keepdims=True)
    rstd = lax.rsqrt(var + LN_EPS)
    xh = (v - mu) * rstd
    return xh, rstd, xh * g + b


def _layer_norm_bwd_rows(dy_hat, xh, rstd):
    return rstd * (dy_hat - jnp.mean(dy_hat, axis=-1, keepdims=True) - xh * jnp.mean(dy_hat * xh, axis=-1, keepdims=True))


def _windows(buf_ref, offsets, t, shift_ref):
    for r in range(8):
        group = sorted(o for o in offsets if o % 8 == r)
        if not group:
            continue
        lo = group[0]
        if r == 0:
            for o in group:
                yield o, buf_ref[o:o + t, :]
        else:
            length = group[-1] - lo + t
            shift_ref[0:length, :] = buf_ref[lo:lo + length, :]
            for o in group:
                yield o, shift_ref[o - lo:o - lo + t, :]


def _mix_ab_fwd(proj, conv_a_w, conv_a_b, ln_a_g, ln_a_b, conv_b_w, exchange=None):
    s = proj.shape[0]
    t = min(ROW_BLOCK, s)
    cw = 512
    halo = CONV_HALO
    (conv_a_w, caw_spec), (conv_a_b, cab_spec) = _layer_param(conv_a_w), _layer_param(conv_a_b)
    (ln_a_g, lg_spec), (ln_a_b, lb_spec), (conv_b_w, cbw_spec) = (_layer_param(ln_a_g), _layer_param(ln_a_b),
                                                                 _layer_param(conv_b_w))

    def body(p_ref, caw_ref, cab_ref, lg_ref, lb_ref, cbw_ref, acta_ref, actb_ref, ca_ref, cb_ref, abuf, zbuf, shifted):
        @pl.when(pl.program_id(0) == 0)
        def _():
            abuf[0:halo, :] = jnp.zeros((halo, cw), F32)
            zbuf[0:8, :] = jnp.zeros((8, cw), F32)

        abuf[halo:halo + t, :] = p_ref[:, 0:512] * _sig(p_ref[:, 512:1024])
        acc = jnp.zeros((t, cw), F32)
        first = halo - (CONV_K - 1)
        for off, win in _windows(abuf, range(first, first + CONV_K), t, shifted):
            acc = acc + caw_ref[off - first:off - first + 1, :] * win
        ca = acc + cab_ref[...]
        ca_ref[...] = ca
        _, _, ln = _layer_norm_rows(ca, lg_ref[...], lb_ref[...])
        ag = p_ref[:, 1024:1536]
        acta_ref[...] = ((ln * _sig(ln)) * (ag * _sig(ag))).astype(BF16)
        abuf[0:halo, :] = abuf[t:t + halo, :]

        zbuf[8:8 + t, :] = p_ref[:, 2560:3072] * p_ref[:, 1536:2048]
        accb = jnp.zeros((t, cw), F32)
        for k in range(SC_K):
            off = 8 - (SC_K - 1) + k
            accb = accb + cbw_ref[k:k + 1, :] * zbuf[off:off + t, :]
        cb_ref[...] = accb
        bg = p_ref[:, 3072:3584]
        actb_ref[...] = ((p_ref[:, 2048:2560] * accb) * (bg * _sig(bg))).astype(BF16)
        zbuf[0:8, :] = zbuf[t:t + 8, :]

    return _call(
        body, (proj, conv_a_w, conv_a_b, ln_a_g, ln_a_b, conv_b_w), exchange, name="mix_ab_fwd", grid=(s // t,),
        in_specs=[_row_spec(t, W_AB), caw_spec, cab_spec, lg_spec, lb_spec, cbw_spec],
        out_specs=[_row_spec(t, cw)] * 4,
        out_shape=[jax.ShapeDtypeStruct((s, cw), BF16)] * 2 + [jax.ShapeDtypeStruct((s, cw), F32)] * 2,
        scratch_shapes=[pltpu.VMEM((halo + t, cw), F32), pltpu.VMEM((8 + t, cw), F32), pltpu.VMEM((halo + t, cw), F32)],
        compiler_params=_cparams(1))


def _mix_ab_bwd(proj, ca, cb, dact_a, dact_b, conv_a_w, ln_a_g, ln_a_b, conv_b_w, dproj):
    s = proj.shape[0]
    t = min(ROW_BLOCK, s)
    nb = s // t
    cw = 512
    halo = CONV_HALO
    per = t // halo
    (conv_a_w, caw_spec), (ln_a_g, lg_spec) = _layer_param(conv_a_w), _layer_param(ln_a_g)
    (ln_a_b, lb_spec), (conv_b_w, cbw_spec) = _layer_param(ln_a_b), _layer_param(conv_b_w)

    def body(p_ref, ph_ref, ca_ref, cb_ref, da_ref, db_ref, caw_ref, lg_ref, lb_ref, cbw_ref, dproj_any,
             dg_ref, dcaw_ref, dcab_ref, dlg_ref, dlb_ref, dcbw_ref, abuf, dcabuf, zbuf, dcbbuf, shifted):
        del dproj_any
        i = pl.program_id(0)
        r = nb - 1 - i

        @pl.when(i == 0)
        def _():
            dcaw_ref[...] = jnp.zeros_like(dcaw_ref)
            dcab_ref[...] = jnp.zeros_like(dcab_ref)
            dlg_ref[...] = jnp.zeros_like(dlg_ref)
            dlb_ref[...] = jnp.zeros_like(dlb_ref)
            dcbw_ref[...] = jnp.zeros_like(dcbw_ref)
            dcabuf[t:t + halo, :] = jnp.zeros((halo, cw), F32)
            dcbbuf[t:t + 8, :] = jnp.zeros((8, cw), F32)

        keep = (r > 0).astype(F32)
        a1 = p_ref[:, 0:512]
        s2 = _sig(p_ref[:, 512:1024])
        abuf[0:halo, :] = (ph_ref[:, 0:512] * _sig(ph_ref[:, 512:1024])) * keep
        abuf[halo:halo + t, :] = a1 * s2
        ca_v = ca_ref[...]
        xh, rstd, ln = _layer_norm_rows(ca_v, lg_ref[...], lb_ref[...])
        s_ln = _sig(ln)
        ag = p_ref[:, 1024:1536]
        sg = _sig(ag)
        dact = da_ref[...]
        dsa = dact * (ag * sg)
        dg_ref[:, 1024:1536] = (dact * (ln * s_ln) * _dsilu(ag, sg)).astype(BF16)
        dln = dsa * _dsilu(ln, s_ln)
        dlg_ref[...] += jnp.sum(dln * xh, axis=0, keepdims=True)
        dlb_ref[...] += jnp.sum(dln, axis=0, keepdims=True)
        dca = _layer_norm_bwd_rows(dln * lg_ref[...], xh, rstd)
        dcab_ref[...] += jnp.sum(dca, axis=0, keepdims=True)
        dcabuf[0:t, :] = dca
        d_a = jnp.zeros((t, cw), F32)
        for off, win in _windows(dcabuf, range(CONV_K), t, shifted):
            k = (CONV_K - 1) - off
            d_a = d_a + caw_ref[k:k + 1, :] * win
        first = halo - (CONV_K - 1)
        for off, win in _windows(abuf, range(first, first + CONV_K), t, shifted):
            dcaw_ref[off - first:off - first + 1, :] += jnp.sum(dca * win, axis=0, keepdims=True)
        dg_ref[:, 0:512] = (d_a * s2).astype(BF16)
        dg_ref[:, 512:1024] = (d_a * a1 * s2 * (1.0 - s2)).astype(BF16)
        dcabuf[t:t + halo, :] = dcabuf[0:halo, :]

        xb = p_ref[:, 1536:2048]
        gb = p_ref[:, 2048:2560]
        gc = p_ref[:, 2560:3072]
        bg = p_ref[:, 3072:3584]
        zbuf[0:8, :] = (ph_ref[halo - 8:halo, 2560:3072] * ph_ref[halo - 8:halo, 1536:2048]) * keep
        zbuf[8:8 + t, :] = gc * xb
        sbg = _sig(bg)
        cbv = cb_ref[...]
        dactb = db_ref[...]
        dyb = dactb * (bg * sbg)
        dg_ref[:, 3072:3584] = (dactb * (gb * cbv) * _dsilu(bg, sbg)).astype(BF16)
        dg_ref[:, 2048:2560] = (dyb * cbv).astype(BF16)
        dcb = dyb * gb
        dcbbuf[0:t, :] = dcb
        dz = jnp.zeros((t, cw), F32)
        for k in range(SC_K):
            off = (SC_K - 1) - k
            dz = dz + cbw_ref[k:k + 1, :] * dcbbuf[off:off + t, :]
            src = 8 - (SC_K - 1) + k
            dcbw_ref[k:k + 1, :] += jnp.sum(dcb * zbuf[src:src + t, :], axis=0, keepdims=True)
        dg_ref[:, 2560:3072] = (dz * xb).astype(BF16)
        dg_ref[:, 1536:2048] = (dz * gc).astype(BF16)
        dcbbuf[t:t + 8, :] = dcbbuf[0:8, :]

    rev = lambda i: (nb - 1 - i, 0)
    outs = pl.pallas_call(
        body, name="mix_ab_bwd", grid=(nb,),
        in_specs=[pl.BlockSpec((t, W_AB), rev),
                  pl.BlockSpec((halo, W_AB), lambda i: (jnp.maximum((nb - 1 - i) * per - 1, 0), 0)),
                  pl.BlockSpec((t, cw), rev), pl.BlockSpec((t, cw), rev), pl.BlockSpec((t, cw), rev),
                  pl.BlockSpec((t, cw), rev),
                  caw_spec, lg_spec, lb_spec, cbw_spec,
                  pl.BlockSpec(memory_space=pl.ANY)],
        out_specs=[pl.BlockSpec((t, W_AB), rev), _full_spec((CONV_HALO, cw)), _full_spec((1, cw)), _full_spec((1, cw)),
                   _full_spec((1, cw)), _full_spec((8, cw))],
        out_shape=[jax.ShapeDtypeStruct(dproj.shape, BF16), jax.ShapeDtypeStruct((CONV_HALO, cw), F32),
                   jax.ShapeDtypeStruct((1, cw), F32), jax.ShapeDtypeStruct((1, cw), F32),
                   jax.ShapeDtypeStruct((1, cw), F32), jax.ShapeDtypeStruct((8, cw), F32)],
        scratch_shapes=[pltpu.VMEM((halo + t, cw), F32), pltpu.VMEM((t + halo, cw), F32),
                        pltpu.VMEM((8 + t, cw), F32), pltpu.VMEM((t + 8, cw), F32), pltpu.VMEM((halo + t, cw), F32)],
        input_output_aliases={10: 0},
        compiler_params=_cparams(1),
    )(proj, proj, ca, cb, dact_a, dact_b, conv_a_w, ln_a_g, ln_a_b, conv_b_w, dproj)
    return outs


def _lane_is_nope():
    return lax.broadcasted_iota(jnp.int32, (1, HEAD_PAD), 1) < QK_NOPE


def _attn_prep_fwd(proj, q_g, kv_g, wq12, wkv, tcos, tsin):
    s = proj.shape[0]
    t = min(ROW_BLOCK, s)
    c0 = (W_AB + W_G5) // W_C1
    (q_g, qg_spec), (kv_g, kg_spec) = _layer_param(q_g), _layer_param(kv_g)

    def body(p_ref, qg_ref, kg_ref, wq_ref, wkv_ref, cos_ref, sin_ref, qf_ref, kf_ref, v_ref):
        ql = p_ref[:, 0:384]
        qn = (ql * lax.rsqrt(jnp.mean(jnp.square(ql), axis=-1, keepdims=True) + RMS_EPS) * qg_ref[...]).astype(BF16)
        q12 = _dot(qn, wq_ref[...])
        kvl = p_ref[:, 384:640]
        kvn = (kvl * lax.rsqrt(jnp.mean(jnp.square(kvl), axis=-1, keepdims=True) + RMS_EPS) * kg_ref[...]).astype(BF16)
        kv12 = _dot(kvn, wkv_ref[...])
        tcos_v = cos_ref[...]
        tsin_v = sin_ref[...]
        tq1 = jnp.where(_lane_is_nope(), 1.0, tcos_v)
        kpe = p_ref[:, 640:768] * tcos_v + p_ref[:, 768:896] * tsin_v
        for h in range(HEADS):
            lo = h * HEAD_PAD
            qf_ref[h] = (q12[:, lo:lo + HEAD_PAD] * tq1 + q12[:, 1024 + lo:1024 + lo + HEAD_PAD] * tsin_v).astype(BF16)
            kf_ref[h] = (kv12[:, lo:lo + HEAD_PAD] + kpe).astype(BF16)
        v_ref[...] = kv12[:, 1024:1536].astype(BF16)

    return pl.pallas_call(
        body, name="attn_prep_fwd", grid=(s // t,),
        in_specs=[pl.BlockSpec((t, W_C1), lambda i: (i, c0)), qg_spec, kg_spec,
                  _full_spec((384, 2048)), _full_spec((256, 1536)), _row_spec(t, HEAD_PAD), _row_spec(t, HEAD_PAD)],
        out_specs=[pl.BlockSpec((HEADS, t, HEAD_PAD), lambda i: (0, i, 0)),
                   pl.BlockSpec((HEADS, t, HEAD_PAD), lambda i: (0, i, 0)), _row_spec(t, 512)],
        out_shape=[jax.ShapeDtypeStruct((HEADS, s, HEAD_PAD), BF16), jax.ShapeDtypeStruct((HEADS, s, HEAD_PAD), BF16),
                   jax.ShapeDtypeStruct((s, 512), BF16)],
        compiler_params=_cparams(1),
    )(proj, q_g, kv_g, wq12, wkv, tcos, tsin)


def _attn_prep_bwd(proj, dqf, dkf, dv, q_g, kv_g, wq12, wkv, tcos, tsin, dproj):
    s = proj.shape[0]
    t = min(ROW_BLOCK, s)
    c0 = (W_AB + W_G5) // W_C1
    (q_g, qg_spec), (kv_g, kg_spec) = _layer_param(q_g), _layer_param(kv_g)

    def body(p_ref, dqf_ref, dkf_ref, dv_ref, qg_ref, kg_ref, wq_ref, wkv_ref, cos_ref, sin_ref, dproj_any,
             dg_ref, qn_ref, kvn_ref, dq12_ref, dkv12_ref, dqg_ref, dkg_ref):
        del dproj_any

        @pl.when(pl.program_id(0) == 0)
        def _():
            dqg_ref[...] = jnp.zeros_like(dqg_ref)
            dkg_ref[...] = jnp.zeros_like(dkg_ref)

        tcos_v = cos_ref[...]
        tsin_v = sin_ref[...]
        tq1 = jnp.where(_lane_is_nope(), 1.0, tcos_v)
        dkpe = jnp.zeros((t, HEAD_PAD), F32)
        for h in range(HEADS):
            lo = h * HEAD_PAD
            dq = dqf_ref[h]
            dq12_ref[:, lo:lo + HEAD_PAD] = (dq * tq1).astype(BF16)
            dq12_ref[:, 1024 + lo:1024 + lo + HEAD_PAD] = (dq * tsin_v).astype(BF16)
            dk = dkf_ref[h]
            dkv12_ref[:, lo:lo + HEAD_PAD] = dk.astype(BF16)
            dkpe = dkpe + dk
        dkv12_ref[:, 1024:1536] = dv_ref[...].astype(BF16)
        dg_ref[:, 640:768] = (dkpe * tcos_v).astype(BF16)
        dg_ref[:, 768:896] = (dkpe * tsin_v).astype(BF16)

        def rms_bwd(xl, g, dn):
            rr = lax.rsqrt(jnp.mean(jnp.square(xl), axis=-1, keepdims=True) + RMS_EPS)
            xn = xl * rr
            tt = dn * g
            return rr * (tt - xn * jnp.mean(tt * xn, axis=-1, keepdims=True)), jnp.sum(dn * xn, axis=0, keepdims=True), xn

        ql = p_ref[:, 0:384]
        dqn = _dot_nt(dq12_ref[...], wq_ref[...])
        dql, dqg, qxn = rms_bwd(ql, qg_ref[...], dqn)
        dg_ref[:, 0:384] = dql.astype(BF16)
        dqg_ref[...] += dqg
        qn_ref[...] = (qxn * qg_ref[...]).astype(BF16)
        kvl = p_ref[:, 384:640]
        dkvn = _dot_nt(dkv12_ref[...], wkv_ref[...])
        dkvl, dkg, kxn = rms_bwd(kvl, kg_ref[...], dkvn)
        dg_ref[:, 384:640] = dkvl.astype(BF16)
        dkg_ref[...] += dkg
        kvn_ref[...] = (kxn * kg_ref[...]).astype(BF16)

    return pl.pallas_call(
        body, name="attn_prep_bwd", grid=(s // t,),
        in_specs=[pl.BlockSpec((t, W_C1), lambda i: (i, c0)),
                  pl.BlockSpec((HEADS, t, HEAD_PAD), lambda i: (0, i, 0)),
                  pl.BlockSpec((HEADS, t, HEAD_PAD), lambda i: (0, i, 0)), _row_spec(t, 512),
                  qg_spec, kg_spec, _full_spec((384, 2048)), _full_spec((256, 1536)),
                  _row_spec(t, HEAD_PAD), _row_spec(t, HEAD_PAD), pl.BlockSpec(memory_space=pl.ANY)],
        out_specs=[pl.BlockSpec((t, W_C1), lambda i: (i, c0)), _row_spec(t, 384), _row_spec(t, 256),
                   _row_spec(t, 2048), _row_spec(t, 1536), _full_spec((1, 384)), _full_spec((1, 256))],
        out_shape=[jax.ShapeDtypeStruct(dproj.shape, BF16), jax.ShapeDtypeStruct((s, 384), BF16),
                   jax.ShapeDtypeStruct((s, 256), BF16), jax.ShapeDtypeStruct((s, 2048), BF16),
                   jax.ShapeDtypeStruct((s, 1536), BF16), jax.ShapeDtypeStruct((1, 384), F32),
                   jax.ShapeDtypeStruct((1, 256), F32)],
        input_output_aliases={10: 0},
        compiler_params=_cparams(1),
    )(proj, dqf, dkf, dv, q_g, kv_g, wq12, wkv, tcos, tsin, dproj)


def _causal_keep(t):
    return lax.broadcasted_iota(jnp.int32, (t, t), 0) >= lax.broadcasted_iota(jnp.int32, (t, t), 1)


def _half_select(e):
    lane = lax.broadcasted_iota(jnp.int32, (1, HEAD_PAD), 1)
    return (lane < V_HEAD) if e == 0 else (lane >= V_HEAD)


def _attention_fwd(qf, kf, v, exchange=None):
    _, s, _ = qf.shape
    t = min(ATT_BLOCK, s)
    nq = s // t

    def body(q_ref, k_ref, v_ref, o_ref, lse_ref):
        def rows_of_block(blk):
            past = blk * t
            for pair in range(HEADS // 2):
                lo = pair * HEAD_PAD
                v_diag = v_ref[past:past + t, lo:lo + HEAD_PAD]
                v_past = v_ref[0:past, lo:lo + HEAD_PAD] if blk else None
                out = None
                for e in range(2):
                    h = 2 * pair + e
                    sel = _half_select(e)
                    q = q_ref[h]
                    s_diag = jnp.where(_causal_keep(t), _dot_nt(q, k_ref[h, past:past + t, :]) * ATT_SCALE, NEG_BIG)
                    m = jnp.max(s_diag, axis=1, keepdims=True)
                    if blk:
                        s_past = _dot_nt(q, k_ref[h, 0:past, :]) * ATT_SCALE
                        m = jnp.maximum(m, jnp.max(s_past, axis=1, keepdims=True))
                    p_diag = jnp.exp(s_diag - m)
                    norm = jnp.sum(p_diag, axis=1, keepdims=True)
                    pv = _dot(p_diag.astype(BF16), jnp.where(sel, v_diag, jnp.zeros_like(v_diag)))
                    if blk:
                        p_past = jnp.exp(s_past - m)
                        norm = norm + jnp.sum(p_past, axis=1, keepdims=True)
                        pv = pv + _dot(p_past.astype(BF16), jnp.where(sel, v_past, jnp.zeros_like(v_past)))
                    pv = pv * (1.0 / norm)
                    out = pv if out is None else out + pv
                    lse_ref[h] = m + jnp.log(norm)
                o_ref[:, lo:lo + HEAD_PAD] = out

        for blk in range(nq):
            pl.when(pl.program_id(0) == blk)(functools.partial(rows_of_block, blk))

    return _call(
        body, (qf, kf, v), exchange, name="attention_fwd", grid=(nq,),
        in_specs=[pl.BlockSpec((HEADS, t, HEAD_PAD), lambda i: (0, i, 0)), _full_spec((HEADS, s, HEAD_PAD)),
                  _full_spec((s, 512))],
        out_specs=[_row_spec(t, 512), pl.BlockSpec((HEADS, t, 1), lambda i: (0, i, 0))],
        out_shape=[jax.ShapeDtypeStruct((s, 512), F32), jax.ShapeDtypeStruct((HEADS, s, 1), F32)],
        compiler_params=_cparams(1))


def _attention_bwd(qf, kf, v, o, do, lse, exchange=None):
    _, s, _ = qf.shape
    t = min(ATT_BLOCK, s)
    nq = s // t

    def body(q_ref, k_ref, v_ref, o_ref, do_ref, lse_ref, dq_ref, dk_ref, dv_ref, dk_acc, dv_acc):
        kj = pl.program_id(0)
        qi = pl.program_id(1)

        @pl.when((kj == 0) & (qi == 0))
        def _():
            dq_ref[...] = jnp.zeros_like(dq_ref)

        @pl.when(qi == kj)
        def _():
            dk_acc[...] = jnp.zeros_like(dk_acc)
            dv_acc[...] = jnp.zeros_like(dv_acc)

        def step(diag):
            rows = pl.ds(pl.multiple_of(qi * t, t), t)
            for h in range(HEADS):
                pair, e = divmod(h, 2)
                sel = _half_select(e)
                lo = pair * HEAD_PAD
                q = q_ref[h]
                k = k_ref[h]
                sc = _dot_nt(q, k) * ATT_SCALE
                if diag:
                    sc = jnp.where(_causal_keep(t), sc, NEG_BIG)
                p = jnp.exp(sc - lse_ref[h])
                do_pair = do_ref[:, lo:lo + HEAD_PAD]
                do_e = jnp.where(sel, do_pair, 0.0)
                do_b = do_e.astype(BF16)
                v2 = v_ref[:, lo:lo + HEAD_PAD]
                dv_acc[:, lo:lo + HEAD_PAD] += _dot_tn(p.astype(BF16), do_b)
                dp = _dot_nt(do_b, jnp.where(sel, v2, jnp.zeros_like(v2)))
                delta = jnp.sum(do_e * o_ref[:, lo:lo + HEAD_PAD], axis=1, keepdims=True)
                ds = (p * (dp - delta) * ATT_SCALE).astype(BF16)
                dq_ref[h, rows, :] += _dot(ds, k)
                dk_acc[h] += _dot_tn(ds, q)

        @pl.when(qi > kj)
        def _():
            step(False)

        @pl.when(qi == kj)
        def _():
            step(True)

        @pl.when(qi == nq - 1)
        def _():
            dk_ref[...] = dk_acc[...]
            dv_ref[...] = dv_acc[...]

    qmap = lambda j, i: (0, jnp.maximum(i, j), 0)
    return _call(
        body, (qf, kf, v, o, do, lse), exchange, name="attention_bwd", grid=(nq, nq),
        in_specs=[pl.BlockSpec((HEADS, t, HEAD_PAD), qmap),
                  pl.BlockSpec((HEADS, t, HEAD_PAD), lambda j, i: (0, j, 0)),
                  pl.BlockSpec((t, 512), lambda j, i: (j, 0)),
                  pl.BlockSpec((t, 512), lambda j, i: (jnp.maximum(i, j), 0)),
                  pl.BlockSpec((t, 512), lambda j, i: (jnp.maximum(i, j), 0)),
                  pl.BlockSpec((HEADS, t, 1), qmap)],
        out_specs=[_full_spec((HEADS, s, HEAD_PAD)),
                   pl.BlockSpec((HEADS, t, HEAD_PAD), lambda j, i: (0, j, 0)),
                   pl.BlockSpec((t, 512), lambda j, i: (j, 0))],
        out_shape=[jax.ShapeDtypeStruct((HEADS, s, HEAD_PAD), F32), jax.ShapeDtypeStruct((HEADS, s, HEAD_PAD), F32),
                   jax.ShapeDtypeStruct((s, 512), F32)],
        scratch_shapes=[pltpu.VMEM((HEADS, t, HEAD_PAD), F32), pltpu.VMEM((t, 512), F32)],
        compiler_params=_cparams(2))


def _merge_fwd(act_a, act_b, o, proj, x, gate, w_a, w_b, w_c, w_o, ln_g, ln_b, alpha, exchange=None):
    s, d = x.shape
    t = min(ROW_BLOCK, s)
    (gate, gate_spec), (ln_g, lg_spec), (ln_b, lb_spec) = _layer_param(gate), _layer_param(ln_g), _layer_param(ln_b)

    def body(aa_ref, ab_ref, o_ref, p_ref, x_ref, gate_ref, wa_ref, wb_ref, wc_ref, wo_ref, lg_ref, lb_ref,
             xn_ref, xh_ref, rstd_ref):
        cg = p_ref[:, 0:512]
        act_c = (o_ref[...] * (cg * _sig(cg))).astype(BF16)
        m = _sig(p_ref[:, 512:1536]) * _dot(aa_ref[...], wa_ref[...])
        m = m + _sig(p_ref[:, 1536:2560]) * _dot(ab_ref[...], wb_ref[...])
        m = m + _sig(p_ref[:, 2560:3584]) * _dot(act_c, wc_ref[...])
        out = _dot(m.astype(BF16), wo_ref[...])
        z = alpha * x_ref[...] + gate_ref[...] * out
        xh, rstd, y = _layer_norm_rows(z, lg_ref[...], lb_ref[...])
        xn_ref[...] = y
        xh_ref[...] = xh
        rstd_ref[...] = rstd

    return _call(
        body, (act_a, act_b, o, proj, x, gate, w_a, w_b, w_c, w_o, ln_g, ln_b), exchange, name="merge_fwd", grid=(s // t,),
        in_specs=[_row_spec(t, 512), _row_spec(t, 512), _row_spec(t, 512),
                  pl.BlockSpec((t, W_G5), lambda i: (i, 1)), _row_spec(t, d), gate_spec,
                  _full_spec((512, d)), _full_spec((512, d)), _full_spec((512, d)), _full_spec((d, d)),
                  lg_spec, lb_spec],
        out_specs=[_row_spec(t, d), _row_spec(t, d), _row_spec(t, 1)],
        out_shape=[jax.ShapeDtypeStruct((s, d), F32), jax.ShapeDtypeStruct((s, d), F32), jax.ShapeDtypeStruct((s, 1), F32)],
        compiler_params=_cparams(1))


def _merge_bwd(dy, xh, rstd, act_a, act_b, o, proj, gate, w_a, w_b, w_c, w_o, ln_g, alpha, exchange=None):
    s, d = dy.shape
    t = min(ROW_BLOCK, s)
    (gate, gate_spec), (ln_g, lg_spec) = _layer_param(gate), _layer_param(ln_g)

    def body(dy_ref, xh_ref, rstd_ref, aa_ref, ab_ref, o_ref, p_ref, gate_ref, wa_ref, wb_ref, wc_ref, wo_ref, lg_ref,
             dxr_ref, dacta_ref, dactb_ref, do_ref, dg_ref, m_ref, dout_ref, dya_ref, dyb_ref, dyc_ref, actc_ref,
             dlg_ref, dlb_ref, dgate_ref):
        @pl.when(pl.program_id(0) == 0)
        def _():
            dlg_ref[...] = jnp.zeros_like(dlg_ref)
            dlb_ref[...] = jnp.zeros_like(dlb_ref)
            dgate_ref[...] = jnp.zeros_like(dgate_ref)

        dyv = dy_ref[...]
        xhv = xh_ref[...]
        dlg_ref[...] += jnp.sum(dyv * xhv, axis=0, keepdims=True)
        dlb_ref[...] += jnp.sum(dyv, axis=0, keepdims=True)
        dz = _layer_norm_bwd_rows(dyv * lg_ref[...], xhv, rstd_ref[...])
        dxr_ref[...] = alpha * dz

        cg = p_ref[:, 0:512]
        scg = _sig(cg)
        silu_cg = cg * scg
        ov = o_ref[...]
        act_c = (ov * silu_cg).astype(BF16)
        actc_ref[...] = act_c
        ya = _dot(aa_ref[...], wa_ref[...])
        yb = _dot(ab_ref[...], wb_ref[...])
        yc = _dot(act_c, wc_ref[...])
        ga = _sig(p_ref[:, 512:1536])
        gb = _sig(p_ref[:, 1536:2560])
        gc = _sig(p_ref[:, 2560:3584])
        mb = (ga * ya + gb * yb + gc * yc).astype(BF16)
        m_ref[...] = mb
        out = _dot(mb, wo_ref[...])
        dgate_ref[...] += jnp.sum(dz * out, axis=0, keepdims=True)
        dout = (gate_ref[...] * dz).astype(BF16)
        dout_ref[...] = dout
        dm = _dot_nt(dout, wo_ref[...])

        dya = (dm * ga).astype(BF16)
        dya_ref[...] = dya
        dg_ref[:, 512:1536] = (dm * ya * ga * (1.0 - ga)).astype(BF16)
        dacta_ref[...] = _dot_nt(dya, wa_ref[...])
        dyb = (dm * gb).astype(BF16)
        dyb_ref[...] = dyb
        dg_ref[:, 1536:2560] = (dm * yb * gb * (1.0 - gb)).astype(BF16)
        dactb_ref[...] = _dot_nt(dyb, wb_ref[...])
        dyc = (dm * gc).astype(BF16)
        dyc_ref[...] = dyc
        dg_ref[:, 2560:3584] = (dm * yc * gc * (1.0 - gc)).astype(BF16)
        dactc = _dot_nt(dyc, wc_ref[...])
        do_ref[...] = dactc * silu_cg
        dg_ref[:, 0:512] = (dactc * ov * _dsilu(cg, scg)).astype(BF16)

    return _call(
        body, (dy, xh, rstd, act_a, act_b, o, proj, gate, w_a, w_b, w_c, w_o, ln_g), exchange, name="merge_bwd", grid=(s // t,),
        in_specs=[_row_spec(t, d), _row_spec(t, d), _row_spec(t, 1), _row_spec(t, 512), _row_spec(t, 512),
                  _row_spec(t, 512), pl.BlockSpec((t, W_G5), lambda i: (i, 1)), gate_spec,
                  _full_spec((512, d)), _full_spec((512, d)), _full_spec((512, d)), _full_spec((d, d)),
                  lg_spec],
        out_specs=[_row_spec(t, d), _row_spec(t, 512), _row_spec(t, 512), _row_spec(t, 512),
                   pl.BlockSpec((t, W_G5), lambda i: (i, 1)),
                   _row_spec(t, d), _row_spec(t, d), _row_spec(t, d), _row_spec(t, d), _row_spec(t, d),
                   _row_spec(t, 512), _full_spec((1, d)), _full_spec((1, d)), _full_spec((1, d))],
        out_shape=[jax.ShapeDtypeStruct((s, d), F32), jax.ShapeDtypeStruct((s, 512), F32),
                   jax.ShapeDtypeStruct((s, 512), F32), jax.ShapeDtypeStruct((s, 512), F32),
                   jax.ShapeDtypeStruct((s, W_PACK), BF16),
                   jax.ShapeDtypeStruct((s, d), BF16), jax.ShapeDtypeStruct((s, d), BF16),
                   jax.ShapeDtypeStruct((s, d), BF16), jax.ShapeDtypeStruct((s, d), BF16),
                   jax.ShapeDtypeStruct((s, d), BF16), jax.ShapeDtypeStruct((s, 512), BF16),
                   jax.ShapeDtypeStruct((1, d), F32), jax.ShapeDtypeStruct((1, d), F32),
                   jax.ShapeDtypeStruct((1, d), F32)],
        compiler_params=_cparams(1))


def _proj_bwd_input(dproj, w_pack, x, scale, dxres, exchange=None):
    s, d = x.shape
    t = min(512, s)
    scale, scale_spec = _layer_param(scale)

    def body(dg_ref, w_ref, x_ref, sc_ref, dxr_ref, dx_ref, dsc_ref, dsh_ref):
        @pl.when(pl.program_id(0) == 0)
        def _():
            dsc_ref[...] = jnp.zeros_like(dsc_ref)
            dsh_ref[...] = jnp.zeros_like(dsh_ref)

        du = _dot(dg_ref[...], w_ref[...])
        dx_ref[...] = du * (1.0 + sc_ref[...]) + dxr_ref[...]
        dsc_ref[...] += jnp.sum(du * x_ref[...], axis=0, keepdims=True)
        dsh_ref[...] += jnp.sum(du, axis=0, keepdims=True)

    w_spec = pl.BlockSpec((W_PACK, d), lambda i: (0, 0), pipeline_mode=pl.Buffered(1))
    return _call(
        body, (dproj, w_pack, x, scale, dxres), exchange, name="proj_bwd_input", grid=(s // t,),
        in_specs=[_row_spec(t, W_PACK), w_spec, _row_spec(t, d), scale_spec, _row_spec(t, d)],
        out_specs=[_row_spec(t, d), _full_spec((1, d)), _full_spec((1, d))],
        out_shape=[jax.ShapeDtypeStruct((s, d), F32), jax.ShapeDtypeStruct((1, d), F32), jax.ShapeDtypeStruct((1, d), F32)],
        compiler_params=_cparams(1))


def _loss_head(y, target):
    s, d = y.shape
    t = min(512, s)

    def body(y_ref, t_ref, dy_ref, loss_ref):
        @pl.when(pl.program_id(0) == 0)
        def _():
            loss_ref[...] = jnp.zeros_like(loss_ref)

        err = y_ref[...] - t_ref[...]
        dy_ref[...] = err / d
        part = 0.5 * jnp.sum(jnp.mean(jnp.square(err), axis=-1, keepdims=True), axis=0, keepdims=True)
        loss_ref[...] += jnp.broadcast_to(part, loss_ref.shape)

    return pl.pallas_call(
        body, name="loss_head", grid=(s // t,),
        in_specs=[_row_spec(t, d), _row_spec(t, d)],
        out_specs=[_row_spec(t, d), _full_spec((1, 128))],
        out_shape=[jax.ShapeDtypeStruct((s, d), F32), jax.ShapeDtypeStruct((1, 128), F32)],
        compiler_params=_cparams(1),
    )(y, target)


def _adamw(g, w, m, v, name):
    r, n = g.shape
    block_bytes = 5 * 2 ** 19
    rb = r
    if r * n * 4 > block_bytes:
        rb = max(k for k in range(8, r, 8) if r % k == 0 and k * n * 4 <= block_bytes)

    def body(g_ref, w_ref, m_ref, v_ref, d_ref, mo_ref, vo_ref):
        gv = g_ref[...]
        mn = ADAM_B1 * m_ref[...] + (1.0 - ADAM_B1) * gv
        vn = ADAM_B2 * v_ref[...] + (1.0 - ADAM_B2) * jnp.square(gv)
        m_hat = mn / (1.0 - ADAM_B1 ** ADAM_STEP)
        v_hat = vn / (1.0 - ADAM_B2 ** ADAM_STEP)
        d_ref[...] = -ADAM_LR * (m_hat / (jnp.sqrt(v_hat) + ADAM_EPS) + ADAM_WD * w_ref[...])
        mo_ref[...] = mn
        vo_ref[...] = vn

    spec = pl.BlockSpec((rb, n), lambda i: (i, 0))
    return pl.pallas_call(
        body, name=name, grid=(r // rb,),
        in_specs=[spec] * 4, out_specs=[spec] * 3,
        out_shape=[jax.ShapeDtypeStruct((r, n), F32)] * 3, compiler_params=_cparams(1),
    )(g, w, m, v)


SMALL = ("b_ada", "conv_a_b", "ln_a_g", "ln_a_b", "q_norm_g", "kv_norm_g", "ln_g", "ln_b", "conv_a_w", "conv_b_w")
ROWS_256 = (("w_a_out", 512), ("w_b_out", 512), ("w_c_out", 512), ("w_ukv", 256))


def _rot_cols(w):
    return jnp.concatenate([-w[..., 16:], w[..., :16]], axis=-1)


def _unrot_cols(g):
    return jnp.concatenate([g[..., 16:], -g[..., :16]], axis=-1)


def _pack_w_in(wt):
    z = lambda n: jnp.zeros((n, wt.shape[1]), wt.dtype)
    wk = wt[4224:4256]
    rot = jnp.concatenate([-wk[16:], wk[:16]], axis=0)
    return jnp.concatenate([wt[0:3584], wt[4256:7840], wt[3584:4224], z(64), wk, z(32), z(64), rot, z(32)], axis=0)


def _unpack_w_in_grad(g):
    c1 = W_AB + W_G5
    g_rot = g[c1 + 768 + 64:c1 + 768 + 96].astype(F32)
    gk = g[c1 + 640 + 64:c1 + 640 + 96].astype(F32) + jnp.concatenate([g_rot[16:], -g_rot[:16]], axis=0)
    return jnp.concatenate([g[0:3584], g[c1:c1 + 640], gk.astype(g.dtype), g[W_AB:W_AB + W_G5]], axis=0)


def _pack_w_uq(w):
    lead = w.shape[:-1]
    wh = w.reshape(lead + (HEADS, QK_NOPE + QK_ROPE))
    nope, rope = wh[..., :QK_NOPE], wh[..., QK_NOPE:]
    z32 = jnp.zeros_like(rope)
    q1 = jnp.concatenate([nope, rope, z32], axis=-1).reshape(lead + (HEADS * HEAD_PAD,))
    q2 = jnp.concatenate([jnp.zeros_like(nope), _rot_cols(rope), z32], axis=-1).reshape(lead + (HEADS * HEAD_PAD,))
    return jnp.concatenate([q1, q2], axis=-1)


def _unpack_w_uq_grad(g):
    lead = g.shape[:-1]
    g1 = g[..., :1024].reshape(lead + (HEADS, HEAD_PAD))
    g2 = g[..., 1024:].reshape(lead + (HEADS, HEAD_PAD))
    rope = g1[..., QK_NOPE:QK_NOPE + QK_ROPE] + _unrot_cols(g2[..., QK_NOPE:QK_NOPE + QK_ROPE])
    return jnp.concatenate([g1[..., :QK_NOPE], rope], axis=-1).reshape(lead + (HEADS * (QK_NOPE + QK_ROPE),))


def _pack_w_ukv(w):
    lead = w.shape[:-1]
    wh = w.reshape(lead + (HEADS, QK_NOPE + V_HEAD))
    kn, vv = wh[..., :QK_NOPE], wh[..., QK_NOPE:]
    k1 = jnp.concatenate([kn, jnp.zeros_like(kn)], axis=-1).reshape(lead + (HEADS * HEAD_PAD,))
    return jnp.concatenate([k1, vv.reshape(lead + (HEADS * V_HEAD,))], axis=-1)


def _unpack_w_ukv_grad(g):
    lead = g.shape[:-1]
    gk = g[..., :1024].reshape(lead + (HEADS, HEAD_PAD))[..., :QK_NOPE]
    gv = g[..., 1024:].reshape(lead + (HEADS, V_HEAD))
    return jnp.concatenate([gk, gv], axis=-1).reshape(lead + (HEADS * (QK_NOPE + V_HEAD),))


def _join_cols(g4):
    return jnp.concatenate([g4[j] for j in range(4)], axis=-1)


def _split_cols(w):
    c4 = w.shape[-1] // 4
    return jnp.stack([w[:, j * c4:(j + 1) * c4] for j in range(4)])


def kernel(x, c, positions, w_ada, b_ada, w_in, conv_a_w, conv_a_b, ln_a_g, ln_a_b, w_a_out, conv_b_w, w_b_out, q_norm_g, kv_norm_g, w_uq, w_ukv, w_c_out, w_o, ln_g, ln_b, loss_target, m_w_ada, m_b_ada, m_w_in, m_conv_a_w, m_conv_a_b, m_ln_a_g, m_ln_a_b, m_w_a_out, m_conv_b_w, m_w_b_out, m_q_norm_g, m_kv_norm_g, m_w_uq, m_w_ukv, m_w_c_out, m_w_o, m_ln_g, m_ln_b, v_w_ada, v_b_ada, v_w_in, v_conv_a_w, v_conv_a_b, v_ln_a_g, v_ln_a_b, v_w_a_out, v_conv_b_w, v_w_b_out, v_q_norm_g, v_kv_norm_g, v_w_uq, v_w_ukv, v_w_c_out, v_w_o, v_ln_g, v_ln_b):
    weights = dict(w_ada=w_ada, b_ada=b_ada, w_in=w_in, conv_a_w=conv_a_w, conv_a_b=conv_a_b, ln_a_g=ln_a_g, ln_a_b=ln_a_b,
                   w_a_out=w_a_out, conv_b_w=conv_b_w, w_b_out=w_b_out, q_norm_g=q_norm_g, kv_norm_g=kv_norm_g, w_uq=w_uq,
                   w_ukv=w_ukv, w_c_out=w_c_out, w_o=w_o, ln_g=ln_g, ln_b=ln_b)
    mom_m = dict(w_ada=m_w_ada, b_ada=m_b_ada, w_in=m_w_in, conv_a_w=m_conv_a_w, conv_a_b=m_conv_a_b, ln_a_g=m_ln_a_g,
                 ln_a_b=m_ln_a_b, w_a_out=m_w_a_out, conv_b_w=m_conv_b_w, w_b_out=m_w_b_out, q_norm_g=m_q_norm_g,
                 kv_norm_g=m_kv_norm_g, w_uq=m_w_uq, w_ukv=m_w_ukv, w_c_out=m_w_c_out, w_o=m_w_o, ln_g=m_ln_g, ln_b=m_ln_b)
    mom_v = dict(w_ada=v_w_ada, b_ada=v_b_ada, w_in=v_w_in, conv_a_w=v_conv_a_w, conv_a_b=v_conv_a_b, ln_a_g=v_ln_a_g,
                 ln_a_b=v_ln_a_b, w_a_out=v_w_a_out, conv_b_w=v_conv_b_w, w_b_out=v_w_b_out, q_norm_g=v_q_norm_g,
                 kv_norm_g=v_kv_norm_g, w_uq=v_w_uq, w_ukv=v_w_ukv, w_c_out=v_w_c_out, w_o=v_w_o, ln_g=v_ln_g, ln_b=v_ln_b)
    order = ["w_ada", "b_ada", "w_in", "conv_a_w", "conv_a_b", "ln_a_g", "ln_a_b", "w_a_out", "conv_b_w", "w_b_out",
             "q_norm_g", "kv_norm_g", "w_uq", "w_ukv", "w_c_out", "w_o", "ln_g", "ln_b"]

    n_layers = w_in.shape[0]
    s, d = x.shape[1], x.shape[2]
    alpha = float((2 * n_layers) ** 0.25)
    ix, iy, ic = lax.axis_index("x"), lax.axis_index("y"), lax.axis_index("c")
    me = 4 * ix + 2 * iy + ic
    chip = 2 * ix + iy
    xs = x[0]

    def four(tree, cast):
        return [cast(jnp.swapaxes(tree["w_in"], 1, 2)), jnp.concatenate([cast(tree[n]) for n, _ in ROWS_256], axis=-2),
                cast(tree["w_uq"]), cast(tree["w_o"])]

    shards = four(weights, lambda a: a.astype(BF16))
    landed_w = _run_exchange(_gather_layer_exchange(shards, 0), "gather_weights_first")
    landed_w = _run_exchange(_gather_pair_exchange(landed_w), "gather_weights_first_pair")
    w_pack, wq12, wkv, w_a, w_b, w_c, w_of = [], [], [], [], [], [], []

    def unpack_layer(g_in, g_256, g_uq, g_wo):
        w_pack.append(_pack_w_in(g_in.reshape(-1, d)))
        wq12.append(_pack_w_uq(_join_cols(g_uq)))
        rows = _join_cols(g_256)
        w_a.append(rows[0:512])
        w_b.append(rows[512:1024])
        w_c.append(rows[1024:1536])
        wkv.append(_pack_w_ukv(rows[1536:1792]))
        w_of.append(g_wo.reshape(d, d))

    cw_a = conv_a_w.reshape(n_layers * CONV_K, 128)
    cw_b = conv_b_w.reshape(n_layers * SC_K, 128)
    c_rows = d // 128
    first_rows = c_rows + n_layers * (CONV_K + SC_K)
    first_pad = (-first_rows) % 8
    first = jnp.concatenate([c.reshape(c_rows, 128), cw_a, cw_b, jnp.zeros((first_pad, 128), F32)], axis=0)
    fr = first_rows + first_pad
    got = _all_gather_small(first, "gather_cond").reshape(8, fr, 128)
    c_all = got[:, :c_rows].reshape(8, d)
    conv_a_full = jnp.moveaxis(got[0::2, c_rows:c_rows + n_layers * CONV_K].reshape(4, n_layers, CONV_K, 128), 0, 2)
    conv_a_full = conv_a_full.reshape(n_layers, CONV_K, 512)
    b0 = c_rows + n_layers * CONV_K
    conv_b_full = jnp.moveaxis(got[0::2, b0:b0 + n_layers * SC_K].reshape(4, n_layers, SC_K, 128), 0, 2)
    conv_b_full = conv_b_full.reshape(n_layers, SC_K, 512)

    ada_sh, c_act = _ada_shard(c_all, w_ada.astype(BF16))
    ada_all = _all_gather_small(ada_sh.reshape(n_layers * 8, 768), "gather_ada").reshape(8, n_layers, 8, 768)
    ada_mine = lax.dynamic_index_in_dim(ada_all[0::2], me, axis=2, keepdims=False)
    ada = jnp.moveaxis(ada_mine, 0, 1).reshape(n_layers, 3 * d) + b_ada
    def stacked(a):
        return a.reshape(n_layers, 1, a.shape[-1])

    shift, scale, gate = stacked(ada[:, 0:d]), stacked(ada[:, d:2 * d]), stacked(ada[:, 2 * d:3 * d])
    conv_a_b3, ln_a_g3, ln_a_b3 = stacked(conv_a_b), stacked(ln_a_g), stacked(ln_a_b)
    q_norm_g3, kv_norm_g3, ln_g3, ln_b3 = stacked(q_norm_g), stacked(kv_norm_g), stacked(ln_g), stacked(ln_b)

    tcos, tsin = _rope_tables(positions.reshape(s, 1))

    saved = []
    h = xs
    for l in range(n_layers):
        unpack_layer(*landed_w)
        u = _modulate(h, (scale, l), (shift, l))
        more = l + 1 < n_layers
        (proj,), lw_in = _matmul_nt(u, w_pack[l], COL_BLK, "proj_fwd",
                                    _gather_layer_exchange(shards[:1], l + 1, (0, 2)) if more else None)
        (act_a, act_b, ca, cb), lw_rest = _mix_ab_fwd(
            proj, (conv_a_full, l), (conv_a_b3, l), (ln_a_g3, l), (ln_a_b3, l), (conv_b_full, l),
            _gather_layer_exchange(shards[1:], l + 1) if more else None)
        qf, kf, vv = _attn_prep_fwd(proj, (q_norm_g3, l), (kv_norm_g3, l), wq12[l], wkv[l], tcos, tsin)
        (o, lse), lw_in = _attention_fwd(
            qf, kf, vv, _gather_layer_exchange(shards[:1], l + 1, (1, 2), lw_in) if more else None)
        (h_next, xh, rstd), landed_w = _merge_fwd(
            act_a, act_b, o, proj, h, (gate, l), w_a[l], w_b[l], w_c[l], w_of[l], (ln_g3, l), (ln_b3, l), alpha,
            _gather_pair_exchange(lw_in + lw_rest) if more else None)
        saved.append(dict(x=h, u=u, proj=proj, act_a=act_a, act_b=act_b, ca=ca, cb=cb, qf=qf, kf=kf, v=vv, o=o, lse=lse,
                          xh=xh, rstd=rstd))
        h = h_next

    dy, loss_part = _loss_head(h, loss_target[0])
    loss = lax.psum(loss_part[0, 0], ("x", "y", "c"))

    c_idx = ic.reshape(1).astype(jnp.int32)
    reduced = None
    pending = None
    small_rows = [None] * n_layers
    for l in reversed(range(n_layers)):
        sv = saved[l]
        ((dxres, dact_a, dact_b, do, dproj, m_bf, dout_bf, dya, dyb, dyc, act_c, d_ln_g, d_ln_b, d_gate),
         rcv_rest) = _merge_bwd(
            dy, sv["xh"], sv["rstd"], sv["act_a"], sv["act_b"], sv["o"], sv["proj"], (gate, l), w_a[l], w_b[l],
            w_c[l], w_of[l], (ln_g3, l), alpha, _scatter_exchange(pending[1:]) if pending else None)
        g_o = _matmul_tn(m_bf, dout_bf, d, "grad_w_o")
        g_a = _matmul_tn(sv["act_a"], dya, d, "grad_w_a_out")
        g_b = _matmul_tn(sv["act_b"], dyb, d, "grad_w_b_out")
        g_c = _matmul_tn(act_c, dyc, d, "grad_w_c_out")
        (dqf, dkf, dvv), rcv_in = _attention_bwd(sv["qf"], sv["kf"], sv["v"], sv["o"], do, sv["lse"],
                                                 _scatter_exchange(pending[:1]) if pending else None)
        if pending:
            reduced = _sum_chips(rcv_in + rcv_rest, shapes, reduced, l + 1, n_layers, c_idx)
        dproj, qn, kvn, dq12, dkv12, d_qg, d_kg = _attn_prep_bwd(
            sv["proj"], dqf, dkf, dvv, (q_norm_g3, l), (kv_norm_g3, l), wq12[l], wkv[l], tcos, tsin, dproj)
        g_uq = _unpack_w_uq_grad(_matmul_tn(qn, dq12, 1024, "grad_w_uq"))
        g_kv = _unpack_w_ukv_grad(_matmul_tn(kvn, dkv12, 768, "grad_w_ukv"))
        dproj, d_caw, d_cab, d_lag, d_lab, d_cbw = _mix_ab_bwd(
            sv["proj"], sv["ca"], sv["cb"], dact_a, dact_b, (conv_a_full, l), (ln_a_g3, l), (ln_a_b3, l),
            (conv_b_full, l), dproj)
        g_in_l = _unpack_w_in_grad(_matmul_tn(dproj, sv["u"], d, "grad_w_in", bm=COL_BLK, whole_sum=True))
        partial = [g_in_l.reshape(4, -1, d),
                   jnp.concatenate([_split_cols(g) for g in (g_a, g_b, g_c, g_kv)], axis=1).astype(BF16),
                   _split_cols(g_uq).astype(BF16), g_o.reshape(4, d // 4, d).astype(BF16)]
        shapes = [p.shape[1:] for p in partial]
        (dy, d_scale, d_shift), from_sibling = _proj_bwd_input(dproj, w_pack[l], sv["x"], (scale, l), dxres,
                                                               _swap_exchange(partial))
        small_rows[l] = jnp.concatenate(
            [d_shift[0], d_scale[0], d_gate[0], d_cab[0], d_lag[0], d_lab[0], d_qg[0], d_kg[0], d_ln_g[0], d_ln_b[0],
             d_caw[:CONV_K].reshape(-1), d_cbw[:SC_K].reshape(-1)])
        pending = _add_halves(partial, from_sibling, c_idx)
    grad_x = dy[None]
    received = _run_exchange(_scatter_exchange(pending), "grad_scatter_last")
    reduced = _sum_chips(received, shapes, reduced, 0, n_layers, c_idx)
    r_in, r_256, r_uq, r_wo = _run_exchange(_gather_pair_exchange(reduced), "grad_pair_halves")
    grads = {"w_in": jnp.swapaxes(r_in, 1, 2), "w_uq": r_uq, "w_o": r_wo}
    row0 = 0
    for n, rows in ROWS_256:
        grads[n] = r_256[:, row0:row0 + rows]
        row0 += rows

    small = jnp.stack(small_rows)
    p_small = small.shape[1]
    n_small = n_layers * p_small
    pad_small = (-n_small) % (8 * 128)
    small_flat = jnp.concatenate([small.reshape(-1), jnp.zeros((pad_small,), F32)]).reshape(-1, 128)
    sr = small_flat.shape[0]
    small_all = _all_gather_small(small_flat, "gather_small_grads").reshape(8, sr, 128)
    small_sum = _sum_devices(small_all).reshape(-1)[:n_small].reshape(n_layers, p_small)
    d_ada_all = small_all.reshape(8, -1)[:, :n_small].reshape(8, n_layers, p_small)[:, :, :3 * d]

    cuts = np.cumsum([0, 3 * d, 512, 512, 512, 384, 256, d, d, CONV_K * 512, SC_K * 512])
    pieces = [small_sum[:, cuts[k]:cuts[k + 1]] for k in range(10)]
    grads.update(zip(SMALL, pieces[:8]))
    ga_full = pieces[8].reshape(n_layers, CONV_K, 512)
    gb_full = pieces[9].reshape(n_layers, SC_K, 512)
    grads["conv_a_w"] = lax.dynamic_slice_in_dim(ga_full, chip * 128, 128, axis=2)
    grads["conv_b_w"] = lax.dynamic_slice_in_dim(gb_full, chip * 128, 128, axis=2)

    dada_sh = lax.dynamic_slice_in_dim(d_ada_all, chip * 768, 768, axis=2)
    dada16 = jnp.concatenate([jnp.moveaxis(dada_sh, 0, 1), jnp.zeros((n_layers, 8, 768), F32)], axis=1).astype(BF16)
    cact16 = jnp.concatenate([c_act, jnp.zeros_like(c_act)], axis=0).astype(BF16)
    grads["w_ada"] = _ada_weight_grad(cact16, dada16)

    def as2d(a):
        return a.reshape(-1, a.shape[-1])

    deltas, new_m, new_v = {}, {}, {}
    for n in ("w_in", "w_a_out", "w_b_out", "w_c_out", "w_uq", "w_ukv", "w_o", "w_ada"):
        view = (lambda a: jnp.swapaxes(a, 1, 2)) if n == "w_in" else (lambda a: a)
        shape = view(weights[n]).shape
        dl, mn, vn = _adamw(as2d(view(grads[n])), as2d(view(weights[n])), as2d(view(mom_m[n])), as2d(view(mom_v[n])),
                            "adamw_" + n)
        deltas[n], new_m[n], new_v[n] = view(dl.reshape(shape)), view(mn.reshape(shape)), view(vn.reshape(shape))

    def pack_small(tree):
        flat = jnp.concatenate([tree[n].reshape(-1) for n in SMALL])
        pad = (-flat.shape[0]) % (8 * 128)
        return jnp.concatenate([flat, jnp.zeros((pad,), F32)]).reshape(-1, 128)

    dl_s, mn_s, vn_s = _adamw(pack_small(grads), pack_small(weights), pack_small(mom_m), pack_small(mom_v), "adamw_small")
    off = 0
    for n in SMALL:
        sz = int(np.prod(weights[n].shape))
        for tree, flat in ((deltas, dl_s), (new_m, mn_s), (new_v, vn_s)):
            tree[n] = flat.reshape(-1)[off:off + sz].reshape(weights[n].shape)
        off += sz

    return (loss, grad_x, *[grads[n] for n in order], *[deltas[n] for n in order], *[new_m[n] for n in order],
            *[new_v[n] for n in order])
```

```python
import functools

import numpy as np
import jax
import jax.numpy as jnp
from jax import lax
from jax.experimental import pallas as pl
from jax.experimental.pallas import tpu as pltpu

F32 = jnp.float32
BF16 = jnp.bfloat16
MESH = pl.DeviceIdType.MESH

HEADS = 8
QK_NOPE = 64
QK_ROPE = 32
V_HEAD = 64
CONV_K = 31
SC_K = 3
LN_EPS = 1e-5
RMS_EPS = 1e-6
ROPE_THETA = 10000.0
ATT_SCALE = (QK_NOPE + QK_ROPE) ** -0.5
ADAM_LR = 0.001
ADAM_B1 = 0.9
ADAM_B2 = 0.999
ADAM_EPS = 1e-08
ADAM_WD = 0.01
ADAM_STEP = 10

W_AB = 3584
W_G5 = 3584
W_C1 = 896
W_PACK = W_AB + W_G5 + W_C1
COL_BLK = 896
HEAD_PAD = 128
CONV_HALO = 32
NEG_BIG = -1e30

VMEM_LIMIT_MB = 56
ATT_BLOCK = 512
ROW_BLOCK = 256


def _cparams(n_grid, mb=VMEM_LIMIT_MB):
    return pltpu.CompilerParams(dimension_semantics=("arbitrary",) * n_grid, vmem_limit_bytes=mb * 2 ** 20)


def _sig(x):
    return 1.0 / (1.0 + jnp.exp(-x))


def _dsilu(x, s):
    return s * (1.0 + x * (1.0 - s))


def _dot(a, b):
    return jnp.dot(a, b, preferred_element_type=F32)


def _dot_nt(a, b):
    return lax.dot_general(a, b, (((1,), (1,)), ((), ())), preferred_element_type=F32)


def _dot_tn(a, b):
    return lax.dot_general(a, b, (((0,), (0,)), ((), ())), preferred_element_type=F32)


def _row_spec(t, w):
    return pl.BlockSpec((t, w), lambda i: (i, 0))


def _full_spec(shape):
    nd = len(shape)
    return pl.BlockSpec(shape, lambda *_: (0,) * nd)


def _layer_param(p):
    arr, layer = p
    return arr, pl.BlockSpec((None,) + arr.shape[1:], lambda *_: (layer, 0, 0))


def _my_place():
    x, y, c = lax.axis_index("x"), lax.axis_index("y"), lax.axis_index("c")
    chips = [(1 - x, y), (x, 1 - y), (1 - x, 1 - y)]
    return x, y, c, chips


def _all_gather_small(v, name):
    m_per, n = v.shape

    def body(x_ref, out_ref, send_sems, recv_sems, local_sem):
        x, y, c, chips = _my_place()
        me, sibling = (x, y, c), (x, y, 1 - c)

        def rows(px, py, pc):
            return out_ref.at[pl.ds((4 * px + 2 * py + pc) * m_per, m_per), :]

        def copy(k, block, to, src=None):
            return pltpu.make_async_remote_copy(
                src_ref=rows(*block) if src is None else src, dst_ref=rows(*block),
                send_sem=send_sems.at[k], recv_sem=recv_sems.at[k], device_id=to, device_id_type=MESH)

        mine = pltpu.make_async_copy(x_ref, rows(*me), local_sem)
        mine.start()
        first = [copy(0, me, sibling, src=x_ref)]
        first += [copy(1 + j, me, (*chip, c), src=x_ref) for j, chip in enumerate(chips)]
        for cp in first:
            cp.start()
        passed = [copy(4 + j, (*chip, c), sibling) for j, chip in enumerate(chips)]
        for j, chip in enumerate(chips):
            copy(1 + j, (*chip, c), me).wait_recv()
            passed[j].start()
        copy(0, sibling, me).wait_recv()
        for j, chip in enumerate(chips):
            copy(4 + j, (*chip, 1 - c), me).wait_recv()
        for cp in first + passed:
            cp.wait_send()
        mine.wait()

    return pl.pallas_call(
        body, name=name,
        out_shape=jax.ShapeDtypeStruct((8 * m_per, n), v.dtype),
        in_specs=[pl.BlockSpec(memory_space=pltpu.VMEM)],
        out_specs=pl.BlockSpec(memory_space=pltpu.VMEM),
        scratch_shapes=[pltpu.SemaphoreType.DMA((7,)), pltpu.SemaphoreType.DMA((7,)), pltpu.SemaphoreType.DMA],
        compiler_params=pltpu.CompilerParams(vmem_limit_bytes=VMEM_LIMIT_MB * 2 ** 20),
    )(v)


def _any_specs(n):
    return [pl.BlockSpec(memory_space=pl.ANY)] * n


def _splits_rows(shape):
    return shape[-2] % 32 == 0


def _window(shape, c, p=0, n_parts=1):
    r, n = shape[-2], shape[-1]
    if _splits_rows(shape):
        size = r // 2 // n_parts
        return (pl.ds(c * (r // 2) + p * size, size), slice(None))
    size = n // 2 // n_parts
    return (slice(None), pl.ds(c * (n // 2) + p * size, size))


def _half_shape(shape):
    r, n = shape[-2], shape[-1]
    return (r // 2, n) if _splits_rows(shape) else (r, n // 2)


def _half_block_index(shape, c):
    return (c, 0) if _splits_rows(shape) else (0, c)


class _Exchange:
    def __init__(self, inputs, out_shapes, sems, start, finish, aliases=None):
        self.inputs, self.out_shapes, self.sems = list(inputs), list(out_shapes), list(sems)
        self.start, self.finish, self.aliases = start, finish, dict(aliases or {})


def _join_exchanges(a, b):
    na, oa, sa = len(a.inputs), len(a.out_shapes), len(a.sems)

    def start(xin, xout, sems):
        a.start(xin[:na], xout[:oa], sems[:sa])
        b.start(xin[na:], xout[oa:], sems[sa:])

    def finish(xin, xout, sems):
        a.finish(xin[:na], xout[:oa], sems[:sa])
        b.finish(xin[na:], xout[oa:], sems[sa:])

    aliases = dict(a.aliases)
    aliases.update({na + i: oa + o for i, o in b.aliases.items()})
    return _Exchange(a.inputs + b.inputs, a.out_shapes + b.out_shapes, a.sems + b.sems, start, finish, aliases)


def _call(body, args, exchange=None, *, grid, in_specs, out_specs, out_shape, scratch_shapes=(),
          input_output_aliases=None, **kw):
    aliases = dict(input_output_aliases or {})
    if exchange is None:
        outs = pl.pallas_call(body, grid=grid, in_specs=list(in_specs), out_specs=list(out_specs),
                              out_shape=list(out_shape), scratch_shapes=list(scratch_shapes),
                              input_output_aliases=aliases, **kw)(*args)
        return list(outs), []
    n_in, n_out, n_scr = len(in_specs), len(out_specs), len(scratch_shapes)
    x_in, x_out = len(exchange.inputs), len(exchange.out_shapes)

    def wrapped(*refs):
        ins, xin = refs[:n_in], refs[n_in:n_in + x_in]
        o0 = n_in + x_in
        outs, xout = refs[o0:o0 + n_out], refs[o0 + n_out:o0 + n_out + x_out]
        s0 = o0 + n_out + x_out
        scr, xsem = refs[s0:s0 + n_scr], refs[s0 + n_scr:]
        ids = [pl.program_id(k) for k in range(len(grid))]
        first = functools.reduce(jnp.logical_and, [i == 0 for i in ids])
        last = functools.reduce(jnp.logical_and, [i == g - 1 for i, g in zip(ids, grid)])

        @pl.when(first)
        def _():
            exchange.start(xin, xout, xsem)

        body(*ins, *outs, *scr)

        @pl.when(last)
        def _():
            exchange.finish(xin, xout, xsem)

    aliases.update({n_in + i: n_out + o for i, o in exchange.aliases.items()})
    outs = pl.pallas_call(
        wrapped, grid=grid, in_specs=list(in_specs) + _any_specs(x_in), out_specs=list(out_specs) + _any_specs(x_out),
        out_shape=list(out_shape) + exchange.out_shapes, scratch_shapes=list(scratch_shapes) + exchange.sems,
        input_output_aliases=aliases, **kw)(*args, *exchange.inputs)
    return list(outs[:n_out]), list(outs[n_out:])


def _run_exchange(exchange, name):
    x_in, x_out = len(exchange.inputs), len(exchange.out_shapes)

    def body(*refs):
        xin, xout, xsem = refs[:x_in], refs[x_in:x_in + x_out], refs[x_in + x_out:]
        exchange.start(xin, xout, xsem)
        exchange.finish(xin, xout, xsem)

    outs = pl.pallas_call(
        body, name=name, in_specs=_any_specs(x_in), out_specs=_any_specs(x_out), out_shape=exchange.out_shapes,
        scratch_shapes=exchange.sems, input_output_aliases=exchange.aliases)(*exchange.inputs)
    return list(outs)


def _gather_layer_exchange(ws, layer, part=(0, 1), lands=None):
    n = len(ws)
    p, n_parts = part

    def copies(xin, xout, sems):
        send_sems, recv_sems, local_sems = sems
        x, y, c, chips = _my_place()
        k_me = 2 * x + y
        local, sends, recvs = [], [], []
        for a in range(n):
            win = _window(ws[a].shape, c, p, n_parts)
            src = xin[a].at[(layer,) + win]
            dst = xout[a].at[(k_me,) + win]
            local.append(pltpu.make_async_copy(src, dst, local_sems.at[a]))
            for j, (px, py) in enumerate(chips):
                sends.append(pltpu.make_async_remote_copy(
                    src_ref=src, dst_ref=dst, send_sem=send_sems.at[a, j], recv_sem=recv_sems.at[a, j],
                    device_id=(px, py, c), device_id_type=MESH))
                recvs.append(pltpu.make_async_remote_copy(
                    src_ref=src, dst_ref=xout[a].at[(2 * px + py,) + win], send_sem=send_sems.at[a, j],
                    recv_sem=recv_sems.at[a, j], device_id=(px, py, c), device_id_type=MESH))
        return local, sends, recvs

    def start(xin, xout, sems):
        local, sends, _ = copies(xin, xout, sems)
        for cp in local + sends:
            cp.start()

    def finish(xin, xout, sems):
        local, sends, recvs = copies(xin, xout, sems)
        for cp in recvs:
            cp.wait_recv()
        for cp in sends:
            cp.wait_send()
        for cp in local:
            cp.wait()

    return _Exchange(list(ws) + list(lands or []), [jax.ShapeDtypeStruct((4,) + w.shape[1:], w.dtype) for w in ws],
                     [pltpu.SemaphoreType.DMA((n, 3)), pltpu.SemaphoreType.DMA((n, 3)), pltpu.SemaphoreType.DMA((n,))],
                     start, finish, aliases={n + a: a for a in range(n)} if lands else None)


def _gather_pair_exchange(lands):
    n = len(lands)

    def copies(xout, sems):
        send_sems, recv_sems = sems
        x, y, c, _ = _my_place()
        def half(a, core):
            return xout[a].at[(slice(None),) + _window(lands[a].shape, core)]

        sends = [pltpu.make_async_remote_copy(
            src_ref=half(a, c), dst_ref=half(a, c), send_sem=send_sems.at[a],
            recv_sem=recv_sems.at[a], device_id=(x, y, 1 - c), device_id_type=MESH) for a in range(n)]
        recvs = [pltpu.make_async_remote_copy(
            src_ref=half(a, c), dst_ref=half(a, 1 - c), send_sem=send_sems.at[a],
            recv_sem=recv_sems.at[a], device_id=(x, y, 1 - c), device_id_type=MESH) for a in range(n)]
        return sends, recvs

    def start(xin, xout, sems):
        for cp in copies(xout, sems)[0]:
            cp.start()

    def finish(xin, xout, sems):
        sends, recvs = copies(xout, sems)
        for cp in recvs:
            cp.wait_recv()
        for cp in sends:
            cp.wait_send()

    return _Exchange(lands, [jax.ShapeDtypeStruct(w.shape, w.dtype) for w in lands],
                     [pltpu.SemaphoreType.DMA((n,)), pltpu.SemaphoreType.DMA((n,))], start, finish,
                     aliases={a: a for a in range(n)})


def _swap_exchange(ps):
    n = len(ps)

    def copies(xin, xout, sems):
        send_sems, recv_sems = sems
        x, y, c, _ = _my_place()
        return [pltpu.make_async_remote_copy(
            src_ref=xin[a].at[(slice(None),) + _window(ps[a].shape, 1 - c)], dst_ref=xout[a],
            send_sem=send_sems.at[a], recv_sem=recv_sems.at[a], device_id=(x, y, 1 - c), device_id_type=MESH)
            for a in range(n)]

    def start(xin, xout, sems):
        for cp in copies(xin, xout, sems):
            cp.start()

    def finish(xin, xout, sems):
        for cp in copies(xin, xout, sems):
            cp.wait()

    return _Exchange(ps, [jax.ShapeDtypeStruct((4,) + _half_shape(p.shape), p.dtype) for p in ps],
                     [pltpu.SemaphoreType.DMA((n,)), pltpu.SemaphoreType.DMA((n,))], start, finish)


def _add_halves(ps, lands, c_idx):
    n = len(ps)

    def body(c_ref, *refs):
        del c_ref
        for a in range(n):
            refs[2 * n + a][...] = (refs[a][...].astype(F32) + refs[n + a][...].astype(F32)).astype(BF16)

    def half_block(p):
        return (None,) + _half_shape(p.shape)

    def own_half(p):
        return lambda j, cr: (j,) + _half_block_index(p.shape, cr[0])

    grid_spec = pltpu.PrefetchScalarGridSpec(
        num_scalar_prefetch=1, grid=(4,),
        in_specs=[pl.BlockSpec(half_block(p), own_half(p)) for p in ps]
        + [pl.BlockSpec(half_block(p), lambda j, cr: (j, 0, 0)) for p in ps],
        out_specs=[pl.BlockSpec(half_block(p), lambda j, cr: (j, 0, 0)) for p in ps])
    return pl.pallas_call(
        body, name="grad_add_halves", grid_spec=grid_spec,
        out_shape=[jax.ShapeDtypeStruct((4,) + _half_shape(p.shape), BF16) for p in ps],
        compiler_params=_cparams(1),
    )(c_idx, *ps, *lands)


def _scatter_exchange(hs):
    n = len(hs)

    def copies(xin, xout, sems):
        send_sems, recv_sems, local_sems = sems
        x, y, c, chips = _my_place()
        k_me = 2 * x + y
        local = [pltpu.make_async_copy(xin[a].at[k_me], xout[a].at[k_me], local_sems.at[a]) for a in range(n)]
        sends, recvs = [], []
        for j, (px, py) in enumerate(chips):
            for a in range(n):
                sends.append(pltpu.make_async_remote_copy(
                    src_ref=xin[a].at[2 * px + py], dst_ref=xout[a].at[k_me], send_sem=send_sems.at[a, j],
                    recv_sem=recv_sems.at[a, j], device_id=(px, py, c), device_id_type=MESH))
                recvs.append(pltpu.make_async_remote_copy(
                    src_ref=xin[a].at[k_me], dst_ref=xout[a].at[2 * px + py], send_sem=send_sems.at[a, j],
                    recv_sem=recv_sems.at[a, j], device_id=(px, py, c), device_id_type=MESH))
        return local, sends, recvs

    def start(xin, xout, sems):
        local, sends, _ = copies(xin, xout, sems)
        for cp in local + sends:
            cp.start()

    def finish(xin, xout, sems):
        local, sends, recvs = copies(xin, xout, sems)
        for cp in recvs:
            cp.wait_recv()
        for cp in sends:
            cp.wait_send()
        for cp in local:
            cp.wait()

    return _Exchange(hs, [jax.ShapeDtypeStruct(h.shape, h.dtype) for h in hs],
                     [pltpu.SemaphoreType.DMA((n, 3)), pltpu.SemaphoreType.DMA((n, 3)), pltpu.SemaphoreType.DMA((n,))],
                     start, finish)


def _sum_chips(rcvs, shapes, accs, layer, n_layers, c_idx):
    n = len(rcvs)

    def body(c_ref, *refs):
        del c_ref
        outs = refs[len(refs) - n:]
        for a in range(n):
            r_ref = refs[a]
            acc = r_ref[0].astype(F32) + r_ref[1].astype(F32)
            acc = acc + r_ref[2].astype(F32)
            outs[a][...] = acc + r_ref[3].astype(F32)

    in_specs = [pl.BlockSpec(r.shape, lambda i, cr: (0, 0, 0)) for r in rcvs]
    args = list(rcvs)
    aliases = {}
    if accs is not None:
        in_specs += _any_specs(n)
        args += list(accs)
        aliases = {1 + n + a: a for a in range(n)}
    def own_half(shape):
        return lambda i, cr: (layer,) + _half_block_index(shape, cr[0])

    grid_spec = pltpu.PrefetchScalarGridSpec(
        num_scalar_prefetch=1, grid=(1,), in_specs=in_specs,
        out_specs=[pl.BlockSpec((None,) + r.shape[1:], own_half(sh)) for r, sh in zip(rcvs, shapes)])
    return pl.pallas_call(
        body, name="grad_sum_chips", grid_spec=grid_spec,
        out_shape=[jax.ShapeDtypeStruct((n_layers,) + tuple(sh), F32) for sh in shapes],
        input_output_aliases=aliases, compiler_params=_cparams(1),
    )(c_idx, *args)


def _sum_devices(g):
    _, r, n = g.shape

    def body(g_ref, o_ref):
        acc = g_ref[0]
        for d in range(1, 8):
            acc = acc + g_ref[d]
        o_ref[...] = acc

    return pl.pallas_call(
        body, name="small_grad_sum", grid=(1,),
        in_specs=[_full_spec((8, r, n))], out_specs=_full_spec((r, n)),
        out_shape=jax.ShapeDtypeStruct((r, n), F32), compiler_params=_cparams(1),
    )(g)


def _matmul_nt(a, wt, bn, name, exchange=None):
    s, k = a.shape
    n = wt.shape[0]
    bs = min(2048, s)

    def body(a_ref, w_ref, o_ref):
        o_ref[...] = _dot_nt(a_ref[...], w_ref[...])

    return _call(
        body, (a, wt), exchange, name=name, grid=(n // bn, s // bs),
        in_specs=[pl.BlockSpec((bs, k), lambda j, i: (i, 0)), pl.BlockSpec((bn, k), lambda j, i: (j, 0))],
        out_specs=[pl.BlockSpec((bs, bn), lambda j, i: (i, j))],
        out_shape=[jax.ShapeDtypeStruct((s, n), F32)], compiler_params=_cparams(2))


def _matmul_tn(a, b, bn, name, bm=None, whole_sum=False):
    s, m = a.shape
    n = b.shape[1]
    bs = s if whole_sum else min(512, s)
    bm = m if bm is None else bm

    def body(a_ref, b_ref, o_ref):
        if whole_sum:
            o_ref[...] = _dot_tn(a_ref[...], b_ref[...]).astype(BF16)
            return

        @pl.when(pl.program_id(2) == 0)
        def _():
            o_ref[...] = jnp.zeros_like(o_ref)

        o_ref[...] += _dot_tn(a_ref[...], b_ref[...])

    return pl.pallas_call(
        body, name=name, grid=(m // bm, n // bn, s // bs),
        in_specs=[pl.BlockSpec((bs, bm), lambda h, j, i: (i, h)), pl.BlockSpec((bs, bn), lambda h, j, i: (i, j))],
        out_specs=pl.BlockSpec((bm, bn), lambda h, j, i: (h, j)),
        out_shape=jax.ShapeDtypeStruct((m, n), BF16 if whole_sum else F32), compiler_params=_cparams(3),
    )(a, b)


def _modulate(x, scale, shift):
    s, d = x.shape
    t = min(512, s)
    (scale, scale_spec), (shift, shift_spec) = _layer_param(scale), _layer_param(shift)

    def body(x_ref, sc_ref, sh_ref, u_ref):
        u_ref[...] = (x_ref[...] * (1.0 + sc_ref[...]) + sh_ref[...]).astype(BF16)

    return pl.pallas_call(
        body, name="modulate", grid=(s // t,),
        in_specs=[_row_spec(t, d), scale_spec, shift_spec],
        out_specs=_row_spec(t, d), out_shape=jax.ShapeDtypeStruct((s, d), BF16), compiler_params=_cparams(1),
    )(x, scale, shift)


def _rope_tables(pos_col):
    s = pos_col.shape[0]
    t = min(512, s)
    inv = ROPE_THETA ** (-np.arange(0, QK_ROPE, 2, dtype=np.float32) / QK_ROPE)
    lane_freq = np.zeros((1, HEAD_PAD), np.float32)
    lane_freq[0, QK_NOPE:QK_NOPE + 16] = inv
    lane_freq[0, QK_NOPE + 16:QK_NOPE + 32] = inv
    lane_mask = np.zeros((1, HEAD_PAD), np.float32)
    lane_mask[0, QK_NOPE:QK_NOPE + QK_ROPE] = 1.0

    def body(p_ref, f_ref, m_ref, cos_ref, sin_ref):
        ang = p_ref[...].astype(F32) * f_ref[...]
        cos_ref[...] = jnp.cos(ang) * m_ref[...]
        sin_ref[...] = jnp.sin(ang) * m_ref[...]

    return pl.pallas_call(
        body, name="rope_tables", grid=(s // t,),
        in_specs=[_row_spec(t, 1), _full_spec((1, HEAD_PAD)), _full_spec((1, HEAD_PAD))],
        out_specs=[_row_spec(t, HEAD_PAD), _row_spec(t, HEAD_PAD)],
        out_shape=[jax.ShapeDtypeStruct((s, HEAD_PAD), F32)] * 2, compiler_params=_cparams(1),
    )(pos_col, jnp.asarray(lane_freq), jnp.asarray(lane_mask))


def _ada_shard(c_all, w_ada_bf):
    n_layers, d, n = w_ada_bf.shape

    def body(c_ref, w_ref, o_ref, ca_ref):
        cv = c_ref[...]
        ca = cv * _sig(cv)
        ca_ref[...] = ca
        o_ref[...] = _dot(ca.astype(BF16), w_ref[...])

    return pl.pallas_call(
        body, name="ada_shard", grid=(n_layers,),
        in_specs=[_full_spec((8, d)), pl.BlockSpec((None, d, n), lambda l: (l, 0, 0))],
        out_specs=[pl.BlockSpec((None, 8, n), lambda l: (l, 0, 0)), _full_spec((8, d))],
        out_shape=[jax.ShapeDtypeStruct((n_layers, 8, n), F32), jax.ShapeDtypeStruct((8, d), F32)],
        compiler_params=_cparams(1),
    )(c_all, w_ada_bf)


def _ada_weight_grad(cact16, dada16):
    n_layers, _, n = dada16.shape
    d = cact16.shape[1]

    def body(c_ref, g_ref, o_ref):
        o_ref[...] = _dot_tn(c_ref[...], g_ref[...])

    return pl.pallas_call(
        body, name="ada_weight_grad", grid=(n_layers,),
        in_specs=[_full_spec((16, d)), pl.BlockSpec((None, 16, n), lambda l: (l, 0, 0))],
        out_specs=pl.BlockSpec((None, d, n), lambda l: (l, 0, 0)),
        out_shape=jax.ShapeDtypeStruct((n_layers, d, n), F32), compiler_params=_cparams(1),
    )(cact16, dada16)


def _layer_norm_rows(v, g, b):
    mu = jnp.mean(v, axis=-1, keepdims=True)
    var = jnp.mean(jnp.square(v - mu), axis=-1, keepdims=True)
    rstd = lax.rsqrt(var + LN_EPS)
    xh = (v - mu) * rstd
    return xh, rstd, xh * g + b


def _layer_norm_bwd_rows(dy_hat, xh, rstd):
    return rstd * (dy_hat - jnp.mean(dy_hat, axis=-1, keepdims=True) - xh * jnp.mean(dy_hat * xh, axis=-1, keepdims=True))


CONV_ROWS = 32


def _aligned_windows(buf_ref, offsets, t, shift_ref):
    base = {}
    for r in range(1, 8):
        group = [o for o in offsets if o % 8 == r]
        if group:
            lo = min(group)
            length = max(group) - lo + t
            shift_ref[r - 1, 0:length, :] = buf_ref[lo:lo + length, :]
            base[r] = lo

    def window(off, row0, rows):
        r = off % 8
        if r == 0:
            return buf_ref[off + row0:off + row0 + rows, :]
        start = off - base[r] + row0
        return shift_ref[r - 1, start:start + rows, :]

    return window


def _mix_ab_fwd(proj, conv_a_w, conv_a_b, ln_a_g, ln_a_b, conv_b_w, exchange=None):
    s = proj.shape[0]
    t = min(ROW_BLOCK, s)
    cw = 512
    halo = CONV_HALO
    (conv_a_w, caw_spec), (conv_a_b, cab_spec) = _layer_param(conv_a_w), _layer_param(conv_a_b)
    (ln_a_g, lg_spec), (ln_a_b, lb_spec), (conv_b_w, cbw_spec) = (_layer_param(ln_a_g), _layer_param(ln_a_b),
                                                                 _layer_param(conv_b_w))

    def body(p_ref, caw_ref, cab_ref, lg_ref, lb_ref, cbw_ref, acta_ref, actb_ref, ca_ref, cb_ref, abuf, zbuf, shifted):
        @pl.when(pl.program_id(0) == 0)
        def _():
            abuf[0:halo, :] = jnp.zeros((halo, cw), F32)
            zbuf[0:8, :] = jnp.zeros((8, cw), F32)

        abuf[halo:halo + t, :] = p_ref[:, 0:512] * _sig(p_ref[:, 512:1024])
        first = halo - (CONV_K - 1)
        window = _aligned_windows(abuf, range(first, first + CONV_K), t, shifted)
        for row0 in range(0, t, CONV_ROWS):
            acc = jnp.zeros((CONV_ROWS, cw), F32)
            for k in range(CONV_K):
                acc = acc + caw_ref[k:k + 1, :] * window(first + k, row0, CONV_ROWS)
            ca_ref[row0:row0 + CONV_ROWS, :] = acc + cab_ref[...]
        ca = ca_ref[...]
        _, _, ln = _layer_norm_rows(ca, lg_ref[...], lb_ref[...])
        ag = p_ref[:, 1024:1536]
        acta_ref[...] = ((ln * _sig(ln)) * (ag * _sig(ag))).astype(BF16)
        abuf[0:halo, :] = abuf[t:t + halo, :]

        zbuf[8:8 + t, :] = p_ref[:, 2560:3072] * p_ref[:, 1536:2048]
        accb = jnp.zeros((t, cw), F32)
        for k in range(SC_K):
            off = 8 - (SC_K - 1) + k
            accb = accb + cbw_ref[k:k + 1, :] * zbuf[off:off + t, :]
        cb_ref[...] = accb
        bg = p_ref[:, 3072:3584]
        actb_ref[...] = ((p_ref[:, 2048:2560] * accb) * (bg * _sig(bg))).astype(BF16)
        zbuf[0:8, :] = zbuf[t:t + 8, :]

    return _call(
        body, (proj, conv_a_w, conv_a_b, ln_a_g, ln_a_b, conv_b_w), exchange, name="mix_ab_fwd", grid=(s // t,),
        in_specs=[_row_spec(t, W_AB), caw_spec, cab_spec, lg_spec, lb_spec, cbw_spec],
        out_specs=[_row_spec(t, cw)] * 4,
        out_shape=[jax.ShapeDtypeStruct((s, cw), BF16)] * 2 + [jax.ShapeDtypeStruct((s, cw), F32)] * 2,
        scratch_shapes=[pltpu.VMEM((halo + t, cw), F32), pltpu.VMEM((8 + t, cw), F32), pltpu.VMEM((7, halo + t, cw), F32)],
        compiler_params=_cparams(1))


def _mix_ab_bwd(proj, ca, cb, dact_a, dact_b, conv_a_w, ln_a_g, ln_a_b, conv_b_w, dproj):
    s = proj.shape[0]
    t = min(ROW_BLOCK, s)
    nb = s // t
    cw = 512
    halo = CONV_HALO
    per = t // halo
    (conv_a_w, caw_spec), (ln_a_g, lg_spec) = _layer_param(conv_a_w), _layer_param(ln_a_g)
    (ln_a_b, lb_spec), (conv_b_w, cbw_spec) = _layer_param(ln_a_b), _layer_param(conv_b_w)

    def body(p_ref, ph_ref, ca_ref, cb_ref, da_ref, db_ref, caw_ref, lg_ref, lb_ref, cbw_ref, dproj_any,
             dg_ref, dcaw_ref, dcab_ref, dlg_ref, dlb_ref, dcbw_ref, abuf, dcabuf, zbuf, dcbbuf, shifted, da_buf, dw_acc):
        del dproj_any
        i = pl.program_id(0)
        r = nb - 1 - i

        @pl.when(i == 0)
        def _():
            dcaw_ref[...] = jnp.zeros_like(dcaw_ref)
            dw_acc[...] = jnp.zeros_like(dw_acc)
            dcab_ref[...] = jnp.zeros_like(dcab_ref)
            dlg_ref[...] = jnp.zeros_like(dlg_ref)
            dlb_ref[...] = jnp.zeros_like(dlb_ref)
            dcbw_ref[...] = jnp.zeros_like(dcbw_ref)
            dcabuf[t:t + halo, :] = jnp.zeros((halo, cw), F32)
            dcbbuf[t:t + 8, :] = jnp.zeros((8, cw), F32)

        keep = (r > 0).astype(F32)
        a1 = p_ref[:, 0:512]
        s2 = _sig(p_ref[:, 512:1024])
        abuf[0:halo, :] = (ph_ref[:, 0:512] * _sig(ph_ref[:, 512:1024])) * keep
        abuf[halo:halo + t, :] = a1 * s2
        ca_v = ca_ref[...]
        xh, rstd, ln = _layer_norm_rows(ca_v, lg_ref[...], lb_ref[...])
        s_ln = _sig(ln)
        ag = p_ref[:, 1024:1536]
        sg = _sig(ag)
        dact = da_ref[...]
        dsa = dact * (ag * sg)
        dg_ref[:, 1024:1536] = (dact * (ln * s_ln) * _dsilu(ag, sg)).astype(BF16)
        dln = dsa * _dsilu(ln, s_ln)
        dlg_ref[...] += jnp.sum(dln * xh, axis=0, keepdims=True)
        dlb_ref[...] += jnp.sum(dln, axis=0, keepdims=True)
        dca = _layer_norm_bwd_rows(dln * lg_ref[...], xh, rstd)
        dcab_ref[...] += jnp.sum(dca, axis=0, keepdims=True)
        dcabuf[0:t, :] = dca
        window = _aligned_windows(dcabuf, range(CONV_K), t, shifted)
        for row0 in range(0, t, CONV_ROWS):
            acc = jnp.zeros((CONV_ROWS, cw), F32)
            for off in range(CONV_K):
                k = (CONV_K - 1) - off
                acc = acc + caw_ref[k:k + 1, :] * window(off, row0, CONV_ROWS)
            da_buf[row0:row0 + CONV_ROWS, :] = acc
        d_a = da_buf[...]
        first = halo - (CONV_K - 1)
        window = _aligned_windows(abuf, range(first, first + CONV_K), t, shifted)
        for row0 in range(0, t, CONV_ROWS):
            dca_rows = dcabuf[row0:row0 + CONV_ROWS, :]
            for k in range(CONV_K):
                prod = dca_rows * window(first + k, row0, CONV_ROWS)
                part = prod[0:8]
                for q in range(8, CONV_ROWS, 8):
                    part = part + prod[q:q + 8]
                dw_acc[8 * k:8 * k + 8, :] += part

        @pl.when(i == nb - 1)
        def _():
            for k in range(CONV_K):
                dcaw_ref[k:k + 1, :] = jnp.sum(dw_acc[8 * k:8 * k + 8, :], axis=0, keepdims=True)

        dg_ref[:, 0:512] = (d_a * s2).astype(BF16)
        dg_ref[:, 512:1024] = (d_a * a1 * s2 * (1.0 - s2)).astype(BF16)
        dcabuf[t:t + halo, :] = dcabuf[0:halo, :]

        xb = p_ref[:, 1536:2048]
        gb = p_ref[:, 2048:2560]
        gc = p_ref[:, 2560:3072]
        bg = p_ref[:, 3072:3584]
        zbuf[0:8, :] = (ph_ref[halo - 8:halo, 2560:3072] * ph_ref[halo - 8:halo, 1536:2048]) * keep
        zbuf[8:8 + t, :] = gc * xb
        sbg = _sig(bg)
        cbv = cb_ref[...]
        dactb = db_ref[...]
        dyb = dactb * (bg * sbg)
        dg_ref[:, 3072:3584] = (dactb * (gb * cbv) * _dsilu(bg, sbg)).astype(BF16)
        dg_ref[:, 2048:2560] = (dyb * cbv).astype(BF16)
        dcb = dyb * gb
        dcbbuf[0:t, :] = dcb
        dz = jnp.zeros((t, cw), F32)
        for k in range(SC_K):
            off = (SC_K - 1) - k
            dz = dz + cbw_ref[k:k + 1, :] * dcbbuf[off:off + t, :]
            src = 8 - (SC_K - 1) + k
            dcbw_ref[k:k + 1, :] += jnp.sum(dcb * zbuf[src:src + t, :], axis=0, keepdims=True)
        dg_ref[:, 2560:3072] = (dz * xb).astype(BF16)
        dg_ref[:, 1536:2048] = (dz * gc).astype(BF16)
        dcbbuf[t:t + 8, :] = dcbbuf[0:8, :]

    rev = lambda i: (nb - 1 - i, 0)
    outs = pl.pallas_call(
        body, name="mix_ab_bwd", grid=(nb,),
        in_specs=[pl.BlockSpec((t, W_AB), rev),
                  pl.BlockSpec((halo, W_AB), lambda i: (jnp.maximum((nb - 1 - i) * per - 1, 0), 0)),
                  pl.BlockSpec((t, cw), rev), pl.BlockSpec((t, cw), rev), pl.BlockSpec((t, cw), rev),
                  pl.BlockSpec((t, cw), rev),
                  caw_spec, lg_spec, lb_spec, cbw_spec,
                  pl.BlockSpec(memory_space=pl.ANY)],
        out_specs=[pl.BlockSpec((t, W_AB), rev), _full_spec((CONV_HALO, cw)), _full_spec((1, cw)), _full_spec((1, cw)),
                   _full_spec((1, cw)), _full_spec((8, cw))],
        out_shape=[jax.ShapeDtypeStruct(dproj.shape, BF16), jax.ShapeDtypeStruct((CONV_HALO, cw), F32),
                   jax.ShapeDtypeStruct((1, cw), F32), jax.ShapeDtypeStruct((1, cw), F32),
                   jax.ShapeDtypeStruct((1, cw), F32), jax.ShapeDtypeStruct((8, cw), F32)],
        scratch_shapes=[pltpu.VMEM((halo + t, cw), F32), pltpu.VMEM((t + halo, cw), F32),
                        pltpu.VMEM((8 + t, cw), F32), pltpu.VMEM((t + 8, cw), F32), pltpu.VMEM((7, halo + t, cw), F32),
                        pltpu.VMEM((t, cw), F32), pltpu.VMEM((8 * CONV_HALO, cw), F32)],
        input_output_aliases={10: 0},
        compiler_params=_cparams(1),
    )(proj, proj, ca, cb, dact_a, dact_b, conv_a_w, ln_a_g, ln_a_b, conv_b_w, dproj)
    return outs


def _lane_is_nope():
    return lax.broadcasted_iota(jnp.int32, (1, HEAD_PAD), 1) < QK_NOPE


def _attn_prep_fwd(proj, q_g, kv_g, wq12, wkv, tcos, tsin):
    s = proj.shape[0]
    t = min(ROW_BLOCK, s)
    c0 = (W_AB + W_G5) // W_C1
    (q_g, qg_spec), (kv_g, kg_spec) = _layer_param(q_g), _layer_param(kv_g)

    def body(p_ref, qg_ref, kg_ref, wq_ref, wkv_ref, cos_ref, sin_ref, qf_ref, kf_ref, v_ref):
        ql = p_ref[:, 0:384]
        qn = (ql * lax.rsqrt(jnp.mean(jnp.square(ql), axis=-1, keepdims=True) + RMS_EPS) * qg_ref[...]).astype(BF16)
        q12 = _dot(qn, wq_ref[...])
        kvl = p_ref[:, 384:640]
        kvn = (kvl * lax.rsqrt(jnp.mean(jnp.square(kvl), axis=-1, keepdims=True) + RMS_EPS) * kg_ref[...]).astype(BF16)
        kv12 = _dot(kvn, wkv_ref[...])
        tcos_v = cos_ref[...]
        tsin_v = sin_ref[...]
        tq1 = jnp.where(_lane_is_nope(), 1.0, tcos_v)
        kpe = p_ref[:, 640:768] * tcos_v + p_ref[:, 768:896] * tsin_v
        for h in range(HEADS):
            lo = h * HEAD_PAD
            qf_ref[h] = (q12[:, lo:lo + HEAD_PAD] * tq1 + q12[:, 1024 + lo:1024 + lo + HEAD_PAD] * tsin_v).astype(BF16)
            kf_ref[h] = (kv12[:, lo:lo + HEAD_PAD] + kpe).astype(BF16)
        v_ref[...] = kv12[:, 1024:1536].astype(BF16)

    return pl.pallas_call(
        body, name="attn_prep_fwd", grid=(s // t,),
        in_specs=[pl.BlockSpec((t, W_C1), lambda i: (i, c0)), qg_spec, kg_spec,
                  _full_spec((384, 2048)), _full_spec((256, 1536)), _row_spec(t, HEAD_PAD), _row_spec(t, HEAD_PAD)],
        out_specs=[pl.BlockSpec((HEADS, t, HEAD_PAD), lambda i: (0, i, 0)),
                   pl.BlockSpec((HEADS, t, HEAD_PAD), lambda i: (0, i, 0)), _row_spec(t, 512)],
        out_shape=[jax.ShapeDtypeStruct((HEADS, s, HEAD_PAD), BF16), jax.ShapeDtypeStruct((HEADS, s, HEAD_PAD), BF16),
                   jax.ShapeDtypeStruct((s, 512), BF16)],
        compiler_params=_cparams(1),
    )(proj, q_g, kv_g, wq12, wkv, tcos, tsin)


def _attn_prep_bwd(proj, dqf, dkf, dv, q_g, kv_g, wq12, wkv, tcos, tsin, dproj):
    s = proj.shape[0]
    t = min(ROW_BLOCK, s)
    c0 = (W_AB + W_G5) // W_C1
    (q_g, qg_spec), (kv_g, kg_spec) = _layer_param(q_g), _layer_param(kv_g)

    def body(p_ref, dqf_ref, dkf_ref, dv_ref, qg_ref, kg_ref, wq_ref, wkv_ref, cos_ref, sin_ref, dproj_any,
             dg_ref, qn_ref, kvn_ref, dq12_ref, dkv12_ref, dqg_ref, dkg_ref):
        del dproj_any

        @pl.when(pl.program_id(0) == 0)
        def _():
            dqg_ref[...] = jnp.zeros_like(dqg_ref)
            dkg_ref[...] = jnp.zeros_like(dkg_ref)

        tcos_v = cos_ref[...]
        tsin_v = sin_ref[...]
        tq1 = jnp.where(_lane_is_nope(), 1.0, tcos_v)
        dkpe = jnp.zeros((t, HEAD_PAD), F32)
        for h in range(HEADS):
            lo = h * HEAD_PAD
            dq = dqf_ref[h]
            dq12_ref[:, lo:lo + HEAD_PAD] = (dq * tq1).astype(BF16)
            dq12_ref[:, 1024 + lo:1024 + lo + HEAD_PAD] = (dq * tsin_v).astype(BF16)
            dk = dkf_ref[h]
            dkv12_ref[:, lo:lo + HEAD_PAD] = dk.astype(BF16)
            dkpe = dkpe + dk
        dkv12_ref[:, 1024:1536] = dv_ref[...].astype(BF16)
        dg_ref[:, 640:768] = (dkpe * tcos_v).astype(BF16)
        dg_ref[:, 768:896] = (dkpe * tsin_v).astype(BF16)

        def rms_bwd(xl, g, dn):
            rr = lax.rsqrt(jnp.mean(jnp.square(xl), axis=-1, keepdims=True) + RMS_EPS)
            xn = xl * rr
            tt = dn * g
            return rr * (tt - xn * jnp.mean(tt * xn, axis=-1, keepdims=True)), jnp.sum(dn * xn, axis=0, keepdims=True), xn

        ql = p_ref[:, 0:384]
        dqn = _dot_nt(dq12_ref[...], wq_ref[...])
        dql, dqg, qxn = rms_bwd(ql, qg_ref[...], dqn)
        dg_ref[:, 0:384] = dql.astype(BF16)
        dqg_ref[...] += dqg
        qn_ref[...] = (qxn * qg_ref[...]).astype(BF16)
        kvl = p_ref[:, 384:640]
        dkvn = _dot_nt(dkv12_ref[...], wkv_ref[...])
        dkvl, dkg, kxn = rms_bwd(kvl, kg_ref[...], dkvn)
        dg_ref[:, 384:640] = dkvl.astype(BF16)
        dkg_ref[...] += dkg
        kvn_ref[...] = (kxn * kg_ref[...]).astype(BF16)

    return pl.pallas_call(
        body, name="attn_prep_bwd", grid=(s // t,),
        in_specs=[pl.BlockSpec((t, W_C1), lambda i: (i, c0)),
                  pl.BlockSpec((HEADS, t, HEAD_PAD), lambda i: (0, i, 0)),
                  pl.BlockSpec((HEADS, t, HEAD_PAD), lambda i: (0, i, 0)), _row_spec(t, 512),
                  qg_spec, kg_spec, _full_spec((384, 2048)), _full_spec((256, 1536)),
                  _row_spec(t, HEAD_PAD), _row_spec(t, HEAD_PAD), pl.BlockSpec(memory_space=pl.ANY)],
        out_specs=[pl.BlockSpec((t, W_C1), lambda i: (i, c0)), _row_spec(t, 384), _row_spec(t, 256),
                   _row_spec(t, 2048), _row_spec(t, 1536), _full_spec((1, 384)), _full_spec((1, 256))],
        out_shape=[jax.ShapeDtypeStruct(dproj.shape, BF16), jax.ShapeDtypeStruct((s, 384), BF16),
                   jax.ShapeDtypeStruct((s, 256), BF16), jax.ShapeDtypeStruct((s, 2048), BF16),
                   jax.ShapeDtypeStruct((s, 1536), BF16), jax.ShapeDtypeStruct((1, 384), F32),
                   jax.ShapeDtypeStruct((1, 256), F32)],
        input_output_aliases={10: 0},
        compiler_params=_cparams(1),
    )(proj, dqf, dkf, dv, q_g, kv_g, wq12, wkv, tcos, tsin, dproj)


def _causal_keep(t):
    return lax.broadcasted_iota(jnp.int32, (t, t), 0) >= lax.broadcasted_iota(jnp.int32, (t, t), 1)


def _half_select(e):
    lane = lax.broadcasted_iota(jnp.int32, (1, HEAD_PAD), 1)
    return (lane < V_HEAD) if e == 0 else (lane >= V_HEAD)


def _attention_fwd(qf, kf, v, exchange=None):
    _, s, _ = qf.shape
    t = min(ATT_BLOCK, s)
    nq = s // t

    def body(q_ref, k_ref, v_ref, o_ref, lse_ref):
        def rows_of_block(blk):
            past = blk * t
            for pair in range(HEADS // 2):
                lo = pair * HEAD_PAD
                v_diag = v_ref[past:past + t, lo:lo + HEAD_PAD]
                v_past = v_ref[0:past, lo:lo + HEAD_PAD] if blk else None
                out = None
                for e in range(2):
                    h = 2 * pair + e
                    sel = _half_select(e)
                    q = q_ref[h]
                    s_diag = jnp.where(_causal_keep(t), _dot_nt(q, k_ref[h, past:past + t, :]) * ATT_SCALE, NEG_BIG)
                    m = jnp.max(s_diag, axis=1, keepdims=True)
                    if blk:
                        s_past = _dot_nt(q, k_ref[h, 0:past, :]) * ATT_SCALE
                        m = jnp.maximum(m, jnp.max(s_past, axis=1, keepdims=True))
                    p_diag = jnp.exp(s_diag - m)
                    norm = jnp.sum(p_diag, axis=1, keepdims=True)
                    pv = _dot(p_diag.astype(BF16), jnp.where(sel, v_diag, jnp.zeros_like(v_diag)))
                    if blk:
                        p_past = jnp.exp(s_past - m)
                        norm = norm + jnp.sum(p_past, axis=1, keepdims=True)
                        pv = pv + _dot(p_past.astype(BF16), jnp.where(sel, v_past, jnp.zeros_like(v_past)))
                    pv = pv * (1.0 / norm)
                    out = pv if out is None else out + pv
                    lse_ref[h] = m + jnp.log(norm)
                o_ref[:, lo:lo + HEAD_PAD] = out

        for blk in range(nq):
            pl.when(pl.program_id(0) == blk)(functools.partial(rows_of_block, blk))

    return _call(
        body, (qf, kf, v), exchange, name="attention_fwd", grid=(nq,),
        in_specs=[pl.BlockSpec((HEADS, t, HEAD_PAD), lambda i: (0, i, 0)), _full_spec((HEADS, s, HEAD_PAD)),
                  _full_spec((s, 512))],
        out_specs=[_row_spec(t, 512), pl.BlockSpec((HEADS, t, 1), lambda i: (0, i, 0))],
        out_shape=[jax.ShapeDtypeStruct((s, 512), F32), jax.ShapeDtypeStruct((HEADS, s, 1), F32)],
        compiler_params=_cparams(1))


def _attention_bwd(qf, kf, v, o, do, lse, exchange=None):
    _, s, _ = qf.shape
    t = min(ATT_BLOCK, s)
    nq = s // t

    def body(q_ref, k_ref, v_ref, o_ref, do_ref, lse_ref, dq_ref, dk_ref, dv_ref, dk_acc, dv_acc):
        kj = pl.program_id(0)
        qi = pl.program_id(1)

        @pl.when((kj == 0) & (qi == 0))
        def _():
            dq_ref[...] = jnp.zeros_like(dq_ref)

        @pl.when(qi == kj)
        def _():
            dk_acc[...] = jnp.zeros_like(dk_acc)
            dv_acc[...] = jnp.zeros_like(dv_acc)

        def step(diag):
            rows = pl.ds(pl.multiple_of(qi * t, t), t)
            for h in range(HEADS):
                pair, e = divmod(h, 2)
                sel = _half_select(e)
                lo = pair * HEAD_PAD
                q = q_ref[h]
                k = k_ref[h]
                sc = _dot_nt(q, k) * ATT_SCALE
                if diag:
                    sc = jnp.where(_causal_keep(t), sc, NEG_BIG)
                p = jnp.exp(sc - lse_ref[h])
                do_pair = do_ref[:, lo:lo + HEAD_PAD]
                do_e = jnp.where(sel, do_pair, 0.0)
                do_b = do_e.astype(BF16)
                v2 = v_ref[:, lo:lo + HEAD_PAD]
                dv_acc[:, lo:lo + HEAD_PAD] += _dot_tn(p.astype(BF16), do_b)
                dp = _dot_nt(do_b, jnp.where(sel, v2, jnp.zeros_like(v2)))
                delta = jnp.sum(do_e * o_ref[:, lo:lo + HEAD_PAD], axis=1, keepdims=True)
                ds = (p * (dp - delta) * ATT_SCALE).astype(BF16)
                dq_ref[h, rows, :] += _dot(ds, k)
                dk_acc[h] += _dot_tn(ds, q)

        @pl.when(qi > kj)
        def _():
            step(False)

        @pl.when(qi == kj)
        def _():
            step(True)

        @pl.when(qi == nq - 1)
        def _():
            dk_ref[...] = dk_acc[...]
            dv_ref[...] = dv_acc[...]

    qmap = lambda j, i: (0, jnp.maximum(i, j), 0)
    return _call(
        body, (qf, kf, v, o, do, lse), exchange, name="attention_bwd", grid=(nq, nq),
        in_specs=[pl.BlockSpec((HEADS, t, HEAD_PAD), qmap),
                  pl.BlockSpec((HEADS, t, HEAD_PAD), lambda j, i: (0, j, 0)),
                  pl.BlockSpec((t, 512), lambda j, i: (j, 0)),
                  pl.BlockSpec((t, 512), lambda j, i: (jnp.maximum(i, j), 0)),
                  pl.BlockSpec((t, 512), lambda j, i: (jnp.maximum(i, j), 0)),
                  pl.BlockSpec((HEADS, t, 1), qmap)],
        out_specs=[_full_spec((HEADS, s, HEAD_PAD)),
                   pl.BlockSpec((HEADS, t, HEAD_PAD), lambda j, i: (0, j, 0)),
                   pl.BlockSpec((t, 512), lambda j, i: (j, 0))],
        out_shape=[jax.ShapeDtypeStruct((HEADS, s, HEAD_PAD), F32), jax.ShapeDtypeStruct((HEADS, s, HEAD_PAD), F32),
                   jax.ShapeDtypeStruct((s, 512), F32)],
        scratch_shapes=[pltpu.VMEM((HEADS, t, HEAD_PAD), F32), pltpu.VMEM((t, 512), F32)],
        compiler_params=_cparams(2))


def _merge_fwd(act_a, act_b, o, proj, x, gate, w_a, w_b, w_c, w_o, ln_g, ln_b, alpha, exchange=None):
    s, d = x.shape
    t = min(ROW_BLOCK, s)
    (gate, gate_spec), (ln_g, lg_spec), (ln_b, lb_spec) = _layer_param(gate), _layer_param(ln_g), _layer_param(ln_b)

    def body(aa_ref, ab_ref, o_ref, p_ref, x_ref, gate_ref, wa_ref, wb_ref, wc_ref, wo_ref, lg_ref, lb_ref,
             xn_ref, xh_ref, rstd_ref):
        cg = p_ref[:, 0:512]
        act_c = (o_ref[...] * (cg * _sig(cg))).astype(BF16)
        m = _sig(p_ref[:, 512:1536]) * _dot(aa_ref[...], wa_ref[...])
        m = m + _sig(p_ref[:, 1536:2560]) * _dot(ab_ref[...], wb_ref[...])
        m = m + _sig(p_ref[:, 2560:3584]) * _dot(act_c, wc_ref[...])
        out = _dot(m.astype(BF16), wo_ref[...])
        z = alpha * x_ref[...] + gate_ref[...] * out
        xh, rstd, y = _layer_norm_rows(z, lg_ref[...], lb_ref[...])
        xn_ref[...] = y
        xh_ref[...] = xh
        rstd_ref[...] = rstd

    return _call(
        body, (act_a, act_b, o, proj, x, gate, w_a, w_b, w_c, w_o, ln_g, ln_b), exchange, name="merge_fwd", grid=(s // t,),
        in_specs=[_row_spec(t, 512), _row_spec(t, 512), _row_spec(t, 512),
                  pl.BlockSpec((t, W_G5), lambda i: (i, 1)), _row_spec(t, d), gate_spec,
                  _full_spec((512, d)), _full_spec((512, d)), _full_spec((512, d)), _full_spec((d, d)),
                  lg_spec, lb_spec],
        out_specs=[_row_spec(t, d), _row_spec(t, d), _row_spec(t, 1)],
        out_shape=[jax.ShapeDtypeStruct((s, d), F32), jax.ShapeDtypeStruct((s, d), F32), jax.ShapeDtypeStruct((s, 1), F32)],
        compiler_params=_cparams(1))


def _merge_bwd(dy, xh, rstd, act_a, act_b, o, proj, gate, w_a, w_b, w_c, w_o, ln_g, alpha, exchange=None):
    s, d = dy.shape
    t = min(ROW_BLOCK, s)
    (gate, gate_spec), (ln_g, lg_spec) = _layer_param(gate), _layer_param(ln_g)

    def body(dy_ref, xh_ref, rstd_ref, aa_ref, ab_ref, o_ref, p_ref, gate_ref, wa_ref, wb_ref, wc_ref, wo_ref, lg_ref,
             dxr_ref, dacta_ref, dactb_ref, do_ref, dg_ref, m_ref, dout_ref, dya_ref, dyb_ref, dyc_ref, actc_ref,
             dlg_ref, dlb_ref, dgate_ref):
        @pl.when(pl.program_id(0) == 0)
        def _():
            dlg_ref[...] = jnp.zeros_like(dlg_ref)
            dlb_ref[...] = jnp.zeros_like(dlb_ref)
            dgate_ref[...] = jnp.zeros_like(dgate_ref)

        dyv = dy_ref[...]
        xhv = xh_ref[...]
        dlg_ref[...] += jnp.sum(dyv * xhv, axis=0, keepdims=True)
        dlb_ref[...] += jnp.sum(dyv, axis=0, keepdims=True)
        dz = _layer_norm_bwd_rows(dyv * lg_ref[...], xhv, rstd_ref[...])
        dxr_ref[...] = alpha * dz

        cg = p_ref[:, 0:512]
        scg = _sig(cg)
        silu_cg = cg * scg
        ov = o_ref[...]
        act_c = (ov * silu_cg).astype(BF16)
        actc_ref[...] = act_c
        ya = _dot(aa_ref[...], wa_ref[...])
        yb = _dot(ab_ref[...], wb_ref[...])
        yc = _dot(act_c, wc_ref[...])
        ga = _sig(p_ref[:, 512:1536])
        gb = _sig(p_ref[:, 1536:2560])
        gc = _sig(p_ref[:, 2560:3584])
        mb = (ga * ya + gb * yb + gc * yc).astype(BF16)
        m_ref[...] = mb
        out = _dot(mb, wo_ref[...])
        dgate_ref[...] += jnp.sum(dz * out, axis=0, keepdims=True)
        dout = (gate_ref[...] * dz).astype(BF16)
        dout_ref[...] = dout
        dm = _dot_nt(dout, wo_ref[...])

        dya = (dm * ga).astype(BF16)
        dya_ref[...] = dya
        dg_ref[:, 512:1536] = (dm * ya * ga * (1.0 - ga)).astype(BF16)
        dacta_ref[...] = _dot_nt(dya, wa_ref[...])
        dyb = (dm * gb).astype(BF16)
        dyb_ref[...] = dyb
        dg_ref[:, 1536:2560] = (dm * yb * gb * (1.0 - gb)).astype(BF16)
        dactb_ref[...] = _dot_nt(dyb, wb_ref[...])
        dyc = (dm * gc).astype(BF16)
        dyc_ref[...] = dyc
        dg_ref[:, 2560:3584] = (dm * yc * gc * (1.0 - gc)).astype(BF16)
        dactc = _dot_nt(dyc, wc_ref[...])
        do_ref[...] = dactc * silu_cg
        dg_ref[:, 0:512] = (dactc * ov * _dsilu(cg, scg)).astype(BF16)

    return _call(
        body, (dy, xh, rstd, act_a, act_b, o, proj, gate, w_a, w_b, w_c, w_o, ln_g), exchange, name="merge_bwd", grid=(s // t,),
        in_specs=[_row_spec(t, d), _row_spec(t, d), _row_spec(t, 1), _row_spec(t, 512), _row_spec(t, 512),
                  _row_spec(t, 512), pl.BlockSpec((t, W_G5), lambda i: (i, 1)), gate_spec,
                  _full_spec((512, d)), _full_spec((512, d)), _full_spec((512, d)), _full_spec((d, d)),
                  lg_spec],
        out_specs=[_row_spec(t, d), _row_spec(t, 512), _row_spec(t, 512), _row_spec(t, 512),
                   pl.BlockSpec((t, W_G5), lambda i: (i, 1)),
                   _row_spec(t, d), _row_spec(t, d), _row_spec(t, d), _row_spec(t, d), _row_spec(t, d),
                   _row_spec(t, 512), _full_spec((1, d)), _full_spec((1, d)), _full_spec((1, d))],
        out_shape=[jax.ShapeDtypeStruct((s, d), F32), jax.ShapeDtypeStruct((s, 512), F32),
                   jax.ShapeDtypeStruct((s, 512), F32), jax.ShapeDtypeStruct((s, 512), F32),
                   jax.ShapeDtypeStruct((s, W_PACK), BF16),
                   jax.ShapeDtypeStruct((s, d), BF16), jax.ShapeDtypeStruct((s, d), BF16),
                   jax.ShapeDtypeStruct((s, d), BF16), jax.ShapeDtypeStruct((s, d), BF16),
                   jax.ShapeDtypeStruct((s, d), BF16), jax.ShapeDtypeStruct((s, 512), BF16),
                   jax.ShapeDtypeStruct((1, d), F32), jax.ShapeDtypeStruct((1, d), F32),
                   jax.ShapeDtypeStruct((1, d), F32)],
        compiler_params=_cparams(1))


def _proj_bwd_input(dproj, w_pack, x, scale, dxres, exchange=None):
    s, d = x.shape
    t = min(512, s)
    scale, scale_spec = _layer_param(scale)

    def body(dg_ref, w_ref, x_ref, sc_ref, dxr_ref, dx_ref, dsc_ref, dsh_ref):
        @pl.when(pl.program_id(0) == 0)
        def _():
            dsc_ref[...] = jnp.zeros_like(dsc_ref)
            dsh_ref[...] = jnp.zeros_like(dsh_ref)

        du = _dot(dg_ref[...], w_ref[...])
        dx_ref[...] = du * (1.0 + sc_ref[...]) + dxr_ref[...]
        dsc_ref[...] += jnp.sum(du * x_ref[...], axis=0, keepdims=True)
        dsh_ref[...] += jnp.sum(du, axis=0, keepdims=True)

    w_spec = pl.BlockSpec((W_PACK, d), lambda i: (0, 0), pipeline_mode=pl.Buffered(1))
    return _call(
        body, (dproj, w_pack, x, scale, dxres), exchange, name="proj_bwd_input", grid=(s // t,),
        in_specs=[_row_spec(t, W_PACK), w_spec, _row_spec(t, d), scale_spec, _row_spec(t, d)],
        out_specs=[_row_spec(t, d), _full_spec((1, d)), _full_spec((1, d))],
        out_shape=[jax.ShapeDtypeStruct((s, d), F32), jax.ShapeDtypeStruct((1, d), F32), jax.ShapeDtypeStruct((1, d), F32)],
        compiler_params=_cparams(1))


def _loss_head(y, target):
    s, d = y.shape
    t = min(512, s)

    def body(y_ref, t_ref, dy_ref, loss_ref):
        @pl.when(pl.program_id(0) == 0)
        def _():
            loss_ref[...] = jnp.zeros_like(loss_ref)

        err = y_ref[...] - t_ref[...]
        dy_ref[...] = err / d
        part = 0.5 * jnp.sum(jnp.mean(jnp.square(err), axis=-1, keepdims=True), axis=0, keepdims=True)
        loss_ref[...] += jnp.broadcast_to(part, loss_ref.shape)

    return pl.pallas_call(
        body, name="loss_head", grid=(s // t,),
        in_specs=[_row_spec(t, d), _row_spec(t, d)],
        out_specs=[_row_spec(t, d), _full_spec((1, 128))],
        out_shape=[jax.ShapeDtypeStruct((s, d), F32), jax.ShapeDtypeStruct((1, 128), F32)],
        compiler_params=_cparams(1),
    )(y, target)


def _adamw(g, w, m, v, name):
    r, n = g.shape
    block_bytes = 5 * 2 ** 19
    rb = r
    if r * n * 4 > block_bytes:
        rb = max(k for k in range(8, r, 8) if r % k == 0 and k * n * 4 <= block_bytes)

    def body(g_ref, w_ref, m_ref, v_ref, d_ref, mo_ref, vo_ref):
        gv = g_ref[...]
        mn = ADAM_B1 * m_ref[...] + (1.0 - ADAM_B1) * gv
        vn = ADAM_B2 * v_ref[...] + (1.0 - ADAM_B2) * jnp.square(gv)
        m_hat = mn / (1.0 - ADAM_B1 ** ADAM_STEP)
        v_hat = vn / (1.0 - ADAM_B2 ** ADAM_STEP)
        d_ref[...] = -ADAM_LR * (m_hat / (jnp.sqrt(v_hat) + ADAM_EPS) + ADAM_WD * w_ref[...])
        mo_ref[...] = mn
        vo_ref[...] = vn

    spec = pl.BlockSpec((rb, n), lambda i: (i, 0))
    return pl.pallas_call(
        body, name=name, grid=(r // rb,),
        in_specs=[spec] * 4, out_specs=[spec] * 3,
        out_shape=[jax.ShapeDtypeStruct((r, n), F32)] * 3, compiler_params=_cparams(1),
    )(g, w, m, v)


SMALL = ("b_ada", "conv_a_b", "ln_a_g", "ln_a_b", "q_norm_g", "kv_norm_g", "ln_g", "ln_b", "conv_a_w", "conv_b_w")
ROWS_256 = (("w_a_out", 512), ("w_b_out", 512), ("w_c_out", 512), ("w_ukv", 256))


def _rot_cols(w):
    return jnp.concatenate([-w[..., 16:], w[..., :16]], axis=-1)


def _unrot_cols(g):
    return jnp.concatenate([g[..., 16:], -g[..., :16]], axis=-1)


def _pack_w_in(wt):
    z = lambda n: jnp.zeros((n, wt.shape[1]), wt.dtype)
    wk = wt[4224:4256]
    rot = jnp.concatenate([-wk[16:], wk[:16]], axis=0)
    return jnp.concatenate([wt[0:3584], wt[4256:7840], wt[3584:4224], z(64), wk, z(32), z(64), rot, z(32)], axis=0)


def _unpack_w_in_grad(g):
    c1 = W_AB + W_G5
    g_rot = g[c1 + 768 + 64:c1 + 768 + 96].astype(F32)
    gk = g[c1 + 640 + 64:c1 + 640 + 96].astype(F32) + jnp.concatenate([g_rot[16:], -g_rot[:16]], axis=0)
    return jnp.concatenate([g[0:3584], g[c1:c1 + 640], gk.astype(g.dtype), g[W_AB:W_AB + W_G5]], axis=0)


def _pack_w_uq(w):
    lead = w.shape[:-1]
    wh = w.reshape(lead + (HEADS, QK_NOPE + QK_ROPE))
    nope, rope = wh[..., :QK_NOPE], wh[..., QK_NOPE:]
    z32 = jnp.zeros_like(rope)
    q1 = jnp.concatenate([nope, rope, z32], axis=-1).reshape(lead + (HEADS * HEAD_PAD,))
    q2 = jnp.concatenate([jnp.zeros_like(nope), _rot_cols(rope), z32], axis=-1).reshape(lead + (HEADS * HEAD_PAD,))
    return jnp.concatenate([q1, q2], axis=-1)


def _unpack_w_uq_grad(g):
    lead = g.shape[:-1]
    g1 = g[..., :1024].reshape(lead + (HEADS, HEAD_PAD))
    g2 = g[..., 1024:].reshape(lead + (HEADS, HEAD_PAD))
    rope = g1[..., QK_NOPE:QK_NOPE + QK_ROPE] + _unrot_cols(g2[..., QK_NOPE:QK_NOPE + QK_ROPE])
    return jnp.concatenate([g1[..., :QK_NOPE], rope], axis=-1).reshape(lead + (HEADS * (QK_NOPE + QK_ROPE),))


def _pack_w_ukv(w):
    lead = w.shape[:-1]
    wh = w.reshape(lead + (HEADS, QK_NOPE + V_HEAD))
    kn, vv = wh[..., :QK_NOPE], wh[..., QK_NOPE:]
    k1 = jnp.concatenate([kn, jnp.zeros_like(kn)], axis=-1).reshape(lead + (HEADS * HEAD_PAD,))
    return jnp.concatenate([k1, vv.reshape(lead + (HEADS * V_HEAD,))], axis=-1)


def _unpack_w_ukv_grad(g):
    lead = g.shape[:-1]
    gk = g[..., :1024].reshape(lead + (HEADS, HEAD_PAD))[..., :QK_NOPE]
    gv = g[..., 1024:].reshape(lead + (HEADS, V_HEAD))
    return jnp.concatenate([gk, gv], axis=-1).reshape(lead + (HEADS * (QK_NOPE + V_HEAD),))


def _join_cols(g4):
    return jnp.concatenate([g4[j] for j in range(4)], axis=-1)


def _split_cols(w):
    c4 = w.shape[-1] // 4
    return jnp.stack([w[:, j * c4:(j + 1) * c4] for j in range(4)])


def kernel(x, c, positions, w_ada, b_ada, w_in, conv_a_w, conv_a_b, ln_a_g, ln_a_b, w_a_out, conv_b_w, w_b_out, q_norm_g, kv_norm_g, w_uq, w_ukv, w_c_out, w_o, ln_g, ln_b, loss_target, m_w_ada, m_b_ada, m_w_in, m_conv_a_w, m_conv_a_b, m_ln_a_g, m_ln_a_b, m_w_a_out, m_conv_b_w, m_w_b_out, m_q_norm_g, m_kv_norm_g, m_w_uq, m_w_ukv, m_w_c_out, m_w_o, m_ln_g, m_ln_b, v_w_ada, v_b_ada, v_w_in, v_conv_a_w, v_conv_a_b, v_ln_a_g, v_ln_a_b, v_w_a_out, v_conv_b_w, v_w_b_out, v_q_norm_g, v_kv_norm_g, v_w_uq, v_w_ukv, v_w_c_out, v_w_o, v_ln_g, v_ln_b):
    weights = dict(w_ada=w_ada, b_ada=b_ada, w_in=w_in, conv_a_w=conv_a_w, conv_a_b=conv_a_b, ln_a_g=ln_a_g, ln_a_b=ln_a_b,
                   w_a_out=w_a_out, conv_b_w=conv_b_w, w_b_out=w_b_out, q_norm_g=q_norm_g, kv_norm_g=kv_norm_g, w_uq=w_uq,
                   w_ukv=w_ukv, w_c_out=w_c_out, w_o=w_o, ln_g=ln_g, ln_b=ln_b)
    mom_m = dict(w_ada=m_w_ada, b_ada=m_b_ada, w_in=m_w_in, conv_a_w=m_conv_a_w, conv_a_b=m_conv_a_b, ln_a_g=m_ln_a_g,
                 ln_a_b=m_ln_a_b, w_a_out=m_w_a_out, conv_b_w=m_conv_b_w, w_b_out=m_w_b_out, q_norm_g=m_q_norm_g,
                 kv_norm_g=m_kv_norm_g, w_uq=m_w_uq, w_ukv=m_w_ukv, w_c_out=m_w_c_out, w_o=m_w_o, ln_g=m_ln_g, ln_b=m_ln_b)
    mom_v = dict(w_ada=v_w_ada, b_ada=v_b_ada, w_in=v_w_in, conv_a_w=v_conv_a_w, conv_a_b=v_conv_a_b, ln_a_g=v_ln_a_g,
                 ln_a_b=v_ln_a_b, w_a_out=v_w_a_out, conv_b_w=v_conv_b_w, w_b_out=v_w_b_out, q_norm_g=v_q_norm_g,
                 kv_norm_g=v_kv_norm_g, w_uq=v_w_uq, w_ukv=v_w_ukv, w_c_out=v_w_c_out, w_o=v_w_o, ln_g=v_ln_g, ln_b=v_ln_b)
    order = ["w_ada", "b_ada", "w_in", "conv_a_w", "conv_a_b", "ln_a_g", "ln_a_b", "w_a_out", "conv_b_w", "w_b_out",
             "q_norm_g", "kv_norm_g", "w_uq", "w_ukv", "w_c_out", "w_o", "ln_g", "ln_b"]

    n_layers = w_in.shape[0]
    s, d = x.shape[1], x.shape[2]
    alpha = float((2 * n_layers) ** 0.25)
    ix, iy, ic = lax.axis_index("x"), lax.axis_index("y"), lax.axis_index("c")
    me = 4 * ix + 2 * iy + ic
    chip = 2 * ix + iy
    xs = x[0]

    def four(tree, cast):
        return [cast(jnp.swapaxes(tree["w_in"], 1, 2)), jnp.concatenate([cast(tree[n]) for n, _ in ROWS_256], axis=-2),
                cast(tree["w_uq"]), cast(tree["w_o"])]

    shards = four(weights, lambda a: a.astype(BF16))
    landed_w = _run_exchange(_gather_layer_exchange(shards, 0), "gather_weights_first")
    landed_w = _run_exchange(_gather_pair_exchange(landed_w), "gather_weights_first_pair")
    w_pack, wq12, wkv, w_a, w_b, w_c, w_of = [], [], [], [], [], [], []

    def unpack_layer(g_in, g_256, g_uq, g_wo):
        w_pack.append(_pack_w_in(g_in.reshape(-1, d)))
        wq12.append(_pack_w_uq(_join_cols(g_uq)))
        rows = _join_cols(g_256)
        w_a.append(rows[0:512])
        w_b.append(rows[512:1024])
        w_c.append(rows[1024:1536])
        wkv.append(_pack_w_ukv(rows[1536:1792]))
        w_of.append(g_wo.reshape(d, d))

    cw_a = conv_a_w.reshape(n_layers * CONV_K, 128)
    cw_b = conv_b_w.reshape(n_layers * SC_K, 128)
    c_rows = d // 128
    first_rows = c_rows + n_layers * (CONV_K + SC_K)
    first_pad = (-first_rows) % 8
    first = jnp.concatenate([c.reshape(c_rows, 128), cw_a, cw_b, jnp.zeros((first_pad, 128), F32)], axis=0)
    fr = first_rows + first_pad
    got = _all_gather_small(first, "gather_cond").reshape(8, fr, 128)
    c_all = got[:, :c_rows].reshape(8, d)
    conv_a_full = jnp.moveaxis(got[0::2, c_rows:c_rows + n_layers * CONV_K].reshape(4, n_layers, CONV_K, 128), 0, 2)
    conv_a_full = conv_a_full.reshape(n_layers, CONV_K, 512)
    b0 = c_rows + n_layers * CONV_K
    conv_b_full = jnp.moveaxis(got[0::2, b0:b0 + n_layers * SC_K].reshape(4, n_layers, SC_K, 128), 0, 2)
    conv_b_full = conv_b_full.reshape(n_layers, SC_K, 512)

    ada_sh, c_act = _ada_shard(c_all, w_ada.astype(BF16))
    ada_all = _all_gather_small(ada_sh.reshape(n_layers * 8, 768), "gather_ada").reshape(8, n_layers, 8, 768)
    ada_mine = lax.dynamic_index_in_dim(ada_all[0::2], me, axis=2, keepdims=False)
    ada = jnp.moveaxis(ada_mine, 0, 1).reshape(n_layers, 3 * d) + b_ada
    def stacked(a):
        return a.reshape(n_layers, 1, a.shape[-1])

    shift, scale, gate = stacked(ada[:, 0:d]), stacked(ada[:, d:2 * d]), stacked(ada[:, 2 * d:3 * d])
    conv_a_b3, ln_a_g3, ln_a_b3 = stacked(conv_a_b), stacked(ln_a_g), stacked(ln_a_b)
    q_norm_g3, kv_norm_g3, ln_g3, ln_b3 = stacked(q_norm_g), stacked(kv_norm_g), stacked(ln_g), stacked(ln_b)

    tcos, tsin = _rope_tables(positions.reshape(s, 1))

    saved = []
    h = xs
    for l in range(n_layers):
        unpack_layer(*landed_w)
        u = _modulate(h, (scale, l), (shift, l))
        more = l + 1 < n_layers
        (proj,), lw_in = _matmul_nt(u, w_pack[l], COL_BLK, "proj_fwd",
                                    _gather_layer_exchange(shards[:1], l + 1, (0, 2)) if more else None)
        (act_a, act_b, ca, cb), lw_small = _mix_ab_fwd(
            proj, (conv_a_full, l), (conv_a_b3, l), (ln_a_g3, l), (ln_a_b3, l), (conv_b_full, l),
            _gather_layer_exchange(shards[2:], l + 1) if more else None)
        qf, kf, vv = _attn_prep_fwd(proj, (q_norm_g3, l), (kv_norm_g3, l), wq12[l], wkv[l], tcos, tsin)
        (o, lse), lw_big = _attention_fwd(
            qf, kf, vv, _join_exchanges(_gather_layer_exchange(shards[:1], l + 1, (1, 2), lw_in),
                                        _gather_layer_exchange(shards[1:2], l + 1)) if more else None)
        (h_next, xh, rstd), landed_w = _merge_fwd(
            act_a, act_b, o, proj, h, (gate, l), w_a[l], w_b[l], w_c[l], w_of[l], (ln_g3, l), (ln_b3, l), alpha,
            _gather_pair_exchange(lw_big + lw_small) if more else None)
        saved.append(dict(x=h, u=u, proj=proj, act_a=act_a, act_b=act_b, ca=ca, cb=cb, qf=qf, kf=kf, v=vv, o=o, lse=lse,
                          xh=xh, rstd=rstd))
        h = h_next

    dy, loss_part = _loss_head(h, loss_target[0])
    loss = lax.psum(loss_part[0, 0], ("x", "y", "c"))

    c_idx = ic.reshape(1).astype(jnp.int32)
    reduced = None
    pending = None
    small_rows = [None] * n_layers
    for l in reversed(range(n_layers)):
        sv = saved[l]
        ((dxres, dact_a, dact_b, do, dproj, m_bf, dout_bf, dya, dyb, dyc, act_c, d_ln_g, d_ln_b, d_gate),
         rcv_rest) = _merge_bwd(
            dy, sv["xh"], sv["rstd"], sv["act_a"], sv["act_b"], sv["o"], sv["proj"], (gate, l), w_a[l], w_b[l],
            w_c[l], w_of[l], (ln_g3, l), alpha, _scatter_exchange(pending[1:]) if pending else None)
        g_o = _matmul_tn(m_bf, dout_bf, d, "grad_w_o")
        g_a = _matmul_tn(sv["act_a"], dya, d, "grad_w_a_out")
        g_b = _matmul_tn(sv["act_b"], dyb, d, "grad_w_b_out")
        g_c = _matmul_tn(act_c, dyc, d, "grad_w_c_out")
        (dqf, dkf, dvv), rcv_in = _attention_bwd(sv["qf"], sv["kf"], sv["v"], sv["o"], do, sv["lse"],
                                                 _scatter_exchange(pending[:1]) if pending else None)
        if pending:
            reduced = _sum_chips(rcv_in + rcv_rest, shapes, reduced, l + 1, n_layers, c_idx)
        dproj, qn, kvn, dq12, dkv12, d_qg, d_kg = _attn_prep_bwd(
            sv["proj"], dqf, dkf, dvv, (q_norm_g3, l), (kv_norm_g3, l), wq12[l], wkv[l], tcos, tsin, dproj)
        g_uq = _unpack_w_uq_grad(_matmul_tn(qn, dq12, 1024, "grad_w_uq"))
        g_kv = _unpack_w_ukv_grad(_matmul_tn(kvn, dkv12, 768, "grad_w_ukv"))
        dproj, d_caw, d_cab, d_lag, d_lab, d_cbw = _mix_ab_bwd(
            sv["proj"], sv["ca"], sv["cb"], dact_a, dact_b, (conv_a_full, l), (ln_a_g3, l), (ln_a_b3, l),
            (conv_b_full, l), dproj)
        g_in_l = _unpack_w_in_grad(_matmul_tn(dproj, sv["u"], d, "grad_w_in", bm=COL_BLK, whole_sum=True))
        partial = [g_in_l.reshape(4, -1, d),
                   jnp.concatenate([_split_cols(g) for g in (g_a, g_b, g_c, g_kv)], axis=1).astype(BF16),
                   _split_cols(g_uq).astype(BF16), g_o.reshape(4, d // 4, d).astype(BF16)]
        shapes = [p.shape[1:] for p in partial]
        (dy, d_scale, d_shift), from_sibling = _proj_bwd_input(dproj, w_pack[l], sv["x"], (scale, l), dxres,
                                                               _swap_exchange(partial))
        small_rows[l] = jnp.concatenate(
            [d_shift[0], d_scale[0], d_gate[0], d_cab[0], d_lag[0], d_lab[0], d_qg[0], d_kg[0], d_ln_g[0], d_ln_b[0],
             d_caw[:CONV_K].reshape(-1), d_cbw[:SC_K].reshape(-1)])
        pending = _add_halves(partial, from_sibling, c_idx)
    grad_x = dy[None]
    received = _run_exchange(_scatter_exchange(pending), "grad_scatter_last")
    reduced = _sum_chips(received, shapes, reduced, 0, n_layers, c_idx)
    r_in, r_256, r_uq, r_wo = _run_exchange(_gather_pair_exchange(reduced), "grad_pair_halves")
    grads = {"w_in": jnp.swapaxes(r_in, 1, 2), "w_uq": r_uq, "w_o": r_wo}
    row0 = 0
    for n, rows in ROWS_256:
        grads[n] = r_256[:, row0:row0 + rows]
        row0 += rows

    small = jnp.stack(small_rows)
    p_small = small.shape[1]
    n_small = n_layers * p_small
    pad_small = (-n_small) % (8 * 128)
    small_flat = jnp.concatenate([small.reshape(-1), jnp.zeros((pad_small,), F32)]).reshape(-1, 128)
    sr = small_flat.shape[0]
    small_all = _all_gather_small(small_flat, "gather_small_grads").reshape(8, sr, 128)
    small_sum = _sum_devices(small_all).reshape(-1)[:n_small].reshape(n_layers, p_small)
    d_ada_all = small_all.reshape(8, -1)[:, :n_small].reshape(8, n_layers, p_small)[:, :, :3 * d]

    cuts = np.cumsum([0, 3 * d, 512, 512, 512, 384, 256, d, d, CONV_K * 512, SC_K * 512])
    pieces = [small_sum[:, cuts[k]:cuts[k + 1]] for k in range(10)]
    grads.update(zip(SMALL, pieces[:8]))
    ga_full = pieces[8].reshape(n_layers, CONV_K, 512)
    gb_full = pieces[9].reshape(n_layers, SC_K, 512)
    grads["conv_a_w"] = lax.dynamic_slice_in_dim(ga_full, chip * 128, 128, axis=2)
    grads["conv_b_w"] = lax.dynamic_slice_in_dim(gb_full, chip * 128, 128, axis=2)

    dada_sh = lax.dynamic_slice_in_dim(d_ada_all, chip * 768, 768, axis=2)
    dada16 = jnp.concatenate([jnp.moveaxis(dada_sh, 0, 1), jnp.zeros((n_layers, 8, 768), F32)], axis=1).astype(BF16)
    cact16 = jnp.concatenate([c_act, jnp.zeros_like(c_act)], axis=0).astype(BF16)
    grads["w_ada"] = _ada_weight_grad(cact16, dada16)

    def as2d(a):
        return a.reshape(-1, a.shape[-1])

    deltas, new_m, new_v = {}, {}, {}
    for n in ("w_in", "w_a_out", "w_b_out", "w_c_out", "w_uq", "w_ukv", "w_o", "w_ada"):
        view = (lambda a: jnp.swapaxes(a, 1, 2)) if n == "w_in" else (lambda a: a)
        shape = view(weights[n]).shape
        dl, mn, vn = _adamw(as2d(view(grads[n])), as2d(view(weights[n])), as2d(view(mom_m[n])), as2d(view(mom_v[n])),
                            "adamw_" + n)
        deltas[n], new_m[n], new_v[n] = view(dl.reshape(shape)), view(mn.reshape(shape)), view(vn.reshape(shape))

    def pack_small(tree):
        flat = jnp.concatenate([tree[n].reshape(-1) for n in SMALL])
        pad = (-flat.shape[0]) % (8 * 128)
        return jnp.concatenate([flat, jnp.zeros((pad,), F32)]).reshape(-1, 128)

    dl_s, mn_s, vn_s = _adamw(pack_small(grads), pack_small(weights), pack_small(mom_m), pack_small(mom_v), "adamw_small")
    off = 0
    for n in SMALL:
        sz = int(np.prod(weights[n].shape))
        for tree, flat in ((deltas, dl_s), (new_m, mn_s), (new_v, vn_s)):
            tree[n] = flat.reshape(-1)[off:off + sz].reshape(weights[n].shape)
        off += sz

    return (loss, grad_x, *[grads[n] for n in order], *[deltas[n] for n in order], *[new_m[n] for n in order],
            *[new_v[n] for n in order])
```

```python
import functools

import numpy as np
import jax
import jax.numpy as jnp
from jax import lax
from jax.experimental import pallas as pl
from jax.experimental.pallas import tpu as pltpu

F32 = jnp.float32
BF16 = jnp.bfloat16
MESH = pl.DeviceIdType.MESH

HEADS = 8
QK_NOPE = 64
QK_ROPE = 32
V_HEAD = 64
CONV_K = 31
SC_K = 3
LN_EPS = 1e-5
RMS_EPS = 1e-6
ROPE_THETA = 10000.0
ATT_SCALE = (QK_NOPE + QK_ROPE) ** -0.5
ADAM_LR = 0.001
ADAM_B1 = 0.9
ADAM_B2 = 0.999
ADAM_EPS = 1e-08
ADAM_WD = 0.01
ADAM_STEP = 10

W_AB = 3584
W_G5 = 3584
W_C1 = 896
W_PACK = W_AB + W_G5 + W_C1
COL_BLK = 896
HEAD_PAD = 128
CONV_HALO = 32
NEG_BIG = -1e30

VMEM_LIMIT_MB = 56
ATT_BLOCK = 512
ROW_BLOCK = 256


def _cparams(n_grid, mb=VMEM_LIMIT_MB):
    return pltpu.CompilerParams(dimension_semantics=("arbitrary",) * n_grid, vmem_limit_bytes=mb * 2 ** 20)


def _sig(x):
    return 1.0 / (1.0 + jnp.exp(-x))


def _dsilu(x, s):
    return s * (1.0 + x * (1.0 - s))


def _dot(a, b):
    return jnp.dot(a, b, preferred_element_type=F32)


def _dot_nt(a, b):
    return lax.dot_general(a, b, (((1,), (1,)), ((), ())), preferred_element_type=F32)


def _dot_tn(a, b):
    return lax.dot_general(a, b, (((0,), (0,)), ((), ())), preferred_element_type=F32)


def _row_spec(t, w):
    return pl.BlockSpec((t, w), lambda i: (i, 0))


def _full_spec(shape):
    nd = len(shape)
    return pl.BlockSpec(shape, lambda *_: (0,) * nd)


def _layer_param(p):
    arr, layer = p
    return arr, pl.BlockSpec((None,) + arr.shape[1:], lambda *_: (layer, 0, 0))


def _my_place():
    x, y, c = lax.axis_index("x"), lax.axis_index("y"), lax.axis_index("c")
    chips = [(1 - x, y), (x, 1 - y), (1 - x, 1 - y)]
    return x, y, c, chips


def _all_gather_small(v, name):
    m_per, n = v.shape

    def body(x_ref, out_ref, send_sems, recv_sems, local_sem):
        x, y, c, chips = _my_place()
        me, sibling = (x, y, c), (x, y, 1 - c)

        def rows(px, py, pc):
            return out_ref.at[pl.ds((4 * px + 2 * py + pc) * m_per, m_per), :]

        def copy(k, block, to, src=None):
            return pltpu.make_async_remote_copy(
                src_ref=rows(*block) if src is None else src, dst_ref=rows(*block),
                send_sem=send_sems.at[k], recv_sem=recv_sems.at[k], device_id=to, device_id_type=MESH)

        mine = pltpu.make_async_copy(x_ref, rows(*me), local_sem)
        mine.start()
        first = [copy(0, me, sibling, src=x_ref)]
        first += [copy(1 + j, me, (*chip, c), src=x_ref) for j, chip in enumerate(chips)]
        for cp in first:
            cp.start()
        passed = [copy(4 + j, (*chip, c), sibling) for j, chip in enumerate(chips)]
        for j, chip in enumerate(chips):
            copy(1 + j, (*chip, c), me).wait_recv()
            passed[j].start()
        copy(0, sibling, me).wait_recv()
        for j, chip in enumerate(chips):
            copy(4 + j, (*chip, 1 - c), me).wait_recv()
        for cp in first + passed:
            cp.wait_send()
        mine.wait()

    return pl.pallas_call(
        body, name=name,
        out_shape=jax.ShapeDtypeStruct((8 * m_per, n), v.dtype),
        in_specs=[pl.BlockSpec(memory_space=pltpu.VMEM)],
        out_specs=pl.BlockSpec(memory_space=pltpu.VMEM),
        scratch_shapes=[pltpu.SemaphoreType.DMA((7,)), pltpu.SemaphoreType.DMA((7,)), pltpu.SemaphoreType.DMA],
        compiler_params=pltpu.CompilerParams(vmem_limit_bytes=VMEM_LIMIT_MB * 2 ** 20),
    )(v)


def _any_specs(n):
    return [pl.BlockSpec(memory_space=pl.ANY)] * n


def _splits_rows(shape):
    return shape[-2] % 32 == 0


def _window(shape, c, p=0, n_parts=1):
    r, n = shape[-2], shape[-1]
    if _splits_rows(shape):
        size = r // 2 // n_parts
        return (pl.ds(c * (r // 2) + p * size, size), slice(None))
    size = n // 2 // n_parts
    return (slice(None), pl.ds(c * (n // 2) + p * size, size))


def _half_shape(shape):
    r, n = shape[-2], shape[-1]
    return (r // 2, n) if _splits_rows(shape) else (r, n // 2)


def _half_block_index(shape, c):
    return (c, 0) if _splits_rows(shape) else (0, c)


class _Exchange:
    def __init__(self, inputs, out_shapes, sems, start, finish, aliases=None):
        self.inputs, self.out_shapes, self.sems = list(inputs), list(out_shapes), list(sems)
        self.start, self.finish, self.aliases = start, finish, dict(aliases or {})


def _call(body, args, exchange=None, *, grid, in_specs, out_specs, out_shape, scratch_shapes=(),
          input_output_aliases=None, **kw):
    aliases = dict(input_output_aliases or {})
    if exchange is None:
        outs = pl.pallas_call(body, grid=grid, in_specs=list(in_specs), out_specs=list(out_specs),
                              out_shape=list(out_shape), scratch_shapes=list(scratch_shapes),
                              input_output_aliases=aliases, **kw)(*args)
        return list(outs), []
    n_in, n_out, n_scr = len(in_specs), len(out_specs), len(scratch_shapes)
    x_in, x_out = len(exchange.inputs), len(exchange.out_shapes)

    def wrapped(*refs):
        ins, xin = refs[:n_in], refs[n_in:n_in + x_in]
        o0 = n_in + x_in
        outs, xout = refs[o0:o0 + n_out], refs[o0 + n_out:o0 + n_out + x_out]
        s0 = o0 + n_out + x_out
        scr, xsem = refs[s0:s0 + n_scr], refs[s0 + n_scr:]
        ids = [pl.program_id(k) for k in range(len(grid))]
        first = functools.reduce(jnp.logical_and, [i == 0 for i in ids])
        last = functools.reduce(jnp.logical_and, [i == g - 1 for i, g in zip(ids, grid)])

        @pl.when(first)
        def _():
            exchange.start(xin, xout, xsem)

        body(*ins, *outs, *scr)

        @pl.when(last)
        def _():
            exchange.finish(xin, xout, xsem)

    aliases.update({n_in + i: n_out + o for i, o in exchange.aliases.items()})
    outs = pl.pallas_call(
        wrapped, grid=grid, in_specs=list(in_specs) + _any_specs(x_in), out_specs=list(out_specs) + _any_specs(x_out),
        out_shape=list(out_shape) + exchange.out_shapes, scratch_shapes=list(scratch_shapes) + exchange.sems,
        input_output_aliases=aliases, **kw)(*args, *exchange.inputs)
    return list(outs[:n_out]), list(outs[n_out:])


def _run_exchange(exchange, name):
    x_in, x_out = len(exchange.inputs), len(exchange.out_shapes)

    def body(*refs):
        xin, xout, xsem = refs[:x_in], refs[x_in:x_in + x_out], refs[x_in + x_out:]
        exchange.start(xin, xout, xsem)
        exchange.finish(xin, xout, xsem)

    outs = pl.pallas_call(
        body, name=name, in_specs=_any_specs(x_in), out_specs=_any_specs(x_out), out_shape=exchange.out_shapes,
        scratch_shapes=exchange.sems, input_output_aliases=exchange.aliases)(*exchange.inputs)
    return list(outs)


def _gather_layer_exchange(ws, layer, part=(0, 1), lands=None):
    n = len(ws)
    p, n_parts = part

    def copies(xin, xout, sems):
        send_sems, recv_sems, local_sems = sems
        x, y, c, chips = _my_place()
        k_me = 2 * x + y
        local, sends, recvs = [], [], []
        for a in range(n):
            win = _window(ws[a].shape, c, p, n_parts)
            src = xin[a].at[(layer,) + win]
            dst = xout[a].at[(k_me,) + win]
            local.append(pltpu.make_async_copy(src, dst, local_sems.at[a]))
            for j, (px, py) in enumerate(chips):
                sends.append(pltpu.make_async_remote_copy(
                    src_ref=src, dst_ref=dst, send_sem=send_sems.at[a, j], recv_sem=recv_sems.at[a, j],
                    device_id=(px, py, c), device_id_type=MESH))
                recvs.append(pltpu.make_async_remote_copy(
                    src_ref=src, dst_ref=xout[a].at[(2 * px + py,) + win], send_sem=send_sems.at[a, j],
                    recv_sem=recv_sems.at[a, j], device_id=(px, py, c), device_id_type=MESH))
        return local, sends, recvs

    def start(xin, xout, sems):
        local, sends, _ = copies(xin, xout, sems)
        for cp in local + sends:
            cp.start()

    def finish(xin, xout, sems):
        local, sends, recvs = copies(xin, xout, sems)
        for cp in recvs:
            cp.wait_recv()
        for cp in sends:
            cp.wait_send()
        for cp in local:
            cp.wait()

    return _Exchange(list(ws) + list(lands or []), [jax.ShapeDtypeStruct((4,) + w.shape[1:], w.dtype) for w in ws],
                     [pltpu.SemaphoreType.DMA((n, 3)), pltpu.SemaphoreType.DMA((n, 3)), pltpu.SemaphoreType.DMA((n,))],
                     start, finish, aliases={n + a: a for a in range(n)} if lands else None)


def _gather_pair_exchange(lands):
    n = len(lands)

    def copies(xout, sems):
        send_sems, recv_sems = sems
        x, y, c, _ = _my_place()
        def half(a, core):
            return xout[a].at[(slice(None),) + _window(lands[a].shape, core)]

        sends = [pltpu.make_async_remote_copy(
            src_ref=half(a, c), dst_ref=half(a, c), send_sem=send_sems.at[a],
            recv_sem=recv_sems.at[a], device_id=(x, y, 1 - c), device_id_type=MESH) for a in range(n)]
        recvs = [pltpu.make_async_remote_copy(
            src_ref=half(a, c), dst_ref=half(a, 1 - c), send_sem=send_sems.at[a],
            recv_sem=recv_sems.at[a], device_id=(x, y, 1 - c), device_id_type=MESH) for a in range(n)]
        return sends, recvs

    def start(xin, xout, sems):
        for cp in copies(xout, sems)[0]:
            cp.start()

    def finish(xin, xout, sems):
        sends, recvs = copies(xout, sems)
        for cp in recvs:
            cp.wait_recv()
        for cp in sends:
            cp.wait_send()

    return _Exchange(lands, [jax.ShapeDtypeStruct(w.shape, w.dtype) for w in lands],
                     [pltpu.SemaphoreType.DMA((n,)), pltpu.SemaphoreType.DMA((n,))], start, finish,
                     aliases={a: a for a in range(n)})


def _swap_exchange(ps):
    n = len(ps)

    def copies(xin, xout, sems):
        send_sems, recv_sems = sems
        x, y, c, _ = _my_place()
        return [pltpu.make_async_remote_copy(
            src_ref=xin[a].at[(slice(None),) + _window(ps[a].shape, 1 - c)], dst_ref=xout[a],
            send_sem=send_sems.at[a], recv_sem=recv_sems.at[a], device_id=(x, y, 1 - c), device_id_type=MESH)
            for a in range(n)]

    def start(xin, xout, sems):
        for cp in copies(xin, xout, sems):
            cp.start()

    def finish(xin, xout, sems):
        for cp in copies(xin, xout, sems):
            cp.wait()

    return _Exchange(ps, [jax.ShapeDtypeStruct((4,) + _half_shape(p.shape), p.dtype) for p in ps],
                     [pltpu.SemaphoreType.DMA((n,)), pltpu.SemaphoreType.DMA((n,))], start, finish)


def _add_halves(ps, lands, c_idx):
    n = len(ps)

    def body(c_ref, *refs):
        del c_ref
        for a in range(n):
            refs[2 * n + a][...] = (refs[a][...].astype(F32) + refs[n + a][...].astype(F32)).astype(BF16)

    def half_block(p):
        return (None,) + _half_shape(p.shape)

    def own_half(p):
        return lambda j, cr: (j,) + _half_block_index(p.shape, cr[0])

    grid_spec = pltpu.PrefetchScalarGridSpec(
        num_scalar_prefetch=1, grid=(4,),
        in_specs=[pl.BlockSpec(half_block(p), own_half(p)) for p in ps]
        + [pl.BlockSpec(half_block(p), lambda j, cr: (j, 0, 0)) for p in ps],
        out_specs=[pl.BlockSpec(half_block(p), lambda j, cr: (j, 0, 0)) for p in ps])
    return pl.pallas_call(
        body, name="grad_add_halves", grid_spec=grid_spec,
        out_shape=[jax.ShapeDtypeStruct((4,) + _half_shape(p.shape), BF16) for p in ps],
        compiler_params=_cparams(1),
    )(c_idx, *ps, *lands)


def _scatter_exchange(hs):
    n = len(hs)

    def copies(xin, xout, sems):
        send_sems, recv_sems, local_sems = sems
        x, y, c, chips = _my_place()
        k_me = 2 * x + y
        local = [pltpu.make_async_copy(xin[a].at[k_me], xout[a].at[k_me], local_sems.at[a]) for a in range(n)]
        sends, recvs = [], []
        for j, (px, py) in enumerate(chips):
            for a in range(n):
                sends.append(pltpu.make_async_remote_copy(
                    src_ref=xin[a].at[2 * px + py], dst_ref=xout[a].at[k_me], send_sem=send_sems.at[a, j],
                    recv_sem=recv_sems.at[a, j], device_id=(px, py, c), device_id_type=MESH))
                recvs.append(pltpu.make_async_remote_copy(
                    src_ref=xin[a].at[k_me], dst_ref=xout[a].at[2 * px + py], send_sem=send_sems.at[a, j],
                    recv_sem=recv_sems.at[a, j], device_id=(px, py, c), device_id_type=MESH))
        return local, sends, recvs

    def start(xin, xout, sems):
        local, sends, _ = copies(xin, xout, sems)
        for cp in local + sends:
            cp.start()

    def finish(xin, xout, sems):
        local, sends, recvs = copies(xin, xout, sems)
        for cp in recvs:
            cp.wait_recv()
        for cp in sends:
            cp.wait_send()
        for cp in local:
            cp.wait()

    return _Exchange(hs, [jax.ShapeDtypeStruct(h.shape, h.dtype) for h in hs],
                     [pltpu.SemaphoreType.DMA((n, 3)), pltpu.SemaphoreType.DMA((n, 3)), pltpu.SemaphoreType.DMA((n,))],
                     start, finish)


def _sum_chips(rcvs, shapes, accs, layer, n_layers, c_idx):
    n = len(rcvs)

    def body(c_ref, *refs):
        del c_ref
        outs = refs[len(refs) - n:]
        for a in range(n):
            r_ref = refs[a]
            acc = r_ref[0].astype(F32) + r_ref[1].astype(F32)
            acc = acc + r_ref[2].astype(F32)
            outs[a][...] = acc + r_ref[3].astype(F32)

    in_specs = [pl.BlockSpec(r.shape, lambda i, cr: (0, 0, 0)) for r in rcvs]
    args = list(rcvs)
    aliases = {}
    if accs is not None:
        in_specs += _any_specs(n)
        args += list(accs)
        aliases = {1 + n + a: a for a in range(n)}
    def own_half(shape):
        return lambda i, cr: (layer,) + _half_block_index(shape, cr[0])

    grid_spec = pltpu.PrefetchScalarGridSpec(
        num_scalar_prefetch=1, grid=(1,), in_specs=in_specs,
        out_specs=[pl.BlockSpec((None,) + r.shape[1:], own_half(sh)) for r, sh in zip(rcvs, shapes)])
    return pl.pallas_call(
        body, name="grad_sum_chips", grid_spec=grid_spec,
        out_shape=[jax.ShapeDtypeStruct((n_layers,) + tuple(sh), F32) for sh in shapes],
        input_output_aliases=aliases, compiler_params=_cparams(1),
    )(c_idx, *args)


def _sum_devices(g):
    _, r, n = g.shape

    def body(g_ref, o_ref):
        acc = g_ref[0]
        for d in range(1, 8):
            acc = acc + g_ref[d]
        o_ref[...] = acc

    return pl.pallas_call(
        body, name="small_grad_sum", grid=(1,),
        in_specs=[_full_spec((8, r, n))], out_specs=_full_spec((r, n)),
        out_shape=jax.ShapeDtypeStruct((r, n), F32), compiler_params=_cparams(1),
    )(g)


def _matmul_nt(a, wt, bn, name, exchange=None):
    s, k = a.shape
    n = wt.shape[0]
    bs = min(2048, s)

    def body(a_ref, w_ref, o_ref):
        o_ref[...] = _dot_nt(a_ref[...], w_ref[...])

    return _call(
        body, (a, wt), exchange, name=name, grid=(n // bn, s // bs),
        in_specs=[pl.BlockSpec((bs, k), lambda j, i: (i, 0)), pl.BlockSpec((bn, k), lambda j, i: (j, 0))],
        out_specs=[pl.BlockSpec((bs, bn), lambda j, i: (i, j))],
        out_shape=[jax.ShapeDtypeStruct((s, n), F32)], compiler_params=_cparams(2))


def _matmul_tn(a, b, bn, name, bm=None, whole_sum=False):
    s, m = a.shape
    n = b.shape[1]
    bs = s if whole_sum else min(512, s)
    bm = m if bm is None else bm

    def body(a_ref, b_ref, o_ref):
        if whole_sum:
            o_ref[...] = _dot_tn(a_ref[...], b_ref[...]).astype(BF16)
            return

        @pl.when(pl.program_id(2) == 0)
        def _():
            o_ref[...] = jnp.zeros_like(o_ref)

        o_ref[...] += _dot_tn(a_ref[...], b_ref[...])

    return pl.pallas_call(
        body, name=name, grid=(m // bm, n // bn, s // bs),
        in_specs=[pl.BlockSpec((bs, bm), lambda h, j, i: (i, h)), pl.BlockSpec((bs, bn), lambda h, j, i: (i, j))],
        out_specs=pl.BlockSpec((bm, bn), lambda h, j, i: (h, j)),
        out_shape=jax.ShapeDtypeStruct((m, n), BF16 if whole_sum else F32), compiler_params=_cparams(3),
    )(a, b)


def _modulate(x, scale, shift):
    s, d = x.shape
    t = min(512, s)
    (scale, scale_spec), (shift, shift_spec) = _layer_param(scale), _layer_param(shift)

    def body(x_ref, sc_ref, sh_ref, u_ref):
        u_ref[...] = (x_ref[...] * (1.0 + sc_ref[...]) + sh_ref[...]).astype(BF16)

    return pl.pallas_call(
        body, name="modulate", grid=(s // t,),
        in_specs=[_row_spec(t, d), scale_spec, shift_spec],
        out_specs=_row_spec(t, d), out_shape=jax.ShapeDtypeStruct((s, d), BF16), compiler_params=_cparams(1),
    )(x, scale, shift)


def _rope_tables(pos_col):
    s = pos_col.shape[0]
    t = min(512, s)
    inv = ROPE_THETA ** (-np.arange(0, QK_ROPE, 2, dtype=np.float32) / QK_ROPE)
    lane_freq = np.zeros((1, HEAD_PAD), np.float32)
    lane_freq[0, QK_NOPE:QK_NOPE + 16] = inv
    lane_freq[0, QK_NOPE + 16:QK_NOPE + 32] = inv
    lane_mask = np.zeros((1, HEAD_PAD), np.float32)
    lane_mask[0, QK_NOPE:QK_NOPE + QK_ROPE] = 1.0

    def body(p_ref, f_ref, m_ref, cos_ref, sin_ref):
        ang = p_ref[...].astype(F32) * f_ref[...]
        cos_ref[...] = jnp.cos(ang) * m_ref[...]
        sin_ref[...] = jnp.sin(ang) * m_ref[...]

    return pl.pallas_call(
        body, name="rope_tables", grid=(s // t,),
        in_specs=[_row_spec(t, 1), _full_spec((1, HEAD_PAD)), _full_spec((1, HEAD_PAD))],
        out_specs=[_row_spec(t, HEAD_PAD), _row_spec(t, HEAD_PAD)],
        out_shape=[jax.ShapeDtypeStruct((s, HEAD_PAD), F32)] * 2, compiler_params=_cparams(1),
    )(pos_col, jnp.asarray(lane_freq), jnp.asarray(lane_mask))


def _ada_shard(c_all, w_ada_bf):
    n_layers, d, n = w_ada_bf.shape

    def body(c_ref, w_ref, o_ref, ca_ref):
        cv = c_ref[...]
        ca = cv * _sig(cv)
        ca_ref[...] = ca
        o_ref[...] = _dot(ca.astype(BF16), w_ref[...])

    return pl.pallas_call(
        body, name="ada_shard", grid=(n_layers,),
        in_specs=[_full_spec((8, d)), pl.BlockSpec((None, d, n), lambda l: (l, 0, 0))],
        out_specs=[pl.BlockSpec((None, 8, n), lambda l: (l, 0, 0)), _full_spec((8, d))],
        out_shape=[jax.ShapeDtypeStruct((n_layers, 8, n), F32), jax.ShapeDtypeStruct((8, d), F32)],
        compiler_params=_cparams(1),
    )(c_all, w_ada_bf)


def _ada_weight_grad(cact16, dada16):
    n_layers, _, n = dada16.shape
    d = cact16.shape[1]

    def body(c_ref, g_ref, o_ref):
        o_ref[...] = _dot_tn(c_ref[...], g_ref[...])

    return pl.pallas_call(
        body, name="ada_weight_grad", grid=(n_layers,),
        in_specs=[_full_spec((16, d)), pl.BlockSpec((None, 16, n), lambda l: (l, 0, 0))],
        out_specs=pl.BlockSpec((None, d, n), lambda l: (l, 0, 0)),
        out_shape=jax.ShapeDtypeStruct((n_layers, d, n), F32), compiler_params=_cparams(1),
    )(cact16, dada16)


def _layer_norm_rows(v, g, b):
    mu = jnp.mean(v, axis=-1, keepdims=True)
    var = jnp.mean(jnp.square(v - mu), axis=-1, keepdims=True)
    rstd = lax.rsqrt(var + LN_EPS)
    xh = (v - mu) * rstd
    return xh, rstd, xh * g + b


def _layer_norm_bwd_rows(dy_hat, xh, rstd):
    return rstd * (dy_hat - jnp.mean(dy_hat, axis=-1, keepdims=True) - xh * jnp.mean(dy_hat * xh, axis=-1, keepdims=True))


CONV_ROWS = 32


def _aligned_windows(buf_ref, offsets, t, shift_ref):
    base = {}
    for r in range(1, 8):
        group = [o for o in offsets if o % 8 == r]
        if group:
            lo = min(group)
            length = max(group) - lo + t
            shift_ref[r - 1, 0:length, :] = buf_ref[lo:lo + length, :]
            base[r] = lo

    def window(off, row0, rows):
        r = off % 8
        if r == 0:
            return buf_ref[off + row0:off + row0 + rows, :]
        start = off - base[r] + row0
        return shift_ref[r - 1, start:start + rows, :]

    return window


def _mix_ab_fwd(proj, conv_a_w, conv_a_b, ln_a_g, ln_a_b, conv_b_w, exchange=None):
    s = proj.shape[0]
    t = min(ROW_BLOCK, s)
    cw = 512
    halo = CONV_HALO
    (conv_a_w, caw_spec), (conv_a_b, cab_spec) = _layer_param(conv_a_w), _layer_param(conv_a_b)
    (ln_a_g, lg_spec), (ln_a_b, lb_spec), (conv_b_w, cbw_spec) = (_layer_param(ln_a_g), _layer_param(ln_a_b),
                                                                 _layer_param(conv_b_w))

    def body(p_ref, caw_ref, cab_ref, lg_ref, lb_ref, cbw_ref, acta_ref, actb_ref, ca_ref, cb_ref, abuf, zbuf, shifted):
        @pl.when(pl.program_id(0) == 0)
        def _():
            abuf[0:halo, :] = jnp.zeros((halo, cw), F32)
            zbuf[0:8, :] = jnp.zeros((8, cw), F32)

        abuf[halo:halo + t, :] = p_ref[:, 0:512] * _sig(p_ref[:, 512:1024])
        first = halo - (CONV_K - 1)
        window = _aligned_windows(abuf, range(first, first + CONV_K), t, shifted)
        for row0 in range(0, t, CONV_ROWS):
            acc = jnp.zeros((CONV_ROWS, cw), F32)
            for k in range(CONV_K):
                acc = acc + caw_ref[k:k + 1, :] * window(first + k, row0, CONV_ROWS)
            ca_ref[row0:row0 + CONV_ROWS, :] = acc + cab_ref[...]
        ca = ca_ref[...]
        _, _, ln = _layer_norm_rows(ca, lg_ref[...], lb_ref[...])
        ag = p_ref[:, 1024:1536]
        acta_ref[...] = ((ln * _sig(ln)) * (ag * _sig(ag))).astype(BF16)
        abuf[0:halo, :] = abuf[t:t + halo, :]

        zbuf[8:8 + t, :] = p_ref[:, 2560:3072] * p_ref[:, 1536:2048]
        accb = jnp.zeros((t, cw), F32)
        for k in range(SC_K):
            off = 8 - (SC_K - 1) + k
            accb = accb + cbw_ref[k:k + 1, :] * zbuf[off:off + t, :]
        cb_ref[...] = accb
        bg = p_ref[:, 3072:3584]
        actb_ref[...] = ((p_ref[:, 2048:2560] * accb) * (bg * _sig(bg))).astype(BF16)
        zbuf[0:8, :] = zbuf[t:t + 8, :]

    return _call(
        body, (proj, conv_a_w, conv_a_b, ln_a_g, ln_a_b, conv_b_w), exchange, name="mix_ab_fwd", grid=(s // t,),
        in_specs=[_row_spec(t, W_AB), caw_spec, cab_spec, lg_spec, lb_spec, cbw_spec],
        out_specs=[_row_spec(t, cw)] * 4,
        out_shape=[jax.ShapeDtypeStruct((s, cw), BF16)] * 2 + [jax.ShapeDtypeStruct((s, cw), F32)] * 2,
        scratch_shapes=[pltpu.VMEM((halo + t, cw), F32), pltpu.VMEM((8 + t, cw), F32), pltpu.VMEM((7, halo + t, cw), F32)],
        compiler_params=_cparams(1))


def _mix_ab_bwd(proj, ca, cb, dact_a, dact_b, conv_a_w, ln_a_g, ln_a_b, conv_b_w, dproj):
    s = proj.shape[0]
    t = min(ROW_BLOCK, s)
    nb = s // t
    cw = 512
    halo = CONV_HALO
    per = t // halo
    (conv_a_w, caw_spec), (ln_a_g, lg_spec) = _layer_param(conv_a_w), _layer_param(ln_a_g)
    (ln_a_b, lb_spec), (conv_b_w, cbw_spec) = _layer_param(ln_a_b), _layer_param(conv_b_w)

    def body(p_ref, ph_ref, ca_ref, cb_ref, da_ref, db_ref, caw_ref, lg_ref, lb_ref, cbw_ref, dproj_any,
             dg_ref, dcaw_ref, dcab_ref, dlg_ref, dlb_ref, dcbw_ref, abuf, dcabuf, zbuf, dcbbuf, shifted, da_buf, dw_acc):
        del dproj_any
        i = pl.program_id(0)
        r = nb - 1 - i

        @pl.when(i == 0)
        def _():
            dcaw_ref[...] = jnp.zeros_like(dcaw_ref)
            dw_acc[...] = jnp.zeros_like(dw_acc)
            dcab_ref[...] = jnp.zeros_like(dcab_ref)
            dlg_ref[...] = jnp.zeros_like(dlg_ref)
            dlb_ref[...] = jnp.zeros_like(dlb_ref)
            dcbw_ref[...] = jnp.zeros_like(dcbw_ref)
            dcabuf[t:t + halo, :] = jnp.zeros((halo, cw), F32)
            dcbbuf[t:t + 8, :] = jnp.zeros((8, cw), F32)

        keep = (r > 0).astype(F32)
        a1 = p_ref[:, 0:512]
        s2 = _sig(p_ref[:, 512:1024])
        abuf[0:halo, :] = (ph_ref[:, 0:512] * _sig(ph_ref[:, 512:1024])) * keep
        abuf[halo:halo + t, :] = a1 * s2
        ca_v = ca_ref[...]
        xh, rstd, ln = _layer_norm_rows(ca_v, lg_ref[...], lb_ref[...])
        s_ln = _sig(ln)
        ag = p_ref[:, 1024:1536]
        sg = _sig(ag)
        dact = da_ref[...]
        dsa = dact * (ag * sg)
        dg_ref[:, 1024:1536] = (dact * (ln * s_ln) * _dsilu(ag, sg)).astype(BF16)
        dln = dsa * _dsilu(ln, s_ln)
        dlg_ref[...] += jnp.sum(dln * xh, axis=0, keepdims=True)
        dlb_ref[...] += jnp.sum(dln, axis=0, keepdims=True)
        dca = _layer_norm_bwd_rows(dln * lg_ref[...], xh, rstd)
        dcab_ref[...] += jnp.sum(dca, axis=0, keepdims=True)
        dcabuf[0:t, :] = dca
        window = _aligned_windows(dcabuf, range(CONV_K), t, shifted)
        for row0 in range(0, t, CONV_ROWS):
            acc = jnp.zeros((CONV_ROWS, cw), F32)
            for off in range(CONV_K):
                k = (CONV_K - 1) - off
                acc = acc + caw_ref[k:k + 1, :] * window(off, row0, CONV_ROWS)
            da_buf[row0:row0 + CONV_ROWS, :] = acc
        d_a = da_buf[...]
        first = halo - (CONV_K - 1)
        window = _aligned_windows(abuf, range(first, first + CONV_K), t, shifted)
        for row0 in range(0, t, CONV_ROWS):
            dca_rows = dcabuf[row0:row0 + CONV_ROWS, :]
            for k in range(CONV_K):
                prod = dca_rows * window(first + k, row0, CONV_ROWS)
                part = prod[0:8]
                for q in range(8, CONV_ROWS, 8):
                    part = part + prod[q:q + 8]
                dw_acc[8 * k:8 * k + 8, :] += part

        @pl.when(i == nb - 1)
        def _():
            for k in range(CONV_K):
                dcaw_ref[k:k + 1, :] = jnp.sum(dw_acc[8 * k:8 * k + 8, :], axis=0, keepdims=True)

        dg_ref[:, 0:512] = (d_a * s2).astype(BF16)
        dg_ref[:, 512:1024] = (d_a * a1 * s2 * (1.0 - s2)).astype(BF16)
        dcabuf[t:t + halo, :] = dcabuf[0:halo, :]

        xb = p_ref[:, 1536:2048]
        gb = p_ref[:, 2048:2560]
        gc = p_ref[:, 2560:3072]
        bg = p_ref[:, 3072:3584]
        zbuf[0:8, :] = (ph_ref[halo - 8:halo, 2560:3072] * ph_ref[halo - 8:halo, 1536:2048]) * keep
        zbuf[8:8 + t, :] = gc * xb
        sbg = _sig(bg)
        cbv = cb_ref[...]
        dactb = db_ref[...]
        dyb = dactb * (bg * sbg)
        dg_ref[:, 3072:3584] = (dactb * (gb * cbv) * _dsilu(bg, sbg)).astype(BF16)
        dg_ref[:, 2048:2560] = (dyb * cbv).astype(BF16)
        dcb = dyb * gb
        dcbbuf[0:t, :] = dcb
        dz = jnp.zeros((t, cw), F32)
        for k in range(SC_K):
            off = (SC_K - 1) - k
            dz = dz + cbw_ref[k:k + 1, :] * dcbbuf[off:off + t, :]
            src = 8 - (SC_K - 1) + k
            dcbw_ref[k:k + 1, :] += jnp.sum(dcb * zbuf[src:src + t, :], axis=0, keepdims=True)
        dg_ref[:, 2560:3072] = (dz * xb).astype(BF16)
        dg_ref[:, 1536:2048] = (dz * gc).astype(BF16)
        dcbbuf[t:t + 8, :] = dcbbuf[0:8, :]

    rev = lambda i: (nb - 1 - i, 0)
    outs = pl.pallas_call(
        body, name="mix_ab_bwd", grid=(nb,),
        in_specs=[pl.BlockSpec((t, W_AB), rev),
                  pl.BlockSpec((halo, W_AB), lambda i: (jnp.maximum((nb - 1 - i) * per - 1, 0), 0)),
                  pl.BlockSpec((t, cw), rev), pl.BlockSpec((t, cw), rev), pl.BlockSpec((t, cw), rev),
                  pl.BlockSpec((t, cw), rev),
                  caw_spec, lg_spec, lb_spec, cbw_spec,
                  pl.BlockSpec(memory_space=pl.ANY)],
        out_specs=[pl.BlockSpec((t, W_AB), rev), _full_spec((CONV_HALO, cw)), _full_spec((1, cw)), _full_spec((1, cw)),
                   _full_spec((1, cw)), _full_spec((8, cw))],
        out_shape=[jax.ShapeDtypeStruct(dproj.shape, BF16), jax.ShapeDtypeStruct((CONV_HALO, cw), F32),
                   jax.ShapeDtypeStruct((1, cw), F32), jax.ShapeDtypeStruct((1, cw), F32),
                   jax.ShapeDtypeStruct((1, cw), F32), jax.ShapeDtypeStruct((8, cw), F32)],
        scratch_shapes=[pltpu.VMEM((halo + t, cw), F32), pltpu.VMEM((t + halo, cw), F32),
                        pltpu.VMEM((8 + t, cw), F32), pltpu.VMEM((t + 8, cw), F32), pltpu.VMEM((7, halo + t, cw), F32),
                        pltpu.VMEM((t, cw), F32), pltpu.VMEM((8 * CONV_HALO, cw), F32)],
        input_output_aliases={10: 0},
        compiler_params=_cparams(1),
    )(proj, proj, ca, cb, dact_a, dact_b, conv_a_w, ln_a_g, ln_a_b, conv_b_w, dproj)
    return outs


def _lane_is_nope():
    return lax.broadcasted_iota(jnp.int32, (1, HEAD_PAD), 1) < QK_NOPE


def _attn_prep_fwd(proj, q_g, kv_g, wq12, wkv, tcos, tsin, exchange=None):
    s = proj.shape[0]
    t = min(ROW_BLOCK, s)
    c0 = (W_AB + W_G5) // W_C1
    (q_g, qg_spec), (kv_g, kg_spec) = _layer_param(q_g), _layer_param(kv_g)

    def body(p_ref, qg_ref, kg_ref, wq_ref, wkv_ref, cos_ref, sin_ref, qf_ref, kf_ref, v_ref):
        ql = p_ref[:, 0:384]
        qn = (ql * lax.rsqrt(jnp.mean(jnp.square(ql), axis=-1, keepdims=True) + RMS_EPS) * qg_ref[...]).astype(BF16)
        q12 = _dot(qn, wq_ref[...])
        kvl = p_ref[:, 384:640]
        kvn = (kvl * lax.rsqrt(jnp.mean(jnp.square(kvl), axis=-1, keepdims=True) + RMS_EPS) * kg_ref[...]).astype(BF16)
        kv12 = _dot(kvn, wkv_ref[...])
        tcos_v = cos_ref[...]
        tsin_v = sin_ref[...]
        tq1 = jnp.where(_lane_is_nope(), 1.0, tcos_v)
        kpe = p_ref[:, 640:768] * tcos_v + p_ref[:, 768:896] * tsin_v
        for h in range(HEADS):
            lo = h * HEAD_PAD
            qf_ref[h] = (q12[:, lo:lo + HEAD_PAD] * tq1 + q12[:, 1024 + lo:1024 + lo + HEAD_PAD] * tsin_v).astype(BF16)
            kf_ref[h] = (kv12[:, lo:lo + HEAD_PAD] + kpe).astype(BF16)
        v_ref[...] = kv12[:, 1024:1536].astype(BF16)

    return _call(
        body, (proj, q_g, kv_g, wq12, wkv, tcos, tsin), exchange, name="attn_prep_fwd", grid=(s // t,),
        in_specs=[pl.BlockSpec((t, W_C1), lambda i: (i, c0)), qg_spec, kg_spec,
                  _full_spec((384, 2048)), _full_spec((256, 1536)), _row_spec(t, HEAD_PAD), _row_spec(t, HEAD_PAD)],
        out_specs=[pl.BlockSpec((HEADS, t, HEAD_PAD), lambda i: (0, i, 0)),
                   pl.BlockSpec((HEADS, t, HEAD_PAD), lambda i: (0, i, 0)), _row_spec(t, 512)],
        out_shape=[jax.ShapeDtypeStruct((HEADS, s, HEAD_PAD), BF16), jax.ShapeDtypeStruct((HEADS, s, HEAD_PAD), BF16),
                   jax.ShapeDtypeStruct((s, 512), BF16)],
        compiler_params=_cparams(1))


def _attn_prep_bwd(proj, dqf, dkf, dv, q_g, kv_g, wq12, wkv, tcos, tsin, dproj):
    s = proj.shape[0]
    t = min(ROW_BLOCK, s)
    c0 = (W_AB + W_G5) // W_C1
    (q_g, qg_spec), (kv_g, kg_spec) = _layer_param(q_g), _layer_param(kv_g)

    def body(p_ref, dqf_ref, dkf_ref, dv_ref, qg_ref, kg_ref, wq_ref, wkv_ref, cos_ref, sin_ref, dproj_any,
             dg_ref, qn_ref, kvn_ref, dq12_ref, dkv12_ref, dqg_ref, dkg_ref):
        del dproj_any

        @pl.when(pl.program_id(0) == 0)
        def _():
            dqg_ref[...] = jnp.zeros_like(dqg_ref)
            dkg_ref[...] = jnp.zeros_like(dkg_ref)

        tcos_v = cos_ref[...]
        tsin_v = sin_ref[...]
        tq1 = jnp.where(_lane_is_nope(), 1.0, tcos_v)
        dkpe = jnp.zeros((t, HEAD_PAD), F32)
        for h in range(HEADS):
            lo = h * HEAD_PAD
            dq = dqf_ref[h]
            dq12_ref[:, lo:lo + HEAD_PAD] = (dq * tq1).astype(BF16)
            dq12_ref[:, 1024 + lo:1024 + lo + HEAD_PAD] = (dq * tsin_v).astype(BF16)
            dk = dkf_ref[h]
            dkv12_ref[:, lo:lo + HEAD_PAD] = dk.astype(BF16)
            dkpe = dkpe + dk
        dkv12_ref[:, 1024:1536] = dv_ref[...].astype(BF16)
        dg_ref[:, 640:768] = (dkpe * tcos_v).astype(BF16)
        dg_ref[:, 768:896] = (dkpe * tsin_v).astype(BF16)

        def rms_bwd(xl, g, dn):
            rr = lax.rsqrt(jnp.mean(jnp.square(xl), axis=-1, keepdims=True) + RMS_EPS)
            xn = xl * rr
            tt = dn * g
            return rr * (tt - xn * jnp.mean(tt * xn, axis=-1, keepdims=True)), jnp.sum(dn * xn, axis=0, keepdims=True), xn

        ql = p_ref[:, 0:384]
        dqn = _dot_nt(dq12_ref[...], wq_ref[...])
        dql, dqg, qxn = rms_bwd(ql, qg_ref[...], dqn)
        dg_ref[:, 0:384] = dql.astype(BF16)
        dqg_ref[...] += dqg
        qn_ref[...] = (qxn * qg_ref[...]).astype(BF16)
        kvl = p_ref[:, 384:640]
        dkvn = _dot_nt(dkv12_ref[...], wkv_ref[...])
        dkvl, dkg, kxn = rms_bwd(kvl, kg_ref[...], dkvn)
        dg_ref[:, 384:640] = dkvl.astype(BF16)
        dkg_ref[...] += dkg
        kvn_ref[...] = (kxn * kg_ref[...]).astype(BF16)

    return pl.pallas_call(
        body, name="attn_prep_bwd", grid=(s // t,),
        in_specs=[pl.BlockSpec((t, W_C1), lambda i: (i, c0)),
                  pl.BlockSpec((HEADS, t, HEAD_PAD), lambda i: (0, i, 0)),
                  pl.BlockSpec((HEADS, t, HEAD_PAD), lambda i: (0, i, 0)), _row_spec(t, 512),
                  qg_spec, kg_spec, _full_spec((384, 2048)), _full_spec((256, 1536)),
                  _row_spec(t, HEAD_PAD), _row_spec(t, HEAD_PAD), pl.BlockSpec(memory_space=pl.ANY)],
        out_specs=[pl.BlockSpec((t, W_C1), lambda i: (i, c0)), _row_spec(t, 384), _row_spec(t, 256),
                   _row_spec(t, 2048), _row_spec(t, 1536), _full_spec((1, 384)), _full_spec((1, 256))],
        out_shape=[jax.ShapeDtypeStruct(dproj.shape, BF16), jax.ShapeDtypeStruct((s, 384), BF16),
                   jax.ShapeDtypeStruct((s, 256), BF16), jax.ShapeDtypeStruct((s, 2048), BF16),
                   jax.ShapeDtypeStruct((s, 1536), BF16), jax.ShapeDtypeStruct((1, 384), F32),
                   jax.ShapeDtypeStruct((1, 256), F32)],
        input_output_aliases={10: 0},
        compiler_params=_cparams(1),
    )(proj, dqf, dkf, dv, q_g, kv_g, wq12, wkv, tcos, tsin, dproj)


def _causal_keep(t):
    return lax.broadcasted_iota(jnp.int32, (t, t), 0) >= lax.broadcasted_iota(jnp.int32, (t, t), 1)


def _half_select(e):
    lane = lax.broadcasted_iota(jnp.int32, (1, HEAD_PAD), 1)
    return (lane < V_HEAD) if e == 0 else (lane >= V_HEAD)


def _attention_fwd(qf, kf, v, exchange=None):
    _, s, _ = qf.shape
    t = min(ATT_BLOCK, s)
    nq = s // t

    def body(q_ref, k_ref, v_ref, o_ref, lse_ref):
        def rows_of_block(blk):
            past = blk * t
            for pair in range(HEADS // 2):
                lo = pair * HEAD_PAD
                v_diag = v_ref[past:past + t, lo:lo + HEAD_PAD]
                v_past = v_ref[0:past, lo:lo + HEAD_PAD] if blk else None
                out = None
                for e in range(2):
                    h = 2 * pair + e
                    sel = _half_select(e)
                    q = q_ref[h]
                    s_diag = jnp.where(_causal_keep(t), _dot_nt(q, k_ref[h, past:past + t, :]) * ATT_SCALE, NEG_BIG)
                    m = jnp.max(s_diag, axis=1, keepdims=True)
                    if blk:
                        s_past = _dot_nt(q, k_ref[h, 0:past, :]) * ATT_SCALE
                        m = jnp.maximum(m, jnp.max(s_past, axis=1, keepdims=True))
                    p_diag = jnp.exp(s_diag - m)
                    norm = jnp.sum(p_diag, axis=1, keepdims=True)
                    pv = _dot(p_diag.astype(BF16), jnp.where(sel, v_diag, jnp.zeros_like(v_diag)))
                    if blk:
                        p_past = jnp.exp(s_past - m)
                        norm = norm + jnp.sum(p_past, axis=1, keepdims=True)
                        pv = pv + _dot(p_past.astype(BF16), jnp.where(sel, v_past, jnp.zeros_like(v_past)))
                    pv = pv * (1.0 / norm)
                    out = pv if out is None else out + pv
                    lse_ref[h] = m + jnp.log(norm)
                o_ref[:, lo:lo + HEAD_PAD] = out

        for blk in range(nq):
            pl.when(pl.program_id(0) == blk)(functools.partial(rows_of_block, blk))

    return _call(
        body, (qf, kf, v), exchange, name="attention_fwd", grid=(nq,),
        in_specs=[pl.BlockSpec((HEADS, t, HEAD_PAD), lambda i: (0, i, 0)), _full_spec((HEADS, s, HEAD_PAD)),
                  _full_spec((s, 512))],
        out_specs=[_row_spec(t, 512), pl.BlockSpec((HEADS, t, 1), lambda i: (0, i, 0))],
        out_shape=[jax.ShapeDtypeStruct((s, 512), F32), jax.ShapeDtypeStruct((HEADS, s, 1), F32)],
        compiler_params=_cparams(1))


def _attention_bwd(qf, kf, v, o, do, lse, exchange=None):
    _, s, _ = qf.shape
    t = min(ATT_BLOCK, s)
    nq = s // t

    def body(q_ref, k_ref, v_ref, o_ref, do_ref, lse_ref, dq_ref, dk_ref, dv_ref, dk_acc, dv_acc):
        kj = pl.program_id(0)
        qi = pl.program_id(1)

        @pl.when((kj == 0) & (qi == 0))
        def _():
            dq_ref[...] = jnp.zeros_like(dq_ref)

        @pl.when(qi == kj)
        def _():
            dk_acc[...] = jnp.zeros_like(dk_acc)
            dv_acc[...] = jnp.zeros_like(dv_acc)

        def step(diag):
            rows = pl.ds(pl.multiple_of(qi * t, t), t)
            for h in range(HEADS):
                pair, e = divmod(h, 2)
                sel = _half_select(e)
                lo = pair * HEAD_PAD
                q = q_ref[h]
                k = k_ref[h]
                sc = _dot_nt(q, k) * ATT_SCALE
                if diag:
                    sc = jnp.where(_causal_keep(t), sc, NEG_BIG)
                p = jnp.exp(sc - lse_ref[h])
                do_pair = do_ref[:, lo:lo + HEAD_PAD]
                do_e = jnp.where(sel, do_pair, 0.0)
                do_b = do_e.astype(BF16)
                v2 = v_ref[:, lo:lo + HEAD_PAD]
                dv_acc[:, lo:lo + HEAD_PAD] += _dot_tn(p.astype(BF16), do_b)
                dp = _dot_nt(do_b, jnp.where(sel, v2, jnp.zeros_like(v2)))
                delta = jnp.sum(do_e * o_ref[:, lo:lo + HEAD_PAD], axis=1, keepdims=True)
                ds = (p * (dp - delta) * ATT_SCALE).astype(BF16)
                dq_ref[h, rows, :] += _dot(ds, k)
                dk_acc[h] += _dot_tn(ds, q)

        @pl.when(qi > kj)
        def _():
            step(False)

        @pl.when(qi == kj)
        def _():
            step(True)

        @pl.when(qi == nq - 1)
        def _():
            dk_ref[...] = dk_acc[...]
            dv_ref[...] = dv_acc[...]

    qmap = lambda j, i: (0, jnp.maximum(i, j), 0)
    return _call(
        body, (qf, kf, v, o, do, lse), exchange, name="attention_bwd", grid=(nq, nq),
        in_specs=[pl.BlockSpec((HEADS, t, HEAD_PAD), qmap),
                  pl.BlockSpec((HEADS, t, HEAD_PAD), lambda j, i: (0, j, 0)),
                  pl.BlockSpec((t, 512), lambda j, i: (j, 0)),
                  pl.BlockSpec((t, 512), lambda j, i: (jnp.maximum(i, j), 0)),
                  pl.BlockSpec((t, 512), lambda j, i: (jnp.maximum(i, j), 0)),
                  pl.BlockSpec((HEADS, t, 1), qmap)],
        out_specs=[_full_spec((HEADS, s, HEAD_PAD)),
                   pl.BlockSpec((HEADS, t, HEAD_PAD), lambda j, i: (0, j, 0)),
                   pl.BlockSpec((t, 512), lambda j, i: (j, 0))],
        out_shape=[jax.ShapeDtypeStruct((HEADS, s, HEAD_PAD), F32), jax.ShapeDtypeStruct((HEADS, s, HEAD_PAD), F32),
                   jax.ShapeDtypeStruct((s, 512), F32)],
        scratch_shapes=[pltpu.VMEM((HEADS, t, HEAD_PAD), F32), pltpu.VMEM((t, 512), F32)],
        compiler_params=_cparams(2))


def _merge_fwd(act_a, act_b, o, proj, x, gate, w_a, w_b, w_c, w_o, ln_g, ln_b, alpha, exchange=None):
    s, d = x.shape
    t = min(ROW_BLOCK, s)
    (gate, gate_spec), (ln_g, lg_spec), (ln_b, lb_spec) = _layer_param(gate), _layer_param(ln_g), _layer_param(ln_b)

    def body(aa_ref, ab_ref, o_ref, p_ref, x_ref, gate_ref, wa_ref, wb_ref, wc_ref, wo_ref, lg_ref, lb_ref,
             xn_ref, xh_ref, rstd_ref):
        cg = p_ref[:, 0:512]
        act_c = (o_ref[...] * (cg * _sig(cg))).astype(BF16)
        m = _sig(p_ref[:, 512:1536]) * _dot(aa_ref[...], wa_ref[...])
        m = m + _sig(p_ref[:, 1536:2560]) * _dot(ab_ref[...], wb_ref[...])
        m = m + _sig(p_ref[:, 2560:3584]) * _dot(act_c, wc_ref[...])
        out = _dot(m.astype(BF16), wo_ref[...])
        z = alpha * x_ref[...] + gate_ref[...] * out
        xh, rstd, y = _layer_norm_rows(z, lg_ref[...], lb_ref[...])
        xn_ref[...] = y
        xh_ref[...] = xh
        rstd_ref[...] = rstd

    return _call(
        body, (act_a, act_b, o, proj, x, gate, w_a, w_b, w_c, w_o, ln_g, ln_b), exchange, name="merge_fwd", grid=(s // t,),
        in_specs=[_row_spec(t, 512), _row_spec(t, 512), _row_spec(t, 512),
                  pl.BlockSpec((t, W_G5), lambda i: (i, 1)), _row_spec(t, d), gate_spec,
                  _full_spec((512, d)), _full_spec((512, d)), _full_spec((512, d)), _full_spec((d, d)),
                  lg_spec, lb_spec],
        out_specs=[_row_spec(t, d), _row_spec(t, d), _row_spec(t, 1)],
        out_shape=[jax.ShapeDtypeStruct((s, d), F32), jax.ShapeDtypeStruct((s, d), F32), jax.ShapeDtypeStruct((s, 1), F32)],
        compiler_params=_cparams(1))


def _merge_bwd(dy, xh, rstd, act_a, act_b, o, proj, gate, w_a, w_b, w_c, w_o, ln_g, alpha, exchange=None):
    s, d = dy.shape
    t = min(ROW_BLOCK, s)
    (gate, gate_spec), (ln_g, lg_spec) = _layer_param(gate), _layer_param(ln_g)

    def body(dy_ref, xh_ref, rstd_ref, aa_ref, ab_ref, o_ref, p_ref, gate_ref, wa_ref, wb_ref, wc_ref, wo_ref, lg_ref,
             dxr_ref, dacta_ref, dactb_ref, do_ref, dg_ref, m_ref, dout_ref, dya_ref, dyb_ref, dyc_ref, actc_ref,
             dlg_ref, dlb_ref, dgate_ref):
        @pl.when(pl.program_id(0) == 0)
        def _():
            dlg_ref[...] = jnp.zeros_like(dlg_ref)
            dlb_ref[...] = jnp.zeros_like(dlb_ref)
            dgate_ref[...] = jnp.zeros_like(dgate_ref)

        dyv = dy_ref[...]
        xhv = xh_ref[...]
        dlg_ref[...] += jnp.sum(dyv * xhv, axis=0, keepdims=True)
        dlb_ref[...] += jnp.sum(dyv, axis=0, keepdims=True)
        dz = _layer_norm_bwd_rows(dyv * lg_ref[...], xhv, rstd_ref[...])
        dxr_ref[...] = alpha * dz

        cg = p_ref[:, 0:512]
        scg = _sig(cg)
        silu_cg = cg * scg
        ov = o_ref[...]
        act_c = (ov * silu_cg).astype(BF16)
        actc_ref[...] = act_c
        ya = _dot(aa_ref[...], wa_ref[...])
        yb = _dot(ab_ref[...], wb_ref[...])
        yc = _dot(act_c, wc_ref[...])
        ga = _sig(p_ref[:, 512:1536])
        gb = _sig(p_ref[:, 1536:2560])
        gc = _sig(p_ref[:, 2560:3584])
        mb = (ga * ya + gb * yb + gc * yc).astype(BF16)
        m_ref[...] = mb
        out = _dot(mb, wo_ref[...])
        dgate_ref[...] += jnp.sum(dz * out, axis=0, keepdims=True)
        dout = (gate_ref[...] * dz).astype(BF16)
        dout_ref[...] = dout
        dm = _dot_nt(dout, wo_ref[...])

        dya = (dm * ga).astype(BF16)
        dya_ref[...] = dya
        dg_ref[:, 512:1536] = (dm * ya * ga * (1.0 - ga)).astype(BF16)
        dacta_ref[...] = _dot_nt(dya, wa_ref[...])
        dyb = (dm * gb).astype(BF16)
        dyb_ref[...] = dyb
        dg_ref[:, 1536:2560] = (dm * yb * gb * (1.0 - gb)).astype(BF16)
        dactb_ref[...] = _dot_nt(dyb, wb_ref[...])
        dyc = (dm * gc).astype(BF16)
        dyc_ref[...] = dyc
        dg_ref[:, 2560:3584] = (dm * yc * gc * (1.0 - gc)).astype(BF16)
        dactc = _dot_nt(dyc, wc_ref[...])
        do_ref[...] = dactc * silu_cg
        dg_ref[:, 0:512] = (dactc * ov * _dsilu(cg, scg)).astype(BF16)

    return _call(
        body, (dy, xh, rstd, act_a, act_b, o, proj, gate, w_a, w_b, w_c, w_o, ln_g), exchange, name="merge_bwd", grid=(s // t,),
        in_specs=[_row_spec(t, d), _row_spec(t, d), _row_spec(t, 1), _row_spec(t, 512), _row_spec(t, 512),
                  _row_spec(t, 512), pl.BlockSpec((t, W_G5), lambda i: (i, 1)), gate_spec,
                  _full_spec((512, d)), _full_spec((512, d)), _full_spec((512, d)), _full_spec((d, d)),
                  lg_spec],
        out_specs=[_row_spec(t, d), _row_spec(t, 512), _row_spec(t, 512), _row_spec(t, 512),
                   pl.BlockSpec((t, W_G5), lambda i: (i, 1)),
                   _row_spec(t, d), _row_spec(t, d), _row_spec(t, d), _row_spec(t, d), _row_spec(t, d),
                   _row_spec(t, 512), _full_spec((1, d)), _full_spec((1, d)), _full_spec((1, d))],
        out_shape=[jax.ShapeDtypeStruct((s, d), F32), jax.ShapeDtypeStruct((s, 512), F32),
                   jax.ShapeDtypeStruct((s, 512), F32), jax.ShapeDtypeStruct((s, 512), F32),
                   jax.ShapeDtypeStruct((s, W_PACK), BF16),
                   jax.ShapeDtypeStruct((s, d), BF16), jax.ShapeDtypeStruct((s, d), BF16),
                   jax.ShapeDtypeStruct((s, d), BF16), jax.ShapeDtypeStruct((s, d), BF16),
                   jax.ShapeDtypeStruct((s, d), BF16), jax.ShapeDtypeStruct((s, 512), BF16),
                   jax.ShapeDtypeStruct((1, d), F32), jax.ShapeDtypeStruct((1, d), F32),
                   jax.ShapeDtypeStruct((1, d), F32)],
        compiler_params=_cparams(1))


def _proj_bwd_input(dproj, w_pack, x, scale, dxres, exchange=None):
    s, d = x.shape
    t = min(512, s)
    scale, scale_spec = _layer_param(scale)

    def body(dg_ref, w_ref, x_ref, sc_ref, dxr_ref, dx_ref, dsc_ref, dsh_ref):
        @pl.when(pl.program_id(0) == 0)
        def _():
            dsc_ref[...] = jnp.zeros_like(dsc_ref)
            dsh_ref[...] = jnp.zeros_like(dsh_ref)

        du = _dot(dg_ref[...], w_ref[...])
        dx_ref[...] = du * (1.0 + sc_ref[...]) + dxr_ref[...]
        dsc_ref[...] += jnp.sum(du * x_ref[...], axis=0, keepdims=True)
        dsh_ref[...] += jnp.sum(du, axis=0, keepdims=True)

    w_spec = pl.BlockSpec((W_PACK, d), lambda i: (0, 0), pipeline_mode=pl.Buffered(1))
    return _call(
        body, (dproj, w_pack, x, scale, dxres), exchange, name="proj_bwd_input", grid=(s // t,),
        in_specs=[_row_spec(t, W_PACK), w_spec, _row_spec(t, d), scale_spec, _row_spec(t, d)],
        out_specs=[_row_spec(t, d), _full_spec((1, d)), _full_spec((1, d))],
        out_shape=[jax.ShapeDtypeStruct((s, d), F32), jax.ShapeDtypeStruct((1, d), F32), jax.ShapeDtypeStruct((1, d), F32)],
        compiler_params=_cparams(1))


def _loss_head(y, target):
    s, d = y.shape
    t = min(512, s)

    def body(y_ref, t_ref, dy_ref, loss_ref):
        @pl.when(pl.program_id(0) == 0)
        def _():
            loss_ref[...] = jnp.zeros_like(loss_ref)

        err = y_ref[...] - t_ref[...]
        dy_ref[...] = err / d
        part = 0.5 * jnp.sum(jnp.mean(jnp.square(err), axis=-1, keepdims=True), axis=0, keepdims=True)
        loss_ref[...] += jnp.broadcast_to(part, loss_ref.shape)

    return pl.pallas_call(
        body, name="loss_head", grid=(s // t,),
        in_specs=[_row_spec(t, d), _row_spec(t, d)],
        out_specs=[_row_spec(t, d), _full_spec((1, 128))],
        out_shape=[jax.ShapeDtypeStruct((s, d), F32), jax.ShapeDtypeStruct((1, 128), F32)],
        compiler_params=_cparams(1),
    )(y, target)


def _adamw(g, w, m, v, name):
    r, n = g.shape
    block_bytes = 5 * 2 ** 19
    rb = r
    if r * n * 4 > block_bytes:
        rb = max(k for k in range(8, r, 8) if r % k == 0 and k * n * 4 <= block_bytes)

    def body(g_ref, w_ref, m_ref, v_ref, d_ref, mo_ref, vo_ref):
        gv = g_ref[...]
        mn = ADAM_B1 * m_ref[...] + (1.0 - ADAM_B1) * gv
        vn = ADAM_B2 * v_ref[...] + (1.0 - ADAM_B2) * jnp.square(gv)
        m_hat = mn / (1.0 - ADAM_B1 ** ADAM_STEP)
        v_hat = vn / (1.0 - ADAM_B2 ** ADAM_STEP)
        d_ref[...] = -ADAM_LR * (m_hat / (jnp.sqrt(v_hat) + ADAM_EPS) + ADAM_WD * w_ref[...])
        mo_ref[...] = mn
        vo_ref[...] = vn

    spec = pl.BlockSpec((rb, n), lambda i: (i, 0))
    return pl.pallas_call(
        body, name=name, grid=(r // rb,),
        in_specs=[spec] * 4, out_specs=[spec] * 3,
        out_shape=[jax.ShapeDtypeStruct((r, n), F32)] * 3, compiler_params=_cparams(1),
    )(g, w, m, v)


SMALL = ("b_ada", "conv_a_b", "ln_a_g", "ln_a_b", "q_norm_g", "kv_norm_g", "ln_g", "ln_b", "conv_a_w", "conv_b_w")
ROWS_256 = (("w_a_out", 512), ("w_b_out", 512), ("w_c_out", 512), ("w_ukv", 256))


def _rot_cols(w):
    return jnp.concatenate([-w[..., 16:], w[..., :16]], axis=-1)


def _unrot_cols(g):
    return jnp.concatenate([g[..., 16:], -g[..., :16]], axis=-1)


def _pack_w_in(wt):
    z = lambda n: jnp.zeros((n, wt.shape[1]), wt.dtype)
    wk = wt[4224:4256]
    rot = jnp.concatenate([-wk[16:], wk[:16]], axis=0)
    return jnp.concatenate([wt[0:3584], wt[4256:7840], wt[3584:4224], z(64), wk, z(32), z(64), rot, z(32)], axis=0)


def _unpack_w_in_grad(g):
    c1 = W_AB + W_G5
    g_rot = g[c1 + 768 + 64:c1 + 768 + 96].astype(F32)
    gk = g[c1 + 640 + 64:c1 + 640 + 96].astype(F32) + jnp.concatenate([g_rot[16:], -g_rot[:16]], axis=0)
    return jnp.concatenate([g[0:3584], g[c1:c1 + 640], gk.astype(g.dtype), g[W_AB:W_AB + W_G5]], axis=0)


def _pack_w_uq(w):
    lead = w.shape[:-1]
    wh = w.reshape(lead + (HEADS, QK_NOPE + QK_ROPE))
    nope, rope = wh[..., :QK_NOPE], wh[..., QK_NOPE:]
    z32 = jnp.zeros_like(rope)
    q1 = jnp.concatenate([nope, rope, z32], axis=-1).reshape(lead + (HEADS * HEAD_PAD,))
    q2 = jnp.concatenate([jnp.zeros_like(nope), _rot_cols(rope), z32], axis=-1).reshape(lead + (HEADS * HEAD_PAD,))
    return jnp.concatenate([q1, q2], axis=-1)


def _unpack_w_uq_grad(g):
    lead = g.shape[:-1]
    g1 = g[..., :1024].reshape(lead + (HEADS, HEAD_PAD))
    g2 = g[..., 1024:].reshape(lead + (HEADS, HEAD_PAD))
    rope = g1[..., QK_NOPE:QK_NOPE + QK_ROPE] + _unrot_cols(g2[..., QK_NOPE:QK_NOPE + QK_ROPE])
    return jnp.concatenate([g1[..., :QK_NOPE], rope], axis=-1).reshape(lead + (HEADS * (QK_NOPE + QK_ROPE),))


def _pack_w_ukv(w):
    lead = w.shape[:-1]
    wh = w.reshape(lead + (HEADS, QK_NOPE + V_HEAD))
    kn, vv = wh[..., :QK_NOPE], wh[..., QK_NOPE:]
    k1 = jnp.concatenate([kn, jnp.zeros_like(kn)], axis=-1).reshape(lead + (HEADS * HEAD_PAD,))
    return jnp.concatenate([k1, vv.reshape(lead + (HEADS * V_HEAD,))], axis=-1)


def _unpack_w_ukv_grad(g):
    lead = g.shape[:-1]
    gk = g[..., :1024].reshape(lead + (HEADS, HEAD_PAD))[..., :QK_NOPE]
    gv = g[..., 1024:].reshape(lead + (HEADS, V_HEAD))
    return jnp.concatenate([gk, gv], axis=-1).reshape(lead + (HEADS * (QK_NOPE + V_HEAD),))


def _join_cols(g4):
    return jnp.concatenate([g4[j] for j in range(4)], axis=-1)


def _split_cols(w):
    c4 = w.shape[-1] // 4
    return jnp.stack([w[:, j * c4:(j + 1) * c4] for j in range(4)])


def kernel(x, c, positions, w_ada, b_ada, w_in, conv_a_w, conv_a_b, ln_a_g, ln_a_b, w_a_out, conv_b_w, w_b_out, q_norm_g, kv_norm_g, w_uq, w_ukv, w_c_out, w_o, ln_g, ln_b, loss_target, m_w_ada, m_b_ada, m_w_in, m_conv_a_w, m_conv_a_b, m_ln_a_g, m_ln_a_b, m_w_a_out, m_conv_b_w, m_w_b_out, m_q_norm_g, m_kv_norm_g, m_w_uq, m_w_ukv, m_w_c_out, m_w_o, m_ln_g, m_ln_b, v_w_ada, v_b_ada, v_w_in, v_conv_a_w, v_conv_a_b, v_ln_a_g, v_ln_a_b, v_w_a_out, v_conv_b_w, v_w_b_out, v_q_norm_g, v_kv_norm_g, v_w_uq, v_w_ukv, v_w_c_out, v_w_o, v_ln_g, v_ln_b):
    weights = dict(w_ada=w_ada, b_ada=b_ada, w_in=w_in, conv_a_w=conv_a_w, conv_a_b=conv_a_b, ln_a_g=ln_a_g, ln_a_b=ln_a_b,
                   w_a_out=w_a_out, conv_b_w=conv_b_w, w_b_out=w_b_out, q_norm_g=q_norm_g, kv_norm_g=kv_norm_g, w_uq=w_uq,
                   w_ukv=w_ukv, w_c_out=w_c_out, w_o=w_o, ln_g=ln_g, ln_b=ln_b)
    mom_m = dict(w_ada=m_w_ada, b_ada=m_b_ada, w_in=m_w_in, conv_a_w=m_conv_a_w, conv_a_b=m_conv_a_b, ln_a_g=m_ln_a_g,
                 ln_a_b=m_ln_a_b, w_a_out=m_w_a_out, conv_b_w=m_conv_b_w, w_b_out=m_w_b_out, q_norm_g=m_q_norm_g,
                 kv_norm_g=m_kv_norm_g, w_uq=m_w_uq, w_ukv=m_w_ukv, w_c_out=m_w_c_out, w_o=m_w_o, ln_g=m_ln_g, ln_b=m_ln_b)
    mom_v = dict(w_ada=v_w_ada, b_ada=v_b_ada, w_in=v_w_in, conv_a_w=v_conv_a_w, conv_a_b=v_conv_a_b, ln_a_g=v_ln_a_g,
                 ln_a_b=v_ln_a_b, w_a_out=v_w_a_out, conv_b_w=v_conv_b_w, w_b_out=v_w_b_out, q_norm_g=v_q_norm_g,
                 kv_norm_g=v_kv_norm_g, w_uq=v_w_uq, w_ukv=v_w_ukv, w_c_out=v_w_c_out, w_o=v_w_o, ln_g=v_ln_g, ln_b=v_ln_b)
    order = ["w_ada", "b_ada", "w_in", "conv_a_w", "conv_a_b", "ln_a_g", "ln_a_b", "w_a_out", "conv_b_w", "w_b_out",
             "q_norm_g", "kv_norm_g", "w_uq", "w_ukv", "w_c_out", "w_o", "ln_g", "ln_b"]

    n_layers = w_in.shape[0]
    s, d = x.shape[1], x.shape[2]
    alpha = float((2 * n_layers) ** 0.25)
    ix, iy, ic = lax.axis_index("x"), lax.axis_index("y"), lax.axis_index("c")
    me = 4 * ix + 2 * iy + ic
    chip = 2 * ix + iy
    xs = x[0]

    def four(tree, cast):
        return [cast(jnp.swapaxes(tree["w_in"], 1, 2)), jnp.concatenate([cast(tree[n]) for n, _ in ROWS_256], axis=-2),
                cast(tree["w_uq"]), cast(tree["w_o"])]

    shards = four(weights, lambda a: a.astype(BF16))
    landed_w = _run_exchange(_gather_layer_exchange(shards, 0), "gather_weights_first")
    landed_w = _run_exchange(_gather_pair_exchange(landed_w), "gather_weights_first_pair")
    w_pack, wq12, wkv, w_a, w_b, w_c, w_of = [], [], [], [], [], [], []

    def unpack_layer(g_in, g_256, g_uq, g_wo):
        w_pack.append(_pack_w_in(g_in.reshape(-1, d)))
        wq12.append(_pack_w_uq(_join_cols(g_uq)))
        rows = _join_cols(g_256)
        w_a.append(rows[0:512])
        w_b.append(rows[512:1024])
        w_c.append(rows[1024:1536])
        wkv.append(_pack_w_ukv(rows[1536:1792]))
        w_of.append(g_wo.reshape(d, d))

    cw_a = conv_a_w.reshape(n_layers * CONV_K, 128)
    cw_b = conv_b_w.reshape(n_layers * SC_K, 128)
    c_rows = d // 128
    first_rows = c_rows + n_layers * (CONV_K + SC_K)
    first_pad = (-first_rows) % 8
    first = jnp.concatenate([c.reshape(c_rows, 128), cw_a, cw_b, jnp.zeros((first_pad, 128), F32)], axis=0)
    fr = first_rows + first_pad
    got = _all_gather_small(first, "gather_cond").reshape(8, fr, 128)
    c_all = got[:, :c_rows].reshape(8, d)
    conv_a_full = jnp.moveaxis(got[0::2, c_rows:c_rows + n_layers * CONV_K].reshape(4, n_layers, CONV_K, 128), 0, 2)
    conv_a_full = conv_a_full.reshape(n_layers, CONV_K, 512)
    b0 = c_rows + n_layers * CONV_K
    conv_b_full = jnp.moveaxis(got[0::2, b0:b0 + n_layers * SC_K].reshape(4, n_layers, SC_K, 128), 0, 2)
    conv_b_full = conv_b_full.reshape(n_layers, SC_K, 512)

    ada_sh, c_act = _ada_shard(c_all, w_ada.astype(BF16))
    ada_all = _all_gather_small(ada_sh.reshape(n_layers * 8, 768), "gather_ada").reshape(8, n_layers, 8, 768)
    ada_mine = lax.dynamic_index_in_dim(ada_all[0::2], me, axis=2, keepdims=False)
    ada = jnp.moveaxis(ada_mine, 0, 1).reshape(n_layers, 3 * d) + b_ada
    def stacked(a):
        return a.reshape(n_layers, 1, a.shape[-1])

    shift, scale, gate = stacked(ada[:, 0:d]), stacked(ada[:, d:2 * d]), stacked(ada[:, 2 * d:3 * d])
    conv_a_b3, ln_a_g3, ln_a_b3 = stacked(conv_a_b), stacked(ln_a_g), stacked(ln_a_b)
    q_norm_g3, kv_norm_g3, ln_g3, ln_b3 = stacked(q_norm_g), stacked(kv_norm_g), stacked(ln_g), stacked(ln_b)

    tcos, tsin = _rope_tables(positions.reshape(s, 1))

    saved = []
    h = xs
    for l in range(n_layers):
        unpack_layer(*landed_w)
        u = _modulate(h, (scale, l), (shift, l))
        more = l + 1 < n_layers
        (proj,), lw_in = _matmul_nt(u, w_pack[l], COL_BLK, "proj_fwd",
                                    _gather_layer_exchange(shards[:1], l + 1, (0, 2)) if more else None)
        (act_a, act_b, ca, cb), lw_256 = _mix_ab_fwd(
            proj, (conv_a_full, l), (conv_a_b3, l), (ln_a_g3, l), (ln_a_b3, l), (conv_b_full, l),
            _gather_layer_exchange(shards[1:2], l + 1) if more else None)
        (qf, kf, vv), lw_small = _attn_prep_fwd(proj, (q_norm_g3, l), (kv_norm_g3, l), wq12[l], wkv[l], tcos, tsin,
                                                _gather_layer_exchange(shards[2:], l + 1) if more else None)
        (o, lse), lw_in = _attention_fwd(
            qf, kf, vv, _gather_layer_exchange(shards[:1], l + 1, (1, 2), lw_in) if more else None)
        (h_next, xh, rstd), landed_w = _merge_fwd(
            act_a, act_b, o, proj, h, (gate, l), w_a[l], w_b[l], w_c[l], w_of[l], (ln_g3, l), (ln_b3, l), alpha,
            _gather_pair_exchange(lw_in + lw_256 + lw_small) if more else None)
        saved.append(dict(x=h, u=u, proj=proj, act_a=act_a, act_b=act_b, ca=ca, cb=cb, qf=qf, kf=kf, v=vv, o=o, lse=lse,
                          xh=xh, rstd=rstd))
        h = h_next

    dy, loss_part = _loss_head(h, loss_target[0])
    loss = lax.psum(loss_part[0, 0], ("x", "y", "c"))

    c_idx = ic.reshape(1).astype(jnp.int32)
    reduced = None
    pending = None
    small_rows = [None] * n_layers
    for l in reversed(range(n_layers)):
        sv = saved[l]
        ((dxres, dact_a, dact_b, do, dproj, m_bf, dout_bf, dya, dyb, dyc, act_c, d_ln_g, d_ln_b, d_gate),
         rcv_rest) = _merge_bwd(
            dy, sv["xh"], sv["rstd"], sv["act_a"], sv["act_b"], sv["o"], sv["proj"], (gate, l), w_a[l], w_b[l],
            w_c[l], w_of[l], (ln_g3, l), alpha, _scatter_exchange(pending[1:]) if pending else None)
        g_o = _matmul_tn(m_bf, dout_bf, d, "grad_w_o")
        g_a = _matmul_tn(sv["act_a"], dya, d, "grad_w_a_out")
        g_b = _matmul_tn(sv["act_b"], dyb, d, "grad_w_b_out")
        g_c = _matmul_tn(act_c, dyc, d, "grad_w_c_out")
        (dqf, dkf, dvv), rcv_in = _attention_bwd(sv["qf"], sv["kf"], sv["v"], sv["o"], do, sv["lse"],
                                                 _scatter_exchange(pending[:1]) if pending else None)
        if pending:
            reduced = _sum_chips(rcv_in + rcv_rest, shapes, reduced, l + 1, n_layers, c_idx)
        dproj, qn, kvn, dq12, dkv12, d_qg, d_kg = _attn_prep_bwd(
            sv["proj"], dqf, dkf, dvv, (q_norm_g3, l), (kv_norm_g3, l), wq12[l], wkv[l], tcos, tsin, dproj)
        g_uq = _unpack_w_uq_grad(_matmul_tn(qn, dq12, 1024, "grad_w_uq"))
        g_kv = _unpack_w_ukv_grad(_matmul_tn(kvn, dkv12, 768, "grad_w_ukv"))
        dproj, d_caw, d_cab, d_lag, d_lab, d_cbw = _mix_ab_bwd(
            sv["proj"], sv["ca"], sv["cb"], dact_a, dact_b, (conv_a_full, l), (ln_a_g3, l), (ln_a_b3, l),
            (conv_b_full, l), dproj)
        g_in_l = _unpack_w_in_grad(_matmul_tn(dproj, sv["u"], d, "grad_w_in", bm=COL_BLK, whole_sum=True))
        partial = [g_in_l.reshape(4, -1, d),
                   jnp.concatenate([_split_cols(g) for g in (g_a, g_b, g_c, g_kv)], axis=1).astype(BF16),
                   _split_cols(g_uq).astype(BF16), g_o.reshape(4, d // 4, d).astype(BF16)]
        shapes = [p.shape[1:] for p in partial]
        (dy, d_scale, d_shift), from_sibling = _proj_bwd_input(dproj, w_pack[l], sv["x"], (scale, l), dxres,
                                                               _swap_exchange(partial))
        small_rows[l] = jnp.concatenate(
            [d_shift[0], d_scale[0], d_gate[0], d_cab[0], d_lag[0], d_lab[0], d_qg[0], d_kg[0], d_ln_g[0], d_ln_b[0],
             d_caw[:CONV_K].reshape(-1), d_cbw[:SC_K].reshape(-1)])
        pending = _add_halves(partial, from_sibling, c_idx)
    grad_x = dy[None]
    received = _run_exchange(_scatter_exchange(pending), "grad_scatter_last")
    reduced = _sum_chips(received, shapes, reduced, 0, n_layers, c_idx)
    r_in, r_256, r_uq, r_wo = _run_exchange(_gather_pair_exchange(reduced), "grad_pair_halves")
    grads = {"w_in": jnp.swapaxes(r_in, 1, 2), "w_uq": r_uq, "w_o": r_wo}
    row0 = 0
    for n, rows in ROWS_256:
        grads[n] = r_256[:, row0:row0 + rows]
        row0 += rows

    small = jnp.stack(small_rows)
    p_small = small.shape[1]
    n_small = n_layers * p_small
    pad_small = (-n_small) % (8 * 128)
    small_flat = jnp.concatenate([small.reshape(-1), jnp.zeros((pad_small,), F32)]).reshape(-1, 128)
    sr = small_flat.shape[0]
    small_all = _all_gather_small(small_flat, "gather_small_grads").reshape(8, sr, 128)
    small_sum = _sum_devices(small_all).reshape(-1)[:n_small].reshape(n_layers, p_small)
    d_ada_all = small_all.reshape(8, -1)[:, :n_small].reshape(8, n_layers, p_small)[:, :, :3 * d]

    cuts = np.cumsum([0, 3 * d, 512, 512, 512, 384, 256, d, d, CONV_K * 512, SC_K * 512])
    pieces = [small_sum[:, cuts[k]:cuts[k + 1]] for k in range(10)]
    grads.update(zip(SMALL, pieces[:8]))
    ga_full = pieces[8].reshape(n_layers, CONV_K, 512)
    gb_full = pieces[9].reshape(n_layers, SC_K, 512)
    grads["conv_a_w"] = lax.dynamic_slice_in_dim(ga_full, chip * 128, 128, axis=2)
    grads["conv_b_w"] = lax.dynamic_slice_in_dim(gb_full, chip * 128, 128, axis=2)

    dada_sh = lax.dynamic_slice_in_dim(d_ada_all, chip * 768, 768, axis=2)
    dada16 = jnp.concatenate([jnp.moveaxis(dada_sh, 0, 1), jnp.zeros((n_layers, 8, 768), F32)], axis=1).astype(BF16)
    cact16 = jnp.concatenate([c_act, jnp.zeros_like(c_act)], axis=0).astype(BF16)
    grads["w_ada"] = _ada_weight_grad(cact16, dada16)

    def as2d(a):
        return a.reshape(-1, a.shape[-1])

    deltas, new_m, new_v = {}, {}, {}
    for n in ("w_in", "w_a_out", "w_b_out", "w_c_out", "w_uq", "w_ukv", "w_o", "w_ada"):
        view = (lambda a: jnp.swapaxes(a, 1, 2)) if n == "w_in" else (lambda a: a)
        shape = view(weights[n]).shape
        dl, mn, vn = _adamw(as2d(view(grads[n])), as2d(view(weights[n])), as2d(view(mom_m[n])), as2d(view(mom_v[n])),
                            "adamw_" + n)
        deltas[n], new_m[n], new_v[n] = view(dl.reshape(shape)), view(mn.reshape(shape)), view(vn.reshape(shape))

    def pack_small(tree):
        flat = jnp.concatenate([tree[n].reshape(-1) for n in SMALL])
        pad = (-flat.shape[0]) % (8 * 128)
        return jnp.concatenate([flat, jnp.zeros((pad,), F32)]).reshape(-1, 128)

    dl_s, mn_s, vn_s = _adamw(pack_small(grads), pack_small(weights), pack_small(mom_m), pack_small(mom_v), "adamw_small")
    off = 0
    for n in SMALL:
        sz = int(np.prod(weights[n].shape))
        for tree, flat in ((deltas, dl_s), (new_m, mn_s), (new_v, vn_s)):
            tree[n] = flat.reshape(-1)[off:off + sz].reshape(weights[n].shape)
        off += sz

    return (loss, grad_x, *[grads[n] for n in order], *[deltas[n] for n in order], *[new_m[n] for n in order],
            *[new_v[n] for n in order])
```

```python
import functools

import numpy as np
import jax
import jax.numpy as jnp
from jax import lax
from jax.experimental import pallas as pl
from jax.experimental.pallas import tpu as pltpu

F32 = jnp.float32
BF16 = jnp.bfloat16
MESH = pl.DeviceIdType.MESH

HEADS = 8
QK_NOPE = 64
QK_ROPE = 32
V_HEAD = 64
CONV_K = 31
SC_K = 3
LN_EPS = 1e-5
RMS_EPS = 1e-6
ROPE_THETA = 10000.0
ATT_SCALE = (QK_NOPE + QK_ROPE) ** -0.5
ADAM_LR = 0.001
ADAM_B1 = 0.9
ADAM_B2 = 0.999
ADAM_EPS = 1e-08
ADAM_WD = 0.01
ADAM_STEP = 10

W_AB = 3584
W_G5 = 3584
W_C1 = 896
W_PACK = W_AB + W_G5 + W_C1
COL_BLK = 896
HEAD_PAD = 128
CONV_HALO = 32
NEG_BIG = -1e30

VMEM_LIMIT_MB = 56
ATT_BLOCK = 512
ATT_BWD_BLOCK = 256
ROW_BLOCK = 256


def _cparams(n_grid, mb=VMEM_LIMIT_MB):
    return pltpu.CompilerParams(dimension_semantics=("arbitrary",) * n_grid, vmem_limit_bytes=mb * 2 ** 20)


def _sig(x):
    return 1.0 / (1.0 + jnp.exp(-x))


def _dsilu(x, s):
    return s * (1.0 + x * (1.0 - s))


def _dot(a, b):
    return jnp.dot(a, b, preferred_element_type=F32)


def _dot_nt(a, b):
    return lax.dot_general(a, b, (((1,), (1,)), ((), ())), preferred_element_type=F32)


def _dot_tn(a, b):
    return lax.dot_general(a, b, (((0,), (0,)), ((), ())), preferred_element_type=F32)


def _row_spec(t, w):
    return pl.BlockSpec((t, w), lambda i: (i, 0))


def _full_spec(shape):
    nd = len(shape)
    return pl.BlockSpec(shape, lambda *_: (0,) * nd)


def _layer_param(p):
    arr, layer = p
    return arr, pl.BlockSpec((None,) + arr.shape[1:], lambda *_: (layer, 0, 0))


def _my_place():
    x, y, c = lax.axis_index("x"), lax.axis_index("y"), lax.axis_index("c")
    chips = [(1 - x, y), (x, 1 - y), (1 - x, 1 - y)]
    return x, y, c, chips


def _all_gather_small(v, name):
    m_per, n = v.shape

    def body(x_ref, out_ref, send_sems, recv_sems, local_sem):
        x, y, c, chips = _my_place()
        me, sibling = (x, y, c), (x, y, 1 - c)

        def rows(px, py, pc):
            return out_ref.at[pl.ds((4 * px + 2 * py + pc) * m_per, m_per), :]

        def copy(k, block, to, src=None):
            return pltpu.make_async_remote_copy(
                src_ref=rows(*block) if src is None else src, dst_ref=rows(*block),
                send_sem=send_sems.at[k], recv_sem=recv_sems.at[k], device_id=to, device_id_type=MESH)

        mine = pltpu.make_async_copy(x_ref, rows(*me), local_sem)
        mine.start()
        first = [copy(0, me, sibling, src=x_ref)]
        first += [copy(1 + j, me, (*chip, c), src=x_ref) for j, chip in enumerate(chips)]
        for cp in first:
            cp.start()
        passed = [copy(4 + j, (*chip, c), sibling) for j, chip in enumerate(chips)]
        for j, chip in enumerate(chips):
            copy(1 + j, (*chip, c), me).wait_recv()
            passed[j].start()
        copy(0, sibling, me).wait_recv()
        for j, chip in enumerate(chips):
            copy(4 + j, (*chip, 1 - c), me).wait_recv()
        for cp in first + passed:
            cp.wait_send()
        mine.wait()

    return pl.pallas_call(
        body, name=name,
        out_shape=jax.ShapeDtypeStruct((8 * m_per, n), v.dtype),
        in_specs=[pl.BlockSpec(memory_space=pltpu.VMEM)],
        out_specs=pl.BlockSpec(memory_space=pltpu.VMEM),
        scratch_shapes=[pltpu.SemaphoreType.DMA((7,)), pltpu.SemaphoreType.DMA((7,)), pltpu.SemaphoreType.DMA],
        compiler_params=pltpu.CompilerParams(vmem_limit_bytes=VMEM_LIMIT_MB * 2 ** 20),
    )(v)


def _any_specs(n):
    return [pl.BlockSpec(memory_space=pl.ANY)] * n


def _splits_rows(shape):
    return shape[-2] % 32 == 0


def _window(shape, c, p=0, n_parts=1):
    r, n = shape[-2], shape[-1]
    if _splits_rows(shape):
        size = r // 2 // n_parts
        return (pl.ds(c * (r // 2) + p * size, size), slice(None))
    size = n // 2 // n_parts
    return (slice(None), pl.ds(c * (n // 2) + p * size, size))


def _half_shape(shape):
    r, n = shape[-2], shape[-1]
    return (r // 2, n) if _splits_rows(shape) else (r, n // 2)


def _half_block_index(shape, c):
    return (c, 0) if _splits_rows(shape) else (0, c)


class _Exchange:
    def __init__(self, inputs, out_shapes, sems, start, finish, aliases=None):
        self.inputs, self.out_shapes, self.sems = list(inputs), list(out_shapes), list(sems)
        self.start, self.finish, self.aliases = start, finish, dict(aliases or {})


def _call(body, args, exchange=None, *, grid, in_specs, out_specs, out_shape, scratch_shapes=(),
          input_output_aliases=None, **kw):
    aliases = dict(input_output_aliases or {})
    if exchange is None:
        outs = pl.pallas_call(body, grid=grid, in_specs=list(in_specs), out_specs=list(out_specs),
                              out_shape=list(out_shape), scratch_shapes=list(scratch_shapes),
                              input_output_aliases=aliases, **kw)(*args)
        return list(outs), []
    n_in, n_out, n_scr = len(in_specs), len(out_specs), len(scratch_shapes)
    x_in, x_out = len(exchange.inputs), len(exchange.out_shapes)

    def wrapped(*refs):
        ins, xin = refs[:n_in], refs[n_in:n_in + x_in]
        o0 = n_in + x_in
        outs, xout = refs[o0:o0 + n_out], refs[o0 + n_out:o0 + n_out + x_out]
        s0 = o0 + n_out + x_out
        scr, xsem = refs[s0:s0 + n_scr], refs[s0 + n_scr:]
        ids = [pl.program_id(k) for k in range(len(grid))]
        first = functools.reduce(jnp.logical_and, [i == 0 for i in ids])
        last = functools.reduce(jnp.logical_and, [i == g - 1 for i, g in zip(ids, grid)])

        @pl.when(first)
        def _():
            exchange.start(xin, xout, xsem)

        body(*ins, *outs, *scr)

        @pl.when(last)
        def _():
            exchange.finish(xin, xout, xsem)

    aliases.update({n_in + i: n_out + o for i, o in exchange.aliases.items()})
    outs = pl.pallas_call(
        wrapped, grid=grid, in_specs=list(in_specs) + _any_specs(x_in), out_specs=list(out_specs) + _any_specs(x_out),
        out_shape=list(out_shape) + exchange.out_shapes, scratch_shapes=list(scratch_shapes) + exchange.sems,
        input_output_aliases=aliases, **kw)(*args, *exchange.inputs)
    return list(outs[:n_out]), list(outs[n_out:])


def _run_exchange(exchange, name):
    x_in, x_out = len(exchange.inputs), len(exchange.out_shapes)

    def body(*refs):
        xin, xout, xsem = refs[:x_in], refs[x_in:x_in + x_out], refs[x_in + x_out:]
        exchange.start(xin, xout, xsem)
        exchange.finish(xin, xout, xsem)

    outs = pl.pallas_call(
        body, name=name, in_specs=_any_specs(x_in), out_specs=_any_specs(x_out), out_shape=exchange.out_shapes,
        scratch_shapes=exchange.sems, input_output_aliases=exchange.aliases)(*exchange.inputs)
    return list(outs)


def _gather_layer_exchange(ws, layer, part=(0, 1), lands=None):
    n = len(ws)
    p, n_parts = part

    def copies(xin, xout, sems):
        send_sems, recv_sems, local_sems = sems
        x, y, c, chips = _my_place()
        k_me = 2 * x + y
        local, sends, recvs = [], [], []
        for a in range(n):
            win = _window(ws[a].shape, c, p, n_parts)
            src = xin[a].at[(layer,) + win]
            dst = xout[a].at[(k_me,) + win]
            local.append(pltpu.make_async_copy(src, dst, local_sems.at[a]))
            for j, (px, py) in enumerate(chips):
                sends.append(pltpu.make_async_remote_copy(
                    src_ref=src, dst_ref=dst, send_sem=send_sems.at[a, j], recv_sem=recv_sems.at[a, j],
                    device_id=(px, py, c), device_id_type=MESH))
                recvs.append(pltpu.make_async_remote_copy(
                    src_ref=src, dst_ref=xout[a].at[(2 * px + py,) + win], send_sem=send_sems.at[a, j],
                    recv_sem=recv_sems.at[a, j], device_id=(px, py, c), device_id_type=MESH))
        return local, sends, recvs

    def start(xin, xout, sems):
        local, sends, _ = copies(xin, xout, sems)
        for cp in local + sends:
            cp.start()

    def finish(xin, xout, sems):
        local, sends, recvs = copies(xin, xout, sems)
        for cp in recvs:
            cp.wait_recv()
        for cp in sends:
            cp.wait_send()
        for cp in local:
            cp.wait()

    return _Exchange(list(ws) + list(lands or []), [jax.ShapeDtypeStruct((4,) + w.shape[1:], w.dtype) for w in ws],
                     [pltpu.SemaphoreType.DMA((n, 3)), pltpu.SemaphoreType.DMA((n, 3)), pltpu.SemaphoreType.DMA((n,))],
                     start, finish, aliases={n + a: a for a in range(n)} if lands else None)


def _gather_pair_exchange(lands):
    n = len(lands)

    def copies(xout, sems):
        send_sems, recv_sems = sems
        x, y, c, _ = _my_place()
        def half(a, core):
            return xout[a].at[(slice(None),) + _window(lands[a].shape, core)]

        sends = [pltpu.make_async_remote_copy(
            src_ref=half(a, c), dst_ref=half(a, c), send_sem=send_sems.at[a],
            recv_sem=recv_sems.at[a], device_id=(x, y, 1 - c), device_id_type=MESH) for a in range(n)]
        recvs = [pltpu.make_async_remote_copy(
            src_ref=half(a, c), dst_ref=half(a, 1 - c), send_sem=send_sems.at[a],
            recv_sem=recv_sems.at[a], device_id=(x, y, 1 - c), device_id_type=MESH) for a in range(n)]
        return sends, recvs

    def start(xin, xout, sems):
        for cp in copies(xout, sems)[0]:
            cp.start()

    def finish(xin, xout, sems):
        sends, recvs = copies(xout, sems)
        for cp in recvs:
            cp.wait_recv()
        for cp in sends:
            cp.wait_send()

    return _Exchange(lands, [jax.ShapeDtypeStruct(w.shape, w.dtype) for w in lands],
                     [pltpu.SemaphoreType.DMA((n,)), pltpu.SemaphoreType.DMA((n,))], start, finish,
                     aliases={a: a for a in range(n)})


def _swap_exchange(ps):
    n = len(ps)

    def copies(xin, xout, sems):
        send_sems, recv_sems = sems
        x, y, c, _ = _my_place()
        return [pltpu.make_async_remote_copy(
            src_ref=xin[a].at[(slice(None),) + _window(ps[a].shape, 1 - c)], dst_ref=xout[a],
            send_sem=send_sems.at[a], recv_sem=recv_sems.at[a], device_id=(x, y, 1 - c), device_id_type=MESH)
            for a in range(n)]

    def start(xin, xout, sems):
        for cp in copies(xin, xout, sems):
            cp.start()

    def finish(xin, xout, sems):
        for cp in copies(xin, xout, sems):
            cp.wait()

    return _Exchange(ps, [jax.ShapeDtypeStruct((4,) + _half_shape(p.shape), p.dtype) for p in ps],
                     [pltpu.SemaphoreType.DMA((n,)), pltpu.SemaphoreType.DMA((n,))], start, finish)


def _add_halves(ps, lands, c_idx):
    n = len(ps)

    def body(c_ref, *refs):
        del c_ref
        for a in range(n):
            refs[2 * n + a][...] = (refs[a][...].astype(F32) + refs[n + a][...].astype(F32)).astype(BF16)

    def half_block(p):
        return (None,) + _half_shape(p.shape)

    def own_half(p):
        return lambda j, cr: (j,) + _half_block_index(p.shape, cr[0])

    grid_spec = pltpu.PrefetchScalarGridSpec(
        num_scalar_prefetch=1, grid=(4,),
        in_specs=[pl.BlockSpec(half_block(p), own_half(p)) for p in ps]
        + [pl.BlockSpec(half_block(p), lambda j, cr: (j, 0, 0)) for p in ps],
        out_specs=[pl.BlockSpec(half_block(p), lambda j, cr: (j, 0, 0)) for p in ps])
    return pl.pallas_call(
        body, name="grad_add_halves", grid_spec=grid_spec,
        out_shape=[jax.ShapeDtypeStruct((4,) + _half_shape(p.shape), BF16) for p in ps],
        compiler_params=_cparams(1),
    )(c_idx, *ps, *lands)


def _scatter_exchange(hs):
    n = len(hs)

    def copies(xin, xout, sems):
        send_sems, recv_sems, local_sems = sems
        x, y, c, chips = _my_place()
        k_me = 2 * x + y
        local = [pltpu.make_async_copy(xin[a].at[k_me], xout[a].at[k_me], local_sems.at[a]) for a in range(n)]
        sends, recvs = [], []
        for j, (px, py) in enumerate(chips):
            for a in range(n):
                sends.append(pltpu.make_async_remote_copy(
                    src_ref=xin[a].at[2 * px + py], dst_ref=xout[a].at[k_me], send_sem=send_sems.at[a, j],
                    recv_sem=recv_sems.at[a, j], device_id=(px, py, c), device_id_type=MESH))
                recvs.append(pltpu.make_async_remote_copy(
                    src_ref=xin[a].at[k_me], dst_ref=xout[a].at[2 * px + py], send_sem=send_sems.at[a, j],
                    recv_sem=recv_sems.at[a, j], device_id=(px, py, c), device_id_type=MESH))
        return local, sends, recvs

    def start(xin, xout, sems):
        local, sends, _ = copies(xin, xout, sems)
        for cp in local + sends:
            cp.start()

    def finish(xin, xout, sems):
        local, sends, recvs = copies(xin, xout, sems)
        for cp in recvs:
            cp.wait_recv()
        for cp in sends:
            cp.wait_send()
        for cp in local:
            cp.wait()

    return _Exchange(hs, [jax.ShapeDtypeStruct(h.shape, h.dtype) for h in hs],
                     [pltpu.SemaphoreType.DMA((n, 3)), pltpu.SemaphoreType.DMA((n, 3)), pltpu.SemaphoreType.DMA((n,))],
                     start, finish)


def _sum_chips(rcvs, shapes, accs, layer, n_layers, c_idx):
    n = len(rcvs)

    def body(c_ref, *refs):
        del c_ref
        outs = refs[len(refs) - n:]
        for a in range(n):
            r_ref = refs[a]
            acc = r_ref[0].astype(F32) + r_ref[1].astype(F32)
            acc = acc + r_ref[2].astype(F32)
            outs[a][...] = acc + r_ref[3].astype(F32)

    in_specs = [pl.BlockSpec(r.shape, lambda i, cr: (0, 0, 0)) for r in rcvs]
    args = list(rcvs)
    aliases = {}
    if accs is not None:
        in_specs += _any_specs(n)
        args += list(accs)
        aliases = {1 + n + a: a for a in range(n)}
    def own_half(shape):
        return lambda i, cr: (layer,) + _half_block_index(shape, cr[0])

    grid_spec = pltpu.PrefetchScalarGridSpec(
        num_scalar_prefetch=1, grid=(1,), in_specs=in_specs,
        out_specs=[pl.BlockSpec((None,) + r.shape[1:], own_half(sh)) for r, sh in zip(rcvs, shapes)])
    return pl.pallas_call(
        body, name="grad_sum_chips", grid_spec=grid_spec,
        out_shape=[jax.ShapeDtypeStruct((n_layers,) + tuple(sh), F32) for sh in shapes],
        input_output_aliases=aliases, compiler_params=_cparams(1),
    )(c_idx, *args)


def _sum_devices(g):
    _, r, n = g.shape

    def body(g_ref, o_ref):
        acc = g_ref[0]
        for d in range(1, 8):
            acc = acc + g_ref[d]
        o_ref[...] = acc

    return pl.pallas_call(
        body, name="small_grad_sum", grid=(1,),
        in_specs=[_full_spec((8, r, n))], out_specs=_full_spec((r, n)),
        out_shape=jax.ShapeDtypeStruct((r, n), F32), compiler_params=_cparams(1),
    )(g)


def _matmul_nt(a, wt, bn, name, exchange=None):
    s, k = a.shape
    n = wt.shape[0]
    bs = min(2048, s)

    def body(a_ref, w_ref, o_ref):
        o_ref[...] = _dot_nt(a_ref[...], w_ref[...])

    return _call(
        body, (a, wt), exchange, name=name, grid=(n // bn, s // bs),
        in_specs=[pl.BlockSpec((bs, k), lambda j, i: (i, 0)), pl.BlockSpec((bn, k), lambda j, i: (j, 0))],
        out_specs=[pl.BlockSpec((bs, bn), lambda j, i: (i, j))],
        out_shape=[jax.ShapeDtypeStruct((s, n), F32)], compiler_params=_cparams(2))


def _matmul_tn(a, b, bn, name, bm=None, whole_sum=False):
    s, m = a.shape
    n = b.shape[1]
    bs = s if whole_sum else min(512, s)
    bm = m if bm is None else bm

    def body(a_ref, b_ref, o_ref):
        if whole_sum:
            o_ref[...] = _dot_tn(a_ref[...], b_ref[...]).astype(BF16)
            return

        @pl.when(pl.program_id(2) == 0)
        def _():
            o_ref[...] = jnp.zeros_like(o_ref)

        o_ref[...] += _dot_tn(a_ref[...], b_ref[...])

    return pl.pallas_call(
        body, name=name, grid=(m // bm, n // bn, s // bs),
        in_specs=[pl.BlockSpec((bs, bm), lambda h, j, i: (i, h)), pl.BlockSpec((bs, bn), lambda h, j, i: (i, j))],
        out_specs=pl.BlockSpec((bm, bn), lambda h, j, i: (h, j)),
        out_shape=jax.ShapeDtypeStruct((m, n), BF16 if whole_sum else F32), compiler_params=_cparams(3),
    )(a, b)


def _modulate(x, scale, shift):
    s, d = x.shape
    t = min(512, s)
    (scale, scale_spec), (shift, shift_spec) = _layer_param(scale), _layer_param(shift)

    def body(x_ref, sc_ref, sh_ref, u_ref):
        u_ref[...] = (x_ref[...] * (1.0 + sc_ref[...]) + sh_ref[...]).astype(BF16)

    return pl.pallas_call(
        body, name="modulate", grid=(s // t,),
        in_specs=[_row_spec(t, d), scale_spec, shift_spec],
        out_specs=_row_spec(t, d), out_shape=jax.ShapeDtypeStruct((s, d), BF16), compiler_params=_cparams(1),
    )(x, scale, shift)


def _rope_tables(pos_col):
    s = pos_col.shape[0]
    t = min(512, s)
    inv = ROPE_THETA ** (-np.arange(0, QK_ROPE, 2, dtype=np.float32) / QK_ROPE)
    lane_freq = np.zeros((1, HEAD_PAD), np.float32)
    lane_freq[0, QK_NOPE:QK_NOPE + 16] = inv
    lane_freq[0, QK_NOPE + 16:QK_NOPE + 32] = inv
    lane_mask = np.zeros((1, HEAD_PAD), np.float32)
    lane_mask[0, QK_NOPE:QK_NOPE + QK_ROPE] = 1.0

    def body(p_ref, f_ref, m_ref, cos_ref, sin_ref):
        ang = p_ref[...].astype(F32) * f_ref[...]
        cos_ref[...] = jnp.cos(ang) * m_ref[...]
        sin_ref[...] = jnp.sin(ang) * m_ref[...]

    return pl.pallas_call(
        body, name="rope_tables", grid=(s // t,),
        in_specs=[_row_spec(t, 1), _full_spec((1, HEAD_PAD)), _full_spec((1, HEAD_PAD))],
        out_specs=[_row_spec(t, HEAD_PAD), _row_spec(t, HEAD_PAD)],
        out_shape=[jax.ShapeDtypeStruct((s, HEAD_PAD), F32)] * 2, compiler_params=_cparams(1),
    )(pos_col, jnp.asarray(lane_freq), jnp.asarray(lane_mask))


def _ada_shard(c_all, w_ada_bf):
    n_layers, d, n = w_ada_bf.shape

    def body(c_ref, w_ref, o_ref, ca_ref):
        cv = c_ref[...]
        ca = cv * _sig(cv)
        ca_ref[...] = ca
        o_ref[...] = _dot(ca.astype(BF16), w_ref[...])

    return pl.pallas_call(
        body, name="ada_shard", grid=(n_layers,),
        in_specs=[_full_spec((8, d)), pl.BlockSpec((None, d, n), lambda l: (l, 0, 0))],
        out_specs=[pl.BlockSpec((None, 8, n), lambda l: (l, 0, 0)), _full_spec((8, d))],
        out_shape=[jax.ShapeDtypeStruct((n_layers, 8, n), F32), jax.ShapeDtypeStruct((8, d), F32)],
        compiler_params=_cparams(1),
    )(c_all, w_ada_bf)


def _ada_weight_grad(cact16, dada16):
    n_layers, _, n = dada16.shape
    d = cact16.shape[1]

    def body(c_ref, g_ref, o_ref):
        o_ref[...] = _dot_tn(c_ref[...], g_ref[...])

    return pl.pallas_call(
        body, name="ada_weight_grad", grid=(n_layers,),
        in_specs=[_full_spec((16, d)), pl.BlockSpec((None, 16, n), lambda l: (l, 0, 0))],
        out_specs=pl.BlockSpec((None, d, n), lambda l: (l, 0, 0)),
        out_shape=jax.ShapeDtypeStruct((n_layers, d, n), F32), compiler_params=_cparams(1),
    )(cact16, dada16)


def _layer_norm_rows(v, g, b):
    mu = jnp.mean(v, axis=-1, keepdims=True)
    var = jnp.mean(jnp.square(v - mu), axis=-1, keepdims=True)
    rstd = lax.rsqrt(var + LN_EPS)
    xh = (v - mu) * rstd
    return xh, rstd, xh * g + b


def _layer_norm_bwd_rows(dy_hat, xh, rstd):
    return rstd * (dy_hat - jnp.mean(dy_hat, axis=-1, keepdims=True) - xh * jnp.mean(dy_hat * xh, axis=-1, keepdims=True))


CONV_ROWS = 32


def _aligned_windows(buf_ref, offsets, t, shift_ref):
    base = {}
    for r in range(1, 8):
        group = [o for o in offsets if o % 8 == r]
        if group:
            lo = min(group)
            length = max(group) - lo + t
            shift_ref[r - 1, 0:length, :] = buf_ref[lo:lo + length, :]
            base[r] = lo

    def window(off, row0, rows):
        r = off % 8
        if r == 0:
            return buf_ref[off + row0:off + row0 + rows, :]
        start = off - base[r] + row0
        return shift_ref[r - 1, start:start + rows, :]

    return window


def _mix_ab_fwd(proj, conv_a_w, conv_a_b, ln_a_g, ln_a_b, conv_b_w, exchange=None):
    s = proj.shape[0]
    t = min(ROW_BLOCK, s)
    cw = 512
    halo = CONV_HALO
    (conv_a_w, caw_spec), (conv_a_b, cab_spec) = _layer_param(conv_a_w), _layer_param(conv_a_b)
    (ln_a_g, lg_spec), (ln_a_b, lb_spec), (conv_b_w, cbw_spec) = (_layer_param(ln_a_g), _layer_param(ln_a_b),
                                                                 _layer_param(conv_b_w))

    def body(p_ref, caw_ref, cab_ref, lg_ref, lb_ref, cbw_ref, acta_ref, actb_ref, ca_ref, cb_ref, abuf, zbuf, shifted):
        @pl.when(pl.program_id(0) == 0)
        def _():
            abuf[0:halo, :] = jnp.zeros((halo, cw), F32)
            zbuf[0:8, :] = jnp.zeros((8, cw), F32)

        abuf[halo:halo + t, :] = p_ref[:, 0:512] * _sig(p_ref[:, 512:1024])
        first = halo - (CONV_K - 1)
        window = _aligned_windows(abuf, range(first, first + CONV_K), t, shifted)
        for row0 in range(0, t, CONV_ROWS):
            acc = jnp.zeros((CONV_ROWS, cw), F32)
            for k in range(CONV_K):
                acc = acc + caw_ref[k:k + 1, :] * window(first + k, row0, CONV_ROWS)
            ca_ref[row0:row0 + CONV_ROWS, :] = acc + cab_ref[...]
        ca = ca_ref[...]
        _, _, ln = _layer_norm_rows(ca, lg_ref[...], lb_ref[...])
        ag = p_ref[:, 1024:1536]
        acta_ref[...] = ((ln * _sig(ln)) * (ag * _sig(ag))).astype(BF16)
        abuf[0:halo, :] = abuf[t:t + halo, :]

        zbuf[8:8 + t, :] = p_ref[:, 2560:3072] * p_ref[:, 1536:2048]
        accb = jnp.zeros((t, cw), F32)
        for k in range(SC_K):
            off = 8 - (SC_K - 1) + k
            accb = accb + cbw_ref[k:k + 1, :] * zbuf[off:off + t, :]
        cb_ref[...] = accb
        bg = p_ref[:, 3072:3584]
        actb_ref[...] = ((p_ref[:, 2048:2560] * accb) * (bg * _sig(bg))).astype(BF16)
        zbuf[0:8, :] = zbuf[t:t + 8, :]

    return _call(
        body, (proj, conv_a_w, conv_a_b, ln_a_g, ln_a_b, conv_b_w), exchange, name="mix_ab_fwd", grid=(s // t,),
        in_specs=[_row_spec(t, W_AB), caw_spec, cab_spec, lg_spec, lb_spec, cbw_spec],
        out_specs=[_row_spec(t, cw)] * 4,
        out_shape=[jax.ShapeDtypeStruct((s, cw), BF16)] * 2 + [jax.ShapeDtypeStruct((s, cw), F32)] * 2,
        scratch_shapes=[pltpu.VMEM((halo + t, cw), F32), pltpu.VMEM((8 + t, cw), F32), pltpu.VMEM((7, halo + t, cw), F32)],
        compiler_params=_cparams(1))


def _mix_ab_bwd(proj, ca, cb, dact_a, dact_b, conv_a_w, ln_a_g, ln_a_b, conv_b_w, dproj):
    s = proj.shape[0]
    t = min(ROW_BLOCK, s)
    nb = s // t
    cw = 512
    halo = CONV_HALO
    per = t // halo
    (conv_a_w, caw_spec), (ln_a_g, lg_spec) = _layer_param(conv_a_w), _layer_param(ln_a_g)
    (ln_a_b, lb_spec), (conv_b_w, cbw_spec) = _layer_param(ln_a_b), _layer_param(conv_b_w)

    def body(p_ref, ph_ref, ca_ref, cb_ref, da_ref, db_ref, caw_ref, lg_ref, lb_ref, cbw_ref, dproj_any,
             dg_ref, dcaw_ref, dcab_ref, dlg_ref, dlb_ref, dcbw_ref, abuf, dcabuf, zbuf, dcbbuf, shifted, da_buf, dw_acc):
        del dproj_any
        i = pl.program_id(0)
        r = nb - 1 - i

        @pl.when(i == 0)
        def _():
            dcaw_ref[...] = jnp.zeros_like(dcaw_ref)
            dw_acc[...] = jnp.zeros_like(dw_acc)
            dcab_ref[...] = jnp.zeros_like(dcab_ref)
            dlg_ref[...] = jnp.zeros_like(dlg_ref)
            dlb_ref[...] = jnp.zeros_like(dlb_ref)
            dcbw_ref[...] = jnp.zeros_like(dcbw_ref)
            dcabuf[t:t + halo, :] = jnp.zeros((halo, cw), F32)
            dcbbuf[t:t + 8, :] = jnp.zeros((8, cw), F32)

        keep = (r > 0).astype(F32)
        a1 = p_ref[:, 0:512]
        s2 = _sig(p_ref[:, 512:1024])
        abuf[0:halo, :] = (ph_ref[:, 0:512] * _sig(ph_ref[:, 512:1024])) * keep
        abuf[halo:halo + t, :] = a1 * s2
        ca_v = ca_ref[...]
        xh, rstd, ln = _layer_norm_rows(ca_v, lg_ref[...], lb_ref[...])
        s_ln = _sig(ln)
        ag = p_ref[:, 1024:1536]
        sg = _sig(ag)
        dact = da_ref[...]
        dsa = dact * (ag * sg)
        dg_ref[:, 1024:1536] = (dact * (ln * s_ln) * _dsilu(ag, sg)).astype(BF16)
        dln = dsa * _dsilu(ln, s_ln)
        dlg_ref[...] += jnp.sum(dln * xh, axis=0, keepdims=True)
        dlb_ref[...] += jnp.sum(dln, axis=0, keepdims=True)
        dca = _layer_norm_bwd_rows(dln * lg_ref[...], xh, rstd)
        dcab_ref[...] += jnp.sum(dca, axis=0, keepdims=True)
        dcabuf[0:t, :] = dca
        window = _aligned_windows(dcabuf, range(CONV_K), t, shifted)
        for row0 in range(0, t, CONV_ROWS):
            acc = jnp.zeros((CONV_ROWS, cw), F32)
            for off in range(CONV_K):
                k = (CONV_K - 1) - off
                acc = acc + caw_ref[k:k + 1, :] * window(off, row0, CONV_ROWS)
            da_buf[row0:row0 + CONV_ROWS, :] = acc
        d_a = da_buf[...]
        first = halo - (CONV_K - 1)
        window = _aligned_windows(abuf, range(first, first + CONV_K), t, shifted)
        for row0 in range(0, t, CONV_ROWS):
            dca_rows = dcabuf[row0:row0 + CONV_ROWS, :]
            for k in range(CONV_K):
                prod = dca_rows * window(first + k, row0, CONV_ROWS)
                part = prod[0:8]
                for q in range(8, CONV_ROWS, 8):
                    part = part + prod[q:q + 8]
                dw_acc[8 * k:8 * k + 8, :] += part

        @pl.when(i == nb - 1)
        def _():
            for k in range(CONV_K):
                dcaw_ref[k:k + 1, :] = jnp.sum(dw_acc[8 * k:8 * k + 8, :], axis=0, keepdims=True)

        dg_ref[:, 0:512] = (d_a * s2).astype(BF16)
        dg_ref[:, 512:1024] = (d_a * a1 * s2 * (1.0 - s2)).astype(BF16)
        dcabuf[t:t + halo, :] = dcabuf[0:halo, :]

        xb = p_ref[:, 1536:2048]
        gb = p_ref[:, 2048:2560]
        gc = p_ref[:, 2560:3072]
        bg = p_ref[:, 3072:3584]
        zbuf[0:8, :] = (ph_ref[halo - 8:halo, 2560:3072] * ph_ref[halo - 8:halo, 1536:2048]) * keep
        zbuf[8:8 + t, :] = gc * xb
        sbg = _sig(bg)
        cbv = cb_ref[...]
        dactb = db_ref[...]
        dyb = dactb * (bg * sbg)
        dg_ref[:, 3072:3584] = (dactb * (gb * cbv) * _dsilu(bg, sbg)).astype(BF16)
        dg_ref[:, 2048:2560] = (dyb * cbv).astype(BF16)
        dcb = dyb * gb
        dcbbuf[0:t, :] = dcb
        dz = jnp.zeros((t, cw), F32)
        for k in range(SC_K):
            off = (SC_K - 1) - k
            dz = dz + cbw_ref[k:k + 1, :] * dcbbuf[off:off + t, :]
            src = 8 - (SC_K - 1) + k
            dcbw_ref[k:k + 1, :] += jnp.sum(dcb * zbuf[src:src + t, :], axis=0, keepdims=True)
        dg_ref[:, 2560:3072] = (dz * xb).astype(BF16)
        dg_ref[:, 1536:2048] = (dz * gc).astype(BF16)
        dcbbuf[t:t + 8, :] = dcbbuf[0:8, :]

    rev = lambda i: (nb - 1 - i, 0)
    outs = pl.pallas_call(
        body, name="mix_ab_bwd", grid=(nb,),
        in_specs=[pl.BlockSpec((t, W_AB), rev),
                  pl.BlockSpec((halo, W_AB), lambda i: (jnp.maximum((nb - 1 - i) * per - 1, 0), 0)),
                  pl.BlockSpec((t, cw), rev), pl.BlockSpec((t, cw), rev), pl.BlockSpec((t, cw), rev),
                  pl.BlockSpec((t, cw), rev),
                  caw_spec, lg_spec, lb_spec, cbw_spec,
                  pl.BlockSpec(memory_space=pl.ANY)],
        out_specs=[pl.BlockSpec((t, W_AB), rev), _full_spec((CONV_HALO, cw)), _full_spec((1, cw)), _full_spec((1, cw)),
                   _full_spec((1, cw)), _full_spec((8, cw))],
        out_shape=[jax.ShapeDtypeStruct(dproj.shape, BF16), jax.ShapeDtypeStruct((CONV_HALO, cw), F32),
                   jax.ShapeDtypeStruct((1, cw), F32), jax.ShapeDtypeStruct((1, cw), F32),
                   jax.ShapeDtypeStruct((1, cw), F32), jax.ShapeDtypeStruct((8, cw), F32)],
        scratch_shapes=[pltpu.VMEM((halo + t, cw), F32), pltpu.VMEM((t + halo, cw), F32),
                        pltpu.VMEM((8 + t, cw), F32), pltpu.VMEM((t + 8, cw), F32), pltpu.VMEM((7, halo + t, cw), F32),
                        pltpu.VMEM((t, cw), F32), pltpu.VMEM((8 * CONV_HALO, cw), F32)],
        input_output_aliases={10: 0},
        compiler_params=_cparams(1),
    )(proj, proj, ca, cb, dact_a, dact_b, conv_a_w, ln_a_g, ln_a_b, conv_b_w, dproj)
    return outs


def _lane_is_nope():
    return lax.broadcasted_iota(jnp.int32, (1, HEAD_PAD), 1) < QK_NOPE


def _attn_prep_fwd(proj, q_g, kv_g, wq12, wkv, tcos, tsin, exchange=None):
    s = proj.shape[0]
    t = min(ROW_BLOCK, s)
    c0 = (W_AB + W_G5) // W_C1
    (q_g, qg_spec), (kv_g, kg_spec) = _layer_param(q_g), _layer_param(kv_g)

    def body(p_ref, qg_ref, kg_ref, wq_ref, wkv_ref, cos_ref, sin_ref, qf_ref, kf_ref, v_ref):
        ql = p_ref[:, 0:384]
        qn = (ql * lax.rsqrt(jnp.mean(jnp.square(ql), axis=-1, keepdims=True) + RMS_EPS) * qg_ref[...]).astype(BF16)
        q12 = _dot(qn, wq_ref[...])
        kvl = p_ref[:, 384:640]
        kvn = (kvl * lax.rsqrt(jnp.mean(jnp.square(kvl), axis=-1, keepdims=True) + RMS_EPS) * kg_ref[...]).astype(BF16)
        kv12 = _dot(kvn, wkv_ref[...])
        tcos_v = cos_ref[...]
        tsin_v = sin_ref[...]
        tq1 = jnp.where(_lane_is_nope(), 1.0, tcos_v)
        kpe = p_ref[:, 640:768] * tcos_v + p_ref[:, 768:896] * tsin_v
        for h in range(HEADS):
            lo = h * HEAD_PAD
            qf_ref[h] = (q12[:, lo:lo + HEAD_PAD] * tq1 + q12[:, 1024 + lo:1024 + lo + HEAD_PAD] * tsin_v).astype(BF16)
            kf_ref[h] = (kv12[:, lo:lo + HEAD_PAD] + kpe).astype(BF16)
        v_ref[...] = kv12[:, 1024:1536].astype(BF16)

    return _call(
        body, (proj, q_g, kv_g, wq12, wkv, tcos, tsin), exchange, name="attn_prep_fwd", grid=(s // t,),
        in_specs=[pl.BlockSpec((t, W_C1), lambda i: (i, c0)), qg_spec, kg_spec,
                  _full_spec((384, 2048)), _full_spec((256, 1536)), _row_spec(t, HEAD_PAD), _row_spec(t, HEAD_PAD)],
        out_specs=[pl.BlockSpec((HEADS, t, HEAD_PAD), lambda i: (0, i, 0)),
                   pl.BlockSpec((HEADS, t, HEAD_PAD), lambda i: (0, i, 0)), _row_spec(t, 512)],
        out_shape=[jax.ShapeDtypeStruct((HEADS, s, HEAD_PAD), BF16), jax.ShapeDtypeStruct((HEADS, s, HEAD_PAD), BF16),
                   jax.ShapeDtypeStruct((s, 512), BF16)],
        compiler_params=_cparams(1))


def _attn_prep_bwd(proj, dqf, dkf, dv, q_g, kv_g, wq12, wkv, tcos, tsin, dproj):
    s = proj.shape[0]
    t = min(ROW_BLOCK, s)
    c0 = (W_AB + W_G5) // W_C1
    (q_g, qg_spec), (kv_g, kg_spec) = _layer_param(q_g), _layer_param(kv_g)

    def body(p_ref, dqf_ref, dkf_ref, dv_ref, qg_ref, kg_ref, wq_ref, wkv_ref, cos_ref, sin_ref, dproj_any,
             dg_ref, qn_ref, kvn_ref, dq12_ref, dkv12_ref, dqg_ref, dkg_ref):
        del dproj_any

        @pl.when(pl.program_id(0) == 0)
        def _():
            dqg_ref[...] = jnp.zeros_like(dqg_ref)
            dkg_ref[...] = jnp.zeros_like(dkg_ref)

        tcos_v = cos_ref[...]
        tsin_v = sin_ref[...]
        tq1 = jnp.where(_lane_is_nope(), 1.0, tcos_v)
        dkpe = jnp.zeros((t, HEAD_PAD), F32)
        for h in range(HEADS):
            lo = h * HEAD_PAD
            dq = dqf_ref[h]
            dq12_ref[:, lo:lo + HEAD_PAD] = (dq * tq1).astype(BF16)
            dq12_ref[:, 1024 + lo:1024 + lo + HEAD_PAD] = (dq * tsin_v).astype(BF16)
            dk = dkf_ref[h]
            dkv12_ref[:, lo:lo + HEAD_PAD] = dk.astype(BF16)
            dkpe = dkpe + dk
        dkv12_ref[:, 1024:1536] = dv_ref[...].astype(BF16)
        dg_ref[:, 640:768] = (dkpe * tcos_v).astype(BF16)
        dg_ref[:, 768:896] = (dkpe * tsin_v).astype(BF16)

        def rms_bwd(xl, g, dn):
            rr = lax.rsqrt(jnp.mean(jnp.square(xl), axis=-1, keepdims=True) + RMS_EPS)
            xn = xl * rr
            tt = dn * g
            return rr * (tt - xn * jnp.mean(tt * xn, axis=-1, keepdims=True)), jnp.sum(dn * xn, axis=0, keepdims=True), xn

        ql = p_ref[:, 0:384]
        dqn = _dot_nt(dq12_ref[...], wq_ref[...])
        dql, dqg, qxn = rms_bwd(ql, qg_ref[...], dqn)
        dg_ref[:, 0:384] = dql.astype(BF16)
        dqg_ref[...] += dqg
        qn_ref[...] = (qxn * qg_ref[...]).astype(BF16)
        kvl = p_ref[:, 384:640]
        dkvn = _dot_nt(dkv12_ref[...], wkv_ref[...])
        dkvl, dkg, kxn = rms_bwd(kvl, kg_ref[...], dkvn)
        dg_ref[:, 384:640] = dkvl.astype(BF16)
        dkg_ref[...] += dkg
        kvn_ref[...] = (kxn * kg_ref[...]).astype(BF16)

    return pl.pallas_call(
        body, name="attn_prep_bwd", grid=(s // t,),
        in_specs=[pl.BlockSpec((t, W_C1), lambda i: (i, c0)),
                  pl.BlockSpec((HEADS, t, HEAD_PAD), lambda i: (0, i, 0)),
                  pl.BlockSpec((HEADS, t, HEAD_PAD), lambda i: (0, i, 0)), _row_spec(t, 512),
                  qg_spec, kg_spec, _full_spec((384, 2048)), _full_spec((256, 1536)),
                  _row_spec(t, HEAD_PAD), _row_spec(t, HEAD_PAD), pl.BlockSpec(memory_space=pl.ANY)],
        out_specs=[pl.BlockSpec((t, W_C1), lambda i: (i, c0)), _row_spec(t, 384), _row_spec(t, 256),
                   _row_spec(t, 2048), _row_spec(t, 1536), _full_spec((1, 384)), _full_spec((1, 256))],
        out_shape=[jax.ShapeDtypeStruct(dproj.shape, BF16), jax.ShapeDtypeStruct((s, 384), BF16),
                   jax.ShapeDtypeStruct((s, 256), BF16), jax.ShapeDtypeStruct((s, 2048), BF16),
                   jax.ShapeDtypeStruct((s, 1536), BF16), jax.ShapeDtypeStruct((1, 384), F32),
                   jax.ShapeDtypeStruct((1, 256), F32)],
        input_output_aliases={10: 0},
        compiler_params=_cparams(1),
    )(proj, dqf, dkf, dv, q_g, kv_g, wq12, wkv, tcos, tsin, dproj)


def _causal_keep(t):
    return lax.broadcasted_iota(jnp.int32, (t, t), 0) >= lax.broadcasted_iota(jnp.int32, (t, t), 1)


def _half_select(e):
    lane = lax.broadcasted_iota(jnp.int32, (1, HEAD_PAD), 1)
    return (lane < V_HEAD) if e == 0 else (lane >= V_HEAD)


def _attention_fwd(qf, kf, v, exchange=None):
    _, s, _ = qf.shape
    t = min(ATT_BLOCK, s)
    nq = s // t

    def body(q_ref, k_ref, v_ref, o_ref, lse_ref):
        def rows_of_block(blk):
            past = blk * t
            for pair in range(HEADS // 2):
                lo = pair * HEAD_PAD
                v_diag = v_ref[past:past + t, lo:lo + HEAD_PAD]
                v_past = v_ref[0:past, lo:lo + HEAD_PAD] if blk else None
                out = None
                for e in range(2):
                    h = 2 * pair + e
                    sel = _half_select(e)
                    q = q_ref[h]
                    s_diag = jnp.where(_causal_keep(t), _dot_nt(q, k_ref[h, past:past + t, :]) * ATT_SCALE, NEG_BIG)
                    m = jnp.max(s_diag, axis=1, keepdims=True)
                    if blk:
                        s_past = _dot_nt(q, k_ref[h, 0:past, :]) * ATT_SCALE
                        m = jnp.maximum(m, jnp.max(s_past, axis=1, keepdims=True))
                    p_diag = jnp.exp(s_diag - m)
                    norm = jnp.sum(p_diag, axis=1, keepdims=True)
                    pv = _dot(p_diag.astype(BF16), jnp.where(sel, v_diag, jnp.zeros_like(v_diag)))
                    if blk:
                        p_past = jnp.exp(s_past - m)
                        norm = norm + jnp.sum(p_past, axis=1, keepdims=True)
                        pv = pv + _dot(p_past.astype(BF16), jnp.where(sel, v_past, jnp.zeros_like(v_past)))
                    pv = pv * (1.0 / norm)
                    out = pv if out is None else out + pv
                    lse_ref[h] = m + jnp.log(norm)
                o_ref[:, lo:lo + HEAD_PAD] = out

        for blk in range(nq):
            pl.when(pl.program_id(0) == blk)(functools.partial(rows_of_block, blk))

    return _call(
        body, (qf, kf, v), exchange, name="attention_fwd", grid=(nq,),
        in_specs=[pl.BlockSpec((HEADS, t, HEAD_PAD), lambda i: (0, i, 0)), _full_spec((HEADS, s, HEAD_PAD)),
                  _full_spec((s, 512))],
        out_specs=[_row_spec(t, 512), pl.BlockSpec((HEADS, t, 1), lambda i: (0, i, 0))],
        out_shape=[jax.ShapeDtypeStruct((s, 512), F32), jax.ShapeDtypeStruct((HEADS, s, 1), F32)],
        compiler_params=_cparams(1))


def _attention_bwd(qf, kf, v, o, do, lse, exchange=None):
    _, s, _ = qf.shape
    t = min(ATT_BWD_BLOCK, s)
    nq = s // t

    def body(q_ref, k_ref, v_ref, o_ref, do_ref, lse_ref, dq_ref, dk_ref, dv_ref, dk_acc, dv_acc):
        kj = pl.program_id(0)
        qi = pl.program_id(1)

        @pl.when((kj == 0) & (qi == 0))
        def _():
            dq_ref[...] = jnp.zeros_like(dq_ref)

        @pl.when(qi == kj)
        def _():
            dk_acc[...] = jnp.zeros_like(dk_acc)
            dv_acc[...] = jnp.zeros_like(dv_acc)

        def step(diag):
            rows = pl.ds(pl.multiple_of(qi * t, t), t)
            for h in range(HEADS):
                pair, e = divmod(h, 2)
                sel = _half_select(e)
                lo = pair * HEAD_PAD
                q = q_ref[h]
                k = k_ref[h]
                sc = _dot_nt(q, k) * ATT_SCALE
                if diag:
                    sc = jnp.where(_causal_keep(t), sc, NEG_BIG)
                p = jnp.exp(sc - lse_ref[h])
                do_pair = do_ref[:, lo:lo + HEAD_PAD]
                do_e = jnp.where(sel, do_pair, 0.0)
                do_b = do_e.astype(BF16)
                v2 = v_ref[:, lo:lo + HEAD_PAD]
                dv_acc[:, lo:lo + HEAD_PAD] += _dot_tn(p.astype(BF16), do_b)
                dp = _dot_nt(do_b, jnp.where(sel, v2, jnp.zeros_like(v2)))
                delta = jnp.sum(do_e * o_ref[:, lo:lo + HEAD_PAD], axis=1, keepdims=True)
                ds = (p * (dp - delta) * ATT_SCALE).astype(BF16)
                dq_ref[h, rows, :] += _dot(ds, k)
                dk_acc[h] += _dot_tn(ds, q)

        @pl.when(qi > kj)
        def _():
            step(False)

        @pl.when(qi == kj)
        def _():
            step(True)

        @pl.when(qi == nq - 1)
        def _():
            dk_ref[...] = dk_acc[...]
            dv_ref[...] = dv_acc[...]

    qmap = lambda j, i: (0, jnp.maximum(i, j), 0)
    return _call(
        body, (qf, kf, v, o, do, lse), exchange, name="attention_bwd", grid=(nq, nq),
        in_specs=[pl.BlockSpec((HEADS, t, HEAD_PAD), qmap),
                  pl.BlockSpec((HEADS, t, HEAD_PAD), lambda j, i: (0, j, 0)),
                  pl.BlockSpec((t, 512), lambda j, i: (j, 0)),
                  pl.BlockSpec((t, 512), lambda j, i: (jnp.maximum(i, j), 0)),
                  pl.BlockSpec((t, 512), lambda j, i: (jnp.maximum(i, j), 0)),
                  pl.BlockSpec((HEADS, t, 1), qmap)],
        out_specs=[_full_spec((HEADS, s, HEAD_PAD)),
                   pl.BlockSpec((HEADS, t, HEAD_PAD), lambda j, i: (0, j, 0)),
                   pl.BlockSpec((t, 512), lambda j, i: (j, 0))],
        out_shape=[jax.ShapeDtypeStruct((HEADS, s, HEAD_PAD), F32), jax.ShapeDtypeStruct((HEADS, s, HEAD_PAD), F32),
                   jax.ShapeDtypeStruct((s, 512), F32)],
        scratch_shapes=[pltpu.VMEM((HEADS, t, HEAD_PAD), F32), pltpu.VMEM((t, 512), F32)],
        compiler_params=_cparams(2))


def _merge_fwd(act_a, act_b, o, proj, x, gate, w_a, w_b, w_c, w_o, ln_g, ln_b, alpha, exchange=None):
    s, d = x.shape
    t = min(ROW_BLOCK, s)
    (gate, gate_spec), (ln_g, lg_spec), (ln_b, lb_spec) = _layer_param(gate), _layer_param(ln_g), _layer_param(ln_b)

    def body(aa_ref, ab_ref, o_ref, p_ref, x_ref, gate_ref, wa_ref, wb_ref, wc_ref, wo_ref, lg_ref, lb_ref,
             xn_ref, xh_ref, rstd_ref):
        cg = p_ref[:, 0:512]
        act_c = (o_ref[...] * (cg * _sig(cg))).astype(BF16)
        m = _sig(p_ref[:, 512:1536]) * _dot(aa_ref[...], wa_ref[...])
        m = m + _sig(p_ref[:, 1536:2560]) * _dot(ab_ref[...], wb_ref[...])
        m = m + _sig(p_ref[:, 2560:3584]) * _dot(act_c, wc_ref[...])
        out = _dot(m.astype(BF16), wo_ref[...])
        z = alpha * x_ref[...] + gate_ref[...] * out
        xh, rstd, y = _layer_norm_rows(z, lg_ref[...], lb_ref[...])
        xn_ref[...] = y
        xh_ref[...] = xh
        rstd_ref[...] = rstd

    return _call(
        body, (act_a, act_b, o, proj, x, gate, w_a, w_b, w_c, w_o, ln_g, ln_b), exchange, name="merge_fwd", grid=(s // t,),
        in_specs=[_row_spec(t, 512), _row_spec(t, 512), _row_spec(t, 512),
                  pl.BlockSpec((t, W_G5), lambda i: (i, 1)), _row_spec(t, d), gate_spec,
                  _full_spec((512, d)), _full_spec((512, d)), _full_spec((512, d)), _full_spec((d, d)),
                  lg_spec, lb_spec],
        out_specs=[_row_spec(t, d), _row_spec(t, d), _row_spec(t, 1)],
        out_shape=[jax.ShapeDtypeStruct((s, d), F32), jax.ShapeDtypeStruct((s, d), F32), jax.ShapeDtypeStruct((s, 1), F32)],
        compiler_params=_cparams(1))


def _merge_bwd(dy, xh, rstd, act_a, act_b, o, proj, gate, w_a, w_b, w_c, w_o, ln_g, alpha, exchange=None):
    s, d = dy.shape
    t = min(ROW_BLOCK, s)
    (gate, gate_spec), (ln_g, lg_spec) = _layer_param(gate), _layer_param(ln_g)

    def body(dy_ref, xh_ref, rstd_ref, aa_ref, ab_ref, o_ref, p_ref, gate_ref, wa_ref, wb_ref, wc_ref, wo_ref, lg_ref,
             dxr_ref, dacta_ref, dactb_ref, do_ref, dg_ref, m_ref, dout_ref, dya_ref, dyb_ref, dyc_ref, actc_ref,
             dlg_ref, dlb_ref, dgate_ref):
        @pl.when(pl.program_id(0) == 0)
        def _():
            dlg_ref[...] = jnp.zeros_like(dlg_ref)
            dlb_ref[...] = jnp.zeros_like(dlb_ref)
            dgate_ref[...] = jnp.zeros_like(dgate_ref)

        dyv = dy_ref[...]
        xhv = xh_ref[...]
        dlg_ref[...] += jnp.sum(dyv * xhv, axis=0, keepdims=True)
        dlb_ref[...] += jnp.sum(dyv, axis=0, keepdims=True)
        dz = _layer_norm_bwd_rows(dyv * lg_ref[...], xhv, rstd_ref[...])
        dxr_ref[...] = alpha * dz

        cg = p_ref[:, 0:512]
        scg = _sig(cg)
        silu_cg = cg * scg
        ov = o_ref[...]
        act_c = (ov * silu_cg).astype(BF16)
        actc_ref[...] = act_c
        ya = _dot(aa_ref[...], wa_ref[...])
        yb = _dot(ab_ref[...], wb_ref[...])
        yc = _dot(act_c, wc_ref[...])
        ga = _sig(p_ref[:, 512:1536])
        gb = _sig(p_ref[:, 1536:2560])
        gc = _sig(p_ref[:, 2560:3584])
        mb = (ga * ya + gb * yb + gc * yc).astype(BF16)
        m_ref[...] = mb
        out = _dot(mb, wo_ref[...])
        dgate_ref[...] += jnp.sum(dz * out, axis=0, keepdims=True)
        dout = (gate_ref[...] * dz).astype(BF16)
        dout_ref[...] = dout
        dm = _dot_nt(dout, wo_ref[...])

        dya = (dm * ga).astype(BF16)
        dya_ref[...] = dya
        dg_ref[:, 512:1536] = (dm * ya * ga * (1.0 - ga)).astype(BF16)
        dacta_ref[...] = _dot_nt(dya, wa_ref[...])
        dyb = (dm * gb).astype(BF16)
        dyb_ref[...] = dyb
        dg_ref[:, 1536:2560] = (dm * yb * gb * (1.0 - gb)).astype(BF16)
        dactb_ref[...] = _dot_nt(dyb, wb_ref[...])
        dyc = (dm * gc).astype(BF16)
        dyc_ref[...] = dyc
        dg_ref[:, 2560:3584] = (dm * yc * gc * (1.0 - gc)).astype(BF16)
        dactc = _dot_nt(dyc, wc_ref[...])
        do_ref[...] = dactc * silu_cg
        dg_ref[:, 0:512] = (dactc * ov * _dsilu(cg, scg)).astype(BF16)

    return _call(
        body, (dy, xh, rstd, act_a, act_b, o, proj, gate, w_a, w_b, w_c, w_o, ln_g), exchange, name="merge_bwd", grid=(s // t,),
        in_specs=[_row_spec(t, d), _row_spec(t, d), _row_spec(t, 1), _row_spec(t, 512), _row_spec(t, 512),
                  _row_spec(t, 512), pl.BlockSpec((t, W_G5), lambda i: (i, 1)), gate_spec,
                  _full_spec((512, d)), _full_spec((512, d)), _full_spec((512, d)), _full_spec((d, d)),
                  lg_spec],
        out_specs=[_row_spec(t, d), _row_spec(t, 512), _row_spec(t, 512), _row_spec(t, 512),
                   pl.BlockSpec((t, W_G5), lambda i: (i, 1)),
                   _row_spec(t, d), _row_spec(t, d), _row_spec(t, d), _row_spec(t, d), _row_spec(t, d),
                   _row_spec(t, 512), _full_spec((1, d)), _full_spec((1, d)), _full_spec((1, d))],
        out_shape=[jax.ShapeDtypeStruct((s, d), F32), jax.ShapeDtypeStruct((s, 512), F32),
                   jax.ShapeDtypeStruct((s, 512), F32), jax.ShapeDtypeStruct((s, 512), F32),
                   jax.ShapeDtypeStruct((s, W_PACK), BF16),
                   jax.ShapeDtypeStruct((s, d), BF16), jax.ShapeDtypeStruct((s, d), BF16),
                   jax.ShapeDtypeStruct((s, d), BF16), jax.ShapeDtypeStruct((s, d), BF16),
                   jax.ShapeDtypeStruct((s, d), BF16), jax.ShapeDtypeStruct((s, 512), BF16),
                   jax.ShapeDtypeStruct((1, d), F32), jax.ShapeDtypeStruct((1, d), F32),
                   jax.ShapeDtypeStruct((1, d), F32)],
        compiler_params=_cparams(1))


def _proj_bwd_input(dproj, w_pack, x, scale, dxres, exchange=None):
    s, d = x.shape
    t = min(512, s)
    scale, scale_spec = _layer_param(scale)

    def body(dg_ref, w_ref, x_ref, sc_ref, dxr_ref, dx_ref, dsc_ref, dsh_ref):
        @pl.when(pl.program_id(0) == 0)
        def _():
            dsc_ref[...] = jnp.zeros_like(dsc_ref)
            dsh_ref[...] = jnp.zeros_like(dsh_ref)

        du = _dot(dg_ref[...], w_ref[...])
        dx_ref[...] = du * (1.0 + sc_ref[...]) + dxr_ref[...]
        dsc_ref[...] += jnp.sum(du * x_ref[...], axis=0, keepdims=True)
        dsh_ref[...] += jnp.sum(du, axis=0, keepdims=True)

    w_spec = pl.BlockSpec((W_PACK, d), lambda i: (0, 0), pipeline_mode=pl.Buffered(1))
    return _call(
        body, (dproj, w_pack, x, scale, dxres), exchange, name="proj_bwd_input", grid=(s // t,),
        in_specs=[_row_spec(t, W_PACK), w_spec, _row_spec(t, d), scale_spec, _row_spec(t, d)],
        out_specs=[_row_spec(t, d), _full_spec((1, d)), _full_spec((1, d))],
        out_shape=[jax.ShapeDtypeStruct((s, d), F32), jax.ShapeDtypeStruct((1, d), F32), jax.ShapeDtypeStruct((1, d), F32)],
        compiler_params=_cparams(1))


def _loss_head(y, target):
    s, d = y.shape
    t = min(512, s)

    def body(y_ref, t_ref, dy_ref, loss_ref):
        @pl.when(pl.program_id(0) == 0)
        def _():
            loss_ref[...] = jnp.zeros_like(loss_ref)

        err = y_ref[...] - t_ref[...]
        dy_ref[...] = err / d
        part = 0.5 * jnp.sum(jnp.mean(jnp.square(err), axis=-1, keepdims=True), axis=0, keepdims=True)
        loss_ref[...] += jnp.broadcast_to(part, loss_ref.shape)

    return pl.pallas_call(
        body, name="loss_head", grid=(s // t,),
        in_specs=[_row_spec(t, d), _row_spec(t, d)],
        out_specs=[_row_spec(t, d), _full_spec((1, 128))],
        out_shape=[jax.ShapeDtypeStruct((s, d), F32), jax.ShapeDtypeStruct((1, 128), F32)],
        compiler_params=_cparams(1),
    )(y, target)


def _adamw(g, w, m, v, name):
    r, n = g.shape
    block_bytes = 5 * 2 ** 19
    rb = r
    if r * n * 4 > block_bytes:
        rb = max(k for k in range(8, r, 8) if r % k == 0 and k * n * 4 <= block_bytes)

    def body(g_ref, w_ref, m_ref, v_ref, d_ref, mo_ref, vo_ref):
        gv = g_ref[...]
        mn = ADAM_B1 * m_ref[...] + (1.0 - ADAM_B1) * gv
        vn = ADAM_B2 * v_ref[...] + (1.0 - ADAM_B2) * jnp.square(gv)
        m_hat = mn / (1.0 - ADAM_B1 ** ADAM_STEP)
        v_hat = vn / (1.0 - ADAM_B2 ** ADAM_STEP)
        d_ref[...] = -ADAM_LR * (m_hat / (jnp.sqrt(v_hat) + ADAM_EPS) + ADAM_WD * w_ref[...])
        mo_ref[...] = mn
        vo_ref[...] = vn

    spec = pl.BlockSpec((rb, n), lambda i: (i, 0))
    return pl.pallas_call(
        body, name=name, grid=(r // rb,),
        in_specs=[spec] * 4, out_specs=[spec] * 3,
        out_shape=[jax.ShapeDtypeStruct((r, n), F32)] * 3, compiler_params=_cparams(1),
    )(g, w, m, v)


SMALL = ("b_ada", "conv_a_b", "ln_a_g", "ln_a_b", "q_norm_g", "kv_norm_g", "ln_g", "ln_b", "conv_a_w", "conv_b_w")
ROWS_256 = (("w_a_out", 512), ("w_b_out", 512), ("w_c_out", 512), ("w_ukv", 256))


def _rot_cols(w):
    return jnp.concatenate([-w[..., 16:], w[..., :16]], axis=-1)


def _unrot_cols(g):
    return jnp.concatenate([g[..., 16:], -g[..., :16]], axis=-1)


def _pack_w_in(wt):
    z = lambda n: jnp.zeros((n, wt.shape[1]), wt.dtype)
    wk = wt[4224:4256]
    rot = jnp.concatenate([-wk[16:], wk[:16]], axis=0)
    return jnp.concatenate([wt[0:3584], wt[4256:7840], wt[3584:4224], z(64), wk, z(32), z(64), rot, z(32)], axis=0)


def _unpack_w_in_grad(g):
    c1 = W_AB + W_G5
    g_rot = g[c1 + 768 + 64:c1 + 768 + 96].astype(F32)
    gk = g[c1 + 640 + 64:c1 + 640 + 96].astype(F32) + jnp.concatenate([g_rot[16:], -g_rot[:16]], axis=0)
    return jnp.concatenate([g[0:3584], g[c1:c1 + 640], gk.astype(g.dtype), g[W_AB:W_AB + W_G5]], axis=0)


def _pack_w_uq(w):
    lead = w.shape[:-1]
    wh = w.reshape(lead + (HEADS, QK_NOPE + QK_ROPE))
    nope, rope = wh[..., :QK_NOPE], wh[..., QK_NOPE:]
    z32 = jnp.zeros_like(rope)
    q1 = jnp.concatenate([nope, rope, z32], axis=-1).reshape(lead + (HEADS * HEAD_PAD,))
    q2 = jnp.concatenate([jnp.zeros_like(nope), _rot_cols(rope), z32], axis=-1).reshape(lead + (HEADS * HEAD_PAD,))
    return jnp.concatenate([q1, q2], axis=-1)


def _unpack_w_uq_grad(g):
    lead = g.shape[:-1]
    g1 = g[..., :1024].reshape(lead + (HEADS, HEAD_PAD))
    g2 = g[..., 1024:].reshape(lead + (HEADS, HEAD_PAD))
    rope = g1[..., QK_NOPE:QK_NOPE + QK_ROPE] + _unrot_cols(g2[..., QK_NOPE:QK_NOPE + QK_ROPE])
    return jnp.concatenate([g1[..., :QK_NOPE], rope], axis=-1).reshape(lead + (HEADS * (QK_NOPE + QK_ROPE),))


def _pack_w_ukv(w):
    lead = w.shape[:-1]
    wh = w.reshape(lead + (HEADS, QK_NOPE + V_HEAD))
    kn, vv = wh[..., :QK_NOPE], wh[..., QK_NOPE:]
    k1 = jnp.concatenate([kn, jnp.zeros_like(kn)], axis=-1).reshape(lead + (HEADS * HEAD_PAD,))
    return jnp.concatenate([k1, vv.reshape(lead + (HEADS * V_HEAD,))], axis=-1)


def _unpack_w_ukv_grad(g):
    lead = g.shape[:-1]
    gk = g[..., :1024].reshape(lead + (HEADS, HEAD_PAD))[..., :QK_NOPE]
    gv = g[..., 1024:].reshape(lead + (HEADS, V_HEAD))
    return jnp.concatenate([gk, gv], axis=-1).reshape(lead + (HEADS * (QK_NOPE + V_HEAD),))


def _join_cols(g4):
    return jnp.concatenate([g4[j] for j in range(4)], axis=-1)


def _split_cols(w):
    c4 = w.shape[-1] // 4
    return jnp.stack([w[:, j * c4:(j + 1) * c4] for j in range(4)])


def kernel(x, c, positions, w_ada, b_ada, w_in, conv_a_w, conv_a_b, ln_a_g, ln_a_b, w_a_out, conv_b_w, w_b_out, q_norm_g, kv_norm_g, w_uq, w_ukv, w_c_out, w_o, ln_g, ln_b, loss_target, m_w_ada, m_b_ada, m_w_in, m_conv_a_w, m_conv_a_b, m_ln_a_g, m_ln_a_b, m_w_a_out, m_conv_b_w, m_w_b_out, m_q_norm_g, m_kv_norm_g, m_w_uq, m_w_ukv, m_w_c_out, m_w_o, m_ln_g, m_ln_b, v_w_ada, v_b_ada, v_w_in, v_conv_a_w, v_conv_a_b, v_ln_a_g, v_ln_a_b, v_w_a_out, v_conv_b_w, v_w_b_out, v_q_norm_g, v_kv_norm_g, v_w_uq, v_w_ukv, v_w_c_out, v_w_o, v_ln_g, v_ln_b):
    weights = dict(w_ada=w_ada, b_ada=b_ada, w_in=w_in, conv_a_w=conv_a_w, conv_a_b=conv_a_b, ln_a_g=ln_a_g, ln_a_b=ln_a_b,
                   w_a_out=w_a_out, conv_b_w=conv_b_w, w_b_out=w_b_out, q_norm_g=q_norm_g, kv_norm_g=kv_norm_g, w_uq=w_uq,
                   w_ukv=w_ukv, w_c_out=w_c_out, w_o=w_o, ln_g=ln_g, ln_b=ln_b)
    mom_m = dict(w_ada=m_w_ada, b_ada=m_b_ada, w_in=m_w_in, conv_a_w=m_conv_a_w, conv_a_b=m_conv_a_b, ln_a_g=m_ln_a_g,
                 ln_a_b=m_ln_a_b, w_a_out=m_w_a_out, conv_b_w=m_conv_b_w, w_b_out=m_w_b_out, q_norm_g=m_q_norm_g,
                 kv_norm_g=m_kv_norm_g, w_uq=m_w_uq, w_ukv=m_w_ukv, w_c_out=m_w_c_out, w_o=m_w_o, ln_g=m_ln_g, ln_b=m_ln_b)
    mom_v = dict(w_ada=v_w_ada, b_ada=v_b_ada, w_in=v_w_in, conv_a_w=v_conv_a_w, conv_a_b=v_conv_a_b, ln_a_g=v_ln_a_g,
                 ln_a_b=v_ln_a_b, w_a_out=v_w_a_out, conv_b_w=v_conv_b_w, w_b_out=v_w_b_out, q_norm_g=v_q_norm_g,
                 kv_norm_g=v_kv_norm_g, w_uq=v_w_uq, w_ukv=v_w_ukv, w_c_out=v_w_c_out, w_o=v_w_o, ln_g=v_ln_g, ln_b=v_ln_b)
    order = ["w_ada", "b_ada", "w_in", "conv_a_w", "conv_a_b", "ln_a_g", "ln_a_b", "w_a_out", "conv_b_w", "w_b_out",
             "q_norm_g", "kv_norm_g", "w_uq", "w_ukv", "w_c_out", "w_o", "ln_g", "ln_b"]

    n_layers = w_in.shape[0]
    s, d = x.shape[1], x.shape[2]
    alpha = float((2 * n_layers) ** 0.25)
    ix, iy, ic = lax.axis_index("x"), lax.axis_index("y"), lax.axis_index("c")
    me = 4 * ix + 2 * iy + ic
    chip = 2 * ix + iy
    xs = x[0]

    def four(tree, cast):
        return [cast(jnp.swapaxes(tree["w_in"], 1, 2)), jnp.concatenate([cast(tree[n]) for n, _ in ROWS_256], axis=-2),
                cast(tree["w_uq"]), cast(tree["w_o"])]

    shards = four(weights, lambda a: a.astype(BF16))
    landed_w = _run_exchange(_gather_layer_exchange(shards, 0), "gather_weights_first")
    landed_w = _run_exchange(_gather_pair_exchange(landed_w), "gather_weights_first_pair")
    w_pack, wq12, wkv, w_a, w_b, w_c, w_of = [], [], [], [], [], [], []

    def unpack_layer(g_in, g_256, g_uq, g_wo):
        w_pack.append(_pack_w_in(g_in.reshape(-1, d)))
        wq12.append(_pack_w_uq(_join_cols(g_uq)))
        rows = _join_cols(g_256)
        w_a.append(rows[0:512])
        w_b.append(rows[512:1024])
        w_c.append(rows[1024:1536])
        wkv.append(_pack_w_ukv(rows[1536:1792]))
        w_of.append(g_wo.reshape(d, d))

    cw_a = conv_a_w.reshape(n_layers * CONV_K, 128)
    cw_b = conv_b_w.reshape(n_layers * SC_K, 128)
    c_rows = d // 128
    first_rows = c_rows + n_layers * (CONV_K + SC_K)
    first_pad = (-first_rows) % 8
    first = jnp.concatenate([c.reshape(c_rows, 128), cw_a, cw_b, jnp.zeros((first_pad, 128), F32)], axis=0)
    fr = first_rows + first_pad
    got = _all_gather_small(first, "gather_cond").reshape(8, fr, 128)
    c_all = got[:, :c_rows].reshape(8, d)
    conv_a_full = jnp.moveaxis(got[0::2, c_rows:c_rows + n_layers * CONV_K].reshape(4, n_layers, CONV_K, 128), 0, 2)
    conv_a_full = conv_a_full.reshape(n_layers, CONV_K, 512)
    b0 = c_rows + n_layers * CONV_K
    conv_b_full = jnp.moveaxis(got[0::2, b0:b0 + n_layers * SC_K].reshape(4, n_layers, SC_K, 128), 0, 2)
    conv_b_full = conv_b_full.reshape(n_layers, SC_K, 512)

    ada_sh, c_act = _ada_shard(c_all, w_ada.astype(BF16))
    ada_all = _all_gather_small(ada_sh.reshape(n_layers * 8, 768), "gather_ada").reshape(8, n_layers, 8, 768)
    ada_mine = lax.dynamic_index_in_dim(ada_all[0::2], me, axis=2, keepdims=False)
    ada = jnp.moveaxis(ada_mine, 0, 1).reshape(n_layers, 3 * d) + b_ada
    def stacked(a):
        return a.reshape(n_layers, 1, a.shape[-1])

    shift, scale, gate = stacked(ada[:, 0:d]), stacked(ada[:, d:2 * d]), stacked(ada[:, 2 * d:3 * d])
    conv_a_b3, ln_a_g3, ln_a_b3 = stacked(conv_a_b), stacked(ln_a_g), stacked(ln_a_b)
    q_norm_g3, kv_norm_g3, ln_g3, ln_b3 = stacked(q_norm_g), stacked(kv_norm_g), stacked(ln_g), stacked(ln_b)

    tcos, tsin = _rope_tables(positions.reshape(s, 1))

    saved = []
    h = xs
    for l in range(n_layers):
        unpack_layer(*landed_w)
        u = _modulate(h, (scale, l), (shift, l))
        more = l + 1 < n_layers
        (proj,), lw_in = _matmul_nt(u, w_pack[l], COL_BLK, "proj_fwd",
                                    _gather_layer_exchange(shards[:1], l + 1, (0, 2)) if more else None)
        (act_a, act_b, ca, cb), lw_256 = _mix_ab_fwd(
            proj, (conv_a_full, l), (conv_a_b3, l), (ln_a_g3, l), (ln_a_b3, l), (conv_b_full, l),
            _gather_layer_exchange(shards[1:2], l + 1) if more else None)
        (qf, kf, vv), lw_small = _attn_prep_fwd(proj, (q_norm_g3, l), (kv_norm_g3, l), wq12[l], wkv[l], tcos, tsin,
                                                _gather_layer_exchange(shards[2:], l + 1) if more else None)
        (o, lse), lw_in = _attention_fwd(
            qf, kf, vv, _gather_layer_exchange(shards[:1], l + 1, (1, 2), lw_in) if more else None)
        (h_next, xh, rstd), landed_w = _merge_fwd(
            act_a, act_b, o, proj, h, (gate, l), w_a[l], w_b[l], w_c[l], w_of[l], (ln_g3, l), (ln_b3, l), alpha,
            _gather_pair_exchange(lw_in + lw_256 + lw_small) if more else None)
        saved.append(dict(x=h, u=u, proj=proj, act_a=act_a, act_b=act_b, ca=ca, cb=cb, qf=qf, kf=kf, v=vv, o=o, lse=lse,
                          xh=xh, rstd=rstd))
        h = h_next

    dy, loss_part = _loss_head(h, loss_target[0])
    loss = lax.psum(loss_part[0, 0], ("x", "y", "c"))

    c_idx = ic.reshape(1).astype(jnp.int32)
    reduced = None
    pending = None
    small_rows = [None] * n_layers
    for l in reversed(range(n_layers)):
        sv = saved[l]
        ((dxres, dact_a, dact_b, do, dproj, m_bf, dout_bf, dya, dyb, dyc, act_c, d_ln_g, d_ln_b, d_gate),
         rcv_rest) = _merge_bwd(
            dy, sv["xh"], sv["rstd"], sv["act_a"], sv["act_b"], sv["o"], sv["proj"], (gate, l), w_a[l], w_b[l],
            w_c[l], w_of[l], (ln_g3, l), alpha, _scatter_exchange(pending[1:]) if pending else None)
        g_o = _matmul_tn(m_bf, dout_bf, d, "grad_w_o")
        g_a = _matmul_tn(sv["act_a"], dya, d, "grad_w_a_out")
        g_b = _matmul_tn(sv["act_b"], dyb, d, "grad_w_b_out")
        g_c = _matmul_tn(act_c, dyc, d, "grad_w_c_out")
        (dqf, dkf, dvv), rcv_in = _attention_bwd(sv["qf"], sv["kf"], sv["v"], sv["o"], do, sv["lse"],
                                                 _scatter_exchange(pending[:1]) if pending else None)
        if pending:
            reduced = _sum_chips(rcv_in + rcv_rest, shapes, reduced, l + 1, n_layers, c_idx)
        dproj, qn, kvn, dq12, dkv12, d_qg, d_kg = _attn_prep_bwd(
            sv["proj"], dqf, dkf, dvv, (q_norm_g3, l), (kv_norm_g3, l), wq12[l], wkv[l], tcos, tsin, dproj)
        g_uq = _unpack_w_uq_grad(_matmul_tn(qn, dq12, 1024, "grad_w_uq"))
        g_kv = _unpack_w_ukv_grad(_matmul_tn(kvn, dkv12, 768, "grad_w_ukv"))
        dproj, d_caw, d_cab, d_lag, d_lab, d_cbw = _mix_ab_bwd(
            sv["proj"], sv["ca"], sv["cb"], dact_a, dact_b, (conv_a_full, l), (ln_a_g3, l), (ln_a_b3, l),
            (conv_b_full, l), dproj)
        g_in_l = _unpack_w_in_grad(_matmul_tn(dproj, sv["u"], d, "grad_w_in", bm=COL_BLK, whole_sum=True))
        partial = [g_in_l.reshape(4, -1, d),
                   jnp.concatenate([_split_cols(g) for g in (g_a, g_b, g_c, g_kv)], axis=1).astype(BF16),
                   _split_cols(g_uq).astype(BF16), g_o.reshape(4, d // 4, d).astype(BF16)]
        shapes = [p.shape[1:] for p in partial]
        (dy, d_scale, d_shift), from_sibling = _proj_bwd_input(dproj, w_pack[l], sv["x"], (scale, l), dxres,
                                                               _swap_exchange(partial))
        small_rows[l] = jnp.concatenate(
            [d_shift[0], d_scale[0], d_gate[0], d_cab[0], d_lag[0], d_lab[0], d_qg[0], d_kg[0], d_ln_g[0], d_ln_b[0],
             d_caw[:CONV_K].reshape(-1), d_cbw[:SC_K].reshape(-1)])
        pending = _add_halves(partial, from_sibling, c_idx)
    grad_x = dy[None]
    received = _run_exchange(_scatter_exchange(pending), "grad_scatter_last")
    reduced = _sum_chips(received, shapes, reduced, 0, n_layers, c_idx)
    r_in, r_256, r_uq, r_wo = _run_exchange(_gather_pair_exchange(reduced), "grad_pair_halves")
    grads = {"w_in": jnp.swapaxes(r_in, 1, 2), "w_uq": r_uq, "w_o": r_wo}
    row0 = 0
    for n, rows in ROWS_256:
        grads[n] = r_256[:, row0:row0 + rows]
        row0 += rows

    small = jnp.stack(small_rows)
    p_small = small.shape[1]
    n_small = n_layers * p_small
    pad_small = (-n_small) % (8 * 128)
    small_flat = jnp.concatenate([small.reshape(-1), jnp.zeros((pad_small,), F32)]).reshape(-1, 128)
    sr = small_flat.shape[0]
    small_all = _all_gather_small(small_flat, "gather_small_grads").reshape(8, sr, 128)
    small_sum = _sum_devices(small_all).reshape(-1)[:n_small].reshape(n_layers, p_small)
    d_ada_all = small_all.reshape(8, -1)[:, :n_small].reshape(8, n_layers, p_small)[:, :, :3 * d]

    cuts = np.cumsum([0, 3 * d, 512, 512, 512, 384, 256, d, d, CONV_K * 512, SC_K * 512])
    pieces = [small_sum[:, cuts[k]:cuts[k + 1]] for k in range(10)]
    grads.update(zip(SMALL, pieces[:8]))
    ga_full = pieces[8].reshape(n_layers, CONV_K, 512)
    gb_full = pieces[9].reshape(n_layers, SC_K, 512)
    grads["conv_a_w"] = lax.dynamic_slice_in_dim(ga_full, chip * 128, 128, axis=2)
    grads["conv_b_w"] = lax.dynamic_slice_in_dim(gb_full, chip * 128, 128, axis=2)

    dada_sh = lax.dynamic_slice_in_dim(d_ada_all, chip * 768, 768, axis=2)
    dada16 = jnp.concatenate([jnp.moveaxis(dada_sh, 0, 1), jnp.zeros((n_layers, 8, 768), F32)], axis=1).astype(BF16)
    cact16 = jnp.concatenate([c_act, jnp.zeros_like(c_act)], axis=0).astype(BF16)
    grads["w_ada"] = _ada_weight_grad(cact16, dada16)

    def as2d(a):
        return a.reshape(-1, a.shape[-1])

    deltas, new_m, new_v = {}, {}, {}
    for n in ("w_in", "w_a_out", "w_b_out", "w_c_out", "w_uq", "w_ukv", "w_o", "w_ada"):
        view = (lambda a: jnp.swapaxes(a, 1, 2)) if n == "w_in" else (lambda a: a)
        shape = view(weights[n]).shape
        dl, mn, vn = _adamw(as2d(view(grads[n])), as2d(view(weights[n])), as2d(view(mom_m[n])), as2d(view(mom_v[n])),
                            "adamw_" + n)
        deltas[n], new_m[n], new_v[n] = view(dl.reshape(shape)), view(mn.reshape(shape)), view(vn.reshape(shape))

    def pack_small(tree):
        flat = jnp.concatenate([tree[n].reshape(-1) for n in SMALL])
        pad = (-flat.shape[0]) % (8 * 128)
        return jnp.concatenate([flat, jnp.zeros((pad,), F32)]).reshape(-1, 128)

    dl_s, mn_s, vn_s = _adamw(pack_small(grads), pack_small(weights), pack_small(mom_m), pack_small(mom_v), "adamw_small")
    off = 0
    for n in SMALL:
        sz = int(np.prod(weights[n].shape))
        for tree, flat in ((deltas, dl_s), (new_m, mn_s), (new_v, vn_s)):
            tree[n] = flat.reshape(-1)[off:off + sz].reshape(weights[n].shape)
        off += sz

    return (loss, grad_x, *[grads[n] for n in order], *[deltas[n] for n in order], *[new_m[n] for n in order],
            *[new_v[n] for n in order])
```
